```python
import jax, jax.numpy as jnp
from jax import lax
import numpy as np

D_MODEL = 1024
BATCH = 8
SEQ = 4096
DEPTH = 1

HEAD_DIM = 64
NSA_HEADS = D_MODEL // (2 * HEAD_DIM)
FOX_HEADS = D_MODEL // (2 * HEAD_DIM)
NSA_KV_HEADS = max(1, NSA_HEADS // 4)
NSA_HPG = NSA_HEADS // NSA_KV_HEADS
NSA_WIDTH = NSA_HEADS * HEAD_DIM
FOX_WIDTH = FOX_HEADS * HEAD_DIM
MIX_WIDTH = NSA_WIDTH + FOX_WIDTH
KV_WIDTH = NSA_KV_HEADS * HEAD_DIM
CMP_BLOCK = 32
CMP_STRIDE = 16
CMP_HIDDEN = 2 * HEAD_DIM
SEL_BLOCK = 64
N_SELECT = 16
WINDOW = 512
NSA_Q_BLOCK = 64
FOX_Q_BLOCK = 128
ROPE_THETA = 500000.0
ROPE_DIM = HEAD_DIM // 4
N_GROUPS = 4
EXPERTS_PER_GROUP = 4
N_EXPERTS = N_GROUPS * EXPERTS_PER_GROUP
TOP_K_IN_GROUP = 2
D_EXPERT = D_MODEL // 2
PLE_DIM = 256
EPS = 1e-6
NEG_INF = -1e30
FORCE_BONUS = 1e4
IN_SPLITS = (NSA_WIDTH, KV_WIDTH, KV_WIDTH, KV_WIDTH, KV_WIDTH, KV_WIDTH, KV_WIDTH,
             3 * NSA_HEADS, FOX_WIDTH, FOX_WIDTH, FOX_WIDTH, FOX_HEADS)
IN_COLS = sum(IN_SPLITS)
IN_OFFSETS = tuple(int(v) for v in np.cumsum(IN_SPLITS)[:-1])

kernel_name = 'hymba_nsa_fox_hmoe_layer'


def _rmsnorm(x, g):
    xf = x.astype(jnp.float32)
    y = xf * lax.rsqrt(jnp.mean(xf * xf, axis=-1, keepdims=True) + EPS)
    return (y * g.astype(jnp.float32)).astype(x.dtype)


def _masked_softmax(logits, mask):
    l = jnp.where(mask, logits.astype(jnp.float32), NEG_INF)
    return jax.nn.softmax(l, axis=-1) * mask


def _partial_rope(x, pos):
    half = ROPE_DIM // 2
    inv_freq = jnp.power(jnp.float32(ROPE_THETA), -jnp.arange(half, dtype=jnp.float32) / half)
    ang = pos.astype(jnp.float32)[..., None] * inv_freq
    cos = jnp.cos(ang)[..., None, :]
    sin = jnp.sin(ang)[..., None, :]
    xf = x.astype(jnp.float32)
    x1 = xf[..., :half]
    x2 = xf[..., half:ROPE_DIM]
    out = jnp.concatenate([x1 * cos - x2 * sin, x2 * cos + x1 * sin, xf[..., ROPE_DIM:]], axis=-1)
    return out.astype(x.dtype)


def _compress(tok, pe, w1, w2, tok_idx):
    B = tok.shape[0]
    n_cmp = tok_idx.shape[0]
    blk = tok[:, tok_idx] + pe[None, None, :, None, :]
    blk = jnp.swapaxes(blk, 2, 3).reshape(B, n_cmp, NSA_KV_HEADS, CMP_BLOCK * HEAD_DIM)
    return jax.nn.silu(blk @ w1) @ w2


def _cmp_to_sel_matrix(n_cmp, n_blk):
    cs = np.arange(n_cmp)[:, None] * CMP_STRIDE
    ss = np.arange(n_blk)[None, :] * SEL_BLOCK
    ov = np.clip(np.minimum(cs + CMP_BLOCK, ss + SEL_BLOCK) - np.maximum(cs, ss), 0, None)
    return jnp.asarray(ov / CMP_BLOCK, dtype=jnp.float32)


def _nsa_mixer(q, kc_tok, vc_tok, ks, vs, kw, vw, gates, positions,
               pe_k, w_k1, w_k2, pe_v, w_v1, w_v2):
    B, S = q.shape[0], q.shape[1]
    G, HPG, HD = NSA_KV_HEADS, NSA_HPG, HEAD_DIM
    QB = NSA_Q_BLOCK
    scale = HEAD_DIM ** -0.5
    n_cmp = (S - CMP_BLOCK) // CMP_STRIDE + 1
    n_blk = S // SEL_BLOCK
    n_sel = min(N_SELECT, n_blk)
    tok_idx = np.arange(n_cmp)[:, None] * CMP_STRIDE + np.arange(CMP_BLOCK)[None, :]
    cmp_end = jnp.asarray(tok_idx[:, -1], dtype=jnp.int32)
    kc = _partial_rope(_compress(kc_tok, pe_k, w_k1, w_k2, tok_idx), positions[:, cmp_end])
    vc = _compress(vc_tok, pe_v, w_v1, w_v2, tok_idx)
    agg = _cmp_to_sel_matrix(n_cmp, n_blk)
    ks_blk = ks.reshape(B, n_blk, SEL_BLOCK, G, HD).transpose(0, 3, 1, 2, 4)
    vs_blk = vs.reshape(B, n_blk, SEL_BLOCK, G, HD).transpose(0, 3, 1, 2, 4)
    kw_pad = jnp.pad(kw, ((0, 0), (WINDOW, 0), (0, 0), (0, 0)))
    vw_pad = jnp.pad(vw, ((0, 0), (WINDOW, 0), (0, 0), (0, 0)))
    b_ix = jnp.arange(B)[:, None, None, None]
    g_ix = jnp.arange(G)[None, :, None, None]
    blk_ids = jnp.arange(n_blk)
    in_blk = jnp.arange(SEL_BLOCK)

    def block_fn(c):
        s0 = c * QB
        t = s0 + jnp.arange(QB)
        qg = lax.dynamic_slice_in_dim(q, s0, QB, axis=1).reshape(B, QB, G, HPG, HD)
        gt = lax.dynamic_slice_in_dim(gates, s0, QB, axis=1)
        s_c = jnp.einsum('bqghd,bngd->bghqn', qg, kc) * scale
        p_c = _masked_softmax(s_c, cmp_end[None, :] <= t[:, None])
        o_c = jnp.einsum('bghqn,bngd->bqghd', p_c.astype(vc.dtype), vc)
        imp = jnp.einsum('bghqn,nj->bgqj', p_c, agg)
        cur = t // SEL_BLOCK
        forced = (blk_ids[None] == 0) | (blk_ids[None] == cur[:, None]) | (blk_ids[None] == cur[:, None] - 1)
        future = blk_ids[None] * SEL_BLOCK > t[:, None]
        score = jnp.where(future, -1.0, imp + FORCE_BONUS * forced)
        _, sel = lax.top_k(score, n_sel)
        k_sel = ks_blk[b_ix, g_ix, sel].reshape(B, G, QB, n_sel * SEL_BLOCK, HD)
        v_sel = vs_blk[b_ix, g_ix, sel].reshape(B, G, QB, n_sel * SEL_BLOCK, HD)
        kpos = (sel[..., None] * SEL_BLOCK + in_blk).reshape(B, G, QB, n_sel * SEL_BLOCK)
        s_s = jnp.einsum('bqghd,bgqkd->bghqk', qg, k_sel) * scale
        p_s = _masked_softmax(s_s, (kpos <= t[:, None])[:, :, None])
        o_s = jnp.einsum('bghqk,bgqkd->bqghd', p_s.astype(v_sel.dtype), v_sel)
        k_w = lax.dynamic_slice_in_dim(kw_pad, s0, WINDOW + QB, axis=1)
        v_w = lax.dynamic_slice_in_dim(vw_pad, s0, WINDOW + QB, axis=1)
        wpos = s0 - WINDOW + jnp.arange(WINDOW + QB)
        rel = t[:, None] - wpos[None, :]
        w_mask = (rel >= 0) & (rel < WINDOW) & (wpos[None, :] >= 0)
        s_w = jnp.einsum('bqghd,bkgd->bghqk', qg, k_w) * scale
        p_w = _masked_softmax(s_w, w_mask)
        o_w = jnp.einsum('bghqk,bkgd->bqghd', p_w.astype(v_w.dtype), v_w)
        o = gt[..., 0:1] * o_c + gt[..., 1:2] * o_s + gt[..., 2:3] * o_w
        return o.reshape(B, QB, NSA_WIDTH)

    out = lax.map(block_fn, jnp.arange(S // QB))
    return out.transpose(1, 0, 2, 3).reshape(B, S, NSA_WIDTH)


def _fox_mixer(q, k, v, f_logit):
    B, S = q.shape[0], q.shape[1]
    QB = FOX_Q_BLOCK
    scale = HEAD_DIM ** -0.5
    log_f = jax.nn.log_sigmoid(f_logit.astype(jnp.float32))
    cum = jnp.cumsum(log_f, axis=1).transpose(0, 2, 1)
    kpos = jnp.arange(S)

    def block_fn(c):
        s0 = c * QB
        t = s0 + jnp.arange(QB)
        qb = lax.dynamic_slice_in_dim(q, s0, QB, axis=1)
        cq = lax.dynamic_slice_in_dim(cum, s0, QB, axis=2)
        logits = (jnp.einsum('bqhd,bkhd->bhqk', qb, k).astype(jnp.float32) * scale
                  + (cq[..., None] - cum[:, :, None, :]))
        p = _masked_softmax(logits, kpos[None, :] <= t[:, None])
        o = jnp.einsum('bhqk,bkhd->bqhd', p.astype(v.dtype), v)
        return o.reshape(B, QB, FOX_WIDTH)

    out = lax.map(block_fn, jnp.arange(S // QB))
    return out.transpose(1, 0, 2, 3).reshape(B, S, FOX_WIDTH)


def _hier_moe(u, w_group, b_group, w_router, b_router, w_gate_e, w_up_e, w_down_e):
    B, S = u.shape[0], u.shape[1]
    grp_logits = (u @ w_group + b_group).astype(jnp.float32)
    grp_prob = jax.nn.softmax(grp_logits, axis=-1)
    g_star = jnp.argmax(grp_logits, axis=-1)
    exp_logits = (u @ w_router + b_router).astype(jnp.float32).reshape(B, S, N_GROUPS, EXPERTS_PER_GROUP)
    in_logits = jnp.take_along_axis(exp_logits, g_star[..., None, None], axis=2)[..., 0, :]
    in_prob = jax.nn.softmax(in_logits, axis=-1)
    top_v, top_i = lax.top_k(in_prob, TOP_K_IN_GROUP)
    top_v = top_v / jnp.sum(top_v, axis=-1, keepdims=True)
    w = jnp.take_along_axis(grp_prob, g_star[..., None], axis=-1) * top_v
    expert_id = g_star[..., None] * EXPERTS_PER_GROUP + top_i
    combine = jnp.einsum('bsk,bske->bse', w.astype(u.dtype),
                         jax.nn.one_hot(expert_id, N_EXPERTS, dtype=u.dtype))
    y = jnp.zeros_like(u)
    for e in range(N_EXPERTS):
        hid = jax.nn.silu(u @ w_gate_e[e]) * (u @ w_up_e[e])
        y = y + combine[..., e:e + 1] * (hid @ w_down_e[e])
    return y


def setup_inputs(seed: int = 0) -> dict:
    key = jax.random.key(seed)
    ks = jax.random.split(key, 32)
    f32 = jnp.float32

    def nrm(k, shape, scale):
        return jax.random.normal(k, shape, f32) * scale

    def gain(k, shape):
        return 1.0 + 0.05 * jax.random.normal(k, shape, f32)

    L = DEPTH
    return {
        'x': nrm(ks[0], (BATCH, SEQ, D_MODEL), 1.0),
        'p': nrm(ks[1], (DEPTH, BATCH, SEQ, PLE_DIM), 1.0),
        'positions': (jnp.arange(SEQ, dtype=jnp.int32)[None, :]
                      + jax.random.randint(ks[2], (BATCH, 1), 0, SEQ, dtype=jnp.int32)),
        'g_mix': gain(ks[3], (L, D_MODEL)),
        'w_in': nrm(ks[4], (L, D_MODEL, IN_COLS), D_MODEL ** -0.5),
        'b_nsa_gate': nrm(ks[5], (L, 3 * NSA_HEADS), 0.01),
        'b_forget': 1.0 + 3.0 * jax.random.uniform(ks[6], (L, FOX_HEADS), f32),
        'pe_cmp_k': nrm(ks[7], (L, CMP_BLOCK, HEAD_DIM), 0.02),
        'w_cmp_k1': nrm(ks[8], (L, CMP_BLOCK * HEAD_DIM, CMP_HIDDEN), (CMP_BLOCK * HEAD_DIM) ** -0.5),
        'w_cmp_k2': nrm(ks[9], (L, CMP_HIDDEN, HEAD_DIM), CMP_HIDDEN ** -0.5),
        'pe_cmp_v': nrm(ks[10], (L, CMP_BLOCK, HEAD_DIM), 0.02),
        'w_cmp_v1': nrm(ks[11], (L, CMP_BLOCK * HEAD_DIM, CMP_HIDDEN), (CMP_BLOCK * HEAD_DIM) ** -0.5),
        'w_cmp_v2': nrm(ks[12], (L, CMP_HIDDEN, HEAD_DIM), CMP_HIDDEN ** -0.5),
        'beta_nsa': gain(ks[13], (L, NSA_WIDTH)),
        'beta_fox': gain(ks[14], (L, FOX_WIDTH)),
        'w_out': nrm(ks[15], (L, MIX_WIDTH, D_MODEL), MIX_WIDTH ** -0.5),
        'g_ffn': gain(ks[16], (L, D_MODEL)),
        'w_group': nrm(ks[17], (L, D_MODEL, N_GROUPS), D_MODEL ** -0.5),
        'b_group': nrm(ks[18], (L, N_GROUPS), 0.01),
        'w_router': nrm(ks[19], (L, D_MODEL, N_EXPERTS), D_MODEL ** -0.5),
        'b_router': nrm(ks[20], (L, N_EXPERTS), 0.01),
        'w_gate_e': nrm(ks[21], (L, N_EXPERTS, D_MODEL, D_EXPERT), D_MODEL ** -0.5),
        'w_up_e': nrm(ks[22], (L, N_EXPERTS, D_MODEL, D_EXPERT), D_MODEL ** -0.5),
        'w_down_e': nrm(ks[23], (L, N_EXPERTS, D_EXPERT, D_MODEL), D_EXPERT ** -0.5),
        'g_ple': gain(ks[24], (L, D_MODEL)),
        'w_ple_gate': nrm(ks[25], (L, D_MODEL, D_MODEL), D_MODEL ** -0.5),
        'b_ple_gate': nrm(ks[26], (L, D_MODEL), 0.01),
        'w_ple_proj': nrm(ks[27], (L, PLE_DIM, D_MODEL), PLE_DIM ** -0.5),
        'g_final': gain(ks[28], (D_MODEL,)),
    }


def reference(x, p, positions, g_mix, w_in, b_nsa_gate, b_forget,
              pe_cmp_k, w_cmp_k1, w_cmp_k2, pe_cmp_v, w_cmp_v1, w_cmp_v2,
              beta_nsa, beta_fox, w_out, g_ffn, w_group, b_group, w_router, b_router,
              w_gate_e, w_up_e, w_down_e, g_ple, w_ple_gate, b_ple_gate, w_ple_proj, g_final):
    B, S = x.shape[0], x.shape[1]
    h = x
    for i in range(DEPTH):
        hn = _rmsnorm(h, g_mix[i])
        proj = hn @ w_in[i]
        (q_a, kc_a, vc_a, ks_a, vs_a, kw_a, vw_a, gate_a,
         q_b, k_b, v_b, f_b) = jnp.split(proj, IN_OFFSETS, axis=-1)
        kv = lambda t: t.reshape(B, S, NSA_KV_HEADS, HEAD_DIM)
        q_nsa = _partial_rope(q_a.reshape(B, S, NSA_HEADS, HEAD_DIM), positions)
        gates = jax.nn.sigmoid(gate_a + b_nsa_gate[i]).reshape(B, S, NSA_KV_HEADS, NSA_HPG, 3)
        o_nsa = _nsa_mixer(q_nsa, kv(kc_a), kv(vc_a),
                           _partial_rope(kv(ks_a), positions), kv(vs_a),
                           _partial_rope(kv(kw_a), positions), kv(vw_a),
                           gates, positions,
                           pe_cmp_k[i], w_cmp_k1[i], w_cmp_k2[i],
                           pe_cmp_v[i], w_cmp_v1[i], w_cmp_v2[i])
        fh = lambda t: t.reshape(B, S, FOX_HEADS, HEAD_DIM)
        o_fox = _fox_mixer(fh(q_b), fh(k_b), fh(v_b), f_b + b_forget[i])
        mixed = jnp.concatenate([_rmsnorm(o_nsa, beta_nsa[i]), _rmsnorm(o_fox, beta_fox[i])], axis=-1)
        h = h + mixed @ w_out[i]
        u = _rmsnorm(h, g_ffn[i])
        h = h + _hier_moe(u, w_group[i], b_group[i], w_router[i], b_router[i],
                          w_gate_e[i], w_up_e[i], w_down_e[i])
        v_ple = _rmsnorm(h, g_ple[i])
        gate = jax.nn.sigmoid(v_ple @ w_ple_gate[i] + b_ple_gate[i])
        h = h + gate * (p[i] @ w_ple_proj[i])
    return _rmsnorm(h, g_final)
```

```python
import functools

import numpy as np
import jax
import jax.numpy as jnp
from jax import lax
from jax.experimental import pallas as pl
from jax.experimental.pallas import tpu as pltpu

D_MODEL = 1024
HEAD_DIM = 64
NSA_HEADS = 8
FOX_HEADS = 8
NSA_GROUPS = 2
NSA_HPG = 4
NSA_WIDTH = 512
FOX_WIDTH = 512
KV_WIDTH = 128
CMP_BLOCK = 32
CMP_STRIDE = 16
CMP_HIDDEN = 128
SEL_BLOCK = 64
N_SELECT = 16
WINDOW = 512
ROPE_THETA = 500000.0
ROPE_DIM = 16
ROPE_HALF = 8
N_GROUPS = 4
EXPERTS_PER_GROUP = 4
N_EXPERTS = 16
D_EXPERT = 512
PLE_DIM = 256
EPS = 1e-6
NEG_INF = -1e30
FORCE_BONUS = 1e4
ATTN_SCALE = 0.125

LANES = 128
N_GATE = 3 * NSA_HEADS
SMALL_F_OFF = N_GATE
ROUTE_E_OFF = N_GROUPS

VMEM_LIMIT = 56 * 1024 * 1024

F32 = jnp.float32
BF16 = jnp.bfloat16


def _cparams(sem):
    return pltpu.CompilerParams(dimension_semantics=sem, vmem_limit_bytes=VMEM_LIMIT)


def _dot(a, b):
    return jnp.dot(a, b, preferred_element_type=F32)


def _dot_nt(a, b):
    return lax.dot_general(a, b, (((1,), (1,)), ((), ())), preferred_element_type=F32)


def _split3_dot(x, e):
    hi = x.astype(BF16)
    r1 = x - hi.astype(F32)
    mid = r1.astype(BF16)
    lo = (r1 - mid.astype(F32)).astype(BF16)
    return _dot(hi, e) + _dot(mid, e) + _dot(lo, e)


def _rms(x, g):
    return x * lax.rsqrt(jnp.mean(x * x, axis=-1, keepdims=True) + EPS) * g


def _rope(r, cos, sin_lo, sin_hi):
    return (r * cos + pltpu.roll(r, LANES - ROPE_HALF, 1) * sin_lo
            + pltpu.roll(r, ROPE_HALF, 1) * sin_hi)


def _rope_tables(pos_col, freq, mlo, mhi):
    ang = pos_col.astype(F32) * freq
    cos = jnp.cos(ang)
    sin = jnp.sin(ang)
    return cos, sin * mlo, sin * mhi


def _inproj_kernel(x_ref, pos_ref, g_ref, w_ref, bias_ref, freq_ref, mlo_ref, mhi_ref,
                   qa_ref, cmp_ref, kv_ref, fox_ref, small_ref):
    x = x_ref[...]
    hb = _rms(x, g_ref[...]).astype(BF16)
    cos, s_lo, s_hi = _rope_tables(pos_ref[...], freq_ref[...], mlo_ref[...], mhi_ref[...])
    off = 0
    r = _dot(hb, w_ref[:, off:off + NSA_WIDTH])
    for c in range(NSA_WIDTH // LANES):
        rc = _rope(r[:, c * LANES:(c + 1) * LANES], cos, s_lo, s_hi) * ATTN_SCALE
        qa_ref[:, c * LANES:(c + 1) * LANES] = rc.astype(qa_ref.dtype)
    off += NSA_WIDTH
    cmp_ref[...] = _dot(hb, w_ref[:, off:off + 2 * KV_WIDTH])
    off += 2 * KV_WIDTH
    r = _dot(hb, w_ref[:, off:off + 4 * KV_WIDTH])
    for c in range(4):
        rc = r[:, c * LANES:(c + 1) * LANES]
        if c % 2 == 0:
            rc = _rope(rc, cos, s_lo, s_hi)
        kv_ref[:, c * LANES:(c + 1) * LANES] = rc.astype(kv_ref.dtype)
    off += 4 * KV_WIDTH
    r = _dot(hb, w_ref[:, off:off + FOX_WIDTH])
    fox_ref[:, 0:FOX_WIDTH] = (r * ATTN_SCALE).astype(fox_ref.dtype)
    off += FOX_WIDTH
    r = _dot(hb, w_ref[:, off:off + 2 * FOX_WIDTH])
    fox_ref[:, FOX_WIDTH:3 * FOX_WIDTH] = r.astype(fox_ref.dtype)
    off += 2 * FOX_WIDTH
    z = _dot(hb, w_ref[:, off:off + LANES]) + bias_ref[...]
    lane = lax.broadcasted_iota(jnp.int32, z.shape, 1)
    sig = 1.0 / (1.0 + jnp.exp(-z))
    logsig = jnp.minimum(z, 0.0) - jnp.log(1.0 + jnp.exp(-jnp.abs(z)))
    small_ref[...] = jnp.where(lane < SMALL_F_OFF, sig, logsig)


def _inproj(x2, pos2, g_mix, w_all, bias_small, freq, mlo, mhi, tm):
    T = x2.shape[0]
    ncol = w_all.shape[1]
    full = lambda shape: pl.BlockSpec(shape, lambda i: (0,) * len(shape))
    row = lambda w: pl.BlockSpec((tm, w), lambda i: (i, 0))
    return pl.pallas_call(
        _inproj_kernel,
        grid=(T // tm,),
        in_specs=[row(D_MODEL), row(1), full((1, D_MODEL)), full((D_MODEL, ncol)),
                  full((1, LANES)), full((1, LANES)), full((1, LANES)), full((1, LANES))],
        out_specs=[row(NSA_WIDTH), row(2 * KV_WIDTH), row(4 * KV_WIDTH), row(3 * FOX_WIDTH),
                   row(LANES)],
        out_shape=[jax.ShapeDtypeStruct((T, NSA_WIDTH), BF16),
                   jax.ShapeDtypeStruct((T, 2 * KV_WIDTH), F32),
                   jax.ShapeDtypeStruct((T, 4 * KV_WIDTH), BF16),
                   jax.ShapeDtypeStruct((T, 3 * FOX_WIDTH), BF16),
                   jax.ShapeDtypeStruct((T, LANES), F32)],
        compiler_params=_cparams(("parallel",)),
        name="inproj",
    )(x2, pos2, g_mix, w_all, bias_small, freq, mlo, mhi)


CUM_CHUNK = 256


def _cumsum_kernel(x_ref, tri_ref, o_ref, carry_ref):
    @pl.when(pl.program_id(1) == 0)
    def _():
        carry_ref[...] = jnp.zeros_like(carry_ref)

    c = _split3_dot_lhs(tri_ref[...], x_ref[...]) + carry_ref[...]
    o_ref[...] = c
    carry_ref[...] = c[CUM_CHUNK - 1:CUM_CHUNK, :]


def _split3_dot_lhs(e, x):
    hi = x.astype(BF16)
    r1 = x - hi.astype(F32)
    mid = r1.astype(BF16)
    lo = (r1 - mid.astype(F32)).astype(BF16)
    return _dot(e, hi) + _dot(e, mid) + _dot(e, lo)


def _cumsum(small3, tri):
    B, S, _ = small3.shape
    return pl.pallas_call(
        _cumsum_kernel,
        grid=(B, S // CUM_CHUNK),
        in_specs=[pl.BlockSpec((None, CUM_CHUNK, LANES), lambda b, i: (b, i, 0)),
                  pl.BlockSpec((CUM_CHUNK, CUM_CHUNK), lambda b, i: (0, 0))],
        out_specs=pl.BlockSpec((None, CUM_CHUNK, LANES), lambda b, i: (b, i, 0)),
        out_shape=jax.ShapeDtypeStruct((B, S, LANES), F32),
        scratch_shapes=[pltpu.VMEM((1, LANES), F32)],
        compiler_params=_cparams(("parallel", "arbitrary")),
        name="forget_cumsum",
    )(small3, tri)


def _compress_kernel(tokk_ref, tokv_ref, pek_ref, pev_ref, bdk1_ref, bdv1_ref, bdk2_ref, bdv2_ref,
                     pos_ref, freq_ref, mlo_ref, mhi_ref, kc_ref, vc_ref, *, n_rows):
    half = CMP_BLOCK // 2
    ak = jnp.zeros((n_rows, 2 * CMP_HIDDEN), F32)
    bk = jnp.zeros((n_rows, 2 * CMP_HIDDEN), F32)
    av = jnp.zeros((n_rows, 2 * CMP_HIDDEN), F32)
    bv = jnp.zeros((n_rows, 2 * CMP_HIDDEN), F32)
    for l in range(half):
        xk = tokk_ref[pl.ds(l, n_rows, stride=CMP_STRIDE), :]
        xv = tokv_ref[pl.ds(l, n_rows, stride=CMP_STRIDE), :]
        ak = ak + _dot((xk + pek_ref[l:l + 1, :]).astype(BF16), bdk1_ref[l])
        bk = bk + _dot((xk + pek_ref[half + l:half + l + 1, :]).astype(BF16), bdk1_ref[half + l])
        av = av + _dot((xv + pev_ref[l:l + 1, :]).astype(BF16), bdv1_ref[l])
        bv = bv + _dot((xv + pev_ref[half + l:half + l + 1, :]).astype(BF16), bdv1_ref[half + l])
    hk = ak + pltpu.roll(bk, n_rows - 1, 0)
    hv = av + pltpu.roll(bv, n_rows - 1, 0)
    hk = hk * (1.0 / (1.0 + jnp.exp(-hk)))
    hv = hv * (1.0 / (1.0 + jnp.exp(-hv)))
    kc = _dot(hk.astype(BF16), bdk2_ref[...])
    vc = _dot(hv.astype(BF16), bdv2_ref[...])
    cos, s_lo, s_hi = _rope_tables(pos_ref[...], freq_ref[...], mlo_ref[...], mhi_ref[...])
    kc_ref[...] = _rope(kc, cos, s_lo, s_hi).astype(kc_ref.dtype)
    vc_ref[...] = vc.astype(vc_ref.dtype)


def _compress(cmp_tok3, pek, pev, bdk1, bdv1, bdk2, bdv2, pos_cmp, freq, mlo, mhi):
    B, S, _ = cmp_tok3.shape
    n_rows = S // CMP_STRIDE
    full = lambda shape: pl.BlockSpec(shape, lambda b: (0,) * len(shape))
    return pl.pallas_call(
        functools.partial(_compress_kernel, n_rows=n_rows),
        grid=(B,),
        in_specs=[pl.BlockSpec((None, S, KV_WIDTH), lambda b: (b, 0, 0)),
                  pl.BlockSpec((None, S, KV_WIDTH), lambda b: (b, 0, 1)),
                  full((CMP_BLOCK, KV_WIDTH)), full((CMP_BLOCK, KV_WIDTH)),
                  full((CMP_BLOCK, KV_WIDTH, 2 * CMP_HIDDEN)), full((CMP_BLOCK, KV_WIDTH, 2 * CMP_HIDDEN)),
                  full((2 * CMP_HIDDEN, KV_WIDTH)), full((2 * CMP_HIDDEN, KV_WIDTH)),
                  pl.BlockSpec((None, n_rows, 1), lambda b: (b, 0, 0)),
                  full((1, LANES)), full((1, LANES)), full((1, LANES))],
        out_specs=[pl.BlockSpec((None, n_rows, KV_WIDTH), lambda b: (b, 0, 0)),
                   pl.BlockSpec((None, n_rows, KV_WIDTH), lambda b: (b, 0, 0))],
        out_shape=[jax.ShapeDtypeStruct((B, n_rows, KV_WIDTH), BF16),
                   jax.ShapeDtypeStruct((B, n_rows, KV_WIDTH), BF16)],
        compiler_params=_cparams(("parallel",)),
        name="nsa_compress",
    )(cmp_tok3, cmp_tok3, pek, pev, bdk1, bdv1, bdk2, bdv2, pos_cmp, freq, mlo, mhi)


NSA_QB = 64
NSA_KC = 512
NSA_ROWS = NSA_HPG * NSA_GROUPS * NSA_QB
NSA_GQ = NSA_GROUPS * NSA_QB
NSA_WSPAN = 640


def _softmax_rows(l):
    m = jnp.max(l, axis=-1, keepdims=True)
    e = jnp.exp(l - m)
    return e / jnp.sum(e, axis=-1, keepdims=True)


def _nsa_kernel(q_ref, gate_ref, kc_ref, vc_ref, ks_ref, vs_ref, kw_ref, vw_ref, agg_ref, exp_ref,
                o_ref, *, n_cmp, n_blk, n_sel):
    c = pl.program_id(1)
    s0 = c * NSA_QB
    lane = lax.broadcasted_iota(jnp.int32, (NSA_QB, LANES), 1)
    low = lane < HEAD_DIM
    qb = q_ref[...]
    zero = jnp.zeros((NSA_QB, LANES), qb.dtype)
    pieces = []
    for h in range(NSA_HPG):
        qh = qb[:, h * LANES:(h + 1) * LANES]
        pieces.append(jnp.where(low, qh, zero))
        pieces.append(jnp.where(low, zero, qh))
    qp = jnp.concatenate(pieces, axis=0)
    t_gq = s0 + (lax.broadcasted_iota(jnp.int32, (NSA_GQ, 1), 0) & (NSA_QB - 1))

    n_pad = kc_ref.shape[0]
    s_c = _dot_nt(qp, kc_ref[...])
    n_idx = lax.broadcasted_iota(jnp.int32, (1, n_pad), 1)
    cmp_end = n_idx * CMP_STRIDE + (CMP_BLOCK - 1)
    mask_c = (cmp_end <= t_gq) & (n_idx < n_cmp)
    mask_c4 = jnp.broadcast_to(mask_c[None], (NSA_HPG, NSA_GQ, n_pad))
    l_c = jnp.where(mask_c4, s_c.reshape(NSA_HPG, NSA_GQ, n_pad), NEG_INF)
    p_c = jnp.where(mask_c4, _softmax_rows(l_c), 0.0)
    o_c = _dot(p_c.reshape(NSA_ROWS, n_pad).astype(BF16), vc_ref[...])

    imp = _split3_dot(jnp.sum(p_c, axis=0), agg_ref[...])
    j_idx = lax.broadcasted_iota(jnp.int32, (NSA_GQ, n_blk), 1)
    forced = (j_idx == 0) | (j_idx == c) | (j_idx == c - 1)
    score = jnp.where(j_idx > c, -1.0, imp + jnp.where(forced, FORCE_BONUS, 0.0))
    rank = jnp.zeros((NSA_GQ, n_blk), F32)
    for i in range(n_blk):
        col = score[:, i:i + 1]
        beats = (col > score) | ((col == score) & (j_idx > i))
        rank = rank + jnp.where(beats, 1.0, 0.0)
    sel = jnp.where(rank < n_sel, 1.0, 0.0).astype(BF16)

    def sel_body(i, carry):
        m, l, acc = carry
        k0 = pl.multiple_of(i * NSA_KC, NSA_KC)
        s = _dot_nt(qp, ks_ref[pl.ds(k0, NSA_KC), :])
        msel = _dot(sel, exp_ref[:, pl.ds(k0, NSA_KC)])
        kpos = k0 + lax.broadcasted_iota(jnp.int32, (1, NSA_KC), 1)
        mask = (msel > 0.5) & (kpos <= t_gq)
        mask4 = jnp.broadcast_to(mask[None], (NSA_HPG, NSA_GQ, NSA_KC))
        lg = jnp.where(mask4, s.reshape(NSA_HPG, NSA_GQ, NSA_KC), NEG_INF).reshape(NSA_ROWS, NSA_KC)
        m_new = jnp.maximum(m, jnp.max(lg, axis=-1, keepdims=True))
        alpha = jnp.exp(m - m_new)
        p = jnp.exp(lg - m_new)
        l_new = alpha * l + jnp.sum(p, axis=-1, keepdims=True)
        acc_new = alpha * acc + _dot(p.astype(BF16), vs_ref[pl.ds(k0, NSA_KC), :])
        return m_new, l_new, acc_new

    n_chunks = (s0 + NSA_QB + NSA_KC - 1) // NSA_KC
    m0 = jnp.full((NSA_ROWS, 1), NEG_INF, F32)
    l0 = jnp.zeros((NSA_ROWS, 1), F32)
    a0 = jnp.zeros((NSA_ROWS, LANES), F32)
    _, l_s, acc_s = lax.fori_loop(0, n_chunks, sel_body, (m0, l0, a0))
    o_s = acc_s / l_s

    wlen = NSA_WSPAN
    start = pl.multiple_of(jnp.maximum(s0 + NSA_QB - wlen, 0), NSA_QB)
    s_w = _dot_nt(qp, kw_ref[pl.ds(start, wlen), :])
    wpos = start + lax.broadcasted_iota(jnp.int32, (1, wlen), 1)
    rel = t_gq - wpos
    mask_w = (rel >= 0) & (rel < WINDOW)
    mask_w4 = jnp.broadcast_to(mask_w[None], (NSA_HPG, NSA_GQ, wlen))
    l_w = jnp.where(mask_w4, s_w.reshape(NSA_HPG, NSA_GQ, wlen), NEG_INF)
    p_w = _softmax_rows(l_w).reshape(NSA_ROWS, wlen)
    o_w = _dot(p_w.astype(BF16), vw_ref[pl.ds(start, wlen), :])

    gates = gate_ref[...]
    for h in range(NSA_HPG):
        blk = []
        for g in range(NSA_GROUPS):
            r0 = (h * NSA_GROUPS + g) * NSA_QB
            gc = (g * NSA_HPG + h) * 3
            blk.append(gates[:, gc:gc + 1] * o_c[r0:r0 + NSA_QB]
                       + gates[:, gc + 1:gc + 2] * o_s[r0:r0 + NSA_QB]
                       + gates[:, gc + 2:gc + 3] * o_w[r0:r0 + NSA_QB])
        o_ref[:, h * LANES:(h + 1) * LANES] = jnp.where(low, blk[0], blk[1])


def _nsa_attention(qa3, small3, kc, vc, kv3, agg, expand):
    B, S, _ = qa3.shape
    n_pad = kc.shape[1]
    n_cmp = (S - CMP_BLOCK) // CMP_STRIDE + 1
    n_blk = S // SEL_BLOCK
    n_sel = min(N_SELECT, n_blk)
    kvspec = lambda col: pl.BlockSpec((None, S, KV_WIDTH), lambda b, c: (b, 0, col))
    cspec = pl.BlockSpec((None, n_pad, KV_WIDTH), lambda b, c: (b, 0, 0))
    return pl.pallas_call(
        functools.partial(_nsa_kernel, n_cmp=n_cmp, n_blk=n_blk, n_sel=n_sel),
        grid=(B, S // NSA_QB),
        in_specs=[pl.BlockSpec((None, NSA_QB, NSA_WIDTH), lambda b, c: (b, c, 0)),
                  pl.BlockSpec((None, NSA_QB, LANES), lambda b, c: (b, c, 0)),
                  cspec, cspec, kvspec(0), kvspec(1), kvspec(2), kvspec(3),
                  pl.BlockSpec((n_pad, n_blk), lambda b, c: (0, 0)),
                  pl.BlockSpec((n_blk, S), lambda b, c: (0, 0))],
        out_specs=pl.BlockSpec((None, NSA_QB, NSA_WIDTH), lambda b, c: (b, c, 0)),
        out_shape=jax.ShapeDtypeStruct((B, S, NSA_WIDTH), F32),
        compiler_params=_cparams(("parallel", "arbitrary")),
        name="nsa_attention",
    )(qa3, small3, kc, vc, kv3, kv3, kv3, kv3, agg, expand)


FOX_TQ = 256
FOX_KC = 512
FOX_PAIR = 2


def _fox_kernel(q_ref, k_ref, v_ref, cq_ref, ck_ref, o_ref):
    qi = pl.program_id(2)
    s0 = qi * FOX_TQ
    rows = FOX_PAIR * FOX_TQ
    lane = lax.broadcasted_iota(jnp.int32, (FOX_TQ, LANES), 1)
    low = lane < HEAD_DIM
    q2 = q_ref[...]
    zero = jnp.zeros_like(q2)
    qp = jnp.concatenate([jnp.where(low, q2, zero), jnp.where(low, zero, q2)], axis=0)
    cqv = cq_ref[...]
    cq = jnp.stack([cqv[:, 0:1], cqv[:, 1:2]], axis=0)
    t_q = s0 + lax.broadcasted_iota(jnp.int32, (FOX_TQ, 1), 0)

    def body(i, carry):
        m, l, acc = carry
        k0 = pl.multiple_of(i * FOX_KC, FOX_KC)
        s = _dot_nt(qp, k_ref[pl.ds(k0, FOX_KC), :])
        ck = ck_ref[:, pl.ds(k0, FOX_KC)]
        kpos = k0 + lax.broadcasted_iota(jnp.int32, (1, FOX_KC), 1)
        mask = jnp.broadcast_to((kpos <= t_q)[None], (FOX_PAIR, FOX_TQ, FOX_KC))
        lg = s.reshape(FOX_PAIR, FOX_TQ, FOX_KC) + (cq - ck[:, None, :])
        lg = jnp.where(mask, lg, NEG_INF).reshape(rows, FOX_KC)
        m_new = jnp.maximum(m, jnp.max(lg, axis=-1, keepdims=True))
        alpha = jnp.exp(m - m_new)
        p = jnp.exp(lg - m_new)
        l_new = alpha * l + jnp.sum(p, axis=-1, keepdims=True)
        acc_new = alpha * acc + _dot(p.astype(BF16), v_ref[pl.ds(k0, FOX_KC), :])
        return m_new, l_new, acc_new

    n_chunks = (s0 + FOX_TQ + FOX_KC - 1) // FOX_KC
    m0 = jnp.full((rows, 1), NEG_INF, F32)
    l0 = jnp.zeros((rows, 1), F32)
    a0 = jnp.zeros((rows, LANES), F32)
    _, l_f, acc = lax.fori_loop(0, n_chunks, body, (m0, l0, a0))
    o = acc / l_f
    o_ref[...] = jnp.where(low, o[0:FOX_TQ], o[FOX_TQ:rows])


def _fox_attention(fox3, cum_c, cum_t):
    B, S, _ = fox3.shape
    n_pair = FOX_HEADS // FOX_PAIR
    return pl.pallas_call(
        _fox_kernel,
        grid=(B, n_pair, S // FOX_TQ),
        in_specs=[pl.BlockSpec((None, FOX_TQ, LANES), lambda b, h, i: (b, i, h)),
                  pl.BlockSpec((None, S, LANES), lambda b, h, i: (b, 0, n_pair + h)),
                  pl.BlockSpec((None, S, LANES), lambda b, h, i: (b, 0, 2 * n_pair + h)),
                  pl.BlockSpec((None, None, FOX_TQ, FOX_PAIR), lambda b, h, i: (b, h, i, 0)),
                  pl.BlockSpec((None, None, FOX_PAIR, S), lambda b, h, i: (b, h, 0, 0))],
        out_specs=pl.BlockSpec((None, FOX_TQ, LANES), lambda b, h, i: (b, i, h)),
        out_shape=jax.ShapeDtypeStruct((B, S, FOX_WIDTH), F32),
        compiler_params=_cparams(("parallel", "parallel", "arbitrary")),
        name="fox_attention",
    )(fox3, fox3, fox3, cum_c, cum_t)


def _lane_max(x, mask):
    return jnp.max(jnp.where(mask, x, NEG_INF), axis=-1, keepdims=True)


def _lane_sum(x, mask):
    return jnp.sum(jnp.where(mask, x, 0.0), axis=-1, keepdims=True)


def _first_lane_eq(x, v, mask, lane):
    return jnp.min(jnp.where(mask & (x == v), lane, LANES), axis=-1, keepdims=True)


def _route(logits):
    lane = lax.broadcasted_iota(jnp.int32, logits.shape, 1)
    gmask = lane < N_GROUPS
    gmax = _lane_max(logits, gmask)
    gexp = jnp.where(gmask, jnp.exp(logits - gmax), 0.0)
    g_star = _first_lane_eq(logits, gmax, gmask, lane)
    p_grp = 1.0 / jnp.sum(gexp, axis=-1, keepdims=True)
    e_lo = ROUTE_E_OFF + g_star * EXPERTS_PER_GROUP
    emask = (lane >= e_lo) & (lane < e_lo + EXPERTS_PER_GROUP)
    emax = _lane_max(logits, emask)
    eexp = jnp.where(emask, jnp.exp(logits - emax), 0.0)
    prob = eexp / jnp.sum(eexp, axis=-1, keepdims=True)
    v1 = _lane_max(prob, emask)
    i1 = _first_lane_eq(prob, v1, emask, lane)
    rest = emask & (lane != i1)
    v2 = _lane_max(prob, rest)
    i2 = _first_lane_eq(prob, v2, rest, lane)
    den = v1 + v2
    w1 = p_grp * (v1 / den)
    w2 = p_grp * (v2 / den)
    return jnp.where(lane == i1, w1, 0.0) + jnp.where(lane == i2, w2, 0.0)


def _outproj_kernel(on_ref, of_ref, x_ref, bn_ref, bf_ref, wn_ref, wf_ref, gffn_ref, wr_ref, br_ref,
                    h_ref, u_ref, comb_ref):
    mn = _rms(on_ref[...], bn_ref[...]).astype(BF16)
    mf = _rms(of_ref[...], bf_ref[...]).astype(BF16)
    h = x_ref[...] + (_dot(mn, wn_ref[...]) + _dot(mf, wf_ref[...]))
    h_ref[...] = h
    u = _rms(h, gffn_ref[...]).astype(BF16)
    u_ref[...] = u
    comb_ref[...] = _route(_dot(u, wr_ref[...]) + br_ref[...])


def _outproj(o_nsa, o_fox, x2, beta_n, beta_f, w_n, w_f, g_ffn, w_r, b_r, tm):
    T = x2.shape[0]
    full = lambda shape: pl.BlockSpec(shape, lambda i: (0,) * len(shape))
    row = lambda w: pl.BlockSpec((tm, w), lambda i: (i, 0))
    return pl.pallas_call(
        _outproj_kernel,
        grid=(T // tm,),
        in_specs=[row(NSA_WIDTH), row(FOX_WIDTH), row(D_MODEL), full((1, NSA_WIDTH)), full((1, FOX_WIDTH)),
                  full((NSA_WIDTH, D_MODEL)), full((FOX_WIDTH, D_MODEL)), full((1, D_MODEL)),
                  full((D_MODEL, LANES)), full((1, LANES))],
        out_specs=[row(D_MODEL), row(D_MODEL), row(LANES)],
        out_shape=[jax.ShapeDtypeStruct((T, D_MODEL), F32),
                   jax.ShapeDtypeStruct((T, D_MODEL), BF16),
                   jax.ShapeDtypeStruct((T, LANES), F32)],
        compiler_params=_cparams(("parallel",)),
        name="outproj_router",
    )(o_nsa, o_fox, x2, beta_n, beta_f, w_n, w_f, g_ffn, w_r, b_r)


def _moe_kernel(u_ref, comb_ref, h_ref, wgu_ref, wd_ref, o_ref):
    e = pl.program_id(1)

    @pl.when(e == 0)
    def _():
        o_ref[...] = h_ref[...]

    comb = comb_ref[...]
    lane = lax.broadcasted_iota(jnp.int32, comb.shape, 1)
    ce = jnp.sum(jnp.where(lane == ROUTE_E_OFF + e, comb, 0.0), axis=-1, keepdims=True)
    gu = _dot(u_ref[...], wgu_ref[...])
    gt = gu[:, 0:D_EXPERT]
    hid = gt * (1.0 / (1.0 + jnp.exp(-gt))) * gu[:, D_EXPERT:2 * D_EXPERT]
    o_ref[...] += _dot((ce * hid).astype(BF16), wd_ref[...])


def _moe(u, comb, h1, wgu, wd, tm):
    T = u.shape[0]
    return pl.pallas_call(
        _moe_kernel,
        grid=(T // tm, N_EXPERTS),
        in_specs=[pl.BlockSpec((tm, D_MODEL), lambda i, e: (i, 0)),
                  pl.BlockSpec((tm, LANES), lambda i, e: (i, 0)),
                  pl.BlockSpec((tm, D_MODEL), lambda i, e: (i, 0)),
                  pl.BlockSpec((None, D_MODEL, 2 * D_EXPERT), lambda i, e: (e, 0, 0)),
                  pl.BlockSpec((None, D_EXPERT, D_MODEL), lambda i, e: (e, 0, 0))],
        out_specs=pl.BlockSpec((tm, D_MODEL), lambda i, e: (i, 0)),
        out_shape=jax.ShapeDtypeStruct((T, D_MODEL), F32),
        compiler_params=_cparams(("parallel", "arbitrary")),
        name="moe_experts",
    )(u, comb, h1, wgu, wd)


def _ple_kernel(h_ref, p_ref, gple_ref, wg_ref, bg_ref, wp_ref, gfin_ref, o_ref):
    h = h_ref[...]
    v = _rms(h, gple_ref[...]).astype(BF16)
    z = _dot(v, wg_ref[...]) + bg_ref[...]
    gate = 1.0 / (1.0 + jnp.exp(-z))
    proj = _dot(p_ref[...].astype(BF16), wp_ref[...])
    o_ref[...] = _rms(h + gate * proj, gfin_ref[...])


def _ple(h2, p2, g_ple, w_g, b_g, w_p, g_final, tm):
    T = h2.shape[0]
    full = lambda shape: pl.BlockSpec(shape, lambda i: (0,) * len(shape))
    row = lambda w: pl.BlockSpec((tm, w), lambda i: (i, 0))
    return pl.pallas_call(
        _ple_kernel,
        grid=(T // tm,),
        in_specs=[row(D_MODEL), row(PLE_DIM), full((1, D_MODEL)), full((D_MODEL, D_MODEL)),
                  full((1, D_MODEL)), full((PLE_DIM, D_MODEL)), full((1, D_MODEL))],
        out_specs=row(D_MODEL),
        out_shape=jax.ShapeDtypeStruct((T, D_MODEL), F32),
        compiler_params=_cparams(("parallel",)),
        name="ple_final",
    )(h2, p2, g_ple, w_g, b_g, w_p, g_final)


def _rope_lane_tables():
    half = ROPE_HALF
    inv_freq = jnp.power(jnp.float32(ROPE_THETA), -jnp.arange(half, dtype=jnp.float32) / half)
    j = np.arange(LANES) % HEAD_DIM
    freq = jnp.where(jnp.asarray(j < ROPE_DIM), inv_freq[jnp.asarray(j % half)], 0.0).reshape(1, LANES)
    mlo = jnp.asarray(np.where(j < half, -1.0, 0.0), F32).reshape(1, LANES)
    mhi = jnp.asarray(np.where((j >= half) & (j < ROPE_DIM), 1.0, 0.0), F32).reshape(1, LANES)
    return freq.astype(F32), mlo, mhi


def _block_diag2(w):
    z = jnp.zeros_like(w)
    return jnp.concatenate([jnp.concatenate([w, z], axis=-1), jnp.concatenate([z, w], axis=-1)], axis=-2)


def _layer(h3, p3, positions, prm, g_final):
    B, S, _ = h3.shape
    T = B * S
    tm = 512
    x2 = h3.reshape(T, D_MODEL)
    freq, mlo, mhi = _rope_lane_tables()

    w_in = prm['w_in']
    offs = np.cumsum([0, NSA_WIDTH, KV_WIDTH, KV_WIDTH, KV_WIDTH, KV_WIDTH, KV_WIDTH, KV_WIDTH,
                      N_GATE, FOX_WIDTH, FOX_WIDTH, FOX_WIDTH, FOX_HEADS])
    seg = lambda k: w_in[:, offs[k]:offs[k + 1]]
    wq = seg(0).reshape(D_MODEL, NSA_GROUPS, NSA_HPG, HEAD_DIM).transpose(0, 2, 1, 3).reshape(D_MODEL, NSA_WIDTH)
    pad = jnp.zeros((D_MODEL, LANES - N_GATE - FOX_HEADS), w_in.dtype)
    w_all = jnp.concatenate([wq, seg(1), seg(2), seg(3), seg(4), seg(5), seg(6),
                             seg(8), seg(9), seg(10), seg(7), seg(11), pad], axis=1).astype(BF16)
    bias_small = jnp.concatenate([prm['b_nsa_gate'], prm['b_forget'],
                                  jnp.zeros((LANES - N_GATE - FOX_HEADS,), F32)]).reshape(1, LANES)

    qa, cmp_tok, kv, fox, small = _inproj(
        x2, positions.reshape(T, 1), prm['g_mix'].reshape(1, D_MODEL), w_all, bias_small, freq, mlo, mhi, tm)

    tri = jnp.asarray(np.tril(np.ones((CUM_CHUNK, CUM_CHUNK), np.float32)), BF16)
    small3 = small.reshape(B, S, LANES)
    cum = _cumsum(small3, tri)[:, :, SMALL_F_OFF:SMALL_F_OFF + FOX_HEADS]
    cum_c = cum.reshape(B, S, FOX_HEADS // FOX_PAIR, FOX_PAIR).transpose(0, 2, 1, 3)
    cum_t = cum_c.transpose(0, 1, 3, 2)

    n_rows = S // CMP_STRIDE
    tile2 = lambda pe: jnp.concatenate([pe, pe], axis=-1)
    bd1 = lambda w: _block_diag2(w.reshape(CMP_BLOCK, HEAD_DIM, CMP_HIDDEN)).astype(BF16)
    bd2 = lambda w: _block_diag2(w).astype(BF16)
    pos_cmp = positions[:, CMP_BLOCK - 1::CMP_STRIDE]
    pos_cmp = jnp.pad(pos_cmp, ((0, 0), (0, n_rows - pos_cmp.shape[1]))).reshape(B, n_rows, 1)
    kc, vc = _compress(cmp_tok.reshape(B, S, 2 * KV_WIDTH), tile2(prm['pe_cmp_k']), tile2(prm['pe_cmp_v']),
                       bd1(prm['w_cmp_k1']), bd1(prm['w_cmp_v1']), bd2(prm['w_cmp_k2']), bd2(prm['w_cmp_v2']),
                       pos_cmp, freq, mlo, mhi)

    n_cmp = (S - CMP_BLOCK) // CMP_STRIDE + 1
    n_blk = S // SEL_BLOCK
    cs = np.arange(n_rows)[:, None] * CMP_STRIDE
    ss = np.arange(n_blk)[None, :] * SEL_BLOCK
    ov = np.clip(np.minimum(cs + CMP_BLOCK, ss + SEL_BLOCK) - np.maximum(cs, ss), 0, None) / CMP_BLOCK
    ov[n_cmp:] = 0.0
    agg = jnp.asarray(ov, BF16)
    expand = jnp.asarray((np.arange(S)[None, :] // SEL_BLOCK) == np.arange(n_blk)[:, None], BF16)
    o_nsa = _nsa_attention(qa.reshape(B, S, NSA_WIDTH), small3, kc, vc, kv.reshape(B, S, 4 * KV_WIDTH),
                           agg, expand)

    o_fox = _fox_attention(fox.reshape(B, S, 3 * FOX_WIDTH), cum_c, cum_t)

    perm = lambda a: a.reshape(NSA_GROUPS, NSA_HPG, HEAD_DIM, -1).transpose(1, 0, 2, 3).reshape(NSA_WIDTH, -1)
    beta_n = perm(prm['beta_nsa'].reshape(NSA_WIDTH, 1)).reshape(1, NSA_WIDTH)
    w_out = prm['w_out']
    w_n = perm(w_out[:NSA_WIDTH]).astype(BF16)
    w_f = w_out[NSA_WIDTH:].astype(BF16)
    w_r = jnp.concatenate([prm['w_group'], prm['w_router'],
                           jnp.zeros((D_MODEL, LANES - N_GROUPS - N_EXPERTS), F32)], axis=1).astype(BF16)
    b_r = jnp.concatenate([prm['b_group'], prm['b_router'],
                           jnp.zeros((LANES - N_GROUPS - N_EXPERTS,), F32)]).reshape(1, LANES)
    h1, u, comb = _outproj(o_nsa.reshape(T, NSA_WIDTH), o_fox.reshape(T, FOX_WIDTH), x2, beta_n,
                           prm['beta_fox'].reshape(1, FOX_WIDTH), w_n, w_f,
                           prm['g_ffn'].reshape(1, D_MODEL), w_r, b_r, tm)

    wgu = jnp.concatenate([prm['w_gate_e'], prm['w_up_e']], axis=-1).astype(BF16)
    h2 = _moe(u, comb, h1, wgu, prm['w_down_e'].astype(BF16), tm)

    out = _ple(h2, p3.reshape(T, PLE_DIM), prm['g_ple'].reshape(1, D_MODEL), prm['w_ple_gate'].astype(BF16),
               prm['b_ple_gate'].reshape(1, D_MODEL), prm['w_ple_proj'].astype(BF16),
               g_final.reshape(1, D_MODEL), tm)
    return out.reshape(B, S, D_MODEL)


_PARAM_NAMES = ('g_mix', 'w_in', 'b_nsa_gate', 'b_forget', 'pe_cmp_k', 'w_cmp_k1', 'w_cmp_k2',
                'pe_cmp_v', 'w_cmp_v1', 'w_cmp_v2', 'beta_nsa', 'beta_fox', 'w_out', 'g_ffn',
                'w_group', 'b_group', 'w_router', 'b_router', 'w_gate_e', 'w_up_e', 'w_down_e',
                'g_ple', 'w_ple_gate', 'b_ple_gate', 'w_ple_proj')


def kernel(x, p, positions, g_mix, w_in, b_nsa_gate, b_forget, pe_cmp_k, w_cmp_k1, w_cmp_k2, pe_cmp_v,
           w_cmp_v1, w_cmp_v2, beta_nsa, beta_fox, w_out, g_ffn, w_group, b_group, w_router, b_router,
           w_gate_e, w_up_e, w_down_e, g_ple, w_ple_gate, b_ple_gate, w_ple_proj, g_final):
    stacked = (g_mix, w_in, b_nsa_gate, b_forget, pe_cmp_k, w_cmp_k1, w_cmp_k2, pe_cmp_v, w_cmp_v1,
               w_cmp_v2, beta_nsa, beta_fox, w_out, g_ffn, w_group, b_group, w_router, b_router,
               w_gate_e, w_up_e, w_down_e, g_ple, w_ple_gate, b_ple_gate, w_ple_proj)
    depth = w_in.shape[0]
    assert depth == 1, "the final norm is fused into the last layer's embedding kernel"
    prm = {n: a[0] for n, a in zip(_PARAM_NAMES, stacked)}
    return _layer(x, p[0], positions, prm, g_final)
```

```python
import functools
import math

import numpy as np
import jax
import jax.numpy as jnp
from jax import lax
from jax.experimental import pallas as pl
from jax.experimental.pallas import tpu as pltpu

D_MODEL = 1024
HEAD_DIM = 64
NSA_HEADS = 8
FOX_HEADS = 8
NSA_GROUPS = 2
NSA_HPG = 4
NSA_WIDTH = 512
FOX_WIDTH = 512
KV_WIDTH = 128
CMP_BLOCK = 32
CMP_STRIDE = 16
CMP_HIDDEN = 128
SEL_BLOCK = 64
N_SELECT = 16
WINDOW = 512
ROPE_THETA = 500000.0
ROPE_DIM = 16
ROPE_HALF = 8
N_GROUPS = 4
EXPERTS_PER_GROUP = 4
N_EXPERTS = 16
D_EXPERT = 512
PLE_DIM = 256
EPS = 1e-6
NEG_INF = -1e30
FORCE_BONUS = 1e4
LOG2E = math.log2(math.e)
Q_SCALE = 0.125 * LOG2E
NEG_BIG = -(2.0 ** 100)

LANES = 128
SUBLANES = 8
N_GATE = 3 * NSA_HEADS
SMALL_F_OFF = N_GATE
ROUTE_E_OFF = N_GROUPS

VMEM_LIMIT = 56 * 1024 * 1024

F32 = jnp.float32
BF16 = jnp.bfloat16


def _cparams(sem):
    return pltpu.CompilerParams(dimension_semantics=sem, vmem_limit_bytes=VMEM_LIMIT)


def _dot(a, b):
    return jnp.dot(a, b, preferred_element_type=F32)


def _dot_nt(a, b):
    return lax.dot_general(a, b, (((1,), (1,)), ((), ())), preferred_element_type=F32)


def _split3(x):
    hi = x.astype(BF16)
    r1 = x - hi.astype(F32)
    mid = r1.astype(BF16)
    lo = (r1 - mid.astype(F32)).astype(BF16)
    return hi, mid, lo


def _split3_dot_lhs(e, x):
    hi, mid, lo = _split3(x)
    return _dot(e, hi) + _dot(e, mid) + _dot(e, lo)


def _rms(x, g):
    return x * lax.rsqrt(jnp.mean(x * x, axis=-1, keepdims=True) + EPS) * g


def _rope(r, cos, sin_lo, sin_hi):
    return (r * cos + pltpu.roll(r, LANES - ROPE_HALF, 1) * sin_lo
            + pltpu.roll(r, ROPE_HALF, 1) * sin_hi)


def _rope_tables(pos_col, freq, mlo, mhi):
    ang = pos_col.astype(F32) * freq
    cos = jnp.cos(ang)
    sin = jnp.sin(ang)
    return cos, sin * mlo, sin * mhi


def _inproj_kernel(x_ref, pos_ref, g_ref, w_ref, wt_ref, bias_ref, freq_ref, mlo_ref, mhi_ref,
                   qa_ref, cmp_ref, knsa_ref, vnsa_ref, fqk_ref, vfox_ref, small_ref):
    x = x_ref[...]
    hb = _rms(x, g_ref[...]).astype(BF16)
    cos, s_lo, s_hi = _rope_tables(pos_ref[...], freq_ref[...], mlo_ref[...], mhi_ref[...])
    off = 0
    r = _dot(hb, w_ref[:, off:off + NSA_WIDTH])
    for c in range(NSA_WIDTH // LANES):
        rc = _rope(r[:, c * LANES:(c + 1) * LANES], cos, s_lo, s_hi) * Q_SCALE
        qa_ref[:, c * LANES:(c + 1) * LANES] = rc.astype(qa_ref.dtype)
    off += NSA_WIDTH
    cmp_ref[...] = _dot(hb, w_ref[:, off:off + 2 * KV_WIDTH])
    off += 2 * KV_WIDTH
    r = _dot(hb, w_ref[:, off:off + 2 * KV_WIDTH])
    for c in range(2):
        rc = _rope(r[:, c * LANES:(c + 1) * LANES], cos, s_lo, s_hi)
        knsa_ref[:, c * LANES:(c + 1) * LANES] = rc.astype(knsa_ref.dtype)
    off += 2 * KV_WIDTH
    r = _dot(hb, w_ref[:, off:off + FOX_WIDTH])
    fqk_ref[:, 0:FOX_WIDTH] = (r * Q_SCALE).astype(fqk_ref.dtype)
    off += FOX_WIDTH
    r = _dot(hb, w_ref[:, off:off + FOX_WIDTH])
    fqk_ref[:, FOX_WIDTH:2 * FOX_WIDTH] = r.astype(fqk_ref.dtype)
    off += FOX_WIDTH
    z = _dot(hb, w_ref[:, off:off + LANES]) + bias_ref[...]
    lane = lax.broadcasted_iota(jnp.int32, z.shape, 1)
    sig = 1.0 / (1.0 + jnp.exp(-z))
    logsig = jnp.minimum(z, 0.0) - jnp.log(1.0 + jnp.exp(-jnp.abs(z)))
    small_ref[...] = jnp.where(lane < SMALL_F_OFF, sig, logsig)
    vt = _dot_nt(wt_ref[...], hb)
    vnsa_ref[...] = vt[0:2 * KV_WIDTH].astype(vnsa_ref.dtype)
    vfox_ref[...] = vt[2 * KV_WIDTH:2 * KV_WIDTH + FOX_WIDTH].astype(vfox_ref.dtype)


def _inproj(x2, pos2, g_mix, w_all, wt_v, bias_small, freq, mlo, mhi, tm, B, S):
    T = x2.shape[0]
    nt = S // tm
    full = lambda shape: pl.BlockSpec(shape, lambda i: (0,) * len(shape))
    row = lambda w: pl.BlockSpec((tm, w), lambda i: (i, 0))
    tr = lambda w: pl.BlockSpec((None, w, tm), lambda i: (i // nt, 0, i % nt))
    return pl.pallas_call(
        _inproj_kernel,
        grid=(T // tm,),
        in_specs=[row(D_MODEL), row(1), full((1, D_MODEL)), full(w_all.shape), full(wt_v.shape),
                  full((1, LANES)), full((1, LANES)), full((1, LANES)), full((1, LANES))],
        out_specs=[row(NSA_WIDTH), row(2 * KV_WIDTH), row(2 * KV_WIDTH), tr(2 * KV_WIDTH),
                   row(2 * FOX_WIDTH), tr(FOX_WIDTH), row(LANES)],
        out_shape=[jax.ShapeDtypeStruct((T, NSA_WIDTH), BF16),
                   jax.ShapeDtypeStruct((T, 2 * KV_WIDTH), F32),
                   jax.ShapeDtypeStruct((T, 2 * KV_WIDTH), BF16),
                   jax.ShapeDtypeStruct((B, 2 * KV_WIDTH, S), BF16),
                   jax.ShapeDtypeStruct((T, 2 * FOX_WIDTH), BF16),
                   jax.ShapeDtypeStruct((B, FOX_WIDTH, S), BF16),
                   jax.ShapeDtypeStruct((T, LANES), F32)],
        compiler_params=_cparams(("parallel",)),
        name="inproj",
    )(x2, pos2, g_mix, w_all, wt_v, bias_small, freq, mlo, mhi)


CUM_CHUNK = 256
FOX_PAIR = 2
FOX_NPAIR = FOX_HEADS // FOX_PAIR
AUX_PIECES = 3


def _cumsum_kernel(x_ref, tri_ref, route_ref, o_ref, carry_ref):
    @pl.when(pl.program_id(1) == 0)
    def _():
        carry_ref[...] = jnp.zeros_like(carry_ref)

    c = _split3_dot_lhs(tri_ref[...], x_ref[...]) + carry_ref[...]
    carry_ref[...] = c[CUM_CHUNK - 1:CUM_CHUNK, :]
    hi, mid, lo = _split3(c * LOG2E)
    aux = _dot(hi, route_ref[0]) + _dot(mid, route_ref[1]) + _dot(lo, route_ref[2])
    o_ref[...] = aux.astype(o_ref.dtype)


def _cumsum(small3, tri, route):
    B, S, _ = small3.shape
    width = FOX_NPAIR * LANES
    return pl.pallas_call(
        _cumsum_kernel,
        grid=(B, S // CUM_CHUNK),
        in_specs=[pl.BlockSpec((None, CUM_CHUNK, LANES), lambda b, i: (b, i, 0)),
                  pl.BlockSpec((CUM_CHUNK, CUM_CHUNK), lambda b, i: (0, 0)),
                  pl.BlockSpec((AUX_PIECES, LANES, width), lambda b, i: (0, 0, 0))],
        out_specs=pl.BlockSpec((None, CUM_CHUNK, width), lambda b, i: (b, i, 0)),
        out_shape=jax.ShapeDtypeStruct((B, S, width), BF16),
        scratch_shapes=[pltpu.VMEM((1, LANES), F32)],
        compiler_params=_cparams(("parallel", "arbitrary")),
        name="forget_cumsum",
    )(small3, tri, route)


def _compress_kernel(tokk_ref, tokv_ref, pek_ref, pev_ref, bdk1_ref, bdv1_ref, bdk2_ref, bdv2t_ref,
                     pos_ref, freq_ref, mlo_ref, mhi_ref, kc_ref, vct_ref, *, n_rows):
    half = CMP_BLOCK // 2
    ak = jnp.zeros((n_rows, 2 * CMP_HIDDEN), F32)
    bk = jnp.zeros((n_rows, 2 * CMP_HIDDEN), F32)
    av = jnp.zeros((n_rows, 2 * CMP_HIDDEN), F32)
    bv = jnp.zeros((n_rows, 2 * CMP_HIDDEN), F32)
    for l in range(half):
        xk = tokk_ref[pl.ds(l, n_rows, stride=CMP_STRIDE), :]
        xv = tokv_ref[pl.ds(l, n_rows, stride=CMP_STRIDE), :]
        ak = ak + _dot((xk + pek_ref[l:l + 1, :]).astype(BF16), bdk1_ref[l])
        bk = bk + _dot((xk + pek_ref[half + l:half + l + 1, :]).astype(BF16), bdk1_ref[half + l])
        av = av + _dot((xv + pev_ref[l:l + 1, :]).astype(BF16), bdv1_ref[l])
        bv = bv + _dot((xv + pev_ref[half + l:half + l + 1, :]).astype(BF16), bdv1_ref[half + l])
    hk = ak + pltpu.roll(bk, n_rows - 1, 0)
    hv = av + pltpu.roll(bv, n_rows - 1, 0)
    hk = hk * (1.0 / (1.0 + jnp.exp(-hk)))
    hv = hv * (1.0 / (1.0 + jnp.exp(-hv)))
    kc = _dot(hk.astype(BF16), bdk2_ref[...])
    cos, s_lo, s_hi = _rope_tables(pos_ref[...], freq_ref[...], mlo_ref[...], mhi_ref[...])
    kc_ref[...] = _rope(kc, cos, s_lo, s_hi).astype(kc_ref.dtype)
    vct_ref[...] = _dot_nt(bdv2t_ref[...], hv.astype(BF16)).astype(vct_ref.dtype)


def _compress(cmp_tok3, pek, pev, bdk1, bdv1, bdk2, bdv2t, pos_cmp, freq, mlo, mhi):
    B, S, _ = cmp_tok3.shape
    n_rows = S // CMP_STRIDE
    full = lambda shape: pl.BlockSpec(shape, lambda b: (0,) * len(shape))
    return pl.pallas_call(
        functools.partial(_compress_kernel, n_rows=n_rows),
        grid=(B,),
        in_specs=[pl.BlockSpec((None, S, KV_WIDTH), lambda b: (b, 0, 0)),
                  pl.BlockSpec((None, S, KV_WIDTH), lambda b: (b, 0, 1)),
                  full((CMP_BLOCK, KV_WIDTH)), full((CMP_BLOCK, KV_WIDTH)),
                  full((CMP_BLOCK, KV_WIDTH, 2 * CMP_HIDDEN)), full((CMP_BLOCK, KV_WIDTH, 2 * CMP_HIDDEN)),
                  full((2 * CMP_HIDDEN, KV_WIDTH)), full((KV_WIDTH, 2 * CMP_HIDDEN)),
                  pl.BlockSpec((None, n_rows, 1), lambda b: (b, 0, 0)),
                  full((1, LANES)), full((1, LANES)), full((1, LANES))],
        out_specs=[pl.BlockSpec((None, n_rows, KV_WIDTH), lambda b: (b, 0, 0)),
                   pl.BlockSpec((None, KV_WIDTH, n_rows), lambda b: (b, 0, 0))],
        out_shape=[jax.ShapeDtypeStruct((B, n_rows, KV_WIDTH), BF16),
                   jax.ShapeDtypeStruct((B, KV_WIDTH, n_rows), BF16)],
        compiler_params=_cparams(("parallel",)),
        name="nsa_compress",
    )(cmp_tok3, cmp_tok3, pek, pev, bdk1, bdv1, bdk2, bdv2t, pos_cmp, freq, mlo, mhi)


NSA_QB = 64
NSA_KC = 512
NSA_GQ = NSA_GROUPS * NSA_QB
NSA_ROWS = NSA_HPG * NSA_GQ
NSA_WSPAN = 640


def _where_heads(mask, x, other):
    return jnp.concatenate([jnp.where(mask, x[:, h * LANES:(h + 1) * LANES], other)
                            for h in range(NSA_HPG)], axis=1)


def _masked_softmax_cols(s, mask):
    l = _where_heads(mask, s, NEG_INF)
    m = jnp.max(l, axis=0, keepdims=True)
    e = jnp.exp2(l - m)
    return e, jnp.sum(e, axis=0, keepdims=True)


def _nsa_kernel(q_ref, gate_ref, kc_ref, vct_ref, ks_ref, kw_ref, vst_ref, vwt_ref, aggt_ref, et_ref,
                eye_ref, o_ref, *, n_cmp, n_blk, n_sel):
    c = pl.program_id(1)
    s0 = c * NSA_QB
    lane = lax.broadcasted_iota(jnp.int32, (NSA_QB, LANES), 1)
    low = lane < HEAD_DIM
    qb = q_ref[...]
    zero = jnp.zeros((NSA_QB, LANES), qb.dtype)
    pieces = []
    for h in range(NSA_HPG):
        qh = qb[:, h * LANES:(h + 1) * LANES]
        pieces.append(jnp.where(low, qh, zero))
        pieces.append(jnp.where(low, zero, qh))
    qp = jnp.concatenate(pieces, axis=0)
    t_l = s0 + (lax.broadcasted_iota(jnp.int32, (1, NSA_GQ), 1) & (NSA_QB - 1))

    n_pad = kc_ref.shape[0]
    s_c = _dot_nt(kc_ref[...], qp)
    n_s = lax.broadcasted_iota(jnp.int32, (n_pad, 1), 0)
    mask_c = ((n_s * CMP_STRIDE + (CMP_BLOCK - 1)) <= t_l) & (n_s < n_cmp)
    e_c, l_c = _masked_softmax_cols(s_c, mask_c)
    e_c = _where_heads(mask_c, e_c, 0.0)
    acc_c = _dot(vct_ref[...], e_c.astype(BF16))
    inv_c = 1.0 / l_c

    j_s = lax.broadcasted_iota(jnp.int32, (n_blk, NSA_GQ), 0)

    def ranked_mask():
        p_c = e_c * inv_c
        psum = p_c[:, 0:LANES]
        for h in range(1, NSA_HPG):
            psum = psum + p_c[:, h * LANES:(h + 1) * LANES]
        imp = _split3_dot_lhs(aggt_ref[...], psum)
        forced = (j_s == 0) | (j_s == c) | (j_s == c - 1)
        score = jnp.where(j_s > c, -1.0, imp + jnp.where(forced, FORCE_BONUS, 0.0))
        nv = n_blk // SUBLANES
        sc = [score[SUBLANES * v:SUBLANES * (v + 1), :] for v in range(nv)]
        rk = [jnp.zeros((SUBLANES, NSA_GQ), F32) for _ in range(nv)]
        sub = lax.broadcasted_iota(jnp.int32, (SUBLANES, NSA_GQ), 0)
        for i in range(n_blk):
            vi, ri = divmod(i, SUBLANES)
            row = sc[vi][ri:ri + 1, :]
            for v in range(nv):
                if v < vi:
                    beats = jnp.where(row > sc[v], 1.0, 0.0)
                elif v > vi:
                    beats = jnp.where(row >= sc[v], 1.0, 0.0)
                else:
                    beats = jnp.where(sub > ri, jnp.where(row >= sc[v], 1.0, 0.0),
                                      jnp.where(row > sc[v], 1.0, 0.0))
                rk[v] = rk[v] + beats
        rank = jnp.concatenate(rk, axis=0)
        return jnp.where((rank < n_sel) & (j_s <= c), 0.0, NEG_BIG)

    def all_mask():
        return jnp.where(j_s <= c, 0.0, NEG_BIG)

    neg_t = lax.cond(c < n_sel, all_mask, ranked_mask)
    neg_t = jnp.concatenate([neg_t, jnp.zeros((LANES - n_blk, NSA_GQ), F32)], axis=0).astype(BF16)
    q_aux = _dot_nt(eye_ref[...], neg_t).astype(BF16)
    qx = jnp.concatenate([qp, jnp.concatenate([q_aux] * NSA_HPG, axis=0)], axis=1)

    def sel_chunk(i, carry, diag):
        m, l, acc = carry
        k0 = pl.multiple_of(i * NSA_KC, NSA_KC)
        kx = jnp.concatenate([ks_ref[pl.ds(k0, NSA_KC), :], et_ref[pl.ds(k0, NSA_KC), :]], axis=1)
        s = _dot_nt(kx, qx)
        if diag:
            kpos = k0 + lax.broadcasted_iota(jnp.int32, (NSA_KC, 1), 0)
            s = _where_heads(kpos <= t_l, s, NEG_INF)
        m_new = jnp.maximum(m, jnp.max(s, axis=0, keepdims=True))
        alpha = jnp.exp2(m - m_new)
        p = jnp.exp2(s - m_new)
        l_new = alpha * l + jnp.sum(p, axis=0, keepdims=True)
        acc_new = alpha * acc + _dot(vst_ref[:, pl.ds(k0, NSA_KC)], p.astype(BF16))
        return m_new, l_new, acc_new

    n_full = s0 // NSA_KC
    carry = (jnp.full((1, NSA_ROWS), NEG_INF, F32), jnp.zeros((1, NSA_ROWS), F32),
             jnp.zeros((LANES, NSA_ROWS), F32))
    carry = lax.fori_loop(0, n_full, functools.partial(sel_chunk, diag=False), carry)
    _, l_s, acc_s = sel_chunk(n_full, carry, True)

    start = pl.multiple_of((jnp.maximum(s0 - WINDOW, 0) // LANES) * LANES, LANES)
    s_w = _dot_nt(kw_ref[pl.ds(start, NSA_WSPAN), :], qp)
    rel = t_l - (start + lax.broadcasted_iota(jnp.int32, (NSA_WSPAN, 1), 0))
    e_w, l_w = _masked_softmax_cols(s_w, (rel >= 0) & (rel < WINDOW))
    acc_w = _dot(vwt_ref[:, pl.ds(start, NSA_WSPAN)], e_w.astype(BF16))

    gates = gate_ref[...]
    o_t = ((gates[0:1, :] * inv_c) * acc_c + (gates[1:2, :] / l_s) * acc_s
           + (gates[2:3, :] / l_w) * acc_w)
    for h in range(NSA_HPG):
        t = o_t[:, h * LANES:(h + 1) * LANES].T
        o_ref[:, h * LANES:(h + 1) * LANES] = jnp.where(low, t[0:NSA_QB], t[NSA_QB:NSA_GQ])


def _nsa_attention(qa3, gexp, kc, vct, knsa3, vnsa_t, aggt, et, eye):
    B, S, _ = qa3.shape
    n_pad = kc.shape[1]
    n_cmp = (S - CMP_BLOCK) // CMP_STRIDE + 1
    n_blk = S // SEL_BLOCK
    n_sel = min(N_SELECT, n_blk)
    const = lambda shape: pl.BlockSpec(shape, lambda b, c: (0,) * len(shape))
    return pl.pallas_call(
        functools.partial(_nsa_kernel, n_cmp=n_cmp, n_blk=n_blk, n_sel=n_sel),
        grid=(B, S // NSA_QB),
        in_specs=[pl.BlockSpec((None, NSA_QB, NSA_WIDTH), lambda b, c: (b, c, 0)),
                  pl.BlockSpec((None, None, 3, NSA_ROWS), lambda b, c: (b, c, 0, 0)),
                  pl.BlockSpec((None, n_pad, KV_WIDTH), lambda b, c: (b, 0, 0)),
                  pl.BlockSpec((None, KV_WIDTH, n_pad), lambda b, c: (b, 0, 0)),
                  pl.BlockSpec((None, S, KV_WIDTH), lambda b, c: (b, 0, 0)),
                  pl.BlockSpec((None, S, KV_WIDTH), lambda b, c: (b, 0, 1)),
                  pl.BlockSpec((None, KV_WIDTH, S), lambda b, c: (b, 0, 0)),
                  pl.BlockSpec((None, KV_WIDTH, S), lambda b, c: (b, 1, 0)),
                  const((n_blk, n_pad)), const((S, LANES)), const((LANES, LANES))],
        out_specs=pl.BlockSpec((None, NSA_QB, NSA_WIDTH), lambda b, c: (b, c, 0)),
        out_shape=jax.ShapeDtypeStruct((B, S, NSA_WIDTH), F32),
        compiler_params=_cparams(("parallel", "arbitrary")),
        name="nsa_attention",
    )(qa3, gexp, kc, vct, knsa3, knsa3, vnsa_t, vnsa_t, aggt, et, eye)


FOX_TQ = 256
FOX_KC = 512
FOX_ROWS = FOX_PAIR * FOX_TQ


def _fox_kernel(q_ref, k_ref, aux_ref, vt_ref, o_ref):
    qi = pl.program_id(2)
    s0 = qi * FOX_TQ
    lane = lax.broadcasted_iota(jnp.int32, (FOX_TQ, LANES), 1)
    low = lane < HEAD_DIM
    q2 = q_ref[...]
    zero = jnp.zeros_like(q2)
    aux0 = jnp.where(lane < AUX_PIECES, -1.0, 0.0).astype(q2.dtype)
    aux1 = jnp.where((lane >= AUX_PIECES) & (lane < 2 * AUX_PIECES), -1.0, 0.0).astype(q2.dtype)
    qx = jnp.concatenate([
        jnp.concatenate([jnp.where(low, q2, zero), aux0], axis=1),
        jnp.concatenate([jnp.where(low, zero, q2), aux1], axis=1),
    ], axis=0)
    t_l = s0 + lax.broadcasted_iota(jnp.int32, (1, FOX_TQ), 1)

    def chunk(i, carry, diag):
        m, l, acc = carry
        k0 = pl.multiple_of(i * FOX_KC, FOX_KC)
        kx = jnp.concatenate([k_ref[pl.ds(k0, FOX_KC), :], aux_ref[pl.ds(k0, FOX_KC), :]], axis=1)
        s = _dot_nt(kx, qx)
        if diag:
            kpos = k0 + lax.broadcasted_iota(jnp.int32, (FOX_KC, 1), 0)
            causal = kpos <= t_l
            s = jnp.concatenate([jnp.where(causal, s[:, h * FOX_TQ:(h + 1) * FOX_TQ], NEG_INF)
                                 for h in range(FOX_PAIR)], axis=1)
        m_new = jnp.maximum(m, jnp.max(s, axis=0, keepdims=True))
        alpha = jnp.exp2(m - m_new)
        p = jnp.exp2(s - m_new)
        l_new = alpha * l + jnp.sum(p, axis=0, keepdims=True)
        acc_new = alpha * acc + _dot(vt_ref[:, pl.ds(k0, FOX_KC)], p.astype(BF16))
        return m_new, l_new, acc_new

    n_full = s0 // FOX_KC
    carry = (jnp.full((1, FOX_ROWS), NEG_INF, F32), jnp.zeros((1, FOX_ROWS), F32),
             jnp.zeros((LANES, FOX_ROWS), F32))
    carry = lax.fori_loop(0, n_full, functools.partial(chunk, diag=False), carry)
    _, l_f, acc = chunk(n_full, carry, True)
    o_t = acc / l_f
    t0 = o_t[:, 0:FOX_TQ].T
    t1 = o_t[:, FOX_TQ:FOX_ROWS].T
    o_ref[...] = jnp.where(low, t0, t1)


def _fox_attention(fqk3, aux3, vfox_t):
    B, S, _ = fqk3.shape
    return pl.pallas_call(
        _fox_kernel,
        grid=(B, FOX_NPAIR, S // FOX_TQ),
        in_specs=[pl.BlockSpec((None, FOX_TQ, LANES), lambda b, h, i: (b, i, h)),
                  pl.BlockSpec((None, S, LANES), lambda b, h, i: (b, 0, FOX_NPAIR + h)),
                  pl.BlockSpec((None, S, LANES), lambda b, h, i: (b, 0, h)),
                  pl.BlockSpec((None, LANES, S), lambda b, h, i: (b, h, 0))],
        out_specs=pl.BlockSpec((None, FOX_TQ, LANES), lambda b, h, i: (b, i, h)),
        out_shape=jax.ShapeDtypeStruct((B, S, FOX_WIDTH), F32),
        compiler_params=_cparams(("parallel", "parallel", "arbitrary")),
        name="fox_attention",
    )(fqk3, fqk3, aux3, vfox_t)


def _lane_max(x, mask):
    return jnp.max(jnp.where(mask, x, NEG_INF), axis=-1, keepdims=True)


def _first_lane_eq(x, v, mask, lane):
    return jnp.min(jnp.where(mask & (x == v), lane, LANES), axis=-1, keepdims=True)


def _route(logits):
    lane = lax.broadcasted_iota(jnp.int32, logits.shape, 1)
    gmask = lane < N_GROUPS
    gmax = _lane_max(logits, gmask)
    gexp = jnp.where(gmask, jnp.exp(logits - gmax), 0.0)
    g_star = _first_lane_eq(logits, gmax, gmask, lane)
    p_grp = 1.0 / jnp.sum(gexp, axis=-1, keepdims=True)
    e_lo = ROUTE_E_OFF + g_star * EXPERTS_PER_GROUP
    emask = (lane >= e_lo) & (lane < e_lo + EXPERTS_PER_GROUP)
    emax = _lane_max(logits, emask)
    eexp = jnp.where(emask, jnp.exp(logits - emax), 0.0)
    prob = eexp / jnp.sum(eexp, axis=-1, keepdims=True)
    v1 = _lane_max(prob, emask)
    i1 = _first_lane_eq(prob, v1, emask, lane)
    rest = emask & (lane != i1)
    v2 = _lane_max(prob, rest)
    i2 = _first_lane_eq(prob, v2, rest, lane)
    den = v1 + v2
    w1 = p_grp * (v1 / den)
    w2 = p_grp * (v2 / den)
    return jnp.where(lane == i1, w1, 0.0) + jnp.where(lane == i2, w2, 0.0)


def _outproj_kernel(on_ref, of_ref, x_ref, bn_ref, bf_ref, wn_ref, wf_ref, gffn_ref, wr_ref, br_ref,
                    h_ref, u_ref, comb_ref):
    mn = _rms(on_ref[...], bn_ref[...]).astype(BF16)
    mf = _rms(of_ref[...], bf_ref[...]).astype(BF16)
    h = x_ref[...] + (_dot(mn, wn_ref[...]) + _dot(mf, wf_ref[...]))
    h_ref[...] = h
    u = _rms(h, gffn_ref[...]).astype(BF16)
    u_ref[...] = u
    comb_ref[...] = _route(_dot(u, wr_ref[...]) + br_ref[...])


def _outproj(o_nsa, o_fox, x2, beta_n, beta_f, w_n, w_f, g_ffn, w_r, b_r, tm):
    T = x2.shape[0]
    full = lambda shape: pl.BlockSpec(shape, lambda i: (0,) * len(shape))
    row = lambda w: pl.BlockSpec((tm, w), lambda i: (i, 0))
    return pl.pallas_call(
        _outproj_kernel,
        grid=(T // tm,),
        in_specs=[row(NSA_WIDTH), row(FOX_WIDTH), row(D_MODEL), full((1, NSA_WIDTH)), full((1, FOX_WIDTH)),
                  full((NSA_WIDTH, D_MODEL)), full((FOX_WIDTH, D_MODEL)), full((1, D_MODEL)),
                  full((D_MODEL, LANES)), full((1, LANES))],
        out_specs=[row(D_MODEL), row(D_MODEL), row(LANES)],
        out_shape=[jax.ShapeDtypeStruct((T, D_MODEL), F32),
                   jax.ShapeDtypeStruct((T, D_MODEL), BF16),
                   jax.ShapeDtypeStruct((T, LANES), F32)],
        compiler_params=_cparams(("parallel",)),
        name="outproj_router",
    )(o_nsa, o_fox, x2, beta_n, beta_f, w_n, w_f, g_ffn, w_r, b_r)


def _moe_kernel(u_ref, comb_ref, h_ref, wgu_ref, wd_ref, o_ref):
    e = pl.program_id(1)

    @pl.when(e == 0)
    def _():
        o_ref[...] = h_ref[...]

    comb = comb_ref[...]
    lane = lax.broadcasted_iota(jnp.int32, comb.shape, 1)
    ce = jnp.sum(jnp.where(lane == ROUTE_E_OFF + e, comb, 0.0), axis=-1, keepdims=True)
    gu = _dot(u_ref[...], wgu_ref[...])
    gt = gu[:, 0:D_EXPERT]
    hid = gt * (1.0 / (1.0 + jnp.exp(-gt))) * gu[:, D_EXPERT:2 * D_EXPERT]
    o_ref[...] += _dot((ce * hid).astype(BF16), wd_ref[...])


def _moe(u, comb, h1, wgu, wd, tm):
    T = u.shape[0]
    return pl.pallas_call(
        _moe_kernel,
        grid=(T // tm, N_EXPERTS),
        in_specs=[pl.BlockSpec((tm, D_MODEL), lambda i, e: (i, 0)),
                  pl.BlockSpec((tm, LANES), lambda i, e: (i, 0)),
                  pl.BlockSpec((tm, D_MODEL), lambda i, e: (i, 0)),
                  pl.BlockSpec((None, D_MODEL, 2 * D_EXPERT), lambda i, e: (e, 0, 0)),
                  pl.BlockSpec((None, D_EXPERT, D_MODEL), lambda i, e: (e, 0, 0))],
        out_specs=pl.BlockSpec((tm, D_MODEL), lambda i, e: (i, 0)),
        out_shape=jax.ShapeDtypeStruct((T, D_MODEL), F32),
        compiler_params=_cparams(("parallel", "arbitrary")),
        name="moe_experts",
    )(u, comb, h1, wgu, wd)


def _ple_kernel(h_ref, p_ref, gple_ref, wg_ref, bg_ref, wp_ref, gfin_ref, o_ref):
    h = h_ref[...]
    v = _rms(h, gple_ref[...]).astype(BF16)
    z = _dot(v, wg_ref[...]) + bg_ref[...]
    gate = 1.0 / (1.0 + jnp.exp(-z))
    proj = _dot(p_ref[...].astype(BF16), wp_ref[...])
    o_ref[...] = _rms(h + gate * proj, gfin_ref[...])


def _ple(h2, p2, g_ple, w_g, b_g, w_p, g_final, tm):
    T = h2.shape[0]
    full = lambda shape: pl.BlockSpec(shape, lambda i: (0,) * len(shape))
    row = lambda w: pl.BlockSpec((tm, w), lambda i: (i, 0))
    return pl.pallas_call(
        _ple_kernel,
        grid=(T // tm,),
        in_specs=[row(D_MODEL), row(PLE_DIM), full((1, D_MODEL)), full((D_MODEL, D_MODEL)),
                  full((1, D_MODEL)), full((PLE_DIM, D_MODEL)), full((1, D_MODEL))],
        out_specs=row(D_MODEL),
        out_shape=jax.ShapeDtypeStruct((T, D_MODEL), F32),
        compiler_params=_cparams(("parallel",)),
        name="ple_final",
    )(h2, p2, g_ple, w_g, b_g, w_p, g_final)


def _rope_lane_tables():
    half = ROPE_HALF
    inv_freq = jnp.power(jnp.float32(ROPE_THETA), -jnp.arange(half, dtype=jnp.float32) / half)
    j = np.arange(LANES) % HEAD_DIM
    freq = jnp.where(jnp.asarray(j < ROPE_DIM), inv_freq[jnp.asarray(j % half)], 0.0).reshape(1, LANES)
    mlo = jnp.asarray(np.where(j < half, -1.0, 0.0), F32).reshape(1, LANES)
    mhi = jnp.asarray(np.where((j >= half) & (j < ROPE_DIM), 1.0, 0.0), F32).reshape(1, LANES)
    return freq.astype(F32), mlo, mhi


def _block_diag2(w):
    z = jnp.zeros_like(w)
    return jnp.concatenate([jnp.concatenate([w, z], axis=-1), jnp.concatenate([z, w], axis=-1)], axis=-2)


def _aux_route_table():
    r = np.zeros((AUX_PIECES, LANES, FOX_NPAIR * LANES), np.float32)
    for head in range(FOX_HEADS):
        for k in range(AUX_PIECES):
            r[k, SMALL_F_OFF + head, (head // FOX_PAIR) * LANES + (head % FOX_PAIR) * AUX_PIECES + k] = 1.0
    return jnp.asarray(r, BF16)


def _layer(h3, p3, positions, prm, g_final):
    B, S, _ = h3.shape
    T = B * S
    tm = 512
    x2 = h3.reshape(T, D_MODEL)
    freq, mlo, mhi = _rope_lane_tables()

    w_in = prm['w_in']
    offs = np.cumsum([0, NSA_WIDTH, KV_WIDTH, KV_WIDTH, KV_WIDTH, KV_WIDTH, KV_WIDTH, KV_WIDTH,
                      N_GATE, FOX_WIDTH, FOX_WIDTH, FOX_WIDTH, FOX_HEADS])
    seg = lambda k: w_in[:, offs[k]:offs[k + 1]]
    wq = seg(0).reshape(D_MODEL, NSA_GROUPS, NSA_HPG, HEAD_DIM).transpose(0, 2, 1, 3).reshape(D_MODEL, NSA_WIDTH)
    pad = jnp.zeros((D_MODEL, LANES - N_GATE - FOX_HEADS), w_in.dtype)
    w_all = jnp.concatenate([wq, seg(1), seg(2), seg(3), seg(5), seg(8), seg(9), seg(7), seg(11), pad],
                            axis=1).astype(BF16)
    wt_v = jnp.concatenate([seg(4), seg(6), seg(10)], axis=1).T.astype(BF16)
    bias_small = jnp.concatenate([prm['b_nsa_gate'], prm['b_forget'],
                                  jnp.zeros((LANES - N_GATE - FOX_HEADS,), F32)]).reshape(1, LANES)

    qa, cmp_tok, knsa, vnsa_t, fqk, vfox_t, small = _inproj(
        x2, positions.reshape(T, 1), prm['g_mix'].reshape(1, D_MODEL), w_all, wt_v, bias_small,
        freq, mlo, mhi, tm, B, S)

    tri = jnp.asarray(np.tril(np.ones((CUM_CHUNK, CUM_CHUNK), np.float32)), BF16)
    small3 = small.reshape(B, S, LANES)
    aux3 = _cumsum(small3, tri, _aux_route_table())

    n_rows = S // CMP_STRIDE
    tile2 = lambda pe: jnp.concatenate([pe, pe], axis=-1)
    bd1 = lambda w: _block_diag2(w.reshape(CMP_BLOCK, HEAD_DIM, CMP_HIDDEN)).astype(BF16)
    pos_cmp = positions[:, CMP_BLOCK - 1::CMP_STRIDE]
    pos_cmp = jnp.pad(pos_cmp, ((0, 0), (0, n_rows - pos_cmp.shape[1]))).reshape(B, n_rows, 1)
    kc, vct = _compress(cmp_tok.reshape(B, S, 2 * KV_WIDTH), tile2(prm['pe_cmp_k']), tile2(prm['pe_cmp_v']),
                        bd1(prm['w_cmp_k1']), bd1(prm['w_cmp_v1']),
                        _block_diag2(prm['w_cmp_k2']).astype(BF16), _block_diag2(prm['w_cmp_v2']).T.astype(BF16),
                        pos_cmp, freq, mlo, mhi)

    n_cmp = (S - CMP_BLOCK) // CMP_STRIDE + 1
    n_blk = S // SEL_BLOCK
    cs = np.arange(n_rows)[:, None] * CMP_STRIDE
    ss = np.arange(n_blk)[None, :] * SEL_BLOCK
    ov = np.clip(np.minimum(cs + CMP_BLOCK, ss + SEL_BLOCK) - np.maximum(cs, ss), 0, None) / CMP_BLOCK
    ov[n_cmp:] = 0.0
    aggt = jnp.asarray(ov.T, BF16)
    et = jnp.asarray((np.arange(S)[:, None] // SEL_BLOCK) == np.arange(LANES)[None, :], BF16)
    eye = jnp.asarray(np.eye(LANES, dtype=np.float32), BF16)
    gexp = small3[:, :, :N_GATE].reshape(B, S // NSA_QB, NSA_QB, NSA_GROUPS, NSA_HPG, 3)
    gexp = gexp.transpose(0, 1, 5, 4, 3, 2).reshape(B, S // NSA_QB, 3, NSA_ROWS)
    o_nsa = _nsa_attention(qa.reshape(B, S, NSA_WIDTH), gexp, kc, vct, knsa.reshape(B, S, 2 * KV_WIDTH),
                           vnsa_t, aggt, et, eye)

    o_fox = _fox_attention(fqk.reshape(B, S, 2 * FOX_WIDTH), aux3, vfox_t)

    perm = lambda a: a.reshape(NSA_GROUPS, NSA_HPG, HEAD_DIM, -1).transpose(1, 0, 2, 3).reshape(NSA_WIDTH, -1)
    beta_n = perm(prm['beta_nsa'].reshape(NSA_WIDTH, 1)).reshape(1, NSA_WIDTH)
    w_out = prm['w_out']
    w_n = perm(w_out[:NSA_WIDTH]).astype(BF16)
    w_f = w_out[NSA_WIDTH:].astype(BF16)
    w_r = jnp.concatenate([prm['w_group'], prm['w_router'],
                           jnp.zeros((D_MODEL, LANES - N_GROUPS - N_EXPERTS), F32)], axis=1).astype(BF16)
    b_r = jnp.concatenate([prm['b_group'], prm['b_router'],
                           jnp.zeros((LANES - N_GROUPS - N_EXPERTS,), F32)]).reshape(1, LANES)
    h1, u, comb = _outproj(o_nsa.reshape(T, NSA_WIDTH), o_fox.reshape(T, FOX_WIDTH), x2, beta_n,
                           prm['beta_fox'].reshape(1, FOX_WIDTH), w_n, w_f,
                           prm['g_ffn'].reshape(1, D_MODEL), w_r, b_r, tm)

    wgu = jnp.concatenate([prm['w_gate_e'], prm['w_up_e']], axis=-1).astype(BF16)
    h2 = _moe(u, comb, h1, wgu, prm['w_down_e'].astype(BF16), tm)

    out = _ple(h2, p3.reshape(T, PLE_DIM), prm['g_ple'].reshape(1, D_MODEL), prm['w_ple_gate'].astype(BF16),
               prm['b_ple_gate'].reshape(1, D_MODEL), prm['w_ple_proj'].astype(BF16),
               g_final.reshape(1, D_MODEL), tm)
    return out.reshape(B, S, D_MODEL)


_PARAM_NAMES = ('g_mix', 'w_in', 'b_nsa_gate', 'b_forget', 'pe_cmp_k', 'w_cmp_k1', 'w_cmp_k2',
                'pe_cmp_v', 'w_cmp_v1', 'w_cmp_v2', 'beta_nsa', 'beta_fox', 'w_out', 'g_ffn',
                'w_group', 'b_group', 'w_router', 'b_router', 'w_gate_e', 'w_up_e', 'w_down_e',
                'g_ple', 'w_ple_gate', 'b_ple_gate', 'w_ple_proj')


def kernel(x, p, positions, g_mix, w_in, b_nsa_gate, b_forget, pe_cmp_k, w_cmp_k1, w_cmp_k2, pe_cmp_v,
           w_cmp_v1, w_cmp_v2, beta_nsa, beta_fox, w_out, g_ffn, w_group, b_group, w_router, b_router,
           w_gate_e, w_up_e, w_down_e, g_ple, w_ple_gate, b_ple_gate, w_ple_proj, g_final):
    stacked = (g_mix, w_in, b_nsa_gate, b_forget, pe_cmp_k, w_cmp_k1, w_cmp_k2, pe_cmp_v, w_cmp_v1,
               w_cmp_v2, beta_nsa, beta_fox, w_out, g_ffn, w_group, b_group, w_router, b_router,
               w_gate_e, w_up_e, w_down_e, g_ple, w_ple_gate, b_ple_gate, w_ple_proj)
    depth = w_in.shape[0]
    assert depth == 1, "the final norm is fused into the last layer's embedding kernel"
    prm = {n: a[0] for n, a in zip(_PARAM_NAMES, stacked)}
    return _layer(x, p[0], positions, prm, g_final)
```

```python
import functools
import math

import numpy as np
import jax
import jax.numpy as jnp
from jax import lax
from jax.experimental import pallas as pl
from jax.experimental.pallas import tpu as pltpu

D_MODEL = 1024
HEAD_DIM = 64
NSA_HEADS = 8
FOX_HEADS = 8
NSA_GROUPS = 2
NSA_HPG = 4
NSA_WIDTH = 512
FOX_WIDTH = 512
KV_WIDTH = 128
CMP_BLOCK = 32
CMP_STRIDE = 16
CMP_HIDDEN = 128
SEL_BLOCK = 64
N_SELECT = 16
WINDOW = 512
ROPE_THETA = 500000.0
ROPE_DIM = 16
ROPE_HALF = 8
N_GROUPS = 4
EXPERTS_PER_GROUP = 4
N_EXPERTS = 16
D_EXPERT = 512
PLE_DIM = 256
EPS = 1e-6
NEG_INF = -1e30
FORCE_BONUS = 1e4
LOG2E = math.log2(math.e)
Q_SCALE = 0.125 * LOG2E
NEG_BIG = -(2.0 ** 100)

LANES = 128
SUBLANES = 8
N_GATE = 3 * NSA_HEADS
SMALL_F_OFF = N_GATE
ROUTE_E_OFF = N_GROUPS

VMEM_LIMIT = 56 * 1024 * 1024

F32 = jnp.float32
BF16 = jnp.bfloat16


def _cparams(sem):
    return pltpu.CompilerParams(dimension_semantics=sem, vmem_limit_bytes=VMEM_LIMIT)


def _dot(a, b):
    return jnp.dot(a, b, preferred_element_type=F32)


def _dot_nt(a, b):
    return lax.dot_general(a, b, (((1,), (1,)), ((), ())), preferred_element_type=F32)


def _split3(x):
    hi = x.astype(BF16)
    r1 = x - hi.astype(F32)
    mid = r1.astype(BF16)
    lo = (r1 - mid.astype(F32)).astype(BF16)
    return hi, mid, lo


def _split3_dot_lhs(e, x):
    hi, mid, lo = _split3(x)
    return _dot(e, hi) + _dot(e, mid) + _dot(e, lo)


def _rms(x, g):
    return x * lax.rsqrt(jnp.mean(x * x, axis=-1, keepdims=True) + EPS) * g


def _rope(r, cos, sin_lo, sin_hi):
    return (r * cos + pltpu.roll(r, LANES - ROPE_HALF, 1) * sin_lo
            + pltpu.roll(r, ROPE_HALF, 1) * sin_hi)


def _rope_tables(pos_col, freq, mlo, mhi):
    ang = pos_col.astype(F32) * freq
    cos = jnp.cos(ang)
    sin = jnp.sin(ang)
    return cos, sin * mlo, sin * mhi


def _inproj_kernel(x_ref, pos_ref, g_ref, w_ref, wt_ref, bias_ref, freq_ref, mlo_ref, mhi_ref,
                   qa_ref, cmp_ref, knsa_ref, vnsa_ref, fqk_ref, vfox_ref, small_ref):
    x = x_ref[...]
    hb = _rms(x, g_ref[...]).astype(BF16)
    cos, s_lo, s_hi = _rope_tables(pos_ref[...], freq_ref[...], mlo_ref[...], mhi_ref[...])
    off = 0
    r = _dot(hb, w_ref[:, off:off + NSA_WIDTH])
    for c in range(NSA_WIDTH // LANES):
        rc = _rope(r[:, c * LANES:(c + 1) * LANES], cos, s_lo, s_hi) * Q_SCALE
        qa_ref[:, c * LANES:(c + 1) * LANES] = rc.astype(qa_ref.dtype)
    off += NSA_WIDTH
    cmp_ref[...] = _dot(hb, w_ref[:, off:off + 2 * KV_WIDTH])
    off += 2 * KV_WIDTH
    r = _dot(hb, w_ref[:, off:off + 2 * KV_WIDTH])
    for c in range(2):
        rc = _rope(r[:, c * LANES:(c + 1) * LANES], cos, s_lo, s_hi)
        knsa_ref[:, c * LANES:(c + 1) * LANES] = rc.astype(knsa_ref.dtype)
    off += 2 * KV_WIDTH
    r = _dot(hb, w_ref[:, off:off + FOX_WIDTH])
    fqk_ref[:, 0:FOX_WIDTH] = (r * Q_SCALE).astype(fqk_ref.dtype)
    off += FOX_WIDTH
    r = _dot(hb, w_ref[:, off:off + FOX_WIDTH])
    fqk_ref[:, FOX_WIDTH:2 * FOX_WIDTH] = r.astype(fqk_ref.dtype)
    off += FOX_WIDTH
    z = _dot(hb, w_ref[:, off:off + LANES]) + bias_ref[...]
    lane = lax.broadcasted_iota(jnp.int32, z.shape, 1)
    sig = 1.0 / (1.0 + jnp.exp(-z))
    logsig = jnp.minimum(z, 0.0) - jnp.log(1.0 + jnp.exp(-jnp.abs(z)))
    small_ref[...] = jnp.where(lane < SMALL_F_OFF, sig, logsig)
    vt = _dot_nt(wt_ref[...], hb)
    vnsa_ref[...] = vt[0:2 * KV_WIDTH].astype(vnsa_ref.dtype)
    vfox_ref[...] = vt[2 * KV_WIDTH:2 * KV_WIDTH + FOX_WIDTH].astype(vfox_ref.dtype)


def _inproj(x2, pos2, g_mix, w_all, wt_v, bias_small, freq, mlo, mhi, tm, B, S):
    T = x2.shape[0]
    nt = S // tm
    full = lambda shape: pl.BlockSpec(shape, lambda i: (0,) * len(shape))
    row = lambda w: pl.BlockSpec((tm, w), lambda i: (i, 0))
    tr = lambda w: pl.BlockSpec((None, w, tm), lambda i: (i // nt, 0, i % nt))
    return pl.pallas_call(
        _inproj_kernel,
        grid=(T // tm,),
        in_specs=[row(D_MODEL), row(1), full((1, D_MODEL)), full(w_all.shape), full(wt_v.shape),
                  full((1, LANES)), full((1, LANES)), full((1, LANES)), full((1, LANES))],
        out_specs=[row(NSA_WIDTH), row(2 * KV_WIDTH), row(2 * KV_WIDTH), tr(2 * KV_WIDTH),
                   row(2 * FOX_WIDTH), tr(FOX_WIDTH), row(LANES)],
        out_shape=[jax.ShapeDtypeStruct((T, NSA_WIDTH), BF16),
                   jax.ShapeDtypeStruct((T, 2 * KV_WIDTH), F32),
                   jax.ShapeDtypeStruct((T, 2 * KV_WIDTH), BF16),
                   jax.ShapeDtypeStruct((B, 2 * KV_WIDTH, S), BF16),
                   jax.ShapeDtypeStruct((T, 2 * FOX_WIDTH), BF16),
                   jax.ShapeDtypeStruct((B, FOX_WIDTH, S), BF16),
                   jax.ShapeDtypeStruct((T, LANES), F32)],
        compiler_params=_cparams(("parallel",)),
        name="inproj",
    )(x2, pos2, g_mix, w_all, wt_v, bias_small, freq, mlo, mhi)


CUM_CHUNK = 256
FOX_PAIR = 2
FOX_NPAIR = FOX_HEADS // FOX_PAIR
AUX_PIECES = 3


def _cumsum_kernel(x_ref, tri_ref, route_ref, o_ref, carry_ref):
    @pl.when(pl.program_id(1) == 0)
    def _():
        carry_ref[...] = jnp.zeros_like(carry_ref)

    c = _split3_dot_lhs(tri_ref[...], x_ref[...]) + carry_ref[...]
    carry_ref[...] = c[CUM_CHUNK - 1:CUM_CHUNK, :]
    hi, mid, lo = _split3(c * LOG2E)
    aux = _dot(hi, route_ref[0]) + _dot(mid, route_ref[1]) + _dot(lo, route_ref[2])
    o_ref[...] = aux.astype(o_ref.dtype)


def _cumsum(small3, tri, route):
    B, S, _ = small3.shape
    width = FOX_NPAIR * LANES
    return pl.pallas_call(
        _cumsum_kernel,
        grid=(B, S // CUM_CHUNK),
        in_specs=[pl.BlockSpec((None, CUM_CHUNK, LANES), lambda b, i: (b, i, 0)),
                  pl.BlockSpec((CUM_CHUNK, CUM_CHUNK), lambda b, i: (0, 0)),
                  pl.BlockSpec((AUX_PIECES, LANES, width), lambda b, i: (0, 0, 0))],
        out_specs=pl.BlockSpec((None, CUM_CHUNK, width), lambda b, i: (b, i, 0)),
        out_shape=jax.ShapeDtypeStruct((B, S, width), BF16),
        scratch_shapes=[pltpu.VMEM((1, LANES), F32)],
        compiler_params=_cparams(("parallel", "arbitrary")),
        name="forget_cumsum",
    )(small3, tri, route)


def _compress_kernel(tokk_ref, tokv_ref, pek_ref, pev_ref, bdk1_ref, bdv1_ref, bdk2_ref, bdv2t_ref,
                     pos_ref, freq_ref, mlo_ref, mhi_ref, kc_ref, vct_ref, *, n_rows):
    half = CMP_BLOCK // 2
    ak = jnp.zeros((n_rows, 2 * CMP_HIDDEN), F32)
    bk = jnp.zeros((n_rows, 2 * CMP_HIDDEN), F32)
    av = jnp.zeros((n_rows, 2 * CMP_HIDDEN), F32)
    bv = jnp.zeros((n_rows, 2 * CMP_HIDDEN), F32)
    for l in range(half):
        xk = tokk_ref[pl.ds(l, n_rows, stride=CMP_STRIDE), :]
        xv = tokv_ref[pl.ds(l, n_rows, stride=CMP_STRIDE), :]
        ak = ak + _dot((xk + pek_ref[l:l + 1, :]).astype(BF16), bdk1_ref[l])
        bk = bk + _dot((xk + pek_ref[half + l:half + l + 1, :]).astype(BF16), bdk1_ref[half + l])
        av = av + _dot((xv + pev_ref[l:l + 1, :]).astype(BF16), bdv1_ref[l])
        bv = bv + _dot((xv + pev_ref[half + l:half + l + 1, :]).astype(BF16), bdv1_ref[half + l])
    hk = ak + pltpu.roll(bk, n_rows - 1, 0)
    hv = av + pltpu.roll(bv, n_rows - 1, 0)
    hk = hk * (1.0 / (1.0 + jnp.exp(-hk)))
    hv = hv * (1.0 / (1.0 + jnp.exp(-hv)))
    kc = _dot(hk.astype(BF16), bdk2_ref[...])
    cos, s_lo, s_hi = _rope_tables(pos_ref[...], freq_ref[...], mlo_ref[...], mhi_ref[...])
    kc_ref[...] = _rope(kc, cos, s_lo, s_hi).astype(kc_ref.dtype)
    vct_ref[...] = _dot_nt(bdv2t_ref[...], hv.astype(BF16)).astype(vct_ref.dtype)


def _compress(cmp_tok3, pek, pev, bdk1, bdv1, bdk2, bdv2t, pos_cmp, freq, mlo, mhi):
    B, S, _ = cmp_tok3.shape
    n_rows = S // CMP_STRIDE
    full = lambda shape: pl.BlockSpec(shape, lambda b: (0,) * len(shape))
    return pl.pallas_call(
        functools.partial(_compress_kernel, n_rows=n_rows),
        grid=(B,),
        in_specs=[pl.BlockSpec((None, S, KV_WIDTH), lambda b: (b, 0, 0)),
                  pl.BlockSpec((None, S, KV_WIDTH), lambda b: (b, 0, 1)),
                  full((CMP_BLOCK, KV_WIDTH)), full((CMP_BLOCK, KV_WIDTH)),
                  full((CMP_BLOCK, KV_WIDTH, 2 * CMP_HIDDEN)), full((CMP_BLOCK, KV_WIDTH, 2 * CMP_HIDDEN)),
                  full((2 * CMP_HIDDEN, KV_WIDTH)), full((KV_WIDTH, 2 * CMP_HIDDEN)),
                  pl.BlockSpec((None, n_rows, 1), lambda b: (b, 0, 0)),
                  full((1, LANES)), full((1, LANES)), full((1, LANES))],
        out_specs=[pl.BlockSpec((None, n_rows, KV_WIDTH), lambda b: (b, 0, 0)),
                   pl.BlockSpec((None, KV_WIDTH, n_rows), lambda b: (b, 0, 0))],
        out_shape=[jax.ShapeDtypeStruct((B, n_rows, KV_WIDTH), BF16),
                   jax.ShapeDtypeStruct((B, KV_WIDTH, n_rows), BF16)],
        compiler_params=_cparams(("parallel",)),
        name="nsa_compress",
    )(cmp_tok3, cmp_tok3, pek, pev, bdk1, bdv1, bdk2, bdv2t, pos_cmp, freq, mlo, mhi)


NSA_QB = 64
NSA_KC = 512
NSA_GQ = NSA_GROUPS * NSA_QB
NSA_ROWS = NSA_HPG * NSA_GQ
NSA_WSPAN = 640


def _where_tiles(mask, x, other):
    w = mask.shape[1]
    return jnp.concatenate([jnp.where(mask, x[:, t * w:(t + 1) * w], other)
                            for t in range(x.shape[1] // w)], axis=1)


def _masked_softmax_cols(s, mask):
    l = _where_tiles(mask, s, NEG_INF)
    m = jnp.max(l, axis=0, keepdims=True)
    e = jnp.exp2(l - m)
    return e, jnp.sum(e, axis=0, keepdims=True)


def _flash_chunks(n_chunks, score_fn, vt_fn, init_fn, s_a, s_b, acc_ref):
    last = jnp.maximum(n_chunks - 1, 0)

    def produce(i, s_ref):
        s = score_fn(jnp.minimum(i, last))
        s_ref[...] = s
        return jnp.max(s, axis=0, keepdims=True)

    def consume(i, s_ref, mx, m, l):
        m_new = jnp.maximum(m, mx)
        alpha = jnp.exp2(m - m_new)
        p = jnp.exp2(s_ref[...] - m_new)
        l_new = alpha * l + jnp.sum(p, axis=0, keepdims=True)
        acc_ref[...] = alpha * acc_ref[...] + _dot(vt_fn(i), p.astype(BF16))
        return m_new, l_new

    def pair(j, carry):
        mx_a, m, l = carry
        i = 2 * j
        mx_b = produce(i + 1, s_b)
        m, l = consume(i, s_a, mx_a, m, l)
        mx_a = produce(i + 2, s_a)
        m, l = consume(i + 1, s_b, mx_b, m, l)
        return mx_a, m, l

    def tail(_, carry):
        mx_a, m, l = carry
        m, l = consume(last, s_a, mx_a, m, l)
        return mx_a, m, l

    mx_a = produce(0, s_a)
    m, l = init_fn()
    carry = (mx_a, m, l)
    carry = lax.fori_loop(0, n_chunks // 2, pair, carry)
    carry = lax.fori_loop(0, n_chunks & 1, tail, carry)
    return carry[2]


def _nsa_kernel(q_ref, gate_ref, kc_ref, vct_ref, ks_ref, kw_ref, vst_ref, vwt_ref, aggt_ref, et_ref,
                eye_ref, o_ref, sa_scr, sb_scr, acc_scr, *, n_cmp, n_blk, n_sel):
    c = pl.program_id(1)
    s0 = c * NSA_QB
    lane = lax.broadcasted_iota(jnp.int32, (NSA_QB, LANES), 1)
    low = lane < HEAD_DIM
    qb = q_ref[...]
    zero = jnp.zeros((NSA_QB, LANES), qb.dtype)
    pieces = []
    for h in range(NSA_HPG):
        qh = qb[:, h * LANES:(h + 1) * LANES]
        pieces.append(jnp.where(low, qh, zero))
        pieces.append(jnp.where(low, zero, qh))
    qp = jnp.concatenate(pieces, axis=0)
    t_l = s0 + (lax.broadcasted_iota(jnp.int32, (1, NSA_GQ), 1) & (NSA_QB - 1))

    n_pad = kc_ref.shape[0]
    n_s = lax.broadcasted_iota(jnp.int32, (n_pad, 1), 0)
    mask_c = ((n_s * CMP_STRIDE + (CMP_BLOCK - 1)) <= t_l) & (n_s < n_cmp)
    start = pl.multiple_of((jnp.maximum(s0 - WINDOW, 0) // LANES) * LANES, LANES)
    rel = t_l - (start + lax.broadcasted_iota(jnp.int32, (NSA_WSPAN, 1), 0))
    mask_w = (rel >= 0) & (rel < WINDOW)
    s_c = _dot_nt(kc_ref[...], qp)
    s_w = _dot_nt(kw_ref[pl.ds(start, NSA_WSPAN), :], qp)
    e_c, sum_c = _masked_softmax_cols(s_c, mask_c)
    inv_c = _where_tiles(t_l >= CMP_BLOCK - 1, 1.0 / sum_c, 0.0)
    acc_c = _dot(vct_ref[...], e_c.astype(BF16))

    j_s = lax.broadcasted_iota(jnp.int32, (n_blk, NSA_GQ), 0)
    nv = n_blk // SUBLANES

    def ranked_mask():
        p_c = e_c * inv_c
        psum = p_c[:, 0:NSA_GQ]
        for h in range(1, NSA_HPG):
            psum = psum + p_c[:, h * NSA_GQ:(h + 1) * NSA_GQ]
        imp = _split3_dot_lhs(aggt_ref[...], psum)
        forced = (j_s == 0) | (j_s == c) | (j_s == c - 1)
        score = jnp.where(j_s > c, -1.0, imp + jnp.where(forced, FORCE_BONUS, 0.0))
        sc = [score[SUBLANES * v:SUBLANES * (v + 1), :] for v in range(nv)]
        sub = lax.broadcasted_iota(jnp.int32, (SUBLANES, NSA_GQ), 0)

        def count_group(vi, rk):
            rk = list(rk)
            for ri in range(SUBLANES):
                row = sc[vi][ri:ri + 1, :]
                for v in range(nv):
                    if v < vi:
                        beats = jnp.where(row > sc[v], 1.0, 0.0)
                    elif v > vi:
                        beats = jnp.where(row >= sc[v], 1.0, 0.0)
                    else:
                        beats = jnp.where(sub > ri, jnp.where(row >= sc[v], 1.0, 0.0),
                                          jnp.where(row > sc[v], 1.0, 0.0))
                    rk[v] = rk[v] + beats
            return tuple(rk)

        rk = tuple(jnp.zeros((SUBLANES, NSA_GQ), F32) for _ in range(nv))
        for vi in range(nv):
            rk = lax.cond(vi * SUBLANES <= c, functools.partial(count_group, vi), lambda r: r, rk)
        rank = jnp.concatenate(rk, axis=0)
        return jnp.where((rank < n_sel) & (j_s < c), 0.0, NEG_BIG)

    def all_mask():
        return jnp.where(j_s < c, 0.0, NEG_BIG)

    neg_t = lax.cond(c < n_sel, all_mask, ranked_mask)

    a0 = pl.multiple_of((s0 // LANES) * LANES, LANES)
    s_d = _dot_nt(ks_ref[pl.ds(a0, LANES), :], qp)
    neg_t = jnp.concatenate([neg_t, jnp.zeros((LANES - n_blk, NSA_GQ), F32)], axis=0).astype(BF16)
    q_aux = _dot_nt(eye_ref[...], neg_t).astype(BF16)
    qx = jnp.concatenate([qp, jnp.concatenate([q_aux] * NSA_HPG, axis=0)], axis=1)

    def sel_scores(i):
        k0 = pl.multiple_of(i * NSA_KC, NSA_KC)
        kx = jnp.concatenate([ks_ref[pl.ds(k0, NSA_KC), :], et_ref[pl.ds(k0, NSA_KC), :]], axis=1)
        return _dot_nt(kx, qx)

    def sel_values(i):
        return vst_ref[:, pl.ds(pl.multiple_of(i * NSA_KC, NSA_KC), NSA_KC)]

    window = {}

    def after_first_scores():
        kpos = a0 + lax.broadcasted_iota(jnp.int32, (LANES, 1), 0)
        l_d = _where_tiles((kpos >= s0) & (kpos <= t_l), s_d, NEG_INF)
        m_d = jnp.max(l_d, axis=0, keepdims=True)
        p_d = jnp.exp2(l_d - m_d)
        acc_scr[...] = _dot(vst_ref[:, pl.ds(a0, LANES)], p_d.astype(BF16))
        e_w, window['l'] = _masked_softmax_cols(s_w, mask_w)
        window['acc'] = _dot(vwt_ref[:, pl.ds(start, NSA_WSPAN)], e_w.astype(BF16))
        return m_d, jnp.sum(p_d, axis=0, keepdims=True)

    n_prev = (s0 + NSA_KC - 1) // NSA_KC
    l_s = _flash_chunks(n_prev, sel_scores, sel_values, after_first_scores, sa_scr, sb_scr, acc_scr)

    gates = gate_ref[...]
    o_t = ((gates[0:1, :] * inv_c) * acc_c + (gates[1:2, :] / l_s) * acc_scr[...]
           + (gates[2:3, :] / window['l']) * window['acc'])
    for h in range(NSA_HPG):
        tt = o_t[:, h * NSA_GQ:(h + 1) * NSA_GQ].T
        o_ref[:, h * LANES:(h + 1) * LANES] = jnp.where(low, tt[0:NSA_QB], tt[NSA_QB:NSA_GQ])


def _nsa_attention(qa3, gexp, kc, vct, knsa3, vnsa_t, aggt, et, eye):
    B, S, _ = qa3.shape
    n_pad = kc.shape[1]
    n_cmp = (S - CMP_BLOCK) // CMP_STRIDE + 1
    n_blk = S // SEL_BLOCK
    n_sel = min(N_SELECT, n_blk)
    const = lambda shape: pl.BlockSpec(shape, lambda b, c: (0,) * len(shape))
    return pl.pallas_call(
        functools.partial(_nsa_kernel, n_cmp=n_cmp, n_blk=n_blk, n_sel=n_sel),
        grid=(B, S // NSA_QB),
        in_specs=[pl.BlockSpec((None, NSA_QB, NSA_WIDTH), lambda b, c: (b, c, 0)),
                  pl.BlockSpec((None, None, 3, NSA_ROWS), lambda b, c: (b, c, 0, 0)),
                  pl.BlockSpec((None, n_pad, KV_WIDTH), lambda b, c: (b, 0, 0)),
                  pl.BlockSpec((None, KV_WIDTH, n_pad), lambda b, c: (b, 0, 0)),
                  pl.BlockSpec((None, S, KV_WIDTH), lambda b, c: (b, 0, 0)),
                  pl.BlockSpec((None, S, KV_WIDTH), lambda b, c: (b, 0, 1)),
                  pl.BlockSpec((None, KV_WIDTH, S), lambda b, c: (b, 0, 0)),
                  pl.BlockSpec((None, KV_WIDTH, S), lambda b, c: (b, 1, 0)),
                  const((n_blk, n_pad)), const((S, LANES)), const((LANES, LANES))],
        out_specs=pl.BlockSpec((None, NSA_QB, NSA_WIDTH), lambda b, c: (b, c, 0)),
        out_shape=jax.ShapeDtypeStruct((B, S, NSA_WIDTH), F32),
        scratch_shapes=[pltpu.VMEM((NSA_KC, NSA_ROWS), F32), pltpu.VMEM((NSA_KC, NSA_ROWS), F32),
                        pltpu.VMEM((LANES, NSA_ROWS), F32)],
        compiler_params=_cparams(("parallel", "arbitrary")),
        name="nsa_attention",
    )(qa3, gexp, kc, vct, knsa3, knsa3, vnsa_t, vnsa_t, aggt, et, eye)


FOX_TQ = 512
FOX_KC = FOX_TQ
FOX_ROWS = FOX_PAIR * FOX_TQ


def _fox_kernel(q_ref, k_ref, aux_ref, vt_ref, o_ref, sa_scr, sb_scr, acc_scr):
    qi = pl.program_id(2)
    s0 = qi * FOX_TQ
    lane = lax.broadcasted_iota(jnp.int32, (FOX_TQ, LANES), 1)
    low = lane < HEAD_DIM
    q2 = q_ref[...]
    zero = jnp.zeros_like(q2)
    aux0 = jnp.where(lane < AUX_PIECES, -1.0, 0.0).astype(q2.dtype)
    aux1 = jnp.where((lane >= AUX_PIECES) & (lane < 2 * AUX_PIECES), -1.0, 0.0).astype(q2.dtype)
    qx = jnp.concatenate([jnp.concatenate([jnp.where(low, q2, zero), aux0], axis=1),
                          jnp.concatenate([jnp.where(low, zero, q2), aux1], axis=1)], axis=0)
    t_l = s0 + lax.broadcasted_iota(jnp.int32, (1, FOX_TQ), 1)

    def scores(i):
        k0 = pl.multiple_of(i * FOX_KC, FOX_KC)
        kx = jnp.concatenate([k_ref[pl.ds(k0, FOX_KC), :], aux_ref[pl.ds(k0, FOX_KC), :]], axis=1)
        return _dot_nt(kx, qx)

    def values(i):
        return vt_ref[:, pl.ds(pl.multiple_of(i * FOX_KC, FOX_KC), FOX_KC)]

    def diagonal():
        kpos = s0 + lax.broadcasted_iota(jnp.int32, (FOX_KC, 1), 0)
        s_d = _where_tiles(kpos <= t_l, scores(qi), NEG_INF)
        m_d = jnp.max(s_d, axis=0, keepdims=True)
        p_d = jnp.exp2(s_d - m_d)
        acc_scr[...] = _dot(values(qi), p_d.astype(BF16))
        return m_d, jnp.sum(p_d, axis=0, keepdims=True)

    l_f = _flash_chunks(qi, scores, values, diagonal, sa_scr, sb_scr, acc_scr)
    o_t = acc_scr[...] / l_f
    o_ref[...] = jnp.where(low, o_t[:, 0:FOX_TQ].T, o_t[:, FOX_TQ:FOX_ROWS].T)


def _fox_attention(fqk3, aux3, vfox_t):
    B, S, _ = fqk3.shape
    return pl.pallas_call(
        _fox_kernel,
        grid=(B, FOX_NPAIR, S // FOX_TQ),
        in_specs=[pl.BlockSpec((None, FOX_TQ, LANES), lambda b, h, i: (b, i, h)),
                  pl.BlockSpec((None, S, LANES), lambda b, h, i: (b, 0, FOX_NPAIR + h)),
                  pl.BlockSpec((None, S, LANES), lambda b, h, i: (b, 0, h)),
                  pl.BlockSpec((None, LANES, S), lambda b, h, i: (b, h, 0))],
        out_specs=pl.BlockSpec((None, FOX_TQ, LANES), lambda b, h, i: (b, i, h)),
        out_shape=jax.ShapeDtypeStruct((B, S, FOX_WIDTH), F32),
        scratch_shapes=[pltpu.VMEM((FOX_KC, FOX_ROWS), F32), pltpu.VMEM((FOX_KC, FOX_ROWS), F32),
                        pltpu.VMEM((LANES, FOX_ROWS), F32)],
        compiler_params=_cparams(("parallel", "parallel", "arbitrary")),
        name="fox_attention",
    )(fqk3, fqk3, aux3, vfox_t)


def _lane_max(x, mask):
    return jnp.max(jnp.where(mask, x, NEG_INF), axis=-1, keepdims=True)


def _first_lane_eq(x, v, mask, lane):
    return jnp.min(jnp.where(mask & (x == v), lane, LANES), axis=-1, keepdims=True)


def _route(logits):
    lane = lax.broadcasted_iota(jnp.int32, logits.shape, 1)
    gmask = lane < N_GROUPS
    gmax = _lane_max(logits, gmask)
    gexp = jnp.where(gmask, jnp.exp(logits - gmax), 0.0)
    g_star = _first_lane_eq(logits, gmax, gmask, lane)
    p_grp = 1.0 / jnp.sum(gexp, axis=-1, keepdims=True)
    e_lo = ROUTE_E_OFF + g_star * EXPERTS_PER_GROUP
    emask = (lane >= e_lo) & (lane < e_lo + EXPERTS_PER_GROUP)
    emax = _lane_max(logits, emask)
    eexp = jnp.where(emask, jnp.exp(logits - emax), 0.0)
    prob = eexp / jnp.sum(eexp, axis=-1, keepdims=True)
    v1 = _lane_max(prob, emask)
    i1 = _first_lane_eq(prob, v1, emask, lane)
    rest = emask & (lane != i1)
    v2 = _lane_max(prob, rest)
    i2 = _first_lane_eq(prob, v2, rest, lane)
    den = v1 + v2
    w1 = p_grp * (v1 / den)
    w2 = p_grp * (v2 / den)
    return jnp.where(lane == i1, w1, 0.0) + jnp.where(lane == i2, w2, 0.0)


def _outproj_kernel(on_ref, of_ref, x_ref, bn_ref, bf_ref, wn_ref, wf_ref, gffn_ref, wr_ref, br_ref,
                    h_ref, u_ref, comb_ref):
    mn = _rms(on_ref[...], bn_ref[...]).astype(BF16)
    mf = _rms(of_ref[...], bf_ref[...]).astype(BF16)
    h = x_ref[...] + (_dot(mn, wn_ref[...]) + _dot(mf, wf_ref[...]))
    h_ref[...] = h
    u = _rms(h, gffn_ref[...]).astype(BF16)
    u_ref[...] = u
    comb_ref[...] = _route(_dot(u, wr_ref[...]) + br_ref[...])


def _outproj(o_nsa, o_fox, x2, beta_n, beta_f, w_n, w_f, g_ffn, w_r, b_r, tm):
    T = x2.shape[0]
    full = lambda shape: pl.BlockSpec(shape, lambda i: (0,) * len(shape))
    row = lambda w: pl.BlockSpec((tm, w), lambda i: (i, 0))
    return pl.pallas_call(
        _outproj_kernel,
        grid=(T // tm,),
        in_specs=[row(NSA_WIDTH), row(FOX_WIDTH), row(D_MODEL), full((1, NSA_WIDTH)), full((1, FOX_WIDTH)),
                  full((NSA_WIDTH, D_MODEL)), full((FOX_WIDTH, D_MODEL)), full((1, D_MODEL)),
                  full((D_MODEL, LANES)), full((1, LANES))],
        out_specs=[row(D_MODEL), row(D_MODEL), row(LANES)],
        out_shape=[jax.ShapeDtypeStruct((T, D_MODEL), F32),
                   jax.ShapeDtypeStruct((T, D_MODEL), BF16),
                   jax.ShapeDtypeStruct((T, LANES), F32)],
        compiler_params=_cparams(("parallel",)),
        name="outproj_router",
    )(o_nsa, o_fox, x2, beta_n, beta_f, w_n, w_f, g_ffn, w_r, b_r)


def _moe_kernel(u_ref, comb_ref, h_ref, wgu_ref, wd_ref, o_ref):
    e = pl.program_id(1)

    @pl.when(e == 0)
    def _():
        o_ref[...] = h_ref[...]

    comb = comb_ref[...]
    lane = lax.broadcasted_iota(jnp.int32, comb.shape, 1)
    ce = jnp.sum(jnp.where(lane == ROUTE_E_OFF + e, comb, 0.0), axis=-1, keepdims=True)
    gu = _dot(u_ref[...], wgu_ref[...])
    gt = gu[:, 0:D_EXPERT]
    hid = gt * (1.0 / (1.0 + jnp.exp(-gt))) * gu[:, D_EXPERT:2 * D_EXPERT]
    o_ref[...] += _dot((ce * hid).astype(BF16), wd_ref[...])


def _moe(u, comb, h1, wgu, wd, tm):
    T = u.shape[0]
    return pl.pallas_call(
        _moe_kernel,
        grid=(T // tm, N_EXPERTS),
        in_specs=[pl.BlockSpec((tm, D_MODEL), lambda i, e: (i, 0)),
                  pl.BlockSpec((tm, LANES), lambda i, e: (i, 0)),
                  pl.BlockSpec((tm, D_MODEL), lambda i, e: (i, 0)),
                  pl.BlockSpec((None, D_MODEL, 2 * D_EXPERT), lambda i, e: (e, 0, 0)),
                  pl.BlockSpec((None, D_EXPERT, D_MODEL), lambda i, e: (e, 0, 0))],
        out_specs=pl.BlockSpec((tm, D_MODEL), lambda i, e: (i, 0)),
        out_shape=jax.ShapeDtypeStruct((T, D_MODEL), F32),
        compiler_params=_cparams(("parallel", "arbitrary")),
        name="moe_experts",
    )(u, comb, h1, wgu, wd)


def _ple_kernel(h_ref, p_ref, gple_ref, wg_ref, bg_ref, wp_ref, gfin_ref, o_ref):
    h = h_ref[...]
    v = _rms(h, gple_ref[...]).astype(BF16)
    z = _dot(v, wg_ref[...]) + bg_ref[...]
    gate = 1.0 / (1.0 + jnp.exp(-z))
    proj = _dot(p_ref[...].astype(BF16), wp_ref[...])
    o_ref[...] = _rms(h + gate * proj, gfin_ref[...])


def _ple(h2, p2, g_ple, w_g, b_g, w_p, g_final, tm):
    T = h2.shape[0]
    full = lambda shape: pl.BlockSpec(shape, lambda i: (0,) * len(shape))
    row = lambda w: pl.BlockSpec((tm, w), lambda i: (i, 0))
    return pl.pallas_call(
        _ple_kernel,
        grid=(T // tm,),
        in_specs=[row(D_MODEL), row(PLE_DIM), full((1, D_MODEL)), full((D_MODEL, D_MODEL)),
                  full((1, D_MODEL)), full((PLE_DIM, D_MODEL)), full((1, D_MODEL))],
        out_specs=row(D_MODEL),
        out_shape=jax.ShapeDtypeStruct((T, D_MODEL), F32),
        compiler_params=_cparams(("parallel",)),
        name="ple_final",
    )(h2, p2, g_ple, w_g, b_g, w_p, g_final)


def _rope_lane_tables():
    half = ROPE_HALF
    inv_freq = jnp.power(jnp.float32(ROPE_THETA), -jnp.arange(half, dtype=jnp.float32) / half)
    j = np.arange(LANES) % HEAD_DIM
    freq = jnp.where(jnp.asarray(j < ROPE_DIM), inv_freq[jnp.asarray(j % half)], 0.0).reshape(1, LANES)
    mlo = jnp.asarray(np.where(j < half, -1.0, 0.0), F32).reshape(1, LANES)
    mhi = jnp.asarray(np.where((j >= half) & (j < ROPE_DIM), 1.0, 0.0), F32).reshape(1, LANES)
    return freq.astype(F32), mlo, mhi


def _block_diag2(w):
    z = jnp.zeros_like(w)
    return jnp.concatenate([jnp.concatenate([w, z], axis=-1), jnp.concatenate([z, w], axis=-1)], axis=-2)


def _aux_route_table():
    r = np.zeros((AUX_PIECES, LANES, FOX_NPAIR * LANES), np.float32)
    for head in range(FOX_HEADS):
        for k in range(AUX_PIECES):
            r[k, SMALL_F_OFF + head, (head // FOX_PAIR) * LANES + (head % FOX_PAIR) * AUX_PIECES + k] = 1.0
    return jnp.asarray(r, BF16)


def _layer(h3, p3, positions, prm, g_final):
    B, S, _ = h3.shape
    T = B * S
    tm = 512
    x2 = h3.reshape(T, D_MODEL)
    freq, mlo, mhi = _rope_lane_tables()

    w_in = prm['w_in']
    offs = np.cumsum([0, NSA_WIDTH, KV_WIDTH, KV_WIDTH, KV_WIDTH, KV_WIDTH, KV_WIDTH, KV_WIDTH,
                      N_GATE, FOX_WIDTH, FOX_WIDTH, FOX_WIDTH, FOX_HEADS])
    seg = lambda k: w_in[:, offs[k]:offs[k + 1]]
    wq = seg(0).reshape(D_MODEL, NSA_GROUPS, NSA_HPG, HEAD_DIM).transpose(0, 2, 1, 3).reshape(D_MODEL, NSA_WIDTH)
    pad = jnp.zeros((D_MODEL, LANES - N_GATE - FOX_HEADS), w_in.dtype)
    w_all = jnp.concatenate([wq, seg(1), seg(2), seg(3), seg(5), seg(8), seg(9), seg(7), seg(11), pad],
                            axis=1).astype(BF16)
    wt_v = jnp.concatenate([seg(4), seg(6), seg(10)], axis=1).T.astype(BF16)
    bias_small = jnp.concatenate([prm['b_nsa_gate'], prm['b_forget'],
                                  jnp.zeros((LANES - N_GATE - FOX_HEADS,), F32)]).reshape(1, LANES)

    qa, cmp_tok, knsa, vnsa_t, fqk, vfox_t, small = _inproj(
        x2, positions.reshape(T, 1), prm['g_mix'].reshape(1, D_MODEL), w_all, wt_v, bias_small,
        freq, mlo, mhi, tm, B, S)

    tri = jnp.asarray(np.tril(np.ones((CUM_CHUNK, CUM_CHUNK), np.float32)), BF16)
    small3 = small.reshape(B, S, LANES)
    aux3 = _cumsum(small3, tri, _aux_route_table())

    n_rows = S // CMP_STRIDE
    tile2 = lambda pe: jnp.concatenate([pe, pe], axis=-1)
    bd1 = lambda w: _block_diag2(w.reshape(CMP_BLOCK, HEAD_DIM, CMP_HIDDEN)).astype(BF16)
    pos_cmp = positions[:, CMP_BLOCK - 1::CMP_STRIDE]
    pos_cmp = jnp.pad(pos_cmp, ((0, 0), (0, n_rows - pos_cmp.shape[1]))).reshape(B, n_rows, 1)
    kc, vct = _compress(cmp_tok.reshape(B, S, 2 * KV_WIDTH), tile2(prm['pe_cmp_k']), tile2(prm['pe_cmp_v']),
                        bd1(prm['w_cmp_k1']), bd1(prm['w_cmp_v1']),
                        _block_diag2(prm['w_cmp_k2']).astype(BF16), _block_diag2(prm['w_cmp_v2']).T.astype(BF16),
                        pos_cmp, freq, mlo, mhi)

    n_cmp = (S - CMP_BLOCK) // CMP_STRIDE + 1
    n_blk = S // SEL_BLOCK
    cs = np.arange(n_rows)[:, None] * CMP_STRIDE
    ss = np.arange(n_blk)[None, :] * SEL_BLOCK
    ov = np.clip(np.minimum(cs + CMP_BLOCK, ss + SEL_BLOCK) - np.maximum(cs, ss), 0, None) / CMP_BLOCK
    ov[n_cmp:] = 0.0
    aggt = jnp.asarray(ov.T, BF16)
    et = jnp.asarray((np.arange(S)[:, None] // SEL_BLOCK) == np.arange(LANES)[None, :], BF16)
    eye = jnp.asarray(np.eye(LANES, dtype=np.float32), BF16)
    gexp = small3[:, :, :N_GATE].reshape(B, S // NSA_QB, NSA_QB, NSA_GROUPS, NSA_HPG, 3)
    gexp = gexp.transpose(0, 1, 5, 4, 3, 2).reshape(B, S // NSA_QB, 3, NSA_ROWS)
    o_nsa = _nsa_attention(qa.reshape(B, S, NSA_WIDTH), gexp, kc, vct, knsa.reshape(B, S, 2 * KV_WIDTH),
                           vnsa_t, aggt, et, eye)

    o_fox = _fox_attention(fqk.reshape(B, S, 2 * FOX_WIDTH), aux3, vfox_t)

    perm = lambda a: a.reshape(NSA_GROUPS, NSA_HPG, HEAD_DIM, -1).transpose(1, 0, 2, 3).reshape(NSA_WIDTH, -1)
    beta_n = perm(prm['beta_nsa'].reshape(NSA_WIDTH, 1)).reshape(1, NSA_WIDTH)
    w_out = prm['w_out']
    w_n = perm(w_out[:NSA_WIDTH]).astype(BF16)
    w_f = w_out[NSA_WIDTH:].astype(BF16)
    w_r = jnp.concatenate([prm['w_group'], prm['w_router'],
                           jnp.zeros((D_MODEL, LANES - N_GROUPS - N_EXPERTS), F32)], axis=1).astype(BF16)
    b_r = jnp.concatenate([prm['b_group'], prm['b_router'],
                           jnp.zeros((LANES - N_GROUPS - N_EXPERTS,), F32)]).reshape(1, LANES)
    h1, u, comb = _outproj(o_nsa.reshape(T, NSA_WIDTH), o_fox.reshape(T, FOX_WIDTH), x2, beta_n,
                           prm['beta_fox'].reshape(1, FOX_WIDTH), w_n, w_f,
                           prm['g_ffn'].reshape(1, D_MODEL), w_r, b_r, tm)

    wgu = jnp.concatenate([prm['w_gate_e'], prm['w_up_e']], axis=-1).astype(BF16)
    h2 = _moe(u, comb, h1, wgu, prm['w_down_e'].astype(BF16), tm)

    out = _ple(h2, p3.reshape(T, PLE_DIM), prm['g_ple'].reshape(1, D_MODEL), prm['w_ple_gate'].astype(BF16),
               prm['b_ple_gate'].reshape(1, D_MODEL), prm['w_ple_proj'].astype(BF16),
               g_final.reshape(1, D_MODEL), tm)
    return out.reshape(B, S, D_MODEL)


_PARAM_NAMES = ('g_mix', 'w_in', 'b_nsa_gate', 'b_forget', 'pe_cmp_k', 'w_cmp_k1', 'w_cmp_k2',
                'pe_cmp_v', 'w_cmp_v1', 'w_cmp_v2', 'beta_nsa', 'beta_fox', 'w_out', 'g_ffn',
                'w_group', 'b_group', 'w_router', 'b_router', 'w_gate_e', 'w_up_e', 'w_down_e',
                'g_ple', 'w_ple_gate', 'b_ple_gate', 'w_ple_proj')


def kernel(x, p, positions, g_mix, w_in, b_nsa_gate, b_forget, pe_cmp_k, w_cmp_k1, w_cmp_k2, pe_cmp_v,
           w_cmp_v1, w_cmp_v2, beta_nsa, beta_fox, w_out, g_ffn, w_group, b_group, w_router, b_router,
           w_gate_e, w_up_e, w_down_e, g_ple, w_ple_gate, b_ple_gate, w_ple_proj, g_final):
    stacked = (g_mix, w_in, b_nsa_gate, b_forget, pe_cmp_k, w_cmp_k1, w_cmp_k2, pe_cmp_v, w_cmp_v1,
               w_cmp_v2, beta_nsa, beta_fox, w_out, g_ffn, w_group, b_group, w_router, b_router,
               w_gate_e, w_up_e, w_down_e, g_ple, w_ple_gate, b_ple_gate, w_ple_proj)
    depth = w_in.shape[0]
    assert depth == 1, "the final norm is fused into the last layer's embedding kernel"
    prm = {n: a[0] for n, a in zip(_PARAM_NAMES, stacked)}
    return _layer(x, p[0], positions, prm, g_final)
```

```python
import functools
import math

import numpy as np
import jax
import jax.numpy as jnp
from jax import lax
from jax.experimental import pallas as pl
from jax.experimental.pallas import tpu as pltpu

D_MODEL = 1024
HEAD_DIM = 64
NSA_HEADS = 8
FOX_HEADS = 8
NSA_GROUPS = 2
NSA_HPG = 4
NSA_WIDTH = 512
FOX_WIDTH = 512
KV_WIDTH = 128
CMP_BLOCK = 32
CMP_STRIDE = 16
CMP_HIDDEN = 128
SEL_BLOCK = 64
N_SELECT = 16
WINDOW = 512
ROPE_THETA = 500000.0
ROPE_DIM = 16
ROPE_HALF = 8
N_GROUPS = 4
EXPERTS_PER_GROUP = 4
N_EXPERTS = 16
D_EXPERT = 512
PLE_DIM = 256
EPS = 1e-6
NEG_INF = -1e30
FORCE_BONUS = 1e4
LOG2E = math.log2(math.e)
Q_SCALE = 0.125 * LOG2E
NEG_BIG = -(2.0 ** 100)

LANES = 128
SUBLANES = 8
N_GATE = 3 * NSA_HEADS
SMALL_F_OFF = N_GATE
ROUTE_E_OFF = N_GROUPS
ROUTE_G_LANE = 0

VMEM_LIMIT = 56 * 1024 * 1024

F32 = jnp.float32
BF16 = jnp.bfloat16


def _cparams(sem):
    return pltpu.CompilerParams(dimension_semantics=sem, vmem_limit_bytes=VMEM_LIMIT)


def _dot(a, b):
    return jnp.dot(a, b, preferred_element_type=F32)


def _dot_nt(a, b):
    return lax.dot_general(a, b, (((1,), (1,)), ((), ())), preferred_element_type=F32)


def _split3(x):
    hi = x.astype(BF16)
    r1 = x - hi.astype(F32)
    mid = r1.astype(BF16)
    lo = (r1 - mid.astype(F32)).astype(BF16)
    return hi, mid, lo


def _split3_dot_lhs(e, x):
    hi, mid, lo = _split3(x)
    return _dot(e, hi) + _dot(e, mid) + _dot(e, lo)


def _rms(x, g):
    return x * lax.rsqrt(jnp.mean(x * x, axis=-1, keepdims=True) + EPS) * g


def _rope(r, cos, sin_lo, sin_hi):
    return (r * cos + pltpu.roll(r, LANES - ROPE_HALF, 1) * sin_lo
            + pltpu.roll(r, ROPE_HALF, 1) * sin_hi)


def _rope_tables(pos_col, freq, mlo, mhi):
    ang = pos_col.astype(F32) * freq
    cos = jnp.cos(ang)
    sin = jnp.sin(ang)
    return cos, sin * mlo, sin * mhi


def _inproj_kernel(x_ref, pos_ref, g_ref, w_ref, wt_ref, bias_ref, freq_ref, mlo_ref, mhi_ref,
                   qa_ref, cmp_ref, knsa_ref, vnsa_ref, fqk_ref, vfox_ref, small_ref):
    x = x_ref[...]
    hb = _rms(x, g_ref[...]).astype(BF16)
    cos, s_lo, s_hi = _rope_tables(pos_ref[...], freq_ref[...], mlo_ref[...], mhi_ref[...])
    off = 0
    r = _dot(hb, w_ref[:, off:off + NSA_WIDTH])
    for c in range(NSA_WIDTH // LANES):
        rc = _rope(r[:, c * LANES:(c + 1) * LANES], cos, s_lo, s_hi) * Q_SCALE
        qa_ref[:, c * LANES:(c + 1) * LANES] = rc.astype(qa_ref.dtype)
    off += NSA_WIDTH
    cmp_ref[...] = _dot(hb, w_ref[:, off:off + 2 * KV_WIDTH])
    off += 2 * KV_WIDTH
    r = _dot(hb, w_ref[:, off:off + 2 * KV_WIDTH])
    for c in range(2):
        rc = _rope(r[:, c * LANES:(c + 1) * LANES], cos, s_lo, s_hi)
        knsa_ref[:, c * LANES:(c + 1) * LANES] = rc.astype(knsa_ref.dtype)
    off += 2 * KV_WIDTH
    r = _dot(hb, w_ref[:, off:off + FOX_WIDTH])
    fqk_ref[:, 0:FOX_WIDTH] = (r * Q_SCALE).astype(fqk_ref.dtype)
    off += FOX_WIDTH
    r = _dot(hb, w_ref[:, off:off + FOX_WIDTH])
    fqk_ref[:, FOX_WIDTH:2 * FOX_WIDTH] = r.astype(fqk_ref.dtype)
    off += FOX_WIDTH
    z = _dot(hb, w_ref[:, off:off + LANES]) + bias_ref[...]
    lane = lax.broadcasted_iota(jnp.int32, z.shape, 1)
    sig = 1.0 / (1.0 + jnp.exp(-z))
    logsig = jnp.minimum(z, 0.0) - jnp.log(1.0 + jnp.exp(-jnp.abs(z)))
    small_ref[...] = jnp.where(lane < SMALL_F_OFF, sig, logsig)
    vt = _dot_nt(wt_ref[...], hb)
    vnsa_ref[...] = vt[0:2 * KV_WIDTH].astype(vnsa_ref.dtype)
    vfox_ref[...] = vt[2 * KV_WIDTH:2 * KV_WIDTH + FOX_WIDTH].astype(vfox_ref.dtype)


def _inproj(x2, pos2, g_mix, w_all, wt_v, bias_small, freq, mlo, mhi, tm, B, S):
    T = x2.shape[0]
    nt = S // tm
    full = lambda shape: pl.BlockSpec(shape, lambda i: (0,) * len(shape))
    row = lambda w: pl.BlockSpec((tm, w), lambda i: (i, 0))
    tr = lambda w: pl.BlockSpec((None, w, tm), lambda i: (i // nt, 0, i % nt))
    return pl.pallas_call(
        _inproj_kernel,
        grid=(T // tm,),
        in_specs=[row(D_MODEL), row(1), full((1, D_MODEL)), full(w_all.shape), full(wt_v.shape),
                  full((1, LANES)), full((1, LANES)), full((1, LANES)), full((1, LANES))],
        out_specs=[row(NSA_WIDTH), row(2 * KV_WIDTH), row(2 * KV_WIDTH), tr(2 * KV_WIDTH),
                   row(2 * FOX_WIDTH), tr(FOX_WIDTH), row(LANES)],
        out_shape=[jax.ShapeDtypeStruct((T, NSA_WIDTH), BF16),
                   jax.ShapeDtypeStruct((T, 2 * KV_WIDTH), F32),
                   jax.ShapeDtypeStruct((T, 2 * KV_WIDTH), BF16),
                   jax.ShapeDtypeStruct((B, 2 * KV_WIDTH, S), BF16),
                   jax.ShapeDtypeStruct((T, 2 * FOX_WIDTH), BF16),
                   jax.ShapeDtypeStruct((B, FOX_WIDTH, S), BF16),
                   jax.ShapeDtypeStruct((T, LANES), F32)],
        compiler_params=_cparams(("parallel",)),
        name="inproj",
    )(x2, pos2, g_mix, w_all, wt_v, bias_small, freq, mlo, mhi)


CUM_CHUNK = 256
FOX_PAIR = 2
FOX_NPAIR = FOX_HEADS // FOX_PAIR
AUX_PIECES = 3


def _cumsum_kernel(x_ref, tri_ref, route_ref, o_ref, carry_ref):
    @pl.when(pl.program_id(1) == 0)
    def _():
        carry_ref[...] = jnp.zeros_like(carry_ref)

    c = _split3_dot_lhs(tri_ref[...], x_ref[...]) + carry_ref[...]
    carry_ref[...] = c[CUM_CHUNK - 1:CUM_CHUNK, :]
    hi, mid, lo = _split3(c * LOG2E)
    aux = _dot(hi, route_ref[0]) + _dot(mid, route_ref[1]) + _dot(lo, route_ref[2])
    o_ref[...] = aux.astype(o_ref.dtype)


def _cumsum(small3, tri, route):
    B, S, _ = small3.shape
    width = FOX_NPAIR * LANES
    return pl.pallas_call(
        _cumsum_kernel,
        grid=(B, S // CUM_CHUNK),
        in_specs=[pl.BlockSpec((None, CUM_CHUNK, LANES), lambda b, i: (b, i, 0)),
                  pl.BlockSpec((CUM_CHUNK, CUM_CHUNK), lambda b, i: (0, 0)),
                  pl.BlockSpec((AUX_PIECES, LANES, width), lambda b, i: (0, 0, 0))],
        out_specs=pl.BlockSpec((None, CUM_CHUNK, width), lambda b, i: (b, i, 0)),
        out_shape=jax.ShapeDtypeStruct((B, S, width), BF16),
        scratch_shapes=[pltpu.VMEM((1, LANES), F32)],
        compiler_params=_cparams(("parallel", "arbitrary")),
        name="forget_cumsum",
    )(small3, tri, route)


def _compress_kernel(tokk_ref, tokv_ref, pek_ref, pev_ref, bdk1_ref, bdv1_ref, bdk2_ref, bdv2t_ref,
                     pos_ref, freq_ref, mlo_ref, mhi_ref, kc_ref, vct_ref, *, n_rows):
    half = CMP_BLOCK // 2
    ak = jnp.zeros((n_rows, 2 * CMP_HIDDEN), F32)
    bk = jnp.zeros((n_rows, 2 * CMP_HIDDEN), F32)
    av = jnp.zeros((n_rows, 2 * CMP_HIDDEN), F32)
    bv = jnp.zeros((n_rows, 2 * CMP_HIDDEN), F32)
    for l in range(half):
        xk = tokk_ref[pl.ds(l, n_rows, stride=CMP_STRIDE), :]
        xv = tokv_ref[pl.ds(l, n_rows, stride=CMP_STRIDE), :]
        ak = ak + _dot((xk + pek_ref[l:l + 1, :]).astype(BF16), bdk1_ref[l])
        bk = bk + _dot((xk + pek_ref[half + l:half + l + 1, :]).astype(BF16), bdk1_ref[half + l])
        av = av + _dot((xv + pev_ref[l:l + 1, :]).astype(BF16), bdv1_ref[l])
        bv = bv + _dot((xv + pev_ref[half + l:half + l + 1, :]).astype(BF16), bdv1_ref[half + l])
    hk = ak + pltpu.roll(bk, n_rows - 1, 0)
    hv = av + pltpu.roll(bv, n_rows - 1, 0)
    hk = hk * (1.0 / (1.0 + jnp.exp(-hk)))
    hv = hv * (1.0 / (1.0 + jnp.exp(-hv)))
    kc = _dot(hk.astype(BF16), bdk2_ref[...])
    cos, s_lo, s_hi = _rope_tables(pos_ref[...], freq_ref[...], mlo_ref[...], mhi_ref[...])
    kc_ref[...] = _rope(kc, cos, s_lo, s_hi).astype(kc_ref.dtype)
    vct_ref[...] = _dot_nt(bdv2t_ref[...], hv.astype(BF16)).astype(vct_ref.dtype)


def _compress(cmp_tok3, pek, pev, bdk1, bdv1, bdk2, bdv2t, pos_cmp, freq, mlo, mhi):
    B, S, _ = cmp_tok3.shape
    n_rows = S // CMP_STRIDE
    full = lambda shape: pl.BlockSpec(shape, lambda b: (0,) * len(shape))
    return pl.pallas_call(
        functools.partial(_compress_kernel, n_rows=n_rows),
        grid=(B,),
        in_specs=[pl.BlockSpec((None, S, KV_WIDTH), lambda b: (b, 0, 0)),
                  pl.BlockSpec((None, S, KV_WIDTH), lambda b: (b, 0, 1)),
                  full((CMP_BLOCK, KV_WIDTH)), full((CMP_BLOCK, KV_WIDTH)),
                  full((CMP_BLOCK, KV_WIDTH, 2 * CMP_HIDDEN)), full((CMP_BLOCK, KV_WIDTH, 2 * CMP_HIDDEN)),
                  full((2 * CMP_HIDDEN, KV_WIDTH)), full((KV_WIDTH, 2 * CMP_HIDDEN)),
                  pl.BlockSpec((None, n_rows, 1), lambda b: (b, 0, 0)),
                  full((1, LANES)), full((1, LANES)), full((1, LANES))],
        out_specs=[pl.BlockSpec((None, n_rows, KV_WIDTH), lambda b: (b, 0, 0)),
                   pl.BlockSpec((None, KV_WIDTH, n_rows), lambda b: (b, 0, 0))],
        out_shape=[jax.ShapeDtypeStruct((B, n_rows, KV_WIDTH), BF16),
                   jax.ShapeDtypeStruct((B, KV_WIDTH, n_rows), BF16)],
        compiler_params=_cparams(("parallel",)),
        name="nsa_compress",
    )(cmp_tok3, cmp_tok3, pek, pev, bdk1, bdv1, bdk2, bdv2t, pos_cmp, freq, mlo, mhi)


NSA_QB = 64
NSA_KC = 512
NSA_GQ = NSA_GROUPS * NSA_QB
NSA_ROWS = NSA_HPG * NSA_GQ
NSA_WSPAN = 640


def _where_tiles(mask, x, other):
    w = mask.shape[1]
    return jnp.concatenate([jnp.where(mask, x[:, t * w:(t + 1) * w], other)
                            for t in range(x.shape[1] // w)], axis=1)


def _masked_softmax_cols(s, mask):
    l = _where_tiles(mask, s, NEG_INF)
    m = jnp.max(l, axis=0, keepdims=True)
    e = jnp.exp2(l - m)
    return e, jnp.sum(e, axis=0, keepdims=True)


def _flash_chunks(n_chunks, score_fn, vt_fn, init_fn, s_a, s_b, acc_ref):
    last = jnp.maximum(n_chunks - 1, 0)

    def produce(i, s_ref):
        s = score_fn(jnp.minimum(i, last))
        s_ref[...] = s
        return jnp.max(s, axis=0, keepdims=True)

    def consume(i, s_ref, mx, m, l):
        m_new = jnp.maximum(m, mx)
        alpha = jnp.exp2(m - m_new)
        p = jnp.exp2(s_ref[...] - m_new)
        l_new = alpha * l + jnp.sum(p, axis=0, keepdims=True)
        acc_ref[...] = alpha * acc_ref[...] + _dot(vt_fn(i), p.astype(BF16))
        return m_new, l_new

    def pair(j, carry):
        mx_a, m, l = carry
        i = 2 * j
        mx_b = produce(i + 1, s_b)
        m, l = consume(i, s_a, mx_a, m, l)
        mx_a = produce(i + 2, s_a)
        m, l = consume(i + 1, s_b, mx_b, m, l)
        return mx_a, m, l

    def tail(_, carry):
        mx_a, m, l = carry
        m, l = consume(last, s_a, mx_a, m, l)
        return mx_a, m, l

    mx_a = produce(0, s_a)
    m, l = init_fn()
    carry = (mx_a, m, l)
    carry = lax.fori_loop(0, n_chunks // 2, pair, carry)
    carry = lax.fori_loop(0, n_chunks & 1, tail, carry)
    return carry[2]


def _nsa_kernel(q_ref, gate_ref, kc_ref, vct_ref, ks_ref, kw_ref, vst_ref, vwt_ref, aggt_ref, et_ref,
                eye_ref, o_ref, sa_scr, sb_scr, acc_scr, *, n_cmp, n_blk, n_sel):
    c = pl.program_id(1)
    s0 = c * NSA_QB
    lane = lax.broadcasted_iota(jnp.int32, (NSA_QB, LANES), 1)
    low = lane < HEAD_DIM
    qb = q_ref[...]
    zero = jnp.zeros((NSA_QB, LANES), qb.dtype)
    pieces = []
    for h in range(NSA_HPG):
        qh = qb[:, h * LANES:(h + 1) * LANES]
        pieces.append(jnp.where(low, qh, zero))
        pieces.append(jnp.where(low, zero, qh))
    qp = jnp.concatenate(pieces, axis=0)
    t_l = s0 + (lax.broadcasted_iota(jnp.int32, (1, NSA_GQ), 1) & (NSA_QB - 1))

    n_pad = kc_ref.shape[0]
    n_s = lax.broadcasted_iota(jnp.int32, (n_pad, 1), 0)
    mask_c = ((n_s * CMP_STRIDE + (CMP_BLOCK - 1)) <= t_l) & (n_s < n_cmp)
    start = pl.multiple_of((jnp.maximum(s0 - WINDOW, 0) // LANES) * LANES, LANES)
    rel = t_l - (start + lax.broadcasted_iota(jnp.int32, (NSA_WSPAN, 1), 0))
    mask_w = (rel >= 0) & (rel < WINDOW)
    s_c = _dot_nt(kc_ref[...], qp)
    s_w = _dot_nt(kw_ref[pl.ds(start, NSA_WSPAN), :], qp)
    e_c, sum_c = _masked_softmax_cols(s_c, mask_c)
    inv_c = _where_tiles(t_l >= CMP_BLOCK - 1, 1.0 / sum_c, 0.0)
    acc_c = _dot(vct_ref[...], e_c.astype(BF16))

    j_s = lax.broadcasted_iota(jnp.int32, (n_blk, NSA_GQ), 0)
    nv = n_blk // SUBLANES

    def ranked_mask():
        p_c = e_c * inv_c
        psum = p_c[:, 0:NSA_GQ]
        for h in range(1, NSA_HPG):
            psum = psum + p_c[:, h * NSA_GQ:(h + 1) * NSA_GQ]
        imp = _split3_dot_lhs(aggt_ref[...], psum)
        forced = (j_s == 0) | (j_s == c) | (j_s == c - 1)
        score = jnp.where(j_s > c, -1.0, imp + jnp.where(forced, FORCE_BONUS, 0.0))
        sc = [score[SUBLANES * v:SUBLANES * (v + 1), :] for v in range(nv)]
        sub = lax.broadcasted_iota(jnp.int32, (SUBLANES, NSA_GQ), 0)

        def count_group(vi, rk):
            rk = list(rk)
            for ri in range(SUBLANES):
                row = sc[vi][ri:ri + 1, :]
                for v in range(nv):
                    if v < vi:
                        beats = jnp.where(row > sc[v], 1.0, 0.0)
                    elif v > vi:
                        beats = jnp.where(row >= sc[v], 1.0, 0.0)
                    else:
                        beats = jnp.where(sub > ri, jnp.where(row >= sc[v], 1.0, 0.0),
                                          jnp.where(row > sc[v], 1.0, 0.0))
                    rk[v] = rk[v] + beats
            return tuple(rk)

        rk = tuple(jnp.zeros((SUBLANES, NSA_GQ), F32) for _ in range(nv))
        for vi in range(nv):
            rk = lax.cond(vi * SUBLANES <= c, functools.partial(count_group, vi), lambda r: r, rk)
        rank = jnp.concatenate(rk, axis=0)
        return jnp.where((rank < n_sel) & (j_s < c), 0.0, NEG_BIG)

    def all_mask():
        return jnp.where(j_s < c, 0.0, NEG_BIG)

    neg_t = lax.cond(c < n_sel, all_mask, ranked_mask)

    a0 = pl.multiple_of((s0 // LANES) * LANES, LANES)
    s_d = _dot_nt(ks_ref[pl.ds(a0, LANES), :], qp)
    neg_t = jnp.concatenate([neg_t, jnp.zeros((LANES - n_blk, NSA_GQ), F32)], axis=0).astype(BF16)
    q_aux = _dot_nt(eye_ref[...], neg_t).astype(BF16)
    qx = jnp.concatenate([qp, jnp.concatenate([q_aux] * NSA_HPG, axis=0)], axis=1)

    def sel_scores(i):
        k0 = pl.multiple_of(i * NSA_KC, NSA_KC)
        kx = jnp.concatenate([ks_ref[pl.ds(k0, NSA_KC), :], et_ref[pl.ds(k0, NSA_KC), :]], axis=1)
        return _dot_nt(kx, qx)

    def sel_values(i):
        return vst_ref[:, pl.ds(pl.multiple_of(i * NSA_KC, NSA_KC), NSA_KC)]

    window = {}

    def after_first_scores():
        kpos = a0 + lax.broadcasted_iota(jnp.int32, (LANES, 1), 0)
        l_d = _where_tiles((kpos >= s0) & (kpos <= t_l), s_d, NEG_INF)
        m_d = jnp.max(l_d, axis=0, keepdims=True)
        p_d = jnp.exp2(l_d - m_d)
        acc_scr[...] = _dot(vst_ref[:, pl.ds(a0, LANES)], p_d.astype(BF16))
        e_w, window['l'] = _masked_softmax_cols(s_w, mask_w)
        window['acc'] = _dot(vwt_ref[:, pl.ds(start, NSA_WSPAN)], e_w.astype(BF16))
        return m_d, jnp.sum(p_d, axis=0, keepdims=True)

    n_prev = (s0 + NSA_KC - 1) // NSA_KC
    l_s = _flash_chunks(n_prev, sel_scores, sel_values, after_first_scores, sa_scr, sb_scr, acc_scr)

    gates = gate_ref[...]
    o_t = ((gates[0:1, :] * inv_c) * acc_c + (gates[1:2, :] / l_s) * acc_scr[...]
           + (gates[2:3, :] / window['l']) * window['acc'])
    for h in range(NSA_HPG):
        tt = o_t[:, h * NSA_GQ:(h + 1) * NSA_GQ].T
        o_ref[:, h * LANES:(h + 1) * LANES] = jnp.where(low, tt[0:NSA_QB], tt[NSA_QB:NSA_GQ])


def _nsa_attention(qa3, gexp, kc, vct, knsa3, vnsa_t, aggt, et, eye):
    B, S, _ = qa3.shape
    n_pad = kc.shape[1]
    n_cmp = (S - CMP_BLOCK) // CMP_STRIDE + 1
    n_blk = S // SEL_BLOCK
    n_sel = min(N_SELECT, n_blk)
    const = lambda shape: pl.BlockSpec(shape, lambda b, c: (0,) * len(shape))
    return pl.pallas_call(
        functools.partial(_nsa_kernel, n_cmp=n_cmp, n_blk=n_blk, n_sel=n_sel),
        grid=(B, S // NSA_QB),
        in_specs=[pl.BlockSpec((None, NSA_QB, NSA_WIDTH), lambda b, c: (b, c, 0)),
                  pl.BlockSpec((None, None, 3, NSA_ROWS), lambda b, c: (b, c, 0, 0)),
                  pl.BlockSpec((None, n_pad, KV_WIDTH), lambda b, c: (b, 0, 0)),
                  pl.BlockSpec((None, KV_WIDTH, n_pad), lambda b, c: (b, 0, 0)),
                  pl.BlockSpec((None, S, KV_WIDTH), lambda b, c: (b, 0, 0)),
                  pl.BlockSpec((None, S, KV_WIDTH), lambda b, c: (b, 0, 1)),
                  pl.BlockSpec((None, KV_WIDTH, S), lambda b, c: (b, 0, 0)),
                  pl.BlockSpec((None, KV_WIDTH, S), lambda b, c: (b, 1, 0)),
                  const((n_blk, n_pad)), const((S, LANES)), const((LANES, LANES))],
        out_specs=pl.BlockSpec((None, NSA_QB, NSA_WIDTH), lambda b, c: (b, c, 0)),
        out_shape=jax.ShapeDtypeStruct((B, S, NSA_WIDTH), F32),
        scratch_shapes=[pltpu.VMEM((NSA_KC, NSA_ROWS), F32), pltpu.VMEM((NSA_KC, NSA_ROWS), F32),
                        pltpu.VMEM((LANES, NSA_ROWS), F32)],
        compiler_params=_cparams(("parallel", "arbitrary")),
        name="nsa_attention",
    )(qa3, gexp, kc, vct, knsa3, knsa3, vnsa_t, vnsa_t, aggt, et, eye)


FOX_TQ = 512
FOX_KC = FOX_TQ
FOX_ROWS = FOX_PAIR * FOX_TQ


def _fox_kernel(q_ref, k_ref, aux_ref, vt_ref, o_ref, sa_scr, sb_scr, acc_scr):
    qi = pl.program_id(2)
    s0 = qi * FOX_TQ
    lane = lax.broadcasted_iota(jnp.int32, (FOX_TQ, LANES), 1)
    low = lane < HEAD_DIM
    q2 = q_ref[...]
    zero = jnp.zeros_like(q2)
    aux0 = jnp.where(lane < AUX_PIECES, -1.0, 0.0).astype(q2.dtype)
    aux1 = jnp.where((lane >= AUX_PIECES) & (lane < 2 * AUX_PIECES), -1.0, 0.0).astype(q2.dtype)
    qx = jnp.concatenate([jnp.concatenate([jnp.where(low, q2, zero), aux0], axis=1),
                          jnp.concatenate([jnp.where(low, zero, q2), aux1], axis=1)], axis=0)
    t_l = s0 + lax.broadcasted_iota(jnp.int32, (1, FOX_TQ), 1)

    def scores(i):
        k0 = pl.multiple_of(i * FOX_KC, FOX_KC)
        kx = jnp.concatenate([k_ref[pl.ds(k0, FOX_KC), :], aux_ref[pl.ds(k0, FOX_KC), :]], axis=1)
        return _dot_nt(kx, qx)

    def values(i):
        return vt_ref[:, pl.ds(pl.multiple_of(i * FOX_KC, FOX_KC), FOX_KC)]

    def diagonal():
        kpos = s0 + lax.broadcasted_iota(jnp.int32, (FOX_KC, 1), 0)
        s_d = _where_tiles(kpos <= t_l, scores(qi), NEG_INF)
        m_d = jnp.max(s_d, axis=0, keepdims=True)
        p_d = jnp.exp2(s_d - m_d)
        acc_scr[...] = _dot(values(qi), p_d.astype(BF16))
        return m_d, jnp.sum(p_d, axis=0, keepdims=True)

    l_f = _flash_chunks(qi, scores, values, diagonal, sa_scr, sb_scr, acc_scr)
    o_t = acc_scr[...] / l_f
    o_ref[...] = jnp.where(low, o_t[:, 0:FOX_TQ].T, o_t[:, FOX_TQ:FOX_ROWS].T)


def _fox_attention(fqk3, aux3, vfox_t):
    B, S, _ = fqk3.shape
    return pl.pallas_call(
        _fox_kernel,
        grid=(B, FOX_NPAIR, S // FOX_TQ),
        in_specs=[pl.BlockSpec((None, FOX_TQ, LANES), lambda b, h, i: (b, i, h)),
                  pl.BlockSpec((None, S, LANES), lambda b, h, i: (b, 0, FOX_NPAIR + h)),
                  pl.BlockSpec((None, S, LANES), lambda b, h, i: (b, 0, h)),
                  pl.BlockSpec((None, LANES, S), lambda b, h, i: (b, h, 0))],
        out_specs=pl.BlockSpec((None, FOX_TQ, LANES), lambda b, h, i: (b, i, h)),
        out_shape=jax.ShapeDtypeStruct((B, S, FOX_WIDTH), F32),
        scratch_shapes=[pltpu.VMEM((FOX_KC, FOX_ROWS), F32), pltpu.VMEM((FOX_KC, FOX_ROWS), F32),
                        pltpu.VMEM((LANES, FOX_ROWS), F32)],
        compiler_params=_cparams(("parallel", "parallel", "arbitrary")),
        name="fox_attention",
    )(fqk3, fqk3, aux3, vfox_t)


def _lane_max(x, mask):
    return jnp.max(jnp.where(mask, x, NEG_INF), axis=-1, keepdims=True)


def _first_lane_eq(x, v, mask, lane):
    return jnp.min(jnp.where(mask & (x == v), lane, LANES), axis=-1, keepdims=True)


def _route(logits):
    lane = lax.broadcasted_iota(jnp.int32, logits.shape, 1)
    gmask = lane < N_GROUPS
    gmax = _lane_max(logits, gmask)
    gexp = jnp.where(gmask, jnp.exp(logits - gmax), 0.0)
    g_star = _first_lane_eq(logits, gmax, gmask, lane)
    p_grp = 1.0 / jnp.sum(gexp, axis=-1, keepdims=True)
    e_lo = ROUTE_E_OFF + g_star * EXPERTS_PER_GROUP
    emask = (lane >= e_lo) & (lane < e_lo + EXPERTS_PER_GROUP)
    emax = _lane_max(logits, emask)
    eexp = jnp.where(emask, jnp.exp(logits - emax), 0.0)
    prob = eexp / jnp.sum(eexp, axis=-1, keepdims=True)
    v1 = _lane_max(prob, emask)
    i1 = _first_lane_eq(prob, v1, emask, lane)
    rest = emask & (lane != i1)
    v2 = _lane_max(prob, rest)
    i2 = _first_lane_eq(prob, v2, rest, lane)
    den = v1 + v2
    w1 = p_grp * (v1 / den)
    w2 = p_grp * (v2 / den)
    return (jnp.where(lane == i1, w1, 0.0) + jnp.where(lane == i2, w2, 0.0)
            + jnp.where(lane == ROUTE_G_LANE, g_star.astype(F32), 0.0))


def _outproj_kernel(on_ref, of_ref, x_ref, bn_ref, bf_ref, wn_ref, wf_ref, gffn_ref, wr_ref, br_ref,
                    h_ref, u_ref, comb_ref):
    mn = _rms(on_ref[...], bn_ref[...]).astype(BF16)
    mf = _rms(of_ref[...], bf_ref[...]).astype(BF16)
    h = x_ref[...] + (_dot(mn, wn_ref[...]) + _dot(mf, wf_ref[...]))
    h_ref[...] = h
    u = _rms(h, gffn_ref[...]).astype(BF16)
    u_ref[...] = u
    comb_ref[...] = _route(_dot(u, wr_ref[...]) + br_ref[...])


def _outproj(o_nsa, o_fox, x2, beta_n, beta_f, w_n, w_f, g_ffn, w_r, b_r, tm):
    T = x2.shape[0]
    full = lambda shape: pl.BlockSpec(shape, lambda i: (0,) * len(shape))
    row = lambda w: pl.BlockSpec((tm, w), lambda i: (i, 0))
    return pl.pallas_call(
        _outproj_kernel,
        grid=(T // tm,),
        in_specs=[row(NSA_WIDTH), row(FOX_WIDTH), row(D_MODEL), full((1, NSA_WIDTH)), full((1, FOX_WIDTH)),
                  full((NSA_WIDTH, D_MODEL)), full((FOX_WIDTH, D_MODEL)), full((1, D_MODEL)),
                  full((D_MODEL, LANES)), full((1, LANES))],
        out_specs=[row(D_MODEL), row(D_MODEL), row(LANES)],
        out_shape=[jax.ShapeDtypeStruct((T, D_MODEL), F32),
                   jax.ShapeDtypeStruct((T, D_MODEL), BF16),
                   jax.ShapeDtypeStruct((T, LANES), F32)],
        compiler_params=_cparams(("parallel",)),
        name="outproj_router",
    )(o_nsa, o_fox, x2, beta_n, beta_f, w_n, w_f, g_ffn, w_r, b_r)


MOE_TS = 1024
MOE_RB = 128
MOE_TSP = MOE_TS + N_GROUPS * MOE_RB
MOE_META = 8


def _moe_sort_kernel(u_ref, comb_ref, tri_ref, us_ref, cs_ref, pos_ref, meta_ref):
    comb = comb_ref[...]
    lane = lax.broadcasted_iota(jnp.int32, comb.shape, 1)
    grp = comb[:, ROUTE_G_LANE:ROUTE_G_LANE + 1].astype(jnp.int32)
    onehot = jnp.where((lane == grp) & (lane < N_GROUPS), 1.0, 0.0)
    rank = _dot(tri_ref[...], onehot.astype(BF16))
    counts = rank[MOE_TS - 1:MOE_TS, :]
    nblk = jnp.floor((counts + (MOE_RB - 1)) * (1.0 / MOE_RB))
    lane1 = lane[0:1, :]
    blk0 = jnp.zeros_like(nblk)
    run = jnp.zeros((1, 1), F32)
    for g in range(1, N_GROUPS):
        run = run + nblk[:, g - 1:g]
        blk0 = blk0 + jnp.where(lane1 == g, run, 0.0)
    val = onehot * (blk0 * MOE_RB + rank - 1.0)
    pos_col = jnp.sum(val, axis=-1, keepdims=True)
    pos_ref[...] = pos_col.astype(jnp.int32)
    ones = jnp.ones((SUBLANES, LANES), BF16)
    hi, mid, lo = _split3(val)
    pos_row = (_dot_nt(ones, hi) + _dot_nt(ones, mid) + _dot_nt(ones, lo))[0:1, :].astype(jnp.int32)
    r_idx = lax.broadcasted_iota(jnp.int32, (MOE_TSP, MOE_TS), 0)
    perm = jnp.where(r_idx == pos_row, 1.0, 0.0).astype(BF16)
    us_ref[...] = _dot(perm, u_ref[...]).astype(us_ref.dtype)
    cs_ref[...] = _split3_dot_lhs(perm, comb)
    meta = blk0 + pltpu.roll(nblk, N_GROUPS, 1)
    meta_ref[...] = jnp.where(lane1 < MOE_META, meta, 0.0).astype(jnp.int32)


def _moe_sort(u, comb, tri):
    T = u.shape[0]
    nt = T // MOE_TS
    return pl.pallas_call(
        _moe_sort_kernel,
        grid=(nt,),
        in_specs=[pl.BlockSpec((MOE_TS, D_MODEL), lambda i: (i, 0)),
                  pl.BlockSpec((MOE_TS, LANES), lambda i: (i, 0)),
                  pl.BlockSpec((MOE_TS, MOE_TS), lambda i: (0, 0))],
        out_specs=[pl.BlockSpec((MOE_TSP, D_MODEL), lambda i: (i, 0)),
                   pl.BlockSpec((MOE_TSP, LANES), lambda i: (i, 0)),
                   pl.BlockSpec((MOE_TS, 1), lambda i: (i, 0)),
                   pl.BlockSpec((None, 1, LANES), lambda i: (i, 0, 0))],
        out_shape=[jax.ShapeDtypeStruct((nt * MOE_TSP, D_MODEL), BF16),
                   jax.ShapeDtypeStruct((nt * MOE_TSP, LANES), F32),
                   jax.ShapeDtypeStruct((T, 1), jnp.int32),
                   jax.ShapeDtypeStruct((nt, 1, LANES), jnp.int32)],
        compiler_params=_cparams(("parallel",)),
        name="moe_sort",
    )(u, comb, tri)


def _moe_kernel(meta_ref, us_ref, cs_ref, pos_ref, h_ref, wgu_ref, wd_ref, o_ref, acc_ref):
    i = pl.program_id(0)
    e = pl.program_id(1)
    g = e // EXPERTS_PER_GROUP

    @pl.when(e == 0)
    def _():
        acc_ref[...] = jnp.zeros_like(acc_ref)

    first = meta_ref[i * MOE_META + g]
    count = meta_ref[i * MOE_META + N_GROUPS + g]

    def blocks(j0, n):
        rows = [pl.multiple_of((first + j0 + k) * MOE_RB, MOE_RB) for k in range(n)]
        gus = [_dot(us_ref[pl.ds(r0, MOE_RB), :], wgu_ref[...]) for r0 in rows]
        for r0, gu in zip(rows, gus):
            cs = cs_ref[pl.ds(r0, MOE_RB), :]
            lane = lax.broadcasted_iota(jnp.int32, cs.shape, 1)
            ce = jnp.sum(jnp.where(lane == ROUTE_E_OFF + e, cs, 0.0), axis=-1, keepdims=True)
            gt = gu[:, 0:D_EXPERT]
            hid = gt * (1.0 / (1.0 + jnp.exp(-gt))) * gu[:, D_EXPERT:2 * D_EXPERT]
            acc_ref[pl.ds(r0, MOE_RB), :] += _dot((ce * hid).astype(BF16), wd_ref[...])

    def pair(j, carry):
        blocks(2 * j, 2)
        return carry

    def single(_, carry):
        blocks(count - 1, 1)
        return carry

    lax.fori_loop(0, count // 2, pair, 0)
    lax.fori_loop(0, count & 1, single, 0)

    @pl.when(e == N_EXPERTS - 1)
    def _():
        r_idx = lax.broadcasted_iota(jnp.int32, (MOE_TS, MOE_TSP), 1)
        unperm = jnp.where(r_idx == pos_ref[...], 1.0, 0.0).astype(BF16)
        o_ref[...] = h_ref[...] + _dot(unperm, acc_ref[...].astype(BF16))


def _moe(meta, us, cs, pos, h1, wgu, wd):
    T = h1.shape[0]
    grid_spec = pltpu.PrefetchScalarGridSpec(
        num_scalar_prefetch=1,
        grid=(T // MOE_TS, N_EXPERTS),
        in_specs=[pl.BlockSpec((MOE_TSP, D_MODEL), lambda i, e, m: (i, 0)),
                  pl.BlockSpec((MOE_TSP, LANES), lambda i, e, m: (i, 0)),
                  pl.BlockSpec((MOE_TS, 1), lambda i, e, m: (i, 0)),
                  pl.BlockSpec((MOE_TS, D_MODEL), lambda i, e, m: (i, 0)),
                  pl.BlockSpec((None, D_MODEL, 2 * D_EXPERT), lambda i, e, m: (e, 0, 0)),
                  pl.BlockSpec((None, D_EXPERT, D_MODEL), lambda i, e, m: (e, 0, 0))],
        out_specs=pl.BlockSpec((MOE_TS, D_MODEL), lambda i, e, m: (i, 0)),
        scratch_shapes=[pltpu.VMEM((MOE_TSP, D_MODEL), F32)])
    return pl.pallas_call(
        _moe_kernel,
        grid_spec=grid_spec,
        out_shape=jax.ShapeDtypeStruct((T, D_MODEL), F32),
        compiler_params=_cparams(("parallel", "arbitrary")),
        name="moe_experts",
    )(meta, us, cs, pos, h1, wgu, wd)


def _ple_kernel(h_ref, p_ref, gple_ref, wg_ref, bg_ref, wp_ref, gfin_ref, o_ref):
    h = h_ref[...]
    v = _rms(h, gple_ref[...]).astype(BF16)
    z = _dot(v, wg_ref[...]) + bg_ref[...]
    gate = 1.0 / (1.0 + jnp.exp(-z))
    proj = _dot(p_ref[...].astype(BF16), wp_ref[...])
    o_ref[...] = _rms(h + gate * proj, gfin_ref[...])


def _ple(h2, p2, g_ple, w_g, b_g, w_p, g_final, tm):
    T = h2.shape[0]
    full = lambda shape: pl.BlockSpec(shape, lambda i: (0,) * len(shape))
    row = lambda w: pl.BlockSpec((tm, w), lambda i: (i, 0))
    return pl.pallas_call(
        _ple_kernel,
        grid=(T // tm,),
        in_specs=[row(D_MODEL), row(PLE_DIM), full((1, D_MODEL)), full((D_MODEL, D_MODEL)),
                  full((1, D_MODEL)), full((PLE_DIM, D_MODEL)), full((1, D_MODEL))],
        out_specs=row(D_MODEL),
        out_shape=jax.ShapeDtypeStruct((T, D_MODEL), F32),
        compiler_params=_cparams(("parallel",)),
        name="ple_final",
    )(h2, p2, g_ple, w_g, b_g, w_p, g_final)


def _rope_lane_tables():
    half = ROPE_HALF
    inv_freq = jnp.power(jnp.float32(ROPE_THETA), -jnp.arange(half, dtype=jnp.float32) / half)
    j = np.arange(LANES) % HEAD_DIM
    freq = jnp.where(jnp.asarray(j < ROPE_DIM), inv_freq[jnp.asarray(j % half)], 0.0).reshape(1, LANES)
    mlo = jnp.asarray(np.where(j < half, -1.0, 0.0), F32).reshape(1, LANES)
    mhi = jnp.asarray(np.where((j >= half) & (j < ROPE_DIM), 1.0, 0.0), F32).reshape(1, LANES)
    return freq.astype(F32), mlo, mhi


def _block_diag2(w):
    z = jnp.zeros_like(w)
    return jnp.concatenate([jnp.concatenate([w, z], axis=-1), jnp.concatenate([z, w], axis=-1)], axis=-2)


def _aux_route_table():
    r = np.zeros((AUX_PIECES, LANES, FOX_NPAIR * LANES), np.float32)
    for head in range(FOX_HEADS):
        for k in range(AUX_PIECES):
            r[k, SMALL_F_OFF + head, (head // FOX_PAIR) * LANES + (head % FOX_PAIR) * AUX_PIECES + k] = 1.0
    return jnp.asarray(r, BF16)


def _layer(h3, p3, positions, prm, g_final):
    B, S, _ = h3.shape
    T = B * S
    tm = 512
    x2 = h3.reshape(T, D_MODEL)
    freq, mlo, mhi = _rope_lane_tables()

    w_in = prm['w_in']
    offs = np.cumsum([0, NSA_WIDTH, KV_WIDTH, KV_WIDTH, KV_WIDTH, KV_WIDTH, KV_WIDTH, KV_WIDTH,
                      N_GATE, FOX_WIDTH, FOX_WIDTH, FOX_WIDTH, FOX_HEADS])
    seg = lambda k: w_in[:, offs[k]:offs[k + 1]]
    wq = seg(0).reshape(D_MODEL, NSA_GROUPS, NSA_HPG, HEAD_DIM).transpose(0, 2, 1, 3).reshape(D_MODEL, NSA_WIDTH)
    pad = jnp.zeros((D_MODEL, LANES - N_GATE - FOX_HEADS), w_in.dtype)
    w_all = jnp.concatenate([wq, seg(1), seg(2), seg(3), seg(5), seg(8), seg(9), seg(7), seg(11), pad],
                            axis=1).astype(BF16)
    wt_v = jnp.concatenate([seg(4), seg(6), seg(10)], axis=1).T.astype(BF16)
    bias_small = jnp.concatenate([prm['b_nsa_gate'], prm['b_forget'],
                                  jnp.zeros((LANES - N_GATE - FOX_HEADS,), F32)]).reshape(1, LANES)

    qa, cmp_tok, knsa, vnsa_t, fqk, vfox_t, small = _inproj(
        x2, positions.reshape(T, 1), prm['g_mix'].reshape(1, D_MODEL), w_all, wt_v, bias_small,
        freq, mlo, mhi, tm, B, S)

    tri = jnp.asarray(np.tril(np.ones((CUM_CHUNK, CUM_CHUNK), np.float32)), BF16)
    small3 = small.reshape(B, S, LANES)
    aux3 = _cumsum(small3, tri, _aux_route_table())

    n_rows = S // CMP_STRIDE
    tile2 = lambda pe: jnp.concatenate([pe, pe], axis=-1)
    bd1 = lambda w: _block_diag2(w.reshape(CMP_BLOCK, HEAD_DIM, CMP_HIDDEN)).astype(BF16)
    pos_cmp = positions[:, CMP_BLOCK - 1::CMP_STRIDE]
    pos_cmp = jnp.pad(pos_cmp, ((0, 0), (0, n_rows - pos_cmp.shape[1]))).reshape(B, n_rows, 1)
    kc, vct = _compress(cmp_tok.reshape(B, S, 2 * KV_WIDTH), tile2(prm['pe_cmp_k']), tile2(prm['pe_cmp_v']),
                        bd1(prm['w_cmp_k1']), bd1(prm['w_cmp_v1']),
                        _block_diag2(prm['w_cmp_k2']).astype(BF16), _block_diag2(prm['w_cmp_v2']).T.astype(BF16),
                        pos_cmp, freq, mlo, mhi)

    n_cmp = (S - CMP_BLOCK) // CMP_STRIDE + 1
    n_blk = S // SEL_BLOCK
    cs = np.arange(n_rows)[:, None] * CMP_STRIDE
    ss = np.arange(n_blk)[None, :] * SEL_BLOCK
    ov = np.clip(np.minimum(cs + CMP_BLOCK, ss + SEL_BLOCK) - np.maximum(cs, ss), 0, None) / CMP_BLOCK
    ov[n_cmp:] = 0.0
    aggt = jnp.asarray(ov.T, BF16)
    et = jnp.asarray((np.arange(S)[:, None] // SEL_BLOCK) == np.arange(LANES)[None, :], BF16)
    eye = jnp.asarray(np.eye(LANES, dtype=np.float32), BF16)
    gexp = small3[:, :, :N_GATE].reshape(B, S // NSA_QB, NSA_QB, NSA_GROUPS, NSA_HPG, 3)
    gexp = gexp.transpose(0, 1, 5, 4, 3, 2).reshape(B, S // NSA_QB, 3, NSA_ROWS)
    o_nsa = _nsa_attention(qa.reshape(B, S, NSA_WIDTH), gexp, kc, vct, knsa.reshape(B, S, 2 * KV_WIDTH),
                           vnsa_t, aggt, et, eye)

    o_fox = _fox_attention(fqk.reshape(B, S, 2 * FOX_WIDTH), aux3, vfox_t)

    perm = lambda a: a.reshape(NSA_GROUPS, NSA_HPG, HEAD_DIM, -1).transpose(1, 0, 2, 3).reshape(NSA_WIDTH, -1)
    beta_n = perm(prm['beta_nsa'].reshape(NSA_WIDTH, 1)).reshape(1, NSA_WIDTH)
    w_out = prm['w_out']
    w_n = perm(w_out[:NSA_WIDTH]).astype(BF16)
    w_f = w_out[NSA_WIDTH:].astype(BF16)
    w_r = jnp.concatenate([prm['w_group'], prm['w_router'],
                           jnp.zeros((D_MODEL, LANES - N_GROUPS - N_EXPERTS), F32)], axis=1).astype(BF16)
    b_r = jnp.concatenate([prm['b_group'], prm['b_router'],
                           jnp.zeros((LANES - N_GROUPS - N_EXPERTS,), F32)]).reshape(1, LANES)
    h1, u, comb = _outproj(o_nsa.reshape(T, NSA_WIDTH), o_fox.reshape(T, FOX_WIDTH), x2, beta_n,
                           prm['beta_fox'].reshape(1, FOX_WIDTH), w_n, w_f,
                           prm['g_ffn'].reshape(1, D_MODEL), w_r, b_r, tm)

    wgu = jnp.concatenate([prm['w_gate_e'], prm['w_up_e']], axis=-1).astype(BF16)
    tri_s = jnp.asarray(np.tril(np.ones((MOE_TS, MOE_TS), np.float32)), BF16)
    us, cs, pos, meta = _moe_sort(u, comb, tri_s)
    h2 = _moe(meta[:, 0, :MOE_META].reshape(-1), us, cs, pos, h1, wgu, prm['w_down_e'].astype(BF16))

    out = _ple(h2, p3.reshape(T, PLE_DIM), prm['g_ple'].reshape(1, D_MODEL), prm['w_ple_gate'].astype(BF16),
               prm['b_ple_gate'].reshape(1, D_MODEL), prm['w_ple_proj'].astype(BF16),
               g_final.reshape(1, D_MODEL), tm)
    return out.reshape(B, S, D_MODEL)


_PARAM_NAMES = ('g_mix', 'w_in', 'b_nsa_gate', 'b_forget', 'pe_cmp_k', 'w_cmp_k1', 'w_cmp_k2',
                'pe_cmp_v', 'w_cmp_v1', 'w_cmp_v2', 'beta_nsa', 'beta_fox', 'w_out', 'g_ffn',
                'w_group', 'b_group', 'w_router', 'b_router', 'w_gate_e', 'w_up_e', 'w_down_e',
                'g_ple', 'w_ple_gate', 'b_ple_gate', 'w_ple_proj')


def kernel(x, p, positions, g_mix, w_in, b_nsa_gate, b_forget, pe_cmp_k, w_cmp_k1, w_cmp_k2, pe_cmp_v,
           w_cmp_v1, w_cmp_v2, beta_nsa, beta_fox, w_out, g_ffn, w_group, b_group, w_router, b_router,
           w_gate_e, w_up_e, w_down_e, g_ple, w_ple_gate, b_ple_gate, w_ple_proj, g_final):
    stacked = (g_mix, w_in, b_nsa_gate, b_forget, pe_cmp_k, w_cmp_k1, w_cmp_k2, pe_cmp_v, w_cmp_v1,
               w_cmp_v2, beta_nsa, beta_fox, w_out, g_ffn, w_group, b_group, w_router, b_router,
               w_gate_e, w_up_e, w_down_e, g_ple, w_ple_gate, b_ple_gate, w_ple_proj)
    depth = w_in.shape[0]
    assert depth == 1, "the final norm is fused into the last layer's embedding kernel"
    prm = {n: a[0] for n, a in zip(_PARAM_NAMES, stacked)}
    return _layer(x, p[0], positions, prm, g_final)
```

```python
import functools
import math

import numpy as np
import jax
import jax.numpy as jnp
from jax import lax
from jax.experimental import pallas as pl
from jax.experimental.pallas import tpu as pltpu

D_MODEL = 1024
HEAD_DIM = 64
NSA_HEADS = 8
FOX_HEADS = 8
NSA_GROUPS = 2
NSA_HPG = 4
NSA_WIDTH = 512
FOX_WIDTH = 512
KV_WIDTH = 128
CMP_BLOCK = 32
CMP_STRIDE = 16
CMP_HIDDEN = 128
SEL_BLOCK = 64
N_SELECT = 16
WINDOW = 512
ROPE_THETA = 500000.0
ROPE_DIM = 16
ROPE_HALF = 8
N_GROUPS = 4
EXPERTS_PER_GROUP = 4
N_EXPERTS = 16
D_EXPERT = 512
PLE_DIM = 256
EPS = 1e-6
NEG_INF = -1e30
FORCE_BONUS = 1e4
LOG2E = math.log2(math.e)
Q_SCALE = 0.125 * LOG2E
NEG_BIG = -(2.0 ** 100)

LANES = 128
SUBLANES = 8
N_GATE = 3 * NSA_HEADS
SMALL_F_OFF = N_GATE
ROUTE_E_OFF = N_GROUPS
ROUTE_G_LANE = 0

VMEM_LIMIT = 56 * 1024 * 1024

F32 = jnp.float32
BF16 = jnp.bfloat16


def _cparams(sem):
    return pltpu.CompilerParams(dimension_semantics=sem, vmem_limit_bytes=VMEM_LIMIT)


def _dot(a, b):
    return jnp.dot(a, b, preferred_element_type=F32)


def _dot_nt(a, b):
    return lax.dot_general(a, b, (((1,), (1,)), ((), ())), preferred_element_type=F32)


def _split3(x):
    hi = x.astype(BF16)
    r1 = x - hi.astype(F32)
    mid = r1.astype(BF16)
    lo = (r1 - mid.astype(F32)).astype(BF16)
    return hi, mid, lo


def _split3_dot_lhs(e, x):
    hi, mid, lo = _split3(x)
    return _dot(e, hi) + _dot(e, mid) + _dot(e, lo)


def _rms(x, g):
    return x * lax.rsqrt(jnp.mean(x * x, axis=-1, keepdims=True) + EPS) * g


def _rope(r, cos, sin_lo, sin_hi):
    return (r * cos + pltpu.roll(r, LANES - ROPE_HALF, 1) * sin_lo
            + pltpu.roll(r, ROPE_HALF, 1) * sin_hi)


def _rope_tables(pos_col, freq, mlo, mhi):
    ang = pos_col.astype(F32) * freq
    cos = jnp.cos(ang)
    sin = jnp.sin(ang)
    return cos, sin * mlo, sin * mhi


def _inproj_kernel(x_ref, pos_ref, g_ref, w_ref, wt_ref, bias_ref, freq_ref, mlo_ref, mhi_ref,
                   qa_ref, cmp_ref, knsa_ref, vnsa_ref, fqk_ref, vfox_ref, small_ref):
    x = x_ref[...]
    hb = _rms(x, g_ref[...]).astype(BF16)
    cos, s_lo, s_hi = _rope_tables(pos_ref[...], freq_ref[...], mlo_ref[...], mhi_ref[...])
    off = 0
    r = _dot(hb, w_ref[:, off:off + NSA_WIDTH])
    for c in range(NSA_WIDTH // LANES):
        rc = _rope(r[:, c * LANES:(c + 1) * LANES], cos, s_lo, s_hi) * Q_SCALE
        qa_ref[:, c * LANES:(c + 1) * LANES] = rc.astype(qa_ref.dtype)
    off += NSA_WIDTH
    cmp_ref[...] = _dot(hb, w_ref[:, off:off + 2 * KV_WIDTH])
    off += 2 * KV_WIDTH
    r = _dot(hb, w_ref[:, off:off + 2 * KV_WIDTH])
    for c in range(2):
        rc = _rope(r[:, c * LANES:(c + 1) * LANES], cos, s_lo, s_hi)
        knsa_ref[:, c * LANES:(c + 1) * LANES] = rc.astype(knsa_ref.dtype)
    off += 2 * KV_WIDTH
    r = _dot(hb, w_ref[:, off:off + FOX_WIDTH])
    fqk_ref[:, 0:FOX_WIDTH] = (r * Q_SCALE).astype(fqk_ref.dtype)
    off += FOX_WIDTH
    r = _dot(hb, w_ref[:, off:off + FOX_WIDTH])
    fqk_ref[:, FOX_WIDTH:2 * FOX_WIDTH] = r.astype(fqk_ref.dtype)
    off += FOX_WIDTH
    z = _dot(hb, w_ref[:, off:off + LANES]) + bias_ref[...]
    lane = lax.broadcasted_iota(jnp.int32, z.shape, 1)
    sig = 1.0 / (1.0 + jnp.exp(-z))
    logsig = jnp.minimum(z, 0.0) - jnp.log(1.0 + jnp.exp(-jnp.abs(z)))
    small_ref[...] = jnp.where(lane < SMALL_F_OFF, sig, logsig)
    vt = _dot_nt(wt_ref[...], hb)
    vnsa_ref[...] = vt[0:2 * KV_WIDTH].astype(vnsa_ref.dtype)
    vfox_ref[...] = vt[2 * KV_WIDTH:2 * KV_WIDTH + FOX_WIDTH].astype(vfox_ref.dtype)


def _inproj(x2, pos2, g_mix, w_all, wt_v, bias_small, freq, mlo, mhi, tm, B, S):
    T = x2.shape[0]
    nt = S // tm
    full = lambda shape: pl.BlockSpec(shape, lambda i: (0,) * len(shape))
    row = lambda w: pl.BlockSpec((tm, w), lambda i: (i, 0))
    tr = lambda w: pl.BlockSpec((None, w, tm), lambda i: (i // nt, 0, i % nt))
    return pl.pallas_call(
        _inproj_kernel,
        grid=(T // tm,),
        in_specs=[row(D_MODEL), row(1), full((1, D_MODEL)), full(w_all.shape), full(wt_v.shape),
                  full((1, LANES)), full((1, LANES)), full((1, LANES)), full((1, LANES))],
        out_specs=[row(NSA_WIDTH), row(2 * KV_WIDTH), row(2 * KV_WIDTH), tr(2 * KV_WIDTH),
                   row(2 * FOX_WIDTH), tr(FOX_WIDTH), row(LANES)],
        out_shape=[jax.ShapeDtypeStruct((T, NSA_WIDTH), BF16),
                   jax.ShapeDtypeStruct((T, 2 * KV_WIDTH), F32),
                   jax.ShapeDtypeStruct((T, 2 * KV_WIDTH), BF16),
                   jax.ShapeDtypeStruct((B, 2 * KV_WIDTH, S), BF16),
                   jax.ShapeDtypeStruct((T, 2 * FOX_WIDTH), BF16),
                   jax.ShapeDtypeStruct((B, FOX_WIDTH, S), BF16),
                   jax.ShapeDtypeStruct((T, LANES), F32)],
        compiler_params=_cparams(("parallel",)),
        name="inproj",
    )(x2, pos2, g_mix, w_all, wt_v, bias_small, freq, mlo, mhi)


CUM_CHUNK = 1024
FOX_PAIR = 2
FOX_NPAIR = FOX_HEADS // FOX_PAIR
AUX_PIECES = 3


def _cumsum_kernel(x_ref, tri_ref, route_ref, o_ref, carry_ref):
    @pl.when(pl.program_id(1) == 0)
    def _():
        carry_ref[...] = jnp.zeros_like(carry_ref)

    c = _split3_dot_lhs(tri_ref[...], x_ref[...]) + carry_ref[...]
    carry_ref[...] = c[CUM_CHUNK - 1:CUM_CHUNK, :]
    hi, mid, lo = _split3(c * LOG2E)
    aux = _dot(hi, route_ref[0]) + _dot(mid, route_ref[1]) + _dot(lo, route_ref[2])
    o_ref[...] = aux.astype(o_ref.dtype)


def _cumsum(small3, tri, route):
    B, S, _ = small3.shape
    width = FOX_NPAIR * LANES
    return pl.pallas_call(
        _cumsum_kernel,
        grid=(B, S // CUM_CHUNK),
        in_specs=[pl.BlockSpec((None, CUM_CHUNK, LANES), lambda b, i: (b, i, 0)),
                  pl.BlockSpec((CUM_CHUNK, CUM_CHUNK), lambda b, i: (0, 0)),
                  pl.BlockSpec((AUX_PIECES, LANES, width), lambda b, i: (0, 0, 0))],
        out_specs=pl.BlockSpec((None, CUM_CHUNK, width), lambda b, i: (b, i, 0)),
        out_shape=jax.ShapeDtypeStruct((B, S, width), BF16),
        scratch_shapes=[pltpu.VMEM((1, LANES), F32)],
        compiler_params=_cparams(("parallel", "arbitrary")),
        name="forget_cumsum",
    )(small3, tri, route)


def _compress_kernel(tokk_ref, tokv_ref, pek_ref, pev_ref, bdk1_ref, bdv1_ref, bdk2_ref, bdv2t_ref,
                     pos_ref, freq_ref, mlo_ref, mhi_ref, kc_ref, vct_ref, *, n_rows):
    half = CMP_BLOCK // 2
    ak = jnp.zeros((n_rows, 2 * CMP_HIDDEN), F32)
    bk = jnp.zeros((n_rows, 2 * CMP_HIDDEN), F32)
    av = jnp.zeros((n_rows, 2 * CMP_HIDDEN), F32)
    bv = jnp.zeros((n_rows, 2 * CMP_HIDDEN), F32)
    for l in range(half):
        xk = tokk_ref[pl.ds(l, n_rows, stride=CMP_STRIDE), :]
        xv = tokv_ref[pl.ds(l, n_rows, stride=CMP_STRIDE), :]
        ak = ak + _dot((xk + pek_ref[l:l + 1, :]).astype(BF16), bdk1_ref[l])
        bk = bk + _dot((xk + pek_ref[half + l:half + l + 1, :]).astype(BF16), bdk1_ref[half + l])
        av = av + _dot((xv + pev_ref[l:l + 1, :]).astype(BF16), bdv1_ref[l])
        bv = bv + _dot((xv + pev_ref[half + l:half + l + 1, :]).astype(BF16), bdv1_ref[half + l])
    hk = ak + pltpu.roll(bk, n_rows - 1, 0)
    hv = av + pltpu.roll(bv, n_rows - 1, 0)
    hk = hk * (1.0 / (1.0 + jnp.exp(-hk)))
    hv = hv * (1.0 / (1.0 + jnp.exp(-hv)))
    kc = _dot(hk.astype(BF16), bdk2_ref[...])
    cos, s_lo, s_hi = _rope_tables(pos_ref[...], freq_ref[...], mlo_ref[...], mhi_ref[...])
    kc_ref[...] = _rope(kc, cos, s_lo, s_hi).astype(kc_ref.dtype)
    vct_ref[...] = _dot_nt(bdv2t_ref[...], hv.astype(BF16)).astype(vct_ref.dtype)


def _compress(cmp_tok3, pek, pev, bdk1, bdv1, bdk2, bdv2t, pos_cmp, freq, mlo, mhi):
    B, S, _ = cmp_tok3.shape
    n_rows = S // CMP_STRIDE
    full = lambda shape: pl.BlockSpec(shape, lambda b: (0,) * len(shape))
    return pl.pallas_call(
        functools.partial(_compress_kernel, n_rows=n_rows),
        grid=(B,),
        in_specs=[pl.BlockSpec((None, S, KV_WIDTH), lambda b: (b, 0, 0)),
                  pl.BlockSpec((None, S, KV_WIDTH), lambda b: (b, 0, 1)),
                  full((CMP_BLOCK, KV_WIDTH)), full((CMP_BLOCK, KV_WIDTH)),
                  full((CMP_BLOCK, KV_WIDTH, 2 * CMP_HIDDEN)), full((CMP_BLOCK, KV_WIDTH, 2 * CMP_HIDDEN)),
                  full((2 * CMP_HIDDEN, KV_WIDTH)), full((KV_WIDTH, 2 * CMP_HIDDEN)),
                  pl.BlockSpec((None, n_rows, 1), lambda b: (b, 0, 0)),
                  full((1, LANES)), full((1, LANES)), full((1, LANES))],
        out_specs=[pl.BlockSpec((None, n_rows, KV_WIDTH), lambda b: (b, 0, 0)),
                   pl.BlockSpec((None, KV_WIDTH, n_rows), lambda b: (b, 0, 0))],
        out_shape=[jax.ShapeDtypeStruct((B, n_rows, KV_WIDTH), BF16),
                   jax.ShapeDtypeStruct((B, KV_WIDTH, n_rows), BF16)],
        compiler_params=_cparams(("parallel",)),
        name="nsa_compress",
    )(cmp_tok3, cmp_tok3, pek, pev, bdk1, bdv1, bdk2, bdv2t, pos_cmp, freq, mlo, mhi)


NSA_QB = 128
NSA_BPQ = NSA_QB // SEL_BLOCK
assert NSA_BPQ in (1, 2) and NSA_QB % LANES == 0
NSA_KC = 512
NSA_GQ = NSA_GROUPS * NSA_QB
NSA_ROWS = NSA_HPG * NSA_GQ
NSA_WSPAN = WINDOW + NSA_QB


def _where_tiles(mask, x, other):
    w = mask.shape[1]
    return jnp.concatenate([jnp.where(mask, x[:, t * w:(t + 1) * w], other)
                            for t in range(x.shape[1] // w)], axis=1)


def _masked_softmax_cols(s, mask):
    l = _where_tiles(mask, s, NEG_INF)
    m = jnp.max(l, axis=0, keepdims=True)
    e = jnp.exp2(l - m)
    return e, jnp.sum(e, axis=0, keepdims=True)


def _flash_chunks(n_chunks, score_fn, vt_fn, init_fn, s_a, s_b, acc_ref):
    last = jnp.maximum(n_chunks - 1, 0)

    def produce(i, s_ref):
        s = score_fn(jnp.minimum(i, last))
        s_ref[...] = s
        return jnp.max(s, axis=0, keepdims=True)

    def consume(i, s_ref, mx, m, l):
        m_new = jnp.maximum(m, mx)
        alpha = jnp.exp2(m - m_new)
        p = jnp.exp2(s_ref[...] - m_new)
        l_new = alpha * l + jnp.sum(p, axis=0, keepdims=True)
        acc_ref[...] = alpha * acc_ref[...] + _dot(vt_fn(i), p.astype(BF16))
        return m_new, l_new

    def pair(j, carry):
        mx_a, m, l = carry
        i = 2 * j
        mx_b = produce(i + 1, s_b)
        m, l = consume(i, s_a, mx_a, m, l)
        mx_a = produce(i + 2, s_a)
        m, l = consume(i + 1, s_b, mx_b, m, l)
        return mx_a, m, l

    def tail(_, carry):
        mx_a, m, l = carry
        m, l = consume(last, s_a, mx_a, m, l)
        return mx_a, m, l

    mx_a = produce(0, s_a)
    m, l = init_fn()
    carry = (mx_a, m, l)
    carry = lax.fori_loop(0, n_chunks // 2, pair, carry)
    carry = lax.fori_loop(0, n_chunks & 1, tail, carry)
    return carry[2]


def _nsa_kernel(q_ref, gate_ref, kc_ref, vct_ref, ks_ref, kw_ref, vst_ref, vwt_ref, aggt_ref, et_ref,
                eye_ref, o_ref, sa_scr, sb_scr, acc_scr, *, n_cmp, n_blk, n_sel):
    c = pl.program_id(1)
    s0 = c * NSA_QB
    lane = lax.broadcasted_iota(jnp.int32, (NSA_QB, LANES), 1)
    low = lane < HEAD_DIM
    qb = q_ref[...]
    zero = jnp.zeros((NSA_QB, LANES), qb.dtype)
    pieces = []
    for h in range(NSA_HPG):
        qh = qb[:, h * LANES:(h + 1) * LANES]
        pieces.append(jnp.where(low, qh, zero))
        pieces.append(jnp.where(low, zero, qh))
    qp = jnp.concatenate(pieces, axis=0)
    t_l = s0 + (lax.broadcasted_iota(jnp.int32, (1, NSA_GQ), 1) & (NSA_QB - 1))
    cur_l = t_l // SEL_BLOCK
    first_blk = c * NSA_BPQ
    last_blk = first_blk + NSA_BPQ - 1

    n_pad = kc_ref.shape[0]
    n_s = lax.broadcasted_iota(jnp.int32, (n_pad, 1), 0)
    mask_c = ((n_s * CMP_STRIDE + (CMP_BLOCK - 1)) <= t_l) & (n_s < n_cmp)
    start = pl.multiple_of(jnp.maximum(s0 - WINDOW, 0), LANES)
    rel = t_l - (start + lax.broadcasted_iota(jnp.int32, (NSA_WSPAN, 1), 0))
    mask_w = (rel >= 0) & (rel < WINDOW)
    s_c = _dot_nt(kc_ref[...], qp)
    s_w = _dot_nt(kw_ref[pl.ds(start, NSA_WSPAN), :], qp)
    e_c, sum_c = _masked_softmax_cols(s_c, mask_c)
    inv_c = _where_tiles(t_l >= CMP_BLOCK - 1, 1.0 / sum_c, 0.0)
    acc_c = _dot(vct_ref[...], e_c.astype(BF16))

    j_s = lax.broadcasted_iota(jnp.int32, (n_blk, NSA_GQ), 0)
    nv = n_blk // SUBLANES

    def ranked_mask():
        p_c = e_c * inv_c
        psum = p_c[:, 0:NSA_GQ]
        for h in range(1, NSA_HPG):
            psum = psum + p_c[:, h * NSA_GQ:(h + 1) * NSA_GQ]
        imp = _split3_dot_lhs(aggt_ref[...], psum)
        forced = (j_s == 0) | (j_s == cur_l) | (j_s == cur_l - 1)
        score = jnp.where(j_s > cur_l, -1.0, imp + jnp.where(forced, FORCE_BONUS, 0.0))
        sc = [score[SUBLANES * v:SUBLANES * (v + 1), :] for v in range(nv)]
        sub = lax.broadcasted_iota(jnp.int32, (SUBLANES, NSA_GQ), 0)

        def count_group(vi, rk):
            rk = list(rk)
            for ri in range(SUBLANES):
                row = sc[vi][ri:ri + 1, :]
                for v in range(nv):
                    if v < vi:
                        beats = jnp.where(row > sc[v], 1.0, 0.0)
                    elif v > vi:
                        beats = jnp.where(row >= sc[v], 1.0, 0.0)
                    else:
                        beats = jnp.where(sub > ri, jnp.where(row >= sc[v], 1.0, 0.0),
                                          jnp.where(row > sc[v], 1.0, 0.0))
                    rk[v] = rk[v] + beats
            return tuple(rk)

        rk = tuple(jnp.zeros((SUBLANES, NSA_GQ), F32) for _ in range(nv))
        for vi in range(nv):
            rk = lax.cond(vi * SUBLANES <= last_blk, functools.partial(count_group, vi), lambda r: r, rk)
        rank = jnp.concatenate(rk, axis=0)
        return jnp.where((rank < n_sel) & (j_s < first_blk), 0.0, NEG_BIG)

    def all_mask():
        return jnp.where(j_s < first_blk, 0.0, NEG_BIG)

    neg_t = lax.cond(last_blk < n_sel, all_mask, ranked_mask)

    a0 = pl.multiple_of(s0, LANES)
    s_d = _dot_nt(ks_ref[pl.ds(a0, NSA_QB), :], qp)
    neg_t = jnp.concatenate([neg_t, jnp.zeros((LANES - n_blk, NSA_GQ), F32)], axis=0).astype(BF16)
    q_aux = _dot_nt(eye_ref[...], neg_t).astype(BF16)
    qx = jnp.concatenate([qp, jnp.concatenate([q_aux] * NSA_HPG, axis=0)], axis=1)

    def sel_scores(i):
        k0 = pl.multiple_of(i * NSA_KC, NSA_KC)
        kx = jnp.concatenate([ks_ref[pl.ds(k0, NSA_KC), :], et_ref[pl.ds(k0, NSA_KC), :]], axis=1)
        return _dot_nt(kx, qx)

    def sel_values(i):
        return vst_ref[:, pl.ds(pl.multiple_of(i * NSA_KC, NSA_KC), NSA_KC)]

    window = {}

    def after_first_scores():
        kpos = a0 + lax.broadcasted_iota(jnp.int32, (NSA_QB, 1), 0)
        l_d = _where_tiles(kpos <= t_l, s_d, NEG_INF)
        m_d = jnp.max(l_d, axis=0, keepdims=True)
        p_d = jnp.exp2(l_d - m_d)
        acc_scr[...] = _dot(vst_ref[:, pl.ds(a0, NSA_QB)], p_d.astype(BF16))
        e_w, window['l'] = _masked_softmax_cols(s_w, mask_w)
        window['acc'] = _dot(vwt_ref[:, pl.ds(start, NSA_WSPAN)], e_w.astype(BF16))
        return m_d, jnp.sum(p_d, axis=0, keepdims=True)

    n_prev = (s0 + NSA_KC - 1) // NSA_KC
    l_s = _flash_chunks(n_prev, sel_scores, sel_values, after_first_scores, sa_scr, sb_scr, acc_scr)

    gates = gate_ref[...]
    o_t = ((gates[0:1, :] * inv_c) * acc_c + (gates[1:2, :] / l_s) * acc_scr[...]
           + (gates[2:3, :] / window['l']) * window['acc'])
    for h in range(NSA_HPG):
        tt = o_t[:, h * NSA_GQ:(h + 1) * NSA_GQ].T
        o_ref[:, h * LANES:(h + 1) * LANES] = jnp.where(low, tt[0:NSA_QB], tt[NSA_QB:NSA_GQ])


def _nsa_attention(qa3, gexp, kc, vct, knsa3, vnsa_t, aggt, et, eye):
    B, S, _ = qa3.shape
    n_pad = kc.shape[1]
    n_cmp = (S - CMP_BLOCK) // CMP_STRIDE + 1
    n_blk = S // SEL_BLOCK
    n_sel = min(N_SELECT, n_blk)
    const = lambda shape: pl.BlockSpec(shape, lambda b, c: (0,) * len(shape))
    return pl.pallas_call(
        functools.partial(_nsa_kernel, n_cmp=n_cmp, n_blk=n_blk, n_sel=n_sel),
        grid=(B, S // NSA_QB),
        in_specs=[pl.BlockSpec((None, NSA_QB, NSA_WIDTH), lambda b, c: (b, c, 0)),
                  pl.BlockSpec((None, None, 3, NSA_ROWS), lambda b, c: (b, c, 0, 0)),
                  pl.BlockSpec((None, n_pad, KV_WIDTH), lambda b, c: (b, 0, 0)),
                  pl.BlockSpec((None, KV_WIDTH, n_pad), lambda b, c: (b, 0, 0)),
                  pl.BlockSpec((None, S, KV_WIDTH), lambda b, c: (b, 0, 0)),
                  pl.BlockSpec((None, S, KV_WIDTH), lambda b, c: (b, 0, 1)),
                  pl.BlockSpec((None, KV_WIDTH, S), lambda b, c: (b, 0, 0)),
                  pl.BlockSpec((None, KV_WIDTH, S), lambda b, c: (b, 1, 0)),
                  const((n_blk, n_pad)), const((S, LANES)), const((NSA_GQ, NSA_GQ))],
        out_specs=pl.BlockSpec((None, NSA_QB, NSA_WIDTH), lambda b, c: (b, c, 0)),
        out_shape=jax.ShapeDtypeStruct((B, S, NSA_WIDTH), F32),
        scratch_shapes=[pltpu.VMEM((NSA_KC, NSA_ROWS), F32), pltpu.VMEM((NSA_KC, NSA_ROWS), F32),
                        pltpu.VMEM((LANES, NSA_ROWS), F32)],
        compiler_params=_cparams(("parallel", "arbitrary")),
        name="nsa_attention",
    )(qa3, gexp, kc, vct, knsa3, knsa3, vnsa_t, vnsa_t, aggt, et, eye)


FOX_TQ = 512
FOX_KC = FOX_TQ
FOX_ROWS = FOX_PAIR * FOX_TQ


def _fox_kernel(q_ref, k_ref, aux_ref, vt_ref, o_ref, sa_scr, sb_scr, acc_scr):
    qi = pl.program_id(2)
    s0 = qi * FOX_TQ
    lane = lax.broadcasted_iota(jnp.int32, (FOX_TQ, LANES), 1)
    low = lane < HEAD_DIM
    q2 = q_ref[...]
    zero = jnp.zeros_like(q2)
    aux0 = jnp.where(lane < AUX_PIECES, -1.0, 0.0).astype(q2.dtype)
    aux1 = jnp.where((lane >= AUX_PIECES) & (lane < 2 * AUX_PIECES), -1.0, 0.0).astype(q2.dtype)
    qx = jnp.concatenate([jnp.concatenate([jnp.where(low, q2, zero), aux0], axis=1),
                          jnp.concatenate([jnp.where(low, zero, q2), aux1], axis=1)], axis=0)
    t_l = s0 + lax.broadcasted_iota(jnp.int32, (1, FOX_TQ), 1)

    def scores(i):
        k0 = pl.multiple_of(i * FOX_KC, FOX_KC)
        kx = jnp.concatenate([k_ref[pl.ds(k0, FOX_KC), :], aux_ref[pl.ds(k0, FOX_KC), :]], axis=1)
        return _dot_nt(kx, qx)

    def values(i):
        return vt_ref[:, pl.ds(pl.multiple_of(i * FOX_KC, FOX_KC), FOX_KC)]

    def diagonal():
        kpos = s0 + lax.broadcasted_iota(jnp.int32, (FOX_KC, 1), 0)
        s_d = _where_tiles(kpos <= t_l, scores(qi), NEG_INF)
        m_d = jnp.max(s_d, axis=0, keepdims=True)
        p_d = jnp.exp2(s_d - m_d)
        acc_scr[...] = _dot(values(qi), p_d.astype(BF16))
        return m_d, jnp.sum(p_d, axis=0, keepdims=True)

    l_f = _flash_chunks(qi, scores, values, diagonal, sa_scr, sb_scr, acc_scr)
    o_t = acc_scr[...] / l_f
    o_ref[...] = jnp.where(low, o_t[:, 0:FOX_TQ].T, o_t[:, FOX_TQ:FOX_ROWS].T)


def _fox_attention(fqk3, aux3, vfox_t):
    B, S, _ = fqk3.shape
    return pl.pallas_call(
        _fox_kernel,
        grid=(B, FOX_NPAIR, S // FOX_TQ),
        in_specs=[pl.BlockSpec((None, FOX_TQ, LANES), lambda b, h, i: (b, i, h)),
                  pl.BlockSpec((None, S, LANES), lambda b, h, i: (b, 0, FOX_NPAIR + h)),
                  pl.BlockSpec((None, S, LANES), lambda b, h, i: (b, 0, h)),
                  pl.BlockSpec((None, LANES, S), lambda b, h, i: (b, h, 0))],
        out_specs=pl.BlockSpec((None, FOX_TQ, LANES), lambda b, h, i: (b, i, h)),
        out_shape=jax.ShapeDtypeStruct((B, S, FOX_WIDTH), F32),
        scratch_shapes=[pltpu.VMEM((FOX_KC, FOX_ROWS), F32), pltpu.VMEM((FOX_KC, FOX_ROWS), F32),
                        pltpu.VMEM((LANES, FOX_ROWS), F32)],
        compiler_params=_cparams(("parallel", "parallel", "arbitrary")),
        name="fox_attention",
    )(fqk3, fqk3, aux3, vfox_t)


def _lane_max(x, mask):
    return jnp.max(jnp.where(mask, x, NEG_INF), axis=-1, keepdims=True)


def _first_lane_eq(x, v, mask, lane):
    return jnp.min(jnp.where(mask & (x == v), lane, LANES), axis=-1, keepdims=True)


def _route(logits):
    lane = lax.broadcasted_iota(jnp.int32, logits.shape, 1)
    gmask = lane < N_GROUPS
    gmax = _lane_max(logits, gmask)
    gexp = jnp.where(gmask, jnp.exp(logits - gmax), 0.0)
    g_star = _first_lane_eq(logits, gmax, gmask, lane)
    p_grp = 1.0 / jnp.sum(gexp, axis=-1, keepdims=True)
    e_lo = ROUTE_E_OFF + g_star * EXPERTS_PER_GROUP
    emask = (lane >= e_lo) & (lane < e_lo + EXPERTS_PER_GROUP)
    emax = _lane_max(logits, emask)
    eexp = jnp.where(emask, jnp.exp(logits - emax), 0.0)
    prob = eexp / jnp.sum(eexp, axis=-1, keepdims=True)
    v1 = _lane_max(prob, emask)
    i1 = _first_lane_eq(prob, v1, emask, lane)
    rest = emask & (lane != i1)
    v2 = _lane_max(prob, rest)
    i2 = _first_lane_eq(prob, v2, rest, lane)
    den = v1 + v2
    w1 = p_grp * (v1 / den)
    w2 = p_grp * (v2 / den)
    return (jnp.where(lane == i1, w1, 0.0) + jnp.where(lane == i2, w2, 0.0)
            + jnp.where(lane == ROUTE_G_LANE, g_star.astype(F32), 0.0))


def _outproj_kernel(on_ref, of_ref, x_ref, bn_ref, bf_ref, wn_ref, wf_ref, gffn_ref, wr_ref, br_ref,
                    h_ref, u_ref, comb_ref):
    mn = _rms(on_ref[...], bn_ref[...]).astype(BF16)
    mf = _rms(of_ref[...], bf_ref[...]).astype(BF16)
    h = x_ref[...] + (_dot(mn, wn_ref[...]) + _dot(mf, wf_ref[...]))
    h_ref[...] = h
    u = _rms(h, gffn_ref[...]).astype(BF16)
    u_ref[...] = u
    comb_ref[...] = _route(_dot(u, wr_ref[...]) + br_ref[...])


def _outproj(o_nsa, o_fox, x2, beta_n, beta_f, w_n, w_f, g_ffn, w_r, b_r, tm):
    T = x2.shape[0]
    full = lambda shape: pl.BlockSpec(shape, lambda i: (0,) * len(shape))
    row = lambda w: pl.BlockSpec((tm, w), lambda i: (i, 0))
    return pl.pallas_call(
        _outproj_kernel,
        grid=(T // tm,),
        in_specs=[row(NSA_WIDTH), row(FOX_WIDTH), row(D_MODEL), full((1, NSA_WIDTH)), full((1, FOX_WIDTH)),
                  full((NSA_WIDTH, D_MODEL)), full((FOX_WIDTH, D_MODEL)), full((1, D_MODEL)),
                  full((D_MODEL, LANES)), full((1, LANES))],
        out_specs=[row(D_MODEL), row(D_MODEL), row(LANES)],
        out_shape=[jax.ShapeDtypeStruct((T, D_MODEL), F32),
                   jax.ShapeDtypeStruct((T, D_MODEL), BF16),
                   jax.ShapeDtypeStruct((T, LANES), F32)],
        compiler_params=_cparams(("parallel",)),
        name="outproj_router",
    )(o_nsa, o_fox, x2, beta_n, beta_f, w_n, w_f, g_ffn, w_r, b_r)


MOE_TS = 1024
MOE_RB = 128
MOE_TSP = MOE_TS + N_GROUPS * MOE_RB
MOE_META = 8


def _moe_sort_kernel(u_ref, comb_ref, tri_ref, us_ref, cs_ref, pos_ref, meta_ref):
    comb = comb_ref[...]
    lane = lax.broadcasted_iota(jnp.int32, comb.shape, 1)
    grp = comb[:, ROUTE_G_LANE:ROUTE_G_LANE + 1].astype(jnp.int32)
    onehot = jnp.where((lane == grp) & (lane < N_GROUPS), 1.0, 0.0)
    rank = _dot(tri_ref[...], onehot.astype(BF16))
    counts = rank[MOE_TS - 1:MOE_TS, :]
    nblk = jnp.floor((counts + (MOE_RB - 1)) * (1.0 / MOE_RB))
    lane1 = lane[0:1, :]
    blk0 = jnp.zeros_like(nblk)
    run = jnp.zeros((1, 1), F32)
    for g in range(1, N_GROUPS):
        run = run + nblk[:, g - 1:g]
        blk0 = blk0 + jnp.where(lane1 == g, run, 0.0)
    val = onehot * (blk0 * MOE_RB + rank - 1.0)
    pos_col = jnp.sum(val, axis=-1, keepdims=True)
    pos_ref[...] = pos_col.astype(jnp.int32)
    ones = jnp.ones((SUBLANES, LANES), BF16)
    hi, mid, lo = _split3(val)
    pos_row = (_dot_nt(ones, hi) + _dot_nt(ones, mid) + _dot_nt(ones, lo))[0:1, :].astype(jnp.int32)
    r_idx = lax.broadcasted_iota(jnp.int32, (MOE_TSP, MOE_TS), 0)
    perm = jnp.where(r_idx == pos_row, 1.0, 0.0).astype(BF16)
    us_ref[...] = _dot(perm, u_ref[...]).astype(us_ref.dtype)
    cs_ref[...] = _split3_dot_lhs(perm, comb)
    meta = blk0 + pltpu.roll(nblk, N_GROUPS, 1)
    meta_ref[...] = jnp.where(lane1 < MOE_META, meta, 0.0).astype(jnp.int32)


def _moe_sort(u, comb, tri):
    T = u.shape[0]
    nt = T // MOE_TS
    return pl.pallas_call(
        _moe_sort_kernel,
        grid=(nt,),
        in_specs=[pl.BlockSpec((MOE_TS, D_MODEL), lambda i: (i, 0)),
                  pl.BlockSpec((MOE_TS, LANES), lambda i: (i, 0)),
                  pl.BlockSpec((MOE_TS, MOE_TS), lambda i: (0, 0))],
        out_specs=[pl.BlockSpec((MOE_TSP, D_MODEL), lambda i: (i, 0)),
                   pl.BlockSpec((MOE_TSP, LANES), lambda i: (i, 0)),
                   pl.BlockSpec((MOE_TS, 1), lambda i: (i, 0)),
                   pl.BlockSpec((None, 1, LANES), lambda i: (i, 0, 0))],
        out_shape=[jax.ShapeDtypeStruct((nt * MOE_TSP, D_MODEL), BF16),
                   jax.ShapeDtypeStruct((nt * MOE_TSP, LANES), F32),
                   jax.ShapeDtypeStruct((T, 1), jnp.int32),
                   jax.ShapeDtypeStruct((nt, 1, LANES), jnp.int32)],
        compiler_params=_cparams(("parallel",)),
        name="moe_sort",
    )(u, comb, tri)


def _moe_kernel(meta_ref, us_ref, cs_ref, pos_ref, h_ref, wgu_ref, wd_ref, o_ref, acc_ref):
    i = pl.program_id(0)
    e = pl.program_id(1)
    g = e // EXPERTS_PER_GROUP

    @pl.when(e == 0)
    def _():
        acc_ref[...] = jnp.zeros_like(acc_ref)

    first = meta_ref[i * MOE_META + g]
    count = meta_ref[i * MOE_META + N_GROUPS + g]

    def blocks(j0, n):
        rows = [pl.multiple_of((first + j0 + k) * MOE_RB, MOE_RB) for k in range(n)]
        gus = [_dot(us_ref[pl.ds(r0, MOE_RB), :], wgu_ref[...]) for r0 in rows]
        for r0, gu in zip(rows, gus):
            cs = cs_ref[pl.ds(r0, MOE_RB), :]
            lane = lax.broadcasted_iota(jnp.int32, cs.shape, 1)
            ce = jnp.sum(jnp.where(lane == ROUTE_E_OFF + e, cs, 0.0), axis=-1, keepdims=True)
            gt = gu[:, 0:D_EXPERT]
            hid = gt * (1.0 / (1.0 + jnp.exp(-gt))) * gu[:, D_EXPERT:2 * D_EXPERT]
            acc_ref[pl.ds(r0, MOE_RB), :] += _dot((ce * hid).astype(BF16), wd_ref[...])

    def pair(j, carry):
        blocks(2 * j, 2)
        return carry

    def single(_, carry):
        blocks(count - 1, 1)
        return carry

    lax.fori_loop(0, count // 2, pair, 0)
    lax.fori_loop(0, count & 1, single, 0)

    @pl.when(e == N_EXPERTS - 1)
    def _():
        r_idx = lax.broadcasted_iota(jnp.int32, (MOE_TS, MOE_TSP), 1)
        unperm = jnp.where(r_idx == pos_ref[...], 1.0, 0.0).astype(BF16)
        o_ref[...] = h_ref[...] + _dot(unperm, acc_ref[...].astype(BF16))


def _moe(meta, us, cs, pos, h1, wgu, wd):
    T = h1.shape[0]
    grid_spec = pltpu.PrefetchScalarGridSpec(
        num_scalar_prefetch=1,
        grid=(T // MOE_TS, N_EXPERTS),
        in_specs=[pl.BlockSpec((MOE_TSP, D_MODEL), lambda i, e, m: (i, 0)),
                  pl.BlockSpec((MOE_TSP, LANES), lambda i, e, m: (i, 0)),
                  pl.BlockSpec((MOE_TS, 1), lambda i, e, m: (i, 0)),
                  pl.BlockSpec((MOE_TS, D_MODEL), lambda i, e, m: (i, 0)),
                  pl.BlockSpec((None, D_MODEL, 2 * D_EXPERT), lambda i, e, m: (e, 0, 0)),
                  pl.BlockSpec((None, D_EXPERT, D_MODEL), lambda i, e, m: (e, 0, 0))],
        out_specs=pl.BlockSpec((MOE_TS, D_MODEL), lambda i, e, m: (i, 0)),
        scratch_shapes=[pltpu.VMEM((MOE_TSP, D_MODEL), F32)])
    return pl.pallas_call(
        _moe_kernel,
        grid_spec=grid_spec,
        out_shape=jax.ShapeDtypeStruct((T, D_MODEL), F32),
        compiler_params=_cparams(("parallel", "arbitrary")),
        name="moe_experts",
    )(meta, us, cs, pos, h1, wgu, wd)


def _ple_kernel(h_ref, p_ref, gple_ref, wg_ref, bg_ref, wp_ref, gfin_ref, o_ref):
    h = h_ref[...]
    v = _rms(h, gple_ref[...]).astype(BF16)
    z = _dot(v, wg_ref[...]) + bg_ref[...]
    gate = 1.0 / (1.0 + jnp.exp(-z))
    proj = _dot(p_ref[...].astype(BF16), wp_ref[...])
    o_ref[...] = _rms(h + gate * proj, gfin_ref[...])


def _ple(h2, p2, g_ple, w_g, b_g, w_p, g_final, tm):
    T = h2.shape[0]
    full = lambda shape: pl.BlockSpec(shape, lambda i: (0,) * len(shape))
    row = lambda w: pl.BlockSpec((tm, w), lambda i: (i, 0))
    return pl.pallas_call(
        _ple_kernel,
        grid=(T // tm,),
        in_specs=[row(D_MODEL), row(PLE_DIM), full((1, D_MODEL)), full((D_MODEL, D_MODEL)),
                  full((1, D_MODEL)), full((PLE_DIM, D_MODEL)), full((1, D_MODEL))],
        out_specs=row(D_MODEL),
        out_shape=jax.ShapeDtypeStruct((T, D_MODEL), F32),
        compiler_params=_cparams(("parallel",)),
        name="ple_final",
    )(h2, p2, g_ple, w_g, b_g, w_p, g_final)


def _rope_lane_tables():
    half = ROPE_HALF
    inv_freq = jnp.power(jnp.float32(ROPE_THETA), -jnp.arange(half, dtype=jnp.float32) / half)
    j = np.arange(LANES) % HEAD_DIM
    freq = jnp.where(jnp.asarray(j < ROPE_DIM), inv_freq[jnp.asarray(j % half)], 0.0).reshape(1, LANES)
    mlo = jnp.asarray(np.where(j < half, -1.0, 0.0), F32).reshape(1, LANES)
    mhi = jnp.asarray(np.where((j >= half) & (j < ROPE_DIM), 1.0, 0.0), F32).reshape(1, LANES)
    return freq.astype(F32), mlo, mhi


def _block_diag2(w):
    z = jnp.zeros_like(w)
    return jnp.concatenate([jnp.concatenate([w, z], axis=-1), jnp.concatenate([z, w], axis=-1)], axis=-2)


def _aux_route_table():
    r = np.zeros((AUX_PIECES, LANES, FOX_NPAIR * LANES), np.float32)
    for head in range(FOX_HEADS):
        for k in range(AUX_PIECES):
            r[k, SMALL_F_OFF + head, (head // FOX_PAIR) * LANES + (head % FOX_PAIR) * AUX_PIECES + k] = 1.0
    return jnp.asarray(r, BF16)


def _layer(h3, p3, positions, prm, g_final):
    B, S, _ = h3.shape
    T = B * S
    tm = 512
    x2 = h3.reshape(T, D_MODEL)
    freq, mlo, mhi = _rope_lane_tables()

    w_in = prm['w_in']
    offs = np.cumsum([0, NSA_WIDTH, KV_WIDTH, KV_WIDTH, KV_WIDTH, KV_WIDTH, KV_WIDTH, KV_WIDTH,
                      N_GATE, FOX_WIDTH, FOX_WIDTH, FOX_WIDTH, FOX_HEADS])
    seg = lambda k: w_in[:, offs[k]:offs[k + 1]]
    wq = seg(0).reshape(D_MODEL, NSA_GROUPS, NSA_HPG, HEAD_DIM).transpose(0, 2, 1, 3).reshape(D_MODEL, NSA_WIDTH)
    pad = jnp.zeros((D_MODEL, LANES - N_GATE - FOX_HEADS), w_in.dtype)
    w_all = jnp.concatenate([wq, seg(1), seg(2), seg(3), seg(5), seg(8), seg(9), seg(7), seg(11), pad],
                            axis=1).astype(BF16)
    wt_v = jnp.concatenate([seg(4), seg(6), seg(10)], axis=1).T.astype(BF16)
    bias_small = jnp.concatenate([prm['b_nsa_gate'], prm['b_forget'],
                                  jnp.zeros((LANES - N_GATE - FOX_HEADS,), F32)]).reshape(1, LANES)

    qa, cmp_tok, knsa, vnsa_t, fqk, vfox_t, small = _inproj(
        x2, positions.reshape(T, 1), prm['g_mix'].reshape(1, D_MODEL), w_all, wt_v, bias_small,
        freq, mlo, mhi, tm, B, S)

    tri = jnp.asarray(np.tril(np.ones((CUM_CHUNK, CUM_CHUNK), np.float32)), BF16)
    small3 = small.reshape(B, S, LANES)
    aux3 = _cumsum(small3, tri, _aux_route_table())

    n_rows = S // CMP_STRIDE
    tile2 = lambda pe: jnp.concatenate([pe, pe], axis=-1)
    bd1 = lambda w: _block_diag2(w.reshape(CMP_BLOCK, HEAD_DIM, CMP_HIDDEN)).astype(BF16)
    pos_cmp = positions[:, CMP_BLOCK - 1::CMP_STRIDE]
    pos_cmp = jnp.pad(pos_cmp, ((0, 0), (0, n_rows - pos_cmp.shape[1]))).reshape(B, n_rows, 1)
    kc, vct = _compress(cmp_tok.reshape(B, S, 2 * KV_WIDTH), tile2(prm['pe_cmp_k']), tile2(prm['pe_cmp_v']),
                        bd1(prm['w_cmp_k1']), bd1(prm['w_cmp_v1']),
                        _block_diag2(prm['w_cmp_k2']).astype(BF16), _block_diag2(prm['w_cmp_v2']).T.astype(BF16),
                        pos_cmp, freq, mlo, mhi)

    n_cmp = (S - CMP_BLOCK) // CMP_STRIDE + 1
    n_blk = S // SEL_BLOCK
    cs = np.arange(n_rows)[:, None] * CMP_STRIDE
    ss = np.arange(n_blk)[None, :] * SEL_BLOCK
    ov = np.clip(np.minimum(cs + CMP_BLOCK, ss + SEL_BLOCK) - np.maximum(cs, ss), 0, None) / CMP_BLOCK
    ov[n_cmp:] = 0.0
    aggt = jnp.asarray(ov.T, BF16)
    et = jnp.asarray((np.arange(S)[:, None] // SEL_BLOCK) == np.arange(LANES)[None, :], BF16)
    eye = jnp.asarray(np.eye(NSA_GQ, dtype=np.float32), BF16)
    gexp = small3[:, :, :N_GATE].reshape(B, S // NSA_QB, NSA_QB, NSA_GROUPS, NSA_HPG, 3)
    gexp = gexp.transpose(0, 1, 5, 4, 3, 2).reshape(B, S // NSA_QB, 3, NSA_ROWS)
    o_nsa = _nsa_attention(qa.reshape(B, S, NSA_WIDTH), gexp, kc, vct, knsa.reshape(B, S, 2 * KV_WIDTH),
                           vnsa_t, aggt, et, eye)

    o_fox = _fox_attention(fqk.reshape(B, S, 2 * FOX_WIDTH), aux3, vfox_t)

    perm = lambda a: a.reshape(NSA_GROUPS, NSA_HPG, HEAD_DIM, -1).transpose(1, 0, 2, 3).reshape(NSA_WIDTH, -1)
    beta_n = perm(prm['beta_nsa'].reshape(NSA_WIDTH, 1)).reshape(1, NSA_WIDTH)
    w_out = prm['w_out']
    w_n = perm(w_out[:NSA_WIDTH]).astype(BF16)
    w_f = w_out[NSA_WIDTH:].astype(BF16)
    w_r = jnp.concatenate([prm['w_group'], prm['w_router'],
                           jnp.zeros((D_MODEL, LANES - N_GROUPS - N_EXPERTS), F32)], axis=1).astype(BF16)
    b_r = jnp.concatenate([prm['b_group'], prm['b_router'],
                           jnp.zeros((LANES - N_GROUPS - N_EXPERTS,), F32)]).reshape(1, LANES)
    h1, u, comb = _outproj(o_nsa.reshape(T, NSA_WIDTH), o_fox.reshape(T, FOX_WIDTH), x2, beta_n,
                           prm['beta_fox'].reshape(1, FOX_WIDTH), w_n, w_f,
                           prm['g_ffn'].reshape(1, D_MODEL), w_r, b_r, tm)

    wgu = jnp.concatenate([prm['w_gate_e'], prm['w_up_e']], axis=-1).astype(BF16)
    tri_s = jnp.asarray(np.tril(np.ones((MOE_TS, MOE_TS), np.float32)), BF16)
    us, cs, pos, meta = _moe_sort(u, comb, tri_s)
    h2 = _moe(meta[:, 0, :MOE_META].reshape(-1), us, cs, pos, h1, wgu, prm['w_down_e'].astype(BF16))

    out = _ple(h2, p3.reshape(T, PLE_DIM), prm['g_ple'].reshape(1, D_MODEL), prm['w_ple_gate'].astype(BF16),
               prm['b_ple_gate'].reshape(1, D_MODEL), prm['w_ple_proj'].astype(BF16),
               g_final.reshape(1, D_MODEL), tm)
    return out.reshape(B, S, D_MODEL)


_PARAM_NAMES = ('g_mix', 'w_in', 'b_nsa_gate', 'b_forget', 'pe_cmp_k', 'w_cmp_k1', 'w_cmp_k2',
                'pe_cmp_v', 'w_cmp_v1', 'w_cmp_v2', 'beta_nsa', 'beta_fox', 'w_out', 'g_ffn',
                'w_group', 'b_group', 'w_router', 'b_router', 'w_gate_e', 'w_up_e', 'w_down_e',
                'g_ple', 'w_ple_gate', 'b_ple_gate', 'w_ple_proj')


def kernel(x, p, positions, g_mix, w_in, b_nsa_gate, b_forget, pe_cmp_k, w_cmp_k1, w_cmp_k2, pe_cmp_v,
           w_cmp_v1, w_cmp_v2, beta_nsa, beta_fox, w_out, g_ffn, w_group, b_group, w_router, b_router,
           w_gate_e, w_up_e, w_down_e, g_ple, w_ple_gate, b_ple_gate, w_ple_proj, g_final):
    stacked = (g_mix, w_in, b_nsa_gate, b_forget, pe_cmp_k, w_cmp_k1, w_cmp_k2, pe_cmp_v, w_cmp_v1,
               w_cmp_v2, beta_nsa, beta_fox, w_out, g_ffn, w_group, b_group, w_router, b_router,
               w_gate_e, w_up_e, w_down_e, g_ple, w_ple_gate, b_ple_gate, w_ple_proj)
    depth = w_in.shape[0]
    assert depth == 1, "the final norm is fused into the last layer's embedding kernel"
    prm = {n: a[0] for n, a in zip(_PARAM_NAMES, stacked)}
    return _layer(x, p[0], positions, prm, g_final)
```

```python
import functools
import math

import numpy as np
import jax
import jax.numpy as jnp
from jax import lax
from jax.experimental import pallas as pl
from jax.experimental.pallas import tpu as pltpu

D_MODEL = 1024
HEAD_DIM = 64
NSA_HEADS = 8
FOX_HEADS = 8
NSA_GROUPS = 2
NSA_HPG = 4
NSA_WIDTH = 512
FOX_WIDTH = 512
KV_WIDTH = 128
CMP_BLOCK = 32
CMP_STRIDE = 16
CMP_HIDDEN = 128
SEL_BLOCK = 64
N_SELECT = 16
WINDOW = 512
ROPE_THETA = 500000.0
ROPE_DIM = 16
ROPE_HALF = 8
N_GROUPS = 4
EXPERTS_PER_GROUP = 4
N_EXPERTS = 16
D_EXPERT = 512
PLE_DIM = 256
EPS = 1e-6
NEG_INF = -1e30
FORCE_BONUS = 1e4
LOG2E = math.log2(math.e)
Q_SCALE = 0.125 * LOG2E
NEG_BIG = -(2.0 ** 100)

LANES = 128
SUBLANES = 8
N_GATE = 3 * NSA_HEADS
SMALL_F_OFF = N_GATE
ROUTE_E_OFF = N_GROUPS
ROUTE_G_LANE = 0

VMEM_LIMIT = 56 * 1024 * 1024

F32 = jnp.float32
BF16 = jnp.bfloat16


def _cparams(sem):
    return pltpu.CompilerParams(dimension_semantics=sem, vmem_limit_bytes=VMEM_LIMIT)


def _dot(a, b):
    return jnp.dot(a, b, preferred_element_type=F32)


def _dot_nt(a, b):
    return lax.dot_general(a, b, (((1,), (1,)), ((), ())), preferred_element_type=F32)


def _split3(x):
    hi = x.astype(BF16)
    r1 = x - hi.astype(F32)
    mid = r1.astype(BF16)
    lo = (r1 - mid.astype(F32)).astype(BF16)
    return hi, mid, lo


def _split3_dot_lhs(e, x):
    hi, mid, lo = _split3(x)
    return _dot(e, hi) + _dot(e, mid) + _dot(e, lo)


def _rms(x, g):
    return x * lax.rsqrt(jnp.mean(x * x, axis=-1, keepdims=True) + EPS) * g


def _rope(r, cos, sin_lo, sin_hi):
    return (r * cos + pltpu.roll(r, LANES - ROPE_HALF, 1) * sin_lo
            + pltpu.roll(r, ROPE_HALF, 1) * sin_hi)


def _rope_tables(pos_col, freq, mlo, mhi):
    ang = pos_col.astype(F32) * freq
    cos = jnp.cos(ang)
    sin = jnp.sin(ang)
    return cos, sin * mlo, sin * mhi


def _inproj_kernel(x_ref, pos_ref, g_ref, w_ref, wt_ref, bias_ref, freq_ref, mlo_ref, mhi_ref,
                   qa_ref, cmp_ref, knsa_ref, vnsa_ref, fqk_ref, vfox_ref, small_ref):
    x = x_ref[...]
    hb = _rms(x, g_ref[...]).astype(BF16)
    cos, s_lo, s_hi = _rope_tables(pos_ref[...], freq_ref[...], mlo_ref[...], mhi_ref[...])
    off = 0
    r = _dot(hb, w_ref[:, off:off + NSA_WIDTH])
    for c in range(NSA_WIDTH // LANES):
        rc = _rope(r[:, c * LANES:(c + 1) * LANES], cos, s_lo, s_hi) * Q_SCALE
        qa_ref[:, c * LANES:(c + 1) * LANES] = rc.astype(qa_ref.dtype)
    off += NSA_WIDTH
    cmp_ref[...] = _dot(hb, w_ref[:, off:off + 2 * KV_WIDTH])
    off += 2 * KV_WIDTH
    r = _dot(hb, w_ref[:, off:off + 2 * KV_WIDTH])
    for c in range(2):
        rc = _rope(r[:, c * LANES:(c + 1) * LANES], cos, s_lo, s_hi)
        knsa_ref[:, c * LANES:(c + 1) * LANES] = rc.astype(knsa_ref.dtype)
    off += 2 * KV_WIDTH
    r = _dot(hb, w_ref[:, off:off + FOX_WIDTH])
    fqk_ref[:, 0:FOX_WIDTH] = (r * Q_SCALE).astype(fqk_ref.dtype)
    off += FOX_WIDTH
    r = _dot(hb, w_ref[:, off:off + FOX_WIDTH])
    fqk_ref[:, FOX_WIDTH:2 * FOX_WIDTH] = r.astype(fqk_ref.dtype)
    off += FOX_WIDTH
    z = _dot(hb, w_ref[:, off:off + LANES]) + bias_ref[...]
    lane = lax.broadcasted_iota(jnp.int32, z.shape, 1)
    sig = 1.0 / (1.0 + jnp.exp(-z))
    logsig = jnp.minimum(z, 0.0) - jnp.log(1.0 + jnp.exp(-jnp.abs(z)))
    small_ref[...] = jnp.where(lane < SMALL_F_OFF, sig, logsig)
    vt = _dot_nt(wt_ref[...], hb)
    vnsa_ref[...] = vt[0:2 * KV_WIDTH].astype(vnsa_ref.dtype)
    vfox_ref[...] = vt[2 * KV_WIDTH:2 * KV_WIDTH + FOX_WIDTH].astype(vfox_ref.dtype)


def _inproj(x2, pos2, g_mix, w_all, wt_v, bias_small, freq, mlo, mhi, tm, B, S):
    T = x2.shape[0]
    nt = S // tm
    full = lambda shape: pl.BlockSpec(shape, lambda i: (0,) * len(shape))
    row = lambda w: pl.BlockSpec((tm, w), lambda i: (i, 0))
    tr = lambda w: pl.BlockSpec((None, w, tm), lambda i: (i // nt, 0, i % nt))
    return pl.pallas_call(
        _inproj_kernel,
        grid=(T // tm,),
        in_specs=[row(D_MODEL), row(1), full((1, D_MODEL)), full(w_all.shape), full(wt_v.shape),
                  full((1, LANES)), full((1, LANES)), full((1, LANES)), full((1, LANES))],
        out_specs=[row(NSA_WIDTH), row(2 * KV_WIDTH), row(2 * KV_WIDTH), tr(2 * KV_WIDTH),
                   row(2 * FOX_WIDTH), tr(FOX_WIDTH), row(LANES)],
        out_shape=[jax.ShapeDtypeStruct((T, NSA_WIDTH), BF16),
                   jax.ShapeDtypeStruct((T, 2 * KV_WIDTH), F32),
                   jax.ShapeDtypeStruct((T, 2 * KV_WIDTH), BF16),
                   jax.ShapeDtypeStruct((B, 2 * KV_WIDTH, S), BF16),
                   jax.ShapeDtypeStruct((T, 2 * FOX_WIDTH), BF16),
                   jax.ShapeDtypeStruct((B, FOX_WIDTH, S), BF16),
                   jax.ShapeDtypeStruct((T, LANES), F32)],
        compiler_params=_cparams(("parallel",)),
        name="inproj",
    )(x2, pos2, g_mix, w_all, wt_v, bias_small, freq, mlo, mhi)


CUM_CHUNK = 1024
FOX_PAIR = 2
FOX_NPAIR = FOX_HEADS // FOX_PAIR
AUX_PIECES = 3


def _cumsum_kernel(x_ref, tri_ref, route_ref, o_ref, carry_ref):
    @pl.when(pl.program_id(1) == 0)
    def _():
        carry_ref[...] = jnp.zeros_like(carry_ref)

    c = _split3_dot_lhs(tri_ref[...], x_ref[...]) + carry_ref[...]
    carry_ref[...] = c[CUM_CHUNK - 1:CUM_CHUNK, :]
    hi, mid, lo = _split3(c * LOG2E)
    aux = _dot(hi, route_ref[0]) + _dot(mid, route_ref[1]) + _dot(lo, route_ref[2])
    o_ref[...] = aux.astype(o_ref.dtype)


def _cumsum(small3, tri, route):
    B, S, _ = small3.shape
    width = FOX_NPAIR * LANES
    return pl.pallas_call(
        _cumsum_kernel,
        grid=(B, S // CUM_CHUNK),
        in_specs=[pl.BlockSpec((None, CUM_CHUNK, LANES), lambda b, i: (b, i, 0)),
                  pl.BlockSpec((CUM_CHUNK, CUM_CHUNK), lambda b, i: (0, 0)),
                  pl.BlockSpec((AUX_PIECES, LANES, width), lambda b, i: (0, 0, 0))],
        out_specs=pl.BlockSpec((None, CUM_CHUNK, width), lambda b, i: (b, i, 0)),
        out_shape=jax.ShapeDtypeStruct((B, S, width), BF16),
        scratch_shapes=[pltpu.VMEM((1, LANES), F32)],
        compiler_params=_cparams(("parallel", "arbitrary")),
        name="forget_cumsum",
    )(small3, tri, route)


def _compress_kernel(tokk_ref, tokv_ref, pek_ref, pev_ref, bdk1_ref, bdv1_ref, bdk2_ref, bdv2t_ref,
                     pos_ref, freq_ref, mlo_ref, mhi_ref, kc_ref, vct_ref, *, n_rows):
    half = CMP_BLOCK // 2
    ak = jnp.zeros((n_rows, 2 * CMP_HIDDEN), F32)
    bk = jnp.zeros((n_rows, 2 * CMP_HIDDEN), F32)
    av = jnp.zeros((n_rows, 2 * CMP_HIDDEN), F32)
    bv = jnp.zeros((n_rows, 2 * CMP_HIDDEN), F32)
    for l in range(half):
        xk = tokk_ref[pl.ds(l, n_rows, stride=CMP_STRIDE), :]
        xv = tokv_ref[pl.ds(l, n_rows, stride=CMP_STRIDE), :]
        ak = ak + _dot((xk + pek_ref[l:l + 1, :]).astype(BF16), bdk1_ref[l])
        bk = bk + _dot((xk + pek_ref[half + l:half + l + 1, :]).astype(BF16), bdk1_ref[half + l])
        av = av + _dot((xv + pev_ref[l:l + 1, :]).astype(BF16), bdv1_ref[l])
        bv = bv + _dot((xv + pev_ref[half + l:half + l + 1, :]).astype(BF16), bdv1_ref[half + l])
    hk = ak + pltpu.roll(bk, n_rows - 1, 0)
    hv = av + pltpu.roll(bv, n_rows - 1, 0)
    hk = hk * (1.0 / (1.0 + jnp.exp(-hk)))
    hv = hv * (1.0 / (1.0 + jnp.exp(-hv)))
    kc = _dot(hk.astype(BF16), bdk2_ref[...])
    cos, s_lo, s_hi = _rope_tables(pos_ref[...], freq_ref[...], mlo_ref[...], mhi_ref[...])
    kc_ref[...] = _rope(kc, cos, s_lo, s_hi).astype(kc_ref.dtype)
    vct_ref[...] = _dot_nt(bdv2t_ref[...], hv.astype(BF16)).astype(vct_ref.dtype)


def _compress(cmp_tok3, pek, pev, bdk1, bdv1, bdk2, bdv2t, pos_cmp, freq, mlo, mhi):
    B, S, _ = cmp_tok3.shape
    n_rows = S // CMP_STRIDE
    full = lambda shape: pl.BlockSpec(shape, lambda b: (0,) * len(shape))
    return pl.pallas_call(
        functools.partial(_compress_kernel, n_rows=n_rows),
        grid=(B,),
        in_specs=[pl.BlockSpec((None, S, KV_WIDTH), lambda b: (b, 0, 0)),
                  pl.BlockSpec((None, S, KV_WIDTH), lambda b: (b, 0, 1)),
                  full((CMP_BLOCK, KV_WIDTH)), full((CMP_BLOCK, KV_WIDTH)),
                  full((CMP_BLOCK, KV_WIDTH, 2 * CMP_HIDDEN)), full((CMP_BLOCK, KV_WIDTH, 2 * CMP_HIDDEN)),
                  full((2 * CMP_HIDDEN, KV_WIDTH)), full((KV_WIDTH, 2 * CMP_HIDDEN)),
                  pl.BlockSpec((None, n_rows, 1), lambda b: (b, 0, 0)),
                  full((1, LANES)), full((1, LANES)), full((1, LANES))],
        out_specs=[pl.BlockSpec((None, n_rows, KV_WIDTH), lambda b: (b, 0, 0)),
                   pl.BlockSpec((None, KV_WIDTH, n_rows), lambda b: (b, 0, 0))],
        out_shape=[jax.ShapeDtypeStruct((B, n_rows, KV_WIDTH), BF16),
                   jax.ShapeDtypeStruct((B, KV_WIDTH, n_rows), BF16)],
        compiler_params=_cparams(("parallel",)),
        name="nsa_compress",
    )(cmp_tok3, cmp_tok3, pek, pev, bdk1, bdv1, bdk2, bdv2t, pos_cmp, freq, mlo, mhi)


NSA_QB = 128
NSA_BPQ = NSA_QB // SEL_BLOCK
assert NSA_BPQ in (1, 2) and NSA_QB % LANES == 0
NSA_KC = 512
NSA_GQ = NSA_GROUPS * NSA_QB
NSA_ROWS = NSA_HPG * NSA_GQ
NSA_WSPAN = WINDOW + NSA_QB


def _where_tiles(mask, x, other):
    w = mask.shape[1]
    return jnp.concatenate([jnp.where(mask, x[:, t * w:(t + 1) * w], other)
                            for t in range(x.shape[1] // w)], axis=1)


def _masked_softmax_cols(s, mask):
    l = _where_tiles(mask, s, NEG_INF)
    m = jnp.max(l, axis=0, keepdims=True)
    e = jnp.exp2(l - m)
    return e, jnp.sum(e, axis=0, keepdims=True)


FLASH_NSUB = 2


def _flash_chunks(n_chunks, score_fn, vt_fn, init_fn, s_a, s_b, acc_ref):
    last = jnp.maximum(n_chunks - 1, 0)
    sub = s_a.shape[0] // FLASH_NSUB

    def stage(i_prod, s_prod, i_cons, s_cons, mx, m, l):
        if i_cons is not None:
            m_new = jnp.maximum(m, mx)
            alpha = jnp.exp2(m - m_new)
            l = alpha * l
        if i_prod is not None:
            i_prod = jnp.minimum(i_prod, last)
        mx_prod, pv = None, None
        for t in range(FLASH_NSUB):
            rows = pl.ds(t * sub, sub)
            if i_prod is not None:
                s = score_fn(i_prod, t)
                s_prod[rows, :] = s
                mx_t = jnp.max(s, axis=0, keepdims=True)
                mx_prod = mx_t if mx_prod is None else jnp.maximum(mx_prod, mx_t)
            if i_cons is not None:
                p = jnp.exp2(s_cons[rows, :] - m_new)
                l = l + jnp.sum(p, axis=0, keepdims=True)
                d = _dot(vt_fn(i_cons, t), p.astype(BF16))
                pv = d if pv is None else pv + d
        if i_cons is not None:
            acc_ref[...] = alpha * acc_ref[...] + pv
            m = m_new
        return mx_prod, m, l

    def pair(j, carry):
        mx_a, m, l = carry
        i = 2 * j
        mx_b, m, l = stage(i + 1, s_b, i, s_a, mx_a, m, l)
        mx_a, m, l = stage(i + 2, s_a, i + 1, s_b, mx_b, m, l)
        return mx_a, m, l

    def tail(_, carry):
        mx_a, m, l = carry
        _, m, l = stage(None, None, last, s_a, mx_a, m, l)
        return mx_a, m, l

    mx_a, _, _ = stage(0, s_a, None, None, None, None, None)
    m, l = init_fn()
    carry = (mx_a, m, l)
    carry = lax.fori_loop(0, n_chunks // 2, pair, carry)
    carry = lax.fori_loop(0, n_chunks & 1, tail, carry)
    return carry[2]


def _nsa_kernel(q_ref, gate_ref, kc_ref, vct_ref, ks_ref, kw_ref, vst_ref, vwt_ref, aggt_ref, et_ref,
                eye_ref, o_ref, sa_scr, sb_scr, acc_scr, *, n_cmp, n_blk, n_sel):
    c = pl.program_id(1)
    s0 = c * NSA_QB
    lane = lax.broadcasted_iota(jnp.int32, (NSA_QB, LANES), 1)
    low = lane < HEAD_DIM
    qb = q_ref[...]
    zero = jnp.zeros((NSA_QB, LANES), qb.dtype)
    pieces = []
    for h in range(NSA_HPG):
        qh = qb[:, h * LANES:(h + 1) * LANES]
        pieces.append(jnp.where(low, qh, zero))
        pieces.append(jnp.where(low, zero, qh))
    qp = jnp.concatenate(pieces, axis=0)
    t_l = s0 + (lax.broadcasted_iota(jnp.int32, (1, NSA_GQ), 1) & (NSA_QB - 1))
    cur_l = t_l // SEL_BLOCK
    first_blk = c * NSA_BPQ
    last_blk = first_blk + NSA_BPQ - 1

    n_pad = kc_ref.shape[0]
    n_s = lax.broadcasted_iota(jnp.int32, (n_pad, 1), 0)
    mask_c = ((n_s * CMP_STRIDE + (CMP_BLOCK - 1)) <= t_l) & (n_s < n_cmp)
    start = pl.multiple_of(jnp.maximum(s0 - WINDOW, 0), LANES)
    rel = t_l - (start + lax.broadcasted_iota(jnp.int32, (NSA_WSPAN, 1), 0))
    mask_w = (rel >= 0) & (rel < WINDOW)
    s_c = _dot_nt(kc_ref[...], qp)
    s_w = _dot_nt(kw_ref[pl.ds(start, NSA_WSPAN), :], qp)
    e_c, sum_c = _masked_softmax_cols(s_c, mask_c)
    inv_c = _where_tiles(t_l >= CMP_BLOCK - 1, 1.0 / sum_c, 0.0)
    acc_c = _dot(vct_ref[...], e_c.astype(BF16))

    j_s = lax.broadcasted_iota(jnp.int32, (n_blk, NSA_GQ), 0)
    nv = n_blk // SUBLANES

    def ranked_mask():
        p_c = e_c * inv_c
        psum = p_c[:, 0:NSA_GQ]
        for h in range(1, NSA_HPG):
            psum = psum + p_c[:, h * NSA_GQ:(h + 1) * NSA_GQ]
        imp = _split3_dot_lhs(aggt_ref[...], psum)
        forced = (j_s == 0) | (j_s == cur_l) | (j_s == cur_l - 1)
        score = jnp.where(j_s > cur_l, -1.0, imp + jnp.where(forced, FORCE_BONUS, 0.0))
        sc = [score[SUBLANES * v:SUBLANES * (v + 1), :] for v in range(nv)]
        sub = lax.broadcasted_iota(jnp.int32, (SUBLANES, NSA_GQ), 0)

        def count_group(vi, rk):
            rk = list(rk)
            for ri in range(SUBLANES):
                row = sc[vi][ri:ri + 1, :]
                for v in range(nv):
                    if v < vi:
                        beats = jnp.where(row > sc[v], 1.0, 0.0)
                    elif v > vi:
                        beats = jnp.where(row >= sc[v], 1.0, 0.0)
                    else:
                        beats = jnp.where(sub > ri, jnp.where(row >= sc[v], 1.0, 0.0),
                                          jnp.where(row > sc[v], 1.0, 0.0))
                    rk[v] = rk[v] + beats
            return tuple(rk)

        rk = tuple(jnp.zeros((SUBLANES, NSA_GQ), F32) for _ in range(nv))
        for vi in range(nv):
            rk = lax.cond(vi * SUBLANES <= last_blk, functools.partial(count_group, vi), lambda r: r, rk)
        rank = jnp.concatenate(rk, axis=0)
        return jnp.where((rank < n_sel) & (j_s < first_blk), 0.0, NEG_BIG)

    def all_mask():
        return jnp.where(j_s < first_blk, 0.0, NEG_BIG)

    neg_t = lax.cond(last_blk < n_sel, all_mask, ranked_mask)

    a0 = pl.multiple_of(s0, LANES)
    s_d = _dot_nt(ks_ref[pl.ds(a0, NSA_QB), :], qp)
    neg_t = jnp.concatenate([neg_t, jnp.zeros((LANES - n_blk, NSA_GQ), F32)], axis=0).astype(BF16)
    q_aux = _dot_nt(eye_ref[...], neg_t).astype(BF16)
    qx = jnp.concatenate([qp, jnp.concatenate([q_aux] * NSA_HPG, axis=0)], axis=1)

    sub_keys = NSA_KC // FLASH_NSUB

    def sel_scores(i, t):
        k0 = pl.multiple_of(i * NSA_KC + t * sub_keys, sub_keys)
        kx = jnp.concatenate([ks_ref[pl.ds(k0, sub_keys), :], et_ref[pl.ds(k0, sub_keys), :]], axis=1)
        return _dot_nt(kx, qx)

    def sel_values(i, t):
        return vst_ref[:, pl.ds(pl.multiple_of(i * NSA_KC + t * sub_keys, sub_keys), sub_keys)]

    window = {}

    def after_first_scores():
        kpos = a0 + lax.broadcasted_iota(jnp.int32, (NSA_QB, 1), 0)
        l_d = _where_tiles(kpos <= t_l, s_d, NEG_INF)
        m_d = jnp.max(l_d, axis=0, keepdims=True)
        p_d = jnp.exp2(l_d - m_d)
        acc_scr[...] = _dot(vst_ref[:, pl.ds(a0, NSA_QB)], p_d.astype(BF16))
        e_w, window['l'] = _masked_softmax_cols(s_w, mask_w)
        window['acc'] = _dot(vwt_ref[:, pl.ds(start, NSA_WSPAN)], e_w.astype(BF16))
        return m_d, jnp.sum(p_d, axis=0, keepdims=True)

    n_prev = (s0 + NSA_KC - 1) // NSA_KC
    l_s = _flash_chunks(n_prev, sel_scores, sel_values, after_first_scores, sa_scr, sb_scr, acc_scr)

    gates = gate_ref[...]
    o_t = ((gates[0:1, :] * inv_c) * acc_c + (gates[1:2, :] / l_s) * acc_scr[...]
           + (gates[2:3, :] / window['l']) * window['acc'])
    for h in range(NSA_HPG):
        tt = o_t[:, h * NSA_GQ:(h + 1) * NSA_GQ].T
        o_ref[:, h * LANES:(h + 1) * LANES] = jnp.where(low, tt[0:NSA_QB], tt[NSA_QB:NSA_GQ])


def _nsa_attention(qa3, gexp, kc, vct, knsa3, vnsa_t, aggt, et, eye):
    B, S, _ = qa3.shape
    n_pad = kc.shape[1]
    n_cmp = (S - CMP_BLOCK) // CMP_STRIDE + 1
    n_blk = S // SEL_BLOCK
    n_sel = min(N_SELECT, n_blk)
    const = lambda shape: pl.BlockSpec(shape, lambda b, c: (0,) * len(shape))
    return pl.pallas_call(
        functools.partial(_nsa_kernel, n_cmp=n_cmp, n_blk=n_blk, n_sel=n_sel),
        grid=(B, S // NSA_QB),
        in_specs=[pl.BlockSpec((None, NSA_QB, NSA_WIDTH), lambda b, c: (b, c, 0)),
                  pl.BlockSpec((None, None, 3, NSA_ROWS), lambda b, c: (b, c, 0, 0)),
                  pl.BlockSpec((None, n_pad, KV_WIDTH), lambda b, c: (b, 0, 0)),
                  pl.BlockSpec((None, KV_WIDTH, n_pad), lambda b, c: (b, 0, 0)),
                  pl.BlockSpec((None, S, KV_WIDTH), lambda b, c: (b, 0, 0)),
                  pl.BlockSpec((None, S, KV_WIDTH), lambda b, c: (b, 0, 1)),
                  pl.BlockSpec((None, KV_WIDTH, S), lambda b, c: (b, 0, 0)),
                  pl.BlockSpec((None, KV_WIDTH, S), lambda b, c: (b, 1, 0)),
                  const((n_blk, n_pad)), const((S, LANES)), const((NSA_GQ, NSA_GQ))],
        out_specs=pl.BlockSpec((None, NSA_QB, NSA_WIDTH), lambda b, c: (b, c, 0)),
        out_shape=jax.ShapeDtypeStruct((B, S, NSA_WIDTH), F32),
        scratch_shapes=[pltpu.VMEM((NSA_KC, NSA_ROWS), F32), pltpu.VMEM((NSA_KC, NSA_ROWS), F32),
                        pltpu.VMEM((LANES, NSA_ROWS), F32)],
        compiler_params=_cparams(("parallel", "arbitrary")),
        name="nsa_attention",
    )(qa3, gexp, kc, vct, knsa3, knsa3, vnsa_t, vnsa_t, aggt, et, eye)


FOX_TQ = 512
FOX_KC = FOX_TQ
FOX_ROWS = FOX_PAIR * FOX_TQ


def _fox_kernel(q_ref, k_ref, aux_ref, vt_ref, o_ref, sa_scr, sb_scr, acc_scr):
    qi = pl.program_id(2)
    s0 = qi * FOX_TQ
    lane = lax.broadcasted_iota(jnp.int32, (FOX_TQ, LANES), 1)
    low = lane < HEAD_DIM
    q2 = q_ref[...]
    zero = jnp.zeros_like(q2)
    aux0 = jnp.where(lane < AUX_PIECES, -1.0, 0.0).astype(q2.dtype)
    aux1 = jnp.where((lane >= AUX_PIECES) & (lane < 2 * AUX_PIECES), -1.0, 0.0).astype(q2.dtype)
    qx = jnp.concatenate([jnp.concatenate([jnp.where(low, q2, zero), aux0], axis=1),
                          jnp.concatenate([jnp.where(low, zero, q2), aux1], axis=1)], axis=0)
    t_l = s0 + lax.broadcasted_iota(jnp.int32, (1, FOX_TQ), 1)

    sub_keys = FOX_KC // FLASH_NSUB

    def scores(i, t, n=sub_keys):
        k0 = pl.multiple_of(i * FOX_KC + t * sub_keys, sub_keys)
        kx = jnp.concatenate([k_ref[pl.ds(k0, n), :], aux_ref[pl.ds(k0, n), :]], axis=1)
        return _dot_nt(kx, qx)

    def values(i, t, n=sub_keys):
        return vt_ref[:, pl.ds(pl.multiple_of(i * FOX_KC + t * sub_keys, sub_keys), n)]

    def diagonal():
        kpos = s0 + lax.broadcasted_iota(jnp.int32, (FOX_KC, 1), 0)
        s_d = _where_tiles(kpos <= t_l, scores(qi, 0, FOX_KC), NEG_INF)
        m_d = jnp.max(s_d, axis=0, keepdims=True)
        p_d = jnp.exp2(s_d - m_d)
        acc_scr[...] = _dot(values(qi, 0, FOX_KC), p_d.astype(BF16))
        return m_d, jnp.sum(p_d, axis=0, keepdims=True)

    l_f = _flash_chunks(qi, scores, values, diagonal, sa_scr, sb_scr, acc_scr)
    o_t = acc_scr[...] / l_f
    o_ref[...] = jnp.where(low, o_t[:, 0:FOX_TQ].T, o_t[:, FOX_TQ:FOX_ROWS].T)


def _fox_attention(fqk3, aux3, vfox_t):
    B, S, _ = fqk3.shape
    return pl.pallas_call(
        _fox_kernel,
        grid=(B, FOX_NPAIR, S // FOX_TQ),
        in_specs=[pl.BlockSpec((None, FOX_TQ, LANES), lambda b, h, i: (b, i, h)),
                  pl.BlockSpec((None, S, LANES), lambda b, h, i: (b, 0, FOX_NPAIR + h)),
                  pl.BlockSpec((None, S, LANES), lambda b, h, i: (b, 0, h)),
                  pl.BlockSpec((None, LANES, S), lambda b, h, i: (b, h, 0))],
        out_specs=pl.BlockSpec((None, FOX_TQ, LANES), lambda b, h, i: (b, i, h)),
        out_shape=jax.ShapeDtypeStruct((B, S, FOX_WIDTH), F32),
        scratch_shapes=[pltpu.VMEM((FOX_KC, FOX_ROWS), F32), pltpu.VMEM((FOX_KC, FOX_ROWS), F32),
                        pltpu.VMEM((LANES, FOX_ROWS), F32)],
        compiler_params=_cparams(("parallel", "parallel", "arbitrary")),
        name="fox_attention",
    )(fqk3, fqk3, aux3, vfox_t)


def _lane_max(x, mask):
    return jnp.max(jnp.where(mask, x, NEG_INF), axis=-1, keepdims=True)


def _first_lane_eq(x, v, mask, lane):
    return jnp.min(jnp.where(mask & (x == v), lane, LANES), axis=-1, keepdims=True)


def _route(logits):
    lane = lax.broadcasted_iota(jnp.int32, logits.shape, 1)
    gmask = lane < N_GROUPS
    gmax = _lane_max(logits, gmask)
    gexp = jnp.where(gmask, jnp.exp(logits - gmax), 0.0)
    g_star = _first_lane_eq(logits, gmax, gmask, lane)
    p_grp = 1.0 / jnp.sum(gexp, axis=-1, keepdims=True)
    e_lo = ROUTE_E_OFF + g_star * EXPERTS_PER_GROUP
    emask = (lane >= e_lo) & (lane < e_lo + EXPERTS_PER_GROUP)
    emax = _lane_max(logits, emask)
    eexp = jnp.where(emask, jnp.exp(logits - emax), 0.0)
    prob = eexp / jnp.sum(eexp, axis=-1, keepdims=True)
    v1 = _lane_max(prob, emask)
    i1 = _first_lane_eq(prob, v1, emask, lane)
    rest = emask & (lane != i1)
    v2 = _lane_max(prob, rest)
    i2 = _first_lane_eq(prob, v2, rest, lane)
    den = v1 + v2
    w1 = p_grp * (v1 / den)
    w2 = p_grp * (v2 / den)
    return (jnp.where(lane == i1, w1, 0.0) + jnp.where(lane == i2, w2, 0.0)
            + jnp.where(lane == ROUTE_G_LANE, g_star.astype(F32), 0.0))


def _outproj_kernel(on_ref, of_ref, x_ref, bn_ref, bf_ref, wn_ref, wf_ref, gffn_ref, wr_ref, br_ref,
                    h_ref, u_ref, comb_ref):
    mn = _rms(on_ref[...], bn_ref[...]).astype(BF16)
    mf = _rms(of_ref[...], bf_ref[...]).astype(BF16)
    h = x_ref[...] + (_dot(mn, wn_ref[...]) + _dot(mf, wf_ref[...]))
    h_ref[...] = h
    u = _rms(h, gffn_ref[...]).astype(BF16)
    u_ref[...] = u
    comb_ref[...] = _route(_dot(u, wr_ref[...]) + br_ref[...])


def _outproj(o_nsa, o_fox, x2, beta_n, beta_f, w_n, w_f, g_ffn, w_r, b_r, tm):
    T = x2.shape[0]
    full = lambda shape: pl.BlockSpec(shape, lambda i: (0,) * len(shape))
    row = lambda w: pl.BlockSpec((tm, w), lambda i: (i, 0))
    return pl.pallas_call(
        _outproj_kernel,
        grid=(T // tm,),
        in_specs=[row(NSA_WIDTH), row(FOX_WIDTH), row(D_MODEL), full((1, NSA_WIDTH)), full((1, FOX_WIDTH)),
                  full((NSA_WIDTH, D_MODEL)), full((FOX_WIDTH, D_MODEL)), full((1, D_MODEL)),
                  full((D_MODEL, LANES)), full((1, LANES))],
        out_specs=[row(D_MODEL), row(D_MODEL), row(LANES)],
        out_shape=[jax.ShapeDtypeStruct((T, D_MODEL), F32),
                   jax.ShapeDtypeStruct((T, D_MODEL), BF16),
                   jax.ShapeDtypeStruct((T, LANES), F32)],
        compiler_params=_cparams(("parallel",)),
        name="outproj_router",
    )(o_nsa, o_fox, x2, beta_n, beta_f, w_n, w_f, g_ffn, w_r, b_r)


MOE_TS = 1024
MOE_RB = 128
MOE_TSP = MOE_TS + N_GROUPS * MOE_RB
MOE_META = 8
MOE_TPS = 2


def _moe_sort_kernel(u_ref, comb_ref, tri_ref, us_ref, cs_ref, pos_ref, meta_ref):
    comb = comb_ref[...]
    lane = lax.broadcasted_iota(jnp.int32, comb.shape, 1)
    grp = comb[:, ROUTE_G_LANE:ROUTE_G_LANE + 1].astype(jnp.int32)
    onehot = jnp.where((lane == grp) & (lane < N_GROUPS), 1.0, 0.0)
    rank = _dot(tri_ref[...], onehot.astype(BF16))
    counts = rank[MOE_TS - 1:MOE_TS, :]
    nblk = jnp.floor((counts + (MOE_RB - 1)) * (1.0 / MOE_RB))
    lane1 = lane[0:1, :]
    blk0 = jnp.zeros_like(nblk)
    run = jnp.zeros((1, 1), F32)
    for g in range(1, N_GROUPS):
        run = run + nblk[:, g - 1:g]
        blk0 = blk0 + jnp.where(lane1 == g, run, 0.0)
    val = onehot * (blk0 * MOE_RB + rank - 1.0)
    pos_col = jnp.sum(val, axis=-1, keepdims=True)
    pos_ref[...] = pos_col.astype(jnp.int32)
    ones = jnp.ones((SUBLANES, LANES), BF16)
    hi, mid, lo = _split3(val)
    pos_row = (_dot_nt(ones, hi) + _dot_nt(ones, mid) + _dot_nt(ones, lo))[0:1, :].astype(jnp.int32)
    r_idx = lax.broadcasted_iota(jnp.int32, (MOE_TSP, MOE_TS), 0)
    perm = jnp.where(r_idx == pos_row, 1.0, 0.0).astype(BF16)
    us_ref[...] = _dot(perm, u_ref[...]).astype(us_ref.dtype)
    cs_ref[...] = _split3_dot_lhs(perm, comb)
    meta = blk0 + pltpu.roll(nblk, N_GROUPS, 1)
    meta_ref[...] = jnp.where(lane1 < MOE_META, meta, 0.0).astype(jnp.int32)


def _moe_sort(u, comb, tri):
    T = u.shape[0]
    nt = T // MOE_TS
    return pl.pallas_call(
        _moe_sort_kernel,
        grid=(nt,),
        in_specs=[pl.BlockSpec((MOE_TS, D_MODEL), lambda i: (i, 0)),
                  pl.BlockSpec((MOE_TS, LANES), lambda i: (i, 0)),
                  pl.BlockSpec((MOE_TS, MOE_TS), lambda i: (0, 0))],
        out_specs=[pl.BlockSpec((MOE_TSP, D_MODEL), lambda i: (i, 0)),
                   pl.BlockSpec((MOE_TSP, LANES), lambda i: (i, 0)),
                   pl.BlockSpec((MOE_TS, 1), lambda i: (i, 0)),
                   pl.BlockSpec((None, 1, LANES), lambda i: (i, 0, 0))],
        out_shape=[jax.ShapeDtypeStruct((nt * MOE_TSP, D_MODEL), BF16),
                   jax.ShapeDtypeStruct((nt * MOE_TSP, LANES), F32),
                   jax.ShapeDtypeStruct((T, 1), jnp.int32),
                   jax.ShapeDtypeStruct((nt, 1, LANES), jnp.int32)],
        compiler_params=_cparams(("parallel",)),
        name="moe_sort",
    )(u, comb, tri)


def _moe_kernel(meta_ref, us_ref, cs_ref, wgu_ref, wd_ref, o_ref, acc_ref):
    i = pl.program_id(0)
    e = pl.program_id(1)
    g = e // EXPERTS_PER_GROUP

    @pl.when(e == 0)
    def _():
        acc_ref[...] = jnp.zeros_like(acc_ref)

    for tile in range(MOE_TPS):
        base = (i * MOE_TPS + tile) * MOE_META
        first = meta_ref[base + g] + tile * (MOE_TSP // MOE_RB)
        count = meta_ref[base + N_GROUPS + g]

        def blocks(j0, n, first=first):
            rows = [pl.multiple_of((first + j0 + k) * MOE_RB, MOE_RB) for k in range(n)]
            gus = [_dot(us_ref[pl.ds(r0, MOE_RB), :], wgu_ref[...]) for r0 in rows]
            for r0, gu in zip(rows, gus):
                cs = cs_ref[pl.ds(r0, MOE_RB), :]
                lane = lax.broadcasted_iota(jnp.int32, cs.shape, 1)
                ce = jnp.sum(jnp.where(lane == ROUTE_E_OFF + e, cs, 0.0), axis=-1, keepdims=True)
                gt = gu[:, 0:D_EXPERT]
                hid = gt * (1.0 / (1.0 + jnp.exp(-gt))) * gu[:, D_EXPERT:2 * D_EXPERT]
                acc_ref[pl.ds(r0, MOE_RB), :] += _dot((ce * hid).astype(BF16), wd_ref[...])

        def pair(j, carry, blocks=blocks):
            blocks(2 * j, 2)
            return carry

        def single(_, carry, blocks=blocks, count=count):
            blocks(count - 1, 1)
            return carry

        lax.fori_loop(0, count // 2, pair, 0)
        lax.fori_loop(0, count & 1, single, 0)

    @pl.when(e == N_EXPERTS - 1)
    def _():
        o_ref[...] = acc_ref[...].astype(o_ref.dtype)


def _moe(meta, us, cs, wgu, wd):
    rows = MOE_TPS * MOE_TSP
    grid_spec = pltpu.PrefetchScalarGridSpec(
        num_scalar_prefetch=1,
        grid=(us.shape[0] // rows, N_EXPERTS),
        in_specs=[pl.BlockSpec((rows, D_MODEL), lambda i, e, m: (i, 0)),
                  pl.BlockSpec((rows, LANES), lambda i, e, m: (i, 0)),
                  pl.BlockSpec((None, D_MODEL, 2 * D_EXPERT), lambda i, e, m: (e, 0, 0)),
                  pl.BlockSpec((None, D_EXPERT, D_MODEL), lambda i, e, m: (e, 0, 0))],
        out_specs=pl.BlockSpec((rows, D_MODEL), lambda i, e, m: (i, 0)),
        scratch_shapes=[pltpu.VMEM((rows, D_MODEL), F32)])
    return pl.pallas_call(
        _moe_kernel,
        grid_spec=grid_spec,
        out_shape=jax.ShapeDtypeStruct(us.shape, BF16),
        compiler_params=_cparams(("parallel", "arbitrary")),
        name="moe_experts",
    )(meta, us, cs, wgu, wd)


def _ple_kernel(ys_ref, pos_ref, h_ref, p_ref, gple_ref, wg_ref, bg_ref, wp_ref, gfin_ref, o_ref):
    r_idx = lax.broadcasted_iota(jnp.int32, (pos_ref.shape[0], MOE_TSP), 1)
    unperm = jnp.where(r_idx == pos_ref[...], 1.0, 0.0).astype(BF16)
    h = h_ref[...] + _dot(unperm, ys_ref[...])
    v = _rms(h, gple_ref[...]).astype(BF16)
    z = _dot(v, wg_ref[...]) + bg_ref[...]
    gate = 1.0 / (1.0 + jnp.exp(-z))
    proj = _dot(p_ref[...].astype(BF16), wp_ref[...])
    o_ref[...] = _rms(h + gate * proj, gfin_ref[...])


def _ple(ys, pos, h1, p2, g_ple, w_g, b_g, w_p, g_final, tm):
    T = h1.shape[0]
    per = MOE_TS // tm
    full = lambda shape: pl.BlockSpec(shape, lambda i: (0,) * len(shape))
    row = lambda w: pl.BlockSpec((tm, w), lambda i: (i, 0))
    return pl.pallas_call(
        _ple_kernel,
        grid=(T // tm,),
        in_specs=[pl.BlockSpec((MOE_TSP, D_MODEL), lambda i: (i // per, 0)), row(1), row(D_MODEL), row(PLE_DIM),
                  full((1, D_MODEL)), full((D_MODEL, D_MODEL)),
                  full((1, D_MODEL)), full((PLE_DIM, D_MODEL)), full((1, D_MODEL))],
        out_specs=row(D_MODEL),
        out_shape=jax.ShapeDtypeStruct((T, D_MODEL), F32),
        compiler_params=_cparams(("parallel",)),
        name="ple_final",
    )(ys, pos, h1, p2, g_ple, w_g, b_g, w_p, g_final)


def _rope_lane_tables():
    half = ROPE_HALF
    inv_freq = jnp.power(jnp.float32(ROPE_THETA), -jnp.arange(half, dtype=jnp.float32) / half)
    j = np.arange(LANES) % HEAD_DIM
    freq = jnp.where(jnp.asarray(j < ROPE_DIM), inv_freq[jnp.asarray(j % half)], 0.0).reshape(1, LANES)
    mlo = jnp.asarray(np.where(j < half, -1.0, 0.0), F32).reshape(1, LANES)
    mhi = jnp.asarray(np.where((j >= half) & (j < ROPE_DIM), 1.0, 0.0), F32).reshape(1, LANES)
    return freq.astype(F32), mlo, mhi


def _block_diag2(w):
    z = jnp.zeros_like(w)
    return jnp.concatenate([jnp.concatenate([w, z], axis=-1), jnp.concatenate([z, w], axis=-1)], axis=-2)


def _aux_route_table():
    r = np.zeros((AUX_PIECES, LANES, FOX_NPAIR * LANES), np.float32)
    for head in range(FOX_HEADS):
        for k in range(AUX_PIECES):
            r[k, SMALL_F_OFF + head, (head // FOX_PAIR) * LANES + (head % FOX_PAIR) * AUX_PIECES + k] = 1.0
    return jnp.asarray(r, BF16)


def _layer(h3, p3, positions, prm, g_final):
    B, S, _ = h3.shape
    T = B * S
    tm = 512
    x2 = h3.reshape(T, D_MODEL)
    freq, mlo, mhi = _rope_lane_tables()

    w_in = prm['w_in']
    offs = np.cumsum([0, NSA_WIDTH, KV_WIDTH, KV_WIDTH, KV_WIDTH, KV_WIDTH, KV_WIDTH, KV_WIDTH,
                      N_GATE, FOX_WIDTH, FOX_WIDTH, FOX_WIDTH, FOX_HEADS])
    seg = lambda k: w_in[:, offs[k]:offs[k + 1]]
    wq = seg(0).reshape(D_MODEL, NSA_GROUPS, NSA_HPG, HEAD_DIM).transpose(0, 2, 1, 3).reshape(D_MODEL, NSA_WIDTH)
    pad = jnp.zeros((D_MODEL, LANES - N_GATE - FOX_HEADS), w_in.dtype)
    w_all = jnp.concatenate([wq, seg(1), seg(2), seg(3), seg(5), seg(8), seg(9), seg(7), seg(11), pad],
                            axis=1).astype(BF16)
    wt_v = jnp.concatenate([seg(4), seg(6), seg(10)], axis=1).T.astype(BF16)
    bias_small = jnp.concatenate([prm['b_nsa_gate'], prm['b_forget'],
                                  jnp.zeros((LANES - N_GATE - FOX_HEADS,), F32)]).reshape(1, LANES)

    qa, cmp_tok, knsa, vnsa_t, fqk, vfox_t, small = _inproj(
        x2, positions.reshape(T, 1), prm['g_mix'].reshape(1, D_MODEL), w_all, wt_v, bias_small,
        freq, mlo, mhi, tm, B, S)

    tri = jnp.asarray(np.tril(np.ones((CUM_CHUNK, CUM_CHUNK), np.float32)), BF16)
    small3 = small.reshape(B, S, LANES)
    aux3 = _cumsum(small3, tri, _aux_route_table())

    n_rows = S // CMP_STRIDE
    tile2 = lambda pe: jnp.concatenate([pe, pe], axis=-1)
    bd1 = lambda w: _block_diag2(w.reshape(CMP_BLOCK, HEAD_DIM, CMP_HIDDEN)).astype(BF16)
    pos_cmp = positions[:, CMP_BLOCK - 1::CMP_STRIDE]
    pos_cmp = jnp.pad(pos_cmp, ((0, 0), (0, n_rows - pos_cmp.shape[1]))).reshape(B, n_rows, 1)
    kc, vct = _compress(cmp_tok.reshape(B, S, 2 * KV_WIDTH), tile2(prm['pe_cmp_k']), tile2(prm['pe_cmp_v']),
                        bd1(prm['w_cmp_k1']), bd1(prm['w_cmp_v1']),
                        _block_diag2(prm['w_cmp_k2']).astype(BF16), _block_diag2(prm['w_cmp_v2']).T.astype(BF16),
                        pos_cmp, freq, mlo, mhi)

    n_cmp = (S - CMP_BLOCK) // CMP_STRIDE + 1
    n_blk = S // SEL_BLOCK
    cs = np.arange(n_rows)[:, None] * CMP_STRIDE
    ss = np.arange(n_blk)[None, :] * SEL_BLOCK
    ov = np.clip(np.minimum(cs + CMP_BLOCK, ss + SEL_BLOCK) - np.maximum(cs, ss), 0, None) / CMP_BLOCK
    ov[n_cmp:] = 0.0
    aggt = jnp.asarray(ov.T, BF16)
    et = jnp.asarray((np.arange(S)[:, None] // SEL_BLOCK) == np.arange(LANES)[None, :], BF16)
    eye = jnp.asarray(np.eye(NSA_GQ, dtype=np.float32), BF16)
    gexp = small3[:, :, :N_GATE].reshape(B, S // NSA_QB, NSA_QB, NSA_GROUPS, NSA_HPG, 3)
    gexp = gexp.transpose(0, 1, 5, 4, 3, 2).reshape(B, S // NSA_QB, 3, NSA_ROWS)
    o_nsa = _nsa_attention(qa.reshape(B, S, NSA_WIDTH), gexp, kc, vct, knsa.reshape(B, S, 2 * KV_WIDTH),
                           vnsa_t, aggt, et, eye)

    o_fox = _fox_attention(fqk.reshape(B, S, 2 * FOX_WIDTH), aux3, vfox_t)

    perm = lambda a: a.reshape(NSA_GROUPS, NSA_HPG, HEAD_DIM, -1).transpose(1, 0, 2, 3).reshape(NSA_WIDTH, -1)
    beta_n = perm(prm['beta_nsa'].reshape(NSA_WIDTH, 1)).reshape(1, NSA_WIDTH)
    w_out = prm['w_out']
    w_n = perm(w_out[:NSA_WIDTH]).astype(BF16)
    w_f = w_out[NSA_WIDTH:].astype(BF16)
    w_r = jnp.concatenate([prm['w_group'], prm['w_router'],
                           jnp.zeros((D_MODEL, LANES - N_GROUPS - N_EXPERTS), F32)], axis=1).astype(BF16)
    b_r = jnp.concatenate([prm['b_group'], prm['b_router'],
                           jnp.zeros((LANES - N_GROUPS - N_EXPERTS,), F32)]).reshape(1, LANES)
    h1, u, comb = _outproj(o_nsa.reshape(T, NSA_WIDTH), o_fox.reshape(T, FOX_WIDTH), x2, beta_n,
                           prm['beta_fox'].reshape(1, FOX_WIDTH), w_n, w_f,
                           prm['g_ffn'].reshape(1, D_MODEL), w_r, b_r, tm)

    wgu = jnp.concatenate([prm['w_gate_e'], prm['w_up_e']], axis=-1).astype(BF16)
    tri_s = jnp.asarray(np.tril(np.ones((MOE_TS, MOE_TS), np.float32)), BF16)
    us, cs, pos, meta = _moe_sort(u, comb, tri_s)
    ys = _moe(meta[:, 0, :MOE_META].reshape(-1), us, cs, wgu, prm['w_down_e'].astype(BF16))

    out = _ple(ys, pos, h1, p3.reshape(T, PLE_DIM), prm['g_ple'].reshape(1, D_MODEL),
               prm['w_ple_gate'].astype(BF16),
               prm['b_ple_gate'].reshape(1, D_MODEL), prm['w_ple_proj'].astype(BF16),
               g_final.reshape(1, D_MODEL), tm)
    return out.reshape(B, S, D_MODEL)


_PARAM_NAMES = ('g_mix', 'w_in', 'b_nsa_gate', 'b_forget', 'pe_cmp_k', 'w_cmp_k1', 'w_cmp_k2',
                'pe_cmp_v', 'w_cmp_v1', 'w_cmp_v2', 'beta_nsa', 'beta_fox', 'w_out', 'g_ffn',
                'w_group', 'b_group', 'w_router', 'b_router', 'w_gate_e', 'w_up_e', 'w_down_e',
                'g_ple', 'w_ple_gate', 'b_ple_gate', 'w_ple_proj')


def kernel(x, p, positions, g_mix, w_in, b_nsa_gate, b_forget, pe_cmp_k, w_cmp_k1, w_cmp_k2, pe_cmp_v,
           w_cmp_v1, w_cmp_v2, beta_nsa, beta_fox, w_out, g_ffn, w_group, b_group, w_router, b_router,
           w_gate_e, w_up_e, w_down_e, g_ple, w_ple_gate, b_ple_gate, w_ple_proj, g_final):
    stacked = (g_mix, w_in, b_nsa_gate, b_forget, pe_cmp_k, w_cmp_k1, w_cmp_k2, pe_cmp_v, w_cmp_v1,
               w_cmp_v2, beta_nsa, beta_fox, w_out, g_ffn, w_group, b_group, w_router, b_router,
               w_gate_e, w_up_e, w_down_e, g_ple, w_ple_gate, b_ple_gate, w_ple_proj)
    depth = w_in.shape[0]
    assert depth == 1, "the final norm is fused into the last layer's embedding kernel"
    prm = {n: a[0] for n, a in zip(_PARAM_NAMES, stacked)}
    return _layer(x, p[0], positions, prm, g_final)
```

```python
import functools
import math

import numpy as np
import jax
import jax.numpy as jnp
from jax import lax
from jax.experimental import pallas as pl
from jax.experimental.pallas import tpu as pltpu

D_MODEL = 1024
HEAD_DIM = 64
NSA_HEADS = 8
FOX_HEADS = 8
NSA_GROUPS = 2
NSA_HPG = 4
NSA_WIDTH = 512
FOX_WIDTH = 512
KV_WIDTH = 128
CMP_BLOCK = 32
CMP_STRIDE = 16
CMP_HIDDEN = 128
SEL_BLOCK = 64
N_SELECT = 16
WINDOW = 512
ROPE_THETA = 500000.0
ROPE_DIM = 16
ROPE_HALF = 8
N_GROUPS = 4
EXPERTS_PER_GROUP = 4
N_EXPERTS = 16
D_EXPERT = 512
PLE_DIM = 256
EPS = 1e-6
NEG_INF = -1e30
FORCE_BONUS = 1e4
LOG2E = math.log2(math.e)
Q_SCALE = 0.125 * LOG2E
NEG_BIG = -(2.0 ** 100)

LANES = 128
SUBLANES = 8
N_GATE = 3 * NSA_HEADS
SMALL_F_OFF = N_GATE
ROUTE_E_OFF = N_GROUPS
ROUTE_G_LANE = 0

VMEM_LIMIT = 56 * 1024 * 1024

F32 = jnp.float32
BF16 = jnp.bfloat16


def _cparams(sem):
    return pltpu.CompilerParams(dimension_semantics=sem, vmem_limit_bytes=VMEM_LIMIT)


def _dot(a, b):
    return jnp.dot(a, b, preferred_element_type=F32)


def _dot_nt(a, b):
    return lax.dot_general(a, b, (((1,), (1,)), ((), ())), preferred_element_type=F32)


def _split3(x):
    hi = x.astype(BF16)
    r1 = x - hi.astype(F32)
    mid = r1.astype(BF16)
    lo = (r1 - mid.astype(F32)).astype(BF16)
    return hi, mid, lo


def _split3_dot_lhs(e, x):
    hi, mid, lo = _split3(x)
    return _dot(e, hi) + _dot(e, mid) + _dot(e, lo)


def _rms(x, g):
    return x * lax.rsqrt(jnp.mean(x * x, axis=-1, keepdims=True) + EPS) * g


def _rope(r, cos, sin_lo, sin_hi):
    return (r * cos + pltpu.roll(r, LANES - ROPE_HALF, 1) * sin_lo
            + pltpu.roll(r, ROPE_HALF, 1) * sin_hi)


def _rope_tables(pos_col, freq, mlo, mhi):
    ang = pos_col.astype(F32) * freq
    cos = jnp.cos(ang)
    sin = jnp.sin(ang)
    return cos, sin * mlo, sin * mhi


def _inproj_kernel(x_ref, pos_ref, g_ref, w_ref, wt_ref, bias_ref, freq_ref, mlo_ref, mhi_ref,
                   qa_ref, cmp_ref, knsa_ref, vnsa_ref, fqk_ref, vfox_ref, small_ref):
    x = x_ref[...]
    hb = _rms(x, g_ref[...]).astype(BF16)
    cos, s_lo, s_hi = _rope_tables(pos_ref[...], freq_ref[...], mlo_ref[...], mhi_ref[...])
    off = 0
    r = _dot(hb, w_ref[:, off:off + NSA_WIDTH])
    for c in range(NSA_WIDTH // LANES):
        rc = _rope(r[:, c * LANES:(c + 1) * LANES], cos, s_lo, s_hi) * Q_SCALE
        qa_ref[:, c * LANES:(c + 1) * LANES] = rc.astype(qa_ref.dtype)
    off += NSA_WIDTH
    cmp_ref[...] = _dot(hb, w_ref[:, off:off + 2 * KV_WIDTH])
    off += 2 * KV_WIDTH
    r = _dot(hb, w_ref[:, off:off + 2 * KV_WIDTH])
    for c in range(2):
        rc = _rope(r[:, c * LANES:(c + 1) * LANES], cos, s_lo, s_hi)
        knsa_ref[:, c * LANES:(c + 1) * LANES] = rc.astype(knsa_ref.dtype)
    off += 2 * KV_WIDTH
    r = _dot(hb, w_ref[:, off:off + FOX_WIDTH])
    fqk_ref[:, 0:FOX_WIDTH] = (r * Q_SCALE).astype(fqk_ref.dtype)
    off += FOX_WIDTH
    r = _dot(hb, w_ref[:, off:off + FOX_WIDTH])
    fqk_ref[:, FOX_WIDTH:2 * FOX_WIDTH] = r.astype(fqk_ref.dtype)
    off += FOX_WIDTH
    z = _dot(hb, w_ref[:, off:off + LANES]) + bias_ref[...]
    lane = lax.broadcasted_iota(jnp.int32, z.shape, 1)
    sig = 1.0 / (1.0 + jnp.exp(-z))
    logsig = jnp.minimum(z, 0.0) - jnp.log(1.0 + jnp.exp(-jnp.abs(z)))
    small_ref[...] = jnp.where(lane < SMALL_F_OFF, sig, logsig)
    vt = _dot_nt(wt_ref[...], hb)
    vnsa_ref[...] = vt[0:2 * KV_WIDTH].astype(vnsa_ref.dtype)
    vfox_ref[...] = vt[2 * KV_WIDTH:2 * KV_WIDTH + FOX_WIDTH].astype(vfox_ref.dtype)


def _inproj(x2, pos2, g_mix, w_all, wt_v, bias_small, freq, mlo, mhi, tm, B, S):
    T = x2.shape[0]
    nt = S // tm
    full = lambda shape: pl.BlockSpec(shape, lambda i: (0,) * len(shape))
    row = lambda w: pl.BlockSpec((tm, w), lambda i: (i, 0))
    tr = lambda w: pl.BlockSpec((None, w, tm), lambda i: (i // nt, 0, i % nt))
    return pl.pallas_call(
        _inproj_kernel,
        grid=(T // tm,),
        in_specs=[row(D_MODEL), row(1), full((1, D_MODEL)), full(w_all.shape), full(wt_v.shape),
                  full((1, LANES)), full((1, LANES)), full((1, LANES)), full((1, LANES))],
        out_specs=[row(NSA_WIDTH), row(2 * KV_WIDTH), row(2 * KV_WIDTH), tr(2 * KV_WIDTH),
                   row(2 * FOX_WIDTH), tr(FOX_WIDTH), row(LANES)],
        out_shape=[jax.ShapeDtypeStruct((T, NSA_WIDTH), BF16),
                   jax.ShapeDtypeStruct((T, 2 * KV_WIDTH), F32),
                   jax.ShapeDtypeStruct((T, 2 * KV_WIDTH), BF16),
                   jax.ShapeDtypeStruct((B, 2 * KV_WIDTH, S), BF16),
                   jax.ShapeDtypeStruct((T, 2 * FOX_WIDTH), BF16),
                   jax.ShapeDtypeStruct((B, FOX_WIDTH, S), BF16),
                   jax.ShapeDtypeStruct((T, LANES), F32)],
        compiler_params=_cparams(("parallel",)),
        name="inproj",
    )(x2, pos2, g_mix, w_all, wt_v, bias_small, freq, mlo, mhi)


CUM_CHUNK = 1024
CUM_SUB = 128
FOX_PAIR = 2
FOX_NPAIR = FOX_HEADS // FOX_PAIR
AUX_PIECES = 3


def _cumsum_kernel(x_ref, tri_ref, route_ref, o_ref, carry_ref):
    @pl.when(pl.program_id(1) == 0)
    def _():
        carry_ref[...] = jnp.zeros_like(carry_ref)

    carry = carry_ref[...]
    pieces = []
    for k in range(CUM_CHUNK // CUM_SUB):
        ck = _split3_dot_lhs(tri_ref[...], x_ref[k * CUM_SUB:(k + 1) * CUM_SUB, :]) + carry
        carry = ck[CUM_SUB - 1:CUM_SUB, :]
        pieces.append(ck)
    carry_ref[...] = carry
    c = jnp.concatenate(pieces, axis=0)
    hi, mid, lo = _split3(c * LOG2E)
    aux = _dot(hi, route_ref[0]) + _dot(mid, route_ref[1]) + _dot(lo, route_ref[2])
    o_ref[...] = aux.astype(o_ref.dtype)


def _cumsum(small3, tri, route):
    B, S, _ = small3.shape
    width = FOX_NPAIR * LANES
    return pl.pallas_call(
        _cumsum_kernel,
        grid=(B, S // CUM_CHUNK),
        in_specs=[pl.BlockSpec((None, CUM_CHUNK, LANES), lambda b, i: (b, i, 0)),
                  pl.BlockSpec((CUM_SUB, CUM_SUB), lambda b, i: (0, 0)),
                  pl.BlockSpec((AUX_PIECES, LANES, width), lambda b, i: (0, 0, 0))],
        out_specs=pl.BlockSpec((None, CUM_CHUNK, width), lambda b, i: (b, i, 0)),
        out_shape=jax.ShapeDtypeStruct((B, S, width), BF16),
        scratch_shapes=[pltpu.VMEM((1, LANES), F32)],
        compiler_params=_cparams(("parallel", "arbitrary")),
        name="forget_cumsum",
    )(small3, tri, route)


def _compress_kernel(tokk_ref, tokv_ref, pek_ref, pev_ref, bdk1_ref, bdv1_ref, bdk2_ref, bdv2t_ref,
                     pos_ref, freq_ref, mlo_ref, mhi_ref, kc_ref, vct_ref, *, n_rows):
    half = CMP_BLOCK // 2
    ak = jnp.zeros((n_rows, 2 * CMP_HIDDEN), F32)
    bk = jnp.zeros((n_rows, 2 * CMP_HIDDEN), F32)
    av = jnp.zeros((n_rows, 2 * CMP_HIDDEN), F32)
    bv = jnp.zeros((n_rows, 2 * CMP_HIDDEN), F32)
    for l in range(half):
        xk = tokk_ref[pl.ds(l, n_rows, stride=CMP_STRIDE), :]
        xv = tokv_ref[pl.ds(l, n_rows, stride=CMP_STRIDE), :]
        ak = ak + _dot((xk + pek_ref[l:l + 1, :]).astype(BF16), bdk1_ref[l])
        bk = bk + _dot((xk + pek_ref[half + l:half + l + 1, :]).astype(BF16), bdk1_ref[half + l])
        av = av + _dot((xv + pev_ref[l:l + 1, :]).astype(BF16), bdv1_ref[l])
        bv = bv + _dot((xv + pev_ref[half + l:half + l + 1, :]).astype(BF16), bdv1_ref[half + l])
    hk = ak + pltpu.roll(bk, n_rows - 1, 0)
    hv = av + pltpu.roll(bv, n_rows - 1, 0)
    hk = hk * (1.0 / (1.0 + jnp.exp(-hk)))
    hv = hv * (1.0 / (1.0 + jnp.exp(-hv)))
    kc = _dot(hk.astype(BF16), bdk2_ref[...])
    cos, s_lo, s_hi = _rope_tables(pos_ref[...], freq_ref[...], mlo_ref[...], mhi_ref[...])
    kc_ref[...] = _rope(kc, cos, s_lo, s_hi).astype(kc_ref.dtype)
    vct_ref[...] = _dot_nt(bdv2t_ref[...], hv.astype(BF16)).astype(vct_ref.dtype)


def _compress(cmp_tok3, pek, pev, bdk1, bdv1, bdk2, bdv2t, pos_cmp, freq, mlo, mhi):
    B, S, _ = cmp_tok3.shape
    n_rows = S // CMP_STRIDE
    full = lambda shape: pl.BlockSpec(shape, lambda b: (0,) * len(shape))
    return pl.pallas_call(
        functools.partial(_compress_kernel, n_rows=n_rows),
        grid=(B,),
        in_specs=[pl.BlockSpec((None, S, KV_WIDTH), lambda b: (b, 0, 0)),
                  pl.BlockSpec((None, S, KV_WIDTH), lambda b: (b, 0, 1)),
                  full((CMP_BLOCK, KV_WIDTH)), full((CMP_BLOCK, KV_WIDTH)),
                  full((CMP_BLOCK, KV_WIDTH, 2 * CMP_HIDDEN)), full((CMP_BLOCK, KV_WIDTH, 2 * CMP_HIDDEN)),
                  full((2 * CMP_HIDDEN, KV_WIDTH)), full((KV_WIDTH, 2 * CMP_HIDDEN)),
                  pl.BlockSpec((None, n_rows, 1), lambda b: (b, 0, 0)),
                  full((1, LANES)), full((1, LANES)), full((1, LANES))],
        out_specs=[pl.BlockSpec((None, n_rows, KV_WIDTH), lambda b: (b, 0, 0)),
                   pl.BlockSpec((None, KV_WIDTH, n_rows), lambda b: (b, 0, 0))],
        out_shape=[jax.ShapeDtypeStruct((B, n_rows, KV_WIDTH), BF16),
                   jax.ShapeDtypeStruct((B, KV_WIDTH, n_rows), BF16)],
        compiler_params=_cparams(("parallel",)),
        name="nsa_compress",
    )(cmp_tok3, cmp_tok3, pek, pev, bdk1, bdv1, bdk2, bdv2t, pos_cmp, freq, mlo, mhi)


NSA_QB = 128
NSA_BPQ = NSA_QB // SEL_BLOCK
assert NSA_BPQ in (1, 2) and NSA_QB % LANES == 0
NSA_KC = 512
NSA_GQ = NSA_GROUPS * NSA_QB
NSA_ROWS = NSA_HPG * NSA_GQ
NSA_WSPAN = WINDOW + NSA_QB


def _where_tiles(mask, x, other):
    w = mask.shape[1]
    return jnp.concatenate([jnp.where(mask, x[:, t * w:(t + 1) * w], other)
                            for t in range(x.shape[1] // w)], axis=1)


def _masked_softmax_cols(s, mask):
    l = _where_tiles(mask, s, NEG_INF)
    m = jnp.max(l, axis=0, keepdims=True)
    e = jnp.exp2(l - m)
    return e, jnp.sum(e, axis=0, keepdims=True)


FLASH_NSUB = 2


def _flash_chunks(n_chunks, score_fn, vt_fn, init_fn, s_a, s_b, acc_ref):
    last = jnp.maximum(n_chunks - 1, 0)
    sub = s_a.shape[0] // FLASH_NSUB

    def stage(i_prod, s_prod, i_cons, s_cons, mx, m, l):
        if i_cons is not None:
            m_new = jnp.maximum(m, mx)
            alpha = jnp.exp2(m - m_new)
            l = alpha * l
        if i_prod is not None:
            i_prod = jnp.minimum(i_prod, last)
        mx_prod, pv = None, None
        for t in range(FLASH_NSUB):
            rows = pl.ds(t * sub, sub)
            if i_prod is not None:
                s = score_fn(i_prod, t)
                s_prod[rows, :] = s
                mx_t = jnp.max(s, axis=0, keepdims=True)
                mx_prod = mx_t if mx_prod is None else jnp.maximum(mx_prod, mx_t)
            if i_cons is not None:
                p = jnp.exp2(s_cons[rows, :] - m_new)
                l = l + jnp.sum(p, axis=0, keepdims=True)
                d = _dot(vt_fn(i_cons, t), p.astype(BF16))
                pv = d if pv is None else pv + d
        if i_cons is not None:
            acc_ref[...] = alpha * acc_ref[...] + pv
            m = m_new
        return mx_prod, m, l

    def pair(j, carry):
        mx_a, m, l = carry
        i = 2 * j
        mx_b, m, l = stage(i + 1, s_b, i, s_a, mx_a, m, l)
        mx_a, m, l = stage(i + 2, s_a, i + 1, s_b, mx_b, m, l)
        return mx_a, m, l

    def tail(_, carry):
        mx_a, m, l = carry
        _, m, l = stage(None, None, last, s_a, mx_a, m, l)
        return mx_a, m, l

    mx_a, _, _ = stage(0, s_a, None, None, None, None, None)
    m, l = init_fn()
    carry = (mx_a, m, l)
    carry = lax.fori_loop(0, n_chunks // 2, pair, carry)
    carry = lax.fori_loop(0, n_chunks & 1, tail, carry)
    return carry[2]


def _nsa_kernel(q_ref, gate_ref, kc_ref, vct_ref, ks_ref, kw_ref, vst_ref, vwt_ref, aggt_ref, et_ref,
                eye_ref, o_ref, sa_scr, sb_scr, acc_scr, *, n_cmp, n_blk, n_sel):
    c = pl.program_id(1)
    s0 = c * NSA_QB
    lane = lax.broadcasted_iota(jnp.int32, (NSA_QB, LANES), 1)
    low = lane < HEAD_DIM
    qb = q_ref[...]
    zero = jnp.zeros((NSA_QB, LANES), qb.dtype)
    pieces = []
    for h in range(NSA_HPG):
        qh = qb[:, h * LANES:(h + 1) * LANES]
        pieces.append(jnp.where(low, qh, zero))
        pieces.append(jnp.where(low, zero, qh))
    qp = jnp.concatenate(pieces, axis=0)
    t_l = s0 + (lax.broadcasted_iota(jnp.int32, (1, NSA_GQ), 1) & (NSA_QB - 1))
    cur_l = t_l // SEL_BLOCK
    first_blk = c * NSA_BPQ
    last_blk = first_blk + NSA_BPQ - 1

    n_pad = kc_ref.shape[0]
    n_s = lax.broadcasted_iota(jnp.int32, (n_pad, 1), 0)
    mask_c = ((n_s * CMP_STRIDE + (CMP_BLOCK - 1)) <= t_l) & (n_s < n_cmp)
    start = pl.multiple_of(jnp.maximum(s0 - WINDOW, 0), LANES)
    rel = t_l - (start + lax.broadcasted_iota(jnp.int32, (NSA_WSPAN, 1), 0))
    mask_w = (rel >= 0) & (rel < WINDOW)
    s_c = _dot_nt(kc_ref[...], qp)
    s_w = _dot_nt(kw_ref[pl.ds(start, NSA_WSPAN), :], qp)
    e_c, sum_c = _masked_softmax_cols(s_c, mask_c)
    inv_c = _where_tiles(t_l >= CMP_BLOCK - 1, 1.0 / sum_c, 0.0)
    acc_c = _dot(vct_ref[...], e_c.astype(BF16))

    j_s = lax.broadcasted_iota(jnp.int32, (n_blk, NSA_GQ), 0)
    nv = n_blk // SUBLANES

    def ranked_mask():
        p_c = e_c * inv_c
        psum = p_c[:, 0:NSA_GQ]
        for h in range(1, NSA_HPG):
            psum = psum + p_c[:, h * NSA_GQ:(h + 1) * NSA_GQ]
        imp = _split3_dot_lhs(aggt_ref[...], psum)
        forced = (j_s == 0) | (j_s == cur_l) | (j_s == cur_l - 1)
        score = jnp.where(j_s > cur_l, -1.0, imp + jnp.where(forced, FORCE_BONUS, 0.0))
        sc = [score[SUBLANES * v:SUBLANES * (v + 1), :] for v in range(nv)]
        sub = lax.broadcasted_iota(jnp.int32, (SUBLANES, NSA_GQ), 0)

        def count_group(vi, rk):
            rk = list(rk)
            for ri in range(SUBLANES):
                row = sc[vi][ri:ri + 1, :]
                for v in range(nv):
                    if v < vi:
                        beats = jnp.where(row > sc[v], 1.0, 0.0)
                    elif v > vi:
                        beats = jnp.where(row >= sc[v], 1.0, 0.0)
                    else:
                        beats = jnp.where(sub > ri, jnp.where(row >= sc[v], 1.0, 0.0),
                                          jnp.where(row > sc[v], 1.0, 0.0))
                    rk[v] = rk[v] + beats
            return tuple(rk)

        rk = tuple(jnp.zeros((SUBLANES, NSA_GQ), F32) for _ in range(nv))
        for vi in range(nv):
            rk = lax.cond(vi * SUBLANES <= last_blk, functools.partial(count_group, vi), lambda r: r, rk)
        rank = jnp.concatenate(rk, axis=0)
        return jnp.where((rank < n_sel) & (j_s < first_blk), 0.0, NEG_BIG)

    def all_mask():
        return jnp.where(j_s < first_blk, 0.0, NEG_BIG)

    neg_t = lax.cond(last_blk < n_sel, all_mask, ranked_mask)

    a0 = pl.multiple_of(s0, LANES)
    s_d = _dot_nt(ks_ref[pl.ds(a0, NSA_QB), :], qp)
    neg_t = jnp.concatenate([neg_t, jnp.zeros((LANES - n_blk, NSA_GQ), F32)], axis=0).astype(BF16)
    q_aux = _dot_nt(eye_ref[...], neg_t).astype(BF16)
    qx = jnp.concatenate([qp, jnp.concatenate([q_aux] * NSA_HPG, axis=0)], axis=1)

    sub_keys = NSA_KC // FLASH_NSUB

    def sel_scores(i, t):
        k0 = pl.multiple_of(i * NSA_KC + t * sub_keys, sub_keys)
        kx = jnp.concatenate([ks_ref[pl.ds(k0, sub_keys), :], et_ref[pl.ds(k0, sub_keys), :]], axis=1)
        return _dot_nt(kx, qx)

    def sel_values(i, t):
        return vst_ref[:, pl.ds(pl.multiple_of(i * NSA_KC + t * sub_keys, sub_keys), sub_keys)]

    window = {}

    def after_first_scores():
        kpos = a0 + lax.broadcasted_iota(jnp.int32, (NSA_QB, 1), 0)
        l_d = _where_tiles(kpos <= t_l, s_d, NEG_INF)
        m_d = jnp.max(l_d, axis=0, keepdims=True)
        p_d = jnp.exp2(l_d - m_d)
        acc_scr[...] = _dot(vst_ref[:, pl.ds(a0, NSA_QB)], p_d.astype(BF16))
        e_w, window['l'] = _masked_softmax_cols(s_w, mask_w)
        window['acc'] = _dot(vwt_ref[:, pl.ds(start, NSA_WSPAN)], e_w.astype(BF16))
        return m_d, jnp.sum(p_d, axis=0, keepdims=True)

    n_prev = (s0 + NSA_KC - 1) // NSA_KC
    l_s = _flash_chunks(n_prev, sel_scores, sel_values, after_first_scores, sa_scr, sb_scr, acc_scr)

    gates = gate_ref[...]
    o_t = ((gates[0:1, :] * inv_c) * acc_c + (gates[1:2, :] / l_s) * acc_scr[...]
           + (gates[2:3, :] / window['l']) * window['acc'])
    for h in range(NSA_HPG):
        tt = o_t[:, h * NSA_GQ:(h + 1) * NSA_GQ].T
        o_ref[:, h * LANES:(h + 1) * LANES] = jnp.where(low, tt[0:NSA_QB], tt[NSA_QB:NSA_GQ])


def _nsa_attention(qa3, gexp, kc, vct, knsa3, vnsa_t, aggt, et, eye):
    B, S, _ = qa3.shape
    n_pad = kc.shape[1]
    n_cmp = (S - CMP_BLOCK) // CMP_STRIDE + 1
    n_blk = S // SEL_BLOCK
    n_sel = min(N_SELECT, n_blk)
    const = lambda shape: pl.BlockSpec(shape, lambda b, c: (0,) * len(shape))
    return pl.pallas_call(
        functools.partial(_nsa_kernel, n_cmp=n_cmp, n_blk=n_blk, n_sel=n_sel),
        grid=(B, S // NSA_QB),
        in_specs=[pl.BlockSpec((None, NSA_QB, NSA_WIDTH), lambda b, c: (b, c, 0)),
                  pl.BlockSpec((None, None, 3, NSA_ROWS), lambda b, c: (b, c, 0, 0)),
                  pl.BlockSpec((None, n_pad, KV_WIDTH), lambda b, c: (b, 0, 0)),
                  pl.BlockSpec((None, KV_WIDTH, n_pad), lambda b, c: (b, 0, 0)),
                  pl.BlockSpec((None, S, KV_WIDTH), lambda b, c: (b, 0, 0)),
                  pl.BlockSpec((None, S, KV_WIDTH), lambda b, c: (b, 0, 1)),
                  pl.BlockSpec((None, KV_WIDTH, S), lambda b, c: (b, 0, 0)),
                  pl.BlockSpec((None, KV_WIDTH, S), lambda b, c: (b, 1, 0)),
                  const((n_blk, n_pad)), const((S, LANES)), const((NSA_GQ, NSA_GQ))],
        out_specs=pl.BlockSpec((None, NSA_QB, NSA_WIDTH), lambda b, c: (b, c, 0)),
        out_shape=jax.ShapeDtypeStruct((B, S, NSA_WIDTH), F32),
        scratch_shapes=[pltpu.VMEM((NSA_KC, NSA_ROWS), F32), pltpu.VMEM((NSA_KC, NSA_ROWS), F32),
                        pltpu.VMEM((LANES, NSA_ROWS), F32)],
        compiler_params=_cparams(("parallel", "arbitrary")),
        name="nsa_attention",
    )(qa3, gexp, kc, vct, knsa3, knsa3, vnsa_t, vnsa_t, aggt, et, eye)


FOX_TQ = 512
FOX_KC = FOX_TQ
FOX_ROWS = FOX_PAIR * FOX_TQ


def _fox_kernel(q_ref, k_ref, aux_ref, vt_ref, o_ref, sa_scr, sb_scr, acc_scr):
    qi = pl.program_id(2)
    s0 = qi * FOX_TQ
    lane = lax.broadcasted_iota(jnp.int32, (FOX_TQ, LANES), 1)
    low = lane < HEAD_DIM
    q2 = q_ref[...]
    zero = jnp.zeros_like(q2)
    aux0 = jnp.where(lane < AUX_PIECES, -1.0, 0.0).astype(q2.dtype)
    aux1 = jnp.where((lane >= AUX_PIECES) & (lane < 2 * AUX_PIECES), -1.0, 0.0).astype(q2.dtype)
    qx = jnp.concatenate([jnp.concatenate([jnp.where(low, q2, zero), aux0], axis=1),
                          jnp.concatenate([jnp.where(low, zero, q2), aux1], axis=1)], axis=0)
    t_l = s0 + lax.broadcasted_iota(jnp.int32, (1, FOX_TQ), 1)

    sub_keys = FOX_KC // FLASH_NSUB

    def scores(i, t, n=sub_keys):
        k0 = pl.multiple_of(i * FOX_KC + t * sub_keys, sub_keys)
        kx = jnp.concatenate([k_ref[pl.ds(k0, n), :], aux_ref[pl.ds(k0, n), :]], axis=1)
        return _dot_nt(kx, qx)

    def values(i, t, n=sub_keys):
        return vt_ref[:, pl.ds(pl.multiple_of(i * FOX_KC + t * sub_keys, sub_keys), n)]

    def diagonal():
        kpos = s0 + lax.broadcasted_iota(jnp.int32, (FOX_KC, 1), 0)
        s_d = _where_tiles(kpos <= t_l, scores(qi, 0, FOX_KC), NEG_INF)
        m_d = jnp.max(s_d, axis=0, keepdims=True)
        p_d = jnp.exp2(s_d - m_d)
        acc_scr[...] = _dot(values(qi, 0, FOX_KC), p_d.astype(BF16))
        return m_d, jnp.sum(p_d, axis=0, keepdims=True)

    l_f = _flash_chunks(qi, scores, values, diagonal, sa_scr, sb_scr, acc_scr)
    o_t = acc_scr[...] / l_f
    o_ref[...] = jnp.where(low, o_t[:, 0:FOX_TQ].T, o_t[:, FOX_TQ:FOX_ROWS].T)


def _fox_attention(fqk3, aux3, vfox_t):
    B, S, _ = fqk3.shape
    return pl.pallas_call(
        _fox_kernel,
        grid=(B, FOX_NPAIR, S // FOX_TQ),
        in_specs=[pl.BlockSpec((None, FOX_TQ, LANES), lambda b, h, i: (b, i, h)),
                  pl.BlockSpec((None, S, LANES), lambda b, h, i: (b, 0, FOX_NPAIR + h)),
                  pl.BlockSpec((None, S, LANES), lambda b, h, i: (b, 0, h)),
                  pl.BlockSpec((None, LANES, S), lambda b, h, i: (b, h, 0))],
        out_specs=pl.BlockSpec((None, FOX_TQ, LANES), lambda b, h, i: (b, i, h)),
        out_shape=jax.ShapeDtypeStruct((B, S, FOX_WIDTH), F32),
        scratch_shapes=[pltpu.VMEM((FOX_KC, FOX_ROWS), F32), pltpu.VMEM((FOX_KC, FOX_ROWS), F32),
                        pltpu.VMEM((LANES, FOX_ROWS), F32)],
        compiler_params=_cparams(("parallel", "parallel", "arbitrary")),
        name="fox_attention",
    )(fqk3, fqk3, aux3, vfox_t)


ROUTE_ROWS = 32


def _first_of(vals, v):
    idx = jnp.full(v.shape, len(vals) - 1, jnp.int32)
    for k in range(len(vals) - 2, -1, -1):
        idx = jnp.where(vals[k] == v, k, idx)
    return idx


def _route_t(lt):
    row = [lt[i:i + 1, :] for i in range(N_GROUPS + N_EXPERTS)]
    grp = row[:N_GROUPS]
    gmax = functools.reduce(jnp.maximum, grp)
    g_star = _first_of(grp, gmax)
    p_grp = 1.0 / functools.reduce(jnp.add, [jnp.exp(g - gmax) for g in grp])
    logit = []
    for k in range(EXPERTS_PER_GROUP):
        ek = row[ROUTE_E_OFF + (N_GROUPS - 1) * EXPERTS_PER_GROUP + k]
        for g in range(N_GROUPS - 2, -1, -1):
            ek = jnp.where(g_star == g, row[ROUTE_E_OFF + g * EXPERTS_PER_GROUP + k], ek)
        logit.append(ek)
    emax = functools.reduce(jnp.maximum, logit)
    eexp = [jnp.exp(e - emax) for e in logit]
    esum = functools.reduce(jnp.add, eexp)
    prob = [e / esum for e in eexp]
    v1 = functools.reduce(jnp.maximum, prob)
    i1 = _first_of(prob, v1)
    rest = [jnp.where(i1 == k, -1.0, prob[k]) for k in range(EXPERTS_PER_GROUP)]
    v2 = functools.reduce(jnp.maximum, rest)
    i2 = _first_of(rest, v2)
    den = v1 + v2
    w1 = p_grp * (v1 / den)
    w2 = p_grp * (v2 / den)
    base = ROUTE_E_OFF + g_star * EXPERTS_PER_GROUP
    r_idx = lax.broadcasted_iota(jnp.int32, lt.shape, 0)
    return (jnp.where(r_idx == base + i1, w1, 0.0) + jnp.where(r_idx == base + i2, w2, 0.0)
            + jnp.where(r_idx == ROUTE_G_LANE, g_star.astype(F32), 0.0))


def _outproj_kernel(on_ref, of_ref, x_ref, bn_ref, bf_ref, wn_ref, wf_ref, gffn_ref, wrt_ref, brt_ref,
                    h_ref, u_ref, comb_ref):
    mn = _rms(on_ref[...], bn_ref[...]).astype(BF16)
    mf = _rms(of_ref[...], bf_ref[...]).astype(BF16)
    h = x_ref[...] + (_dot(mn, wn_ref[...]) + _dot(mf, wf_ref[...]))
    h_ref[...] = h
    u = _rms(h, gffn_ref[...]).astype(BF16)
    u_ref[...] = u
    lt = _dot_nt(wrt_ref[...], u) + brt_ref[...]
    comb_t = _route_t(lt[0:ROUTE_ROWS])
    comb_t = jnp.concatenate([comb_t, jnp.zeros((LANES - ROUTE_ROWS, comb_t.shape[1]), F32)], axis=0)
    comb_ref[...] = comb_t.T


def _outproj(o_nsa, o_fox, x2, beta_n, beta_f, w_n, w_f, g_ffn, w_r, b_r, tm):
    T = x2.shape[0]
    full = lambda shape: pl.BlockSpec(shape, lambda i: (0,) * len(shape))
    row = lambda w: pl.BlockSpec((tm, w), lambda i: (i, 0))
    return pl.pallas_call(
        _outproj_kernel,
        grid=(T // tm,),
        in_specs=[row(NSA_WIDTH), row(FOX_WIDTH), row(D_MODEL), full((1, NSA_WIDTH)), full((1, FOX_WIDTH)),
                  full((NSA_WIDTH, D_MODEL)), full((FOX_WIDTH, D_MODEL)), full((1, D_MODEL)),
                  full((LANES, D_MODEL)), full((LANES, 1))],
        out_specs=[row(D_MODEL), row(D_MODEL), row(LANES)],
        out_shape=[jax.ShapeDtypeStruct((T, D_MODEL), F32),
                   jax.ShapeDtypeStruct((T, D_MODEL), BF16),
                   jax.ShapeDtypeStruct((T, LANES), F32)],
        compiler_params=_cparams(("parallel",)),
        name="outproj_router",
    )(o_nsa, o_fox, x2, beta_n, beta_f, w_n, w_f, g_ffn, w_r, b_r)


MOE_TS = 1024
MOE_RB = 128
MOE_TSP = MOE_TS + N_GROUPS * MOE_RB
MOE_META = 8
MOE_TPS = 2


def _moe_sort_kernel(u_ref, comb_ref, tri_ref, us_ref, cs_ref, pos_ref, meta_ref):
    comb = comb_ref[...]
    lane = lax.broadcasted_iota(jnp.int32, comb.shape, 1)
    grp = comb[:, ROUTE_G_LANE:ROUTE_G_LANE + 1].astype(jnp.int32)
    onehot = jnp.where((lane == grp) & (lane < N_GROUPS), 1.0, 0.0)
    rank = _dot(tri_ref[...], onehot.astype(BF16))
    counts = rank[MOE_TS - 1:MOE_TS, :]
    nblk = jnp.floor((counts + (MOE_RB - 1)) * (1.0 / MOE_RB))
    lane1 = lane[0:1, :]
    blk0 = jnp.zeros_like(nblk)
    run = jnp.zeros((1, 1), F32)
    for g in range(1, N_GROUPS):
        run = run + nblk[:, g - 1:g]
        blk0 = blk0 + jnp.where(lane1 == g, run, 0.0)
    val = onehot * (blk0 * MOE_RB + rank - 1.0)
    pos_col = jnp.sum(val, axis=-1, keepdims=True)
    pos_ref[...] = pos_col.astype(jnp.int32)
    ones = jnp.ones((SUBLANES, LANES), BF16)
    hi, mid, lo = _split3(val)
    pos_row = (_dot_nt(ones, hi) + _dot_nt(ones, mid) + _dot_nt(ones, lo))[0:1, :].astype(jnp.int32)
    r_idx = lax.broadcasted_iota(jnp.int32, (MOE_TSP, MOE_TS), 0)
    perm = jnp.where(r_idx == pos_row, 1.0, 0.0).astype(BF16)
    c_hi = comb.astype(BF16)
    c_lo = (comb - c_hi.astype(F32)).astype(BF16)
    moved = _dot(perm, jnp.concatenate([u_ref[...], c_hi, c_lo], axis=1))
    us_ref[...] = moved[:, 0:D_MODEL].astype(us_ref.dtype)
    cs_ref[...] = moved[:, D_MODEL:D_MODEL + LANES] + moved[:, D_MODEL + LANES:D_MODEL + 2 * LANES]
    meta = blk0 + pltpu.roll(nblk, N_GROUPS, 1)
    meta_ref[...] = jnp.where(lane1 < MOE_META, meta, 0.0).astype(jnp.int32)


def _moe_sort(u, comb, tri):
    T = u.shape[0]
    nt = T // MOE_TS
    return pl.pallas_call(
        _moe_sort_kernel,
        grid=(nt,),
        in_specs=[pl.BlockSpec((MOE_TS, D_MODEL), lambda i: (i, 0)),
                  pl.BlockSpec((MOE_TS, LANES), lambda i: (i, 0)),
                  pl.BlockSpec((MOE_TS, MOE_TS), lambda i: (0, 0))],
        out_specs=[pl.BlockSpec((MOE_TSP, D_MODEL), lambda i: (i, 0)),
                   pl.BlockSpec((MOE_TSP, LANES), lambda i: (i, 0)),
                   pl.BlockSpec((MOE_TS, 1), lambda i: (i, 0)),
                   pl.BlockSpec((None, 1, LANES), lambda i: (i, 0, 0))],
        out_shape=[jax.ShapeDtypeStruct((nt * MOE_TSP, D_MODEL), BF16),
                   jax.ShapeDtypeStruct((nt * MOE_TSP, LANES), F32),
                   jax.ShapeDtypeStruct((T, 1), jnp.int32),
                   jax.ShapeDtypeStruct((nt, 1, LANES), jnp.int32)],
        compiler_params=_cparams(("parallel",)),
        name="moe_sort",
    )(u, comb, tri)


def _moe_kernel(meta_ref, us_ref, cs_ref, wgu_ref, wd_ref, o_ref, acc_ref):
    i = pl.program_id(0)
    e = pl.program_id(1)
    g = e // EXPERTS_PER_GROUP

    @pl.when(e == 0)
    def _():
        acc_ref[...] = jnp.zeros_like(acc_ref)

    for tile in range(MOE_TPS):
        base = (i * MOE_TPS + tile) * MOE_META
        first = meta_ref[base + g] + tile * (MOE_TSP // MOE_RB)
        count = meta_ref[base + N_GROUPS + g]

        def blocks(j0, n, first=first):
            rows = [pl.multiple_of((first + j0 + k) * MOE_RB, MOE_RB) for k in range(n)]
            gus = [_dot(us_ref[pl.ds(r0, MOE_RB), :], wgu_ref[...]) for r0 in rows]
            for r0, gu in zip(rows, gus):
                cs = cs_ref[pl.ds(r0, MOE_RB), :]
                lane = lax.broadcasted_iota(jnp.int32, cs.shape, 1)
                ce = jnp.sum(jnp.where(lane == ROUTE_E_OFF + e, cs, 0.0), axis=-1, keepdims=True)
                gt = gu[:, 0:D_EXPERT]
                hid = gt * (1.0 / (1.0 + jnp.exp(-gt))) * gu[:, D_EXPERT:2 * D_EXPERT]
                acc_ref[pl.ds(r0, MOE_RB), :] += _dot((ce * hid).astype(BF16), wd_ref[...])

        def pair(j, carry, blocks=blocks):
            blocks(2 * j, 2)
            return carry

        def single(_, carry, blocks=blocks, count=count):
            blocks(count - 1, 1)
            return carry

        lax.fori_loop(0, count // 2, pair, 0)
        lax.fori_loop(0, count & 1, single, 0)

    @pl.when(e == N_EXPERTS - 1)
    def _():
        o_ref[...] = acc_ref[...].astype(o_ref.dtype)


def _moe(meta, us, cs, wgu, wd):
    rows = MOE_TPS * MOE_TSP
    grid_spec = pltpu.PrefetchScalarGridSpec(
        num_scalar_prefetch=1,
        grid=(us.shape[0] // rows, N_EXPERTS),
        in_specs=[pl.BlockSpec((rows, D_MODEL), lambda i, e, m: (i, 0)),
                  pl.BlockSpec((rows, LANES), lambda i, e, m: (i, 0)),
                  pl.BlockSpec((None, D_MODEL, 2 * D_EXPERT), lambda i, e, m: (e, 0, 0)),
                  pl.BlockSpec((None, D_EXPERT, D_MODEL), lambda i, e, m: (e, 0, 0))],
        out_specs=pl.BlockSpec((rows, D_MODEL), lambda i, e, m: (i, 0)),
        scratch_shapes=[pltpu.VMEM((rows, D_MODEL), F32)])
    return pl.pallas_call(
        _moe_kernel,
        grid_spec=grid_spec,
        out_shape=jax.ShapeDtypeStruct(us.shape, BF16),
        compiler_params=_cparams(("parallel", "arbitrary")),
        name="moe_experts",
    )(meta, us, cs, wgu, wd)


def _ple_kernel(ys_ref, pos_ref, h_ref, p_ref, gple_ref, wg_ref, bg_ref, wp_ref, gfin_ref, o_ref):
    r_idx = lax.broadcasted_iota(jnp.int32, (pos_ref.shape[0], MOE_TSP), 1)
    unperm = jnp.where(r_idx == pos_ref[...], 1.0, 0.0).astype(BF16)
    h = h_ref[...] + _dot(unperm, ys_ref[...])
    v = _rms(h, gple_ref[...]).astype(BF16)
    z = _dot(v, wg_ref[...]) + bg_ref[...]
    gate = 1.0 / (1.0 + jnp.exp(-z))
    proj = _dot(p_ref[...].astype(BF16), wp_ref[...])
    o_ref[...] = _rms(h + gate * proj, gfin_ref[...])


def _ple(ys, pos, h1, p2, g_ple, w_g, b_g, w_p, g_final, tm):
    T = h1.shape[0]
    per = MOE_TS // tm
    full = lambda shape: pl.BlockSpec(shape, lambda i: (0,) * len(shape))
    row = lambda w: pl.BlockSpec((tm, w), lambda i: (i, 0))
    return pl.pallas_call(
        _ple_kernel,
        grid=(T // tm,),
        in_specs=[pl.BlockSpec((MOE_TSP, D_MODEL), lambda i: (i // per, 0)), row(1), row(D_MODEL), row(PLE_DIM),
                  full((1, D_MODEL)), full((D_MODEL, D_MODEL)),
                  full((1, D_MODEL)), full((PLE_DIM, D_MODEL)), full((1, D_MODEL))],
        out_specs=row(D_MODEL),
        out_shape=jax.ShapeDtypeStruct((T, D_MODEL), F32),
        compiler_params=_cparams(("parallel",)),
        name="ple_final",
    )(ys, pos, h1, p2, g_ple, w_g, b_g, w_p, g_final)


def _rope_lane_tables():
    half = ROPE_HALF
    inv_freq = jnp.power(jnp.float32(ROPE_THETA), -jnp.arange(half, dtype=jnp.float32) / half)
    j = np.arange(LANES) % HEAD_DIM
    freq = jnp.where(jnp.asarray(j < ROPE_DIM), inv_freq[jnp.asarray(j % half)], 0.0).reshape(1, LANES)
    mlo = jnp.asarray(np.where(j < half, -1.0, 0.0), F32).reshape(1, LANES)
    mhi = jnp.asarray(np.where((j >= half) & (j < ROPE_DIM), 1.0, 0.0), F32).reshape(1, LANES)
    return freq.astype(F32), mlo, mhi


def _block_diag2(w):
    z = jnp.zeros_like(w)
    return jnp.concatenate([jnp.concatenate([w, z], axis=-1), jnp.concatenate([z, w], axis=-1)], axis=-2)


def _aux_route_table():
    r = np.zeros((AUX_PIECES, LANES, FOX_NPAIR * LANES), np.float32)
    for head in range(FOX_HEADS):
        for k in range(AUX_PIECES):
            r[k, SMALL_F_OFF + head, (head // FOX_PAIR) * LANES + (head % FOX_PAIR) * AUX_PIECES + k] = 1.0
    return jnp.asarray(r, BF16)


def _layer(h3, p3, positions, prm, g_final):
    B, S, _ = h3.shape
    T = B * S
    tm = 512
    x2 = h3.reshape(T, D_MODEL)
    freq, mlo, mhi = _rope_lane_tables()

    w_in = prm['w_in']
    offs = np.cumsum([0, NSA_WIDTH, KV_WIDTH, KV_WIDTH, KV_WIDTH, KV_WIDTH, KV_WIDTH, KV_WIDTH,
                      N_GATE, FOX_WIDTH, FOX_WIDTH, FOX_WIDTH, FOX_HEADS])
    seg = lambda k: w_in[:, offs[k]:offs[k + 1]]
    wq = seg(0).reshape(D_MODEL, NSA_GROUPS, NSA_HPG, HEAD_DIM).transpose(0, 2, 1, 3).reshape(D_MODEL, NSA_WIDTH)
    pad = jnp.zeros((D_MODEL, LANES - N_GATE - FOX_HEADS), w_in.dtype)
    w_all = jnp.concatenate([wq, seg(1), seg(2), seg(3), seg(5), seg(8), seg(9), seg(7), seg(11), pad],
                            axis=1).astype(BF16)
    wt_v = jnp.concatenate([seg(4), seg(6), seg(10)], axis=1).T.astype(BF16)
    bias_small = jnp.concatenate([prm['b_nsa_gate'], prm['b_forget'],
                                  jnp.zeros((LANES - N_GATE - FOX_HEADS,), F32)]).reshape(1, LANES)

    qa, cmp_tok, knsa, vnsa_t, fqk, vfox_t, small = _inproj(
        x2, positions.reshape(T, 1), prm['g_mix'].reshape(1, D_MODEL), w_all, wt_v, bias_small,
        freq, mlo, mhi, tm, B, S)

    tri = jnp.asarray(np.tril(np.ones((CUM_SUB, CUM_SUB), np.float32)), BF16)
    small3 = small.reshape(B, S, LANES)
    aux3 = _cumsum(small3, tri, _aux_route_table())

    n_rows = S // CMP_STRIDE
    tile2 = lambda pe: jnp.concatenate([pe, pe], axis=-1)
    bd1 = lambda w: _block_diag2(w.reshape(CMP_BLOCK, HEAD_DIM, CMP_HIDDEN)).astype(BF16)
    pos_cmp = positions[:, CMP_BLOCK - 1::CMP_STRIDE]
    pos_cmp = jnp.pad(pos_cmp, ((0, 0), (0, n_rows - pos_cmp.shape[1]))).reshape(B, n_rows, 1)
    kc, vct = _compress(cmp_tok.reshape(B, S, 2 * KV_WIDTH), tile2(prm['pe_cmp_k']), tile2(prm['pe_cmp_v']),
                        bd1(prm['w_cmp_k1']), bd1(prm['w_cmp_v1']),
                        _block_diag2(prm['w_cmp_k2']).astype(BF16), _block_diag2(prm['w_cmp_v2']).T.astype(BF16),
                        pos_cmp, freq, mlo, mhi)

    n_cmp = (S - CMP_BLOCK) // CMP_STRIDE + 1
    n_blk = S // SEL_BLOCK
    cs = np.arange(n_rows)[:, None] * CMP_STRIDE
    ss = np.arange(n_blk)[None, :] * SEL_BLOCK
    ov = np.clip(np.minimum(cs + CMP_BLOCK, ss + SEL_BLOCK) - np.maximum(cs, ss), 0, None) / CMP_BLOCK
    ov[n_cmp:] = 0.0
    aggt = jnp.asarray(ov.T, BF16)
    et = jnp.asarray((np.arange(S)[:, None] // SEL_BLOCK) == np.arange(LANES)[None, :], BF16)
    eye = jnp.asarray(np.eye(NSA_GQ, dtype=np.float32), BF16)
    gexp = small3[:, :, :N_GATE].reshape(B, S // NSA_QB, NSA_QB, NSA_GROUPS, NSA_HPG, 3)
    gexp = gexp.transpose(0, 1, 5, 4, 3, 2).reshape(B, S // NSA_QB, 3, NSA_ROWS)
    o_nsa = _nsa_attention(qa.reshape(B, S, NSA_WIDTH), gexp, kc, vct, knsa.reshape(B, S, 2 * KV_WIDTH),
                           vnsa_t, aggt, et, eye)

    o_fox = _fox_attention(fqk.reshape(B, S, 2 * FOX_WIDTH), aux3, vfox_t)

    perm = lambda a: a.reshape(NSA_GROUPS, NSA_HPG, HEAD_DIM, -1).transpose(1, 0, 2, 3).reshape(NSA_WIDTH, -1)
    beta_n = perm(prm['beta_nsa'].reshape(NSA_WIDTH, 1)).reshape(1, NSA_WIDTH)
    w_out = prm['w_out']
    w_n = perm(w_out[:NSA_WIDTH]).astype(BF16)
    w_f = w_out[NSA_WIDTH:].astype(BF16)
    w_r = jnp.concatenate([prm['w_group'], prm['w_router'],
                           jnp.zeros((D_MODEL, LANES - N_GROUPS - N_EXPERTS), F32)], axis=1).T.astype(BF16)
    b_r = jnp.concatenate([prm['b_group'], prm['b_router'],
                           jnp.zeros((LANES - N_GROUPS - N_EXPERTS,), F32)]).reshape(LANES, 1)
    h1, u, comb = _outproj(o_nsa.reshape(T, NSA_WIDTH), o_fox.reshape(T, FOX_WIDTH), x2, beta_n,
                           prm['beta_fox'].reshape(1, FOX_WIDTH), w_n, w_f,
                           prm['g_ffn'].reshape(1, D_MODEL), w_r, b_r, tm)

    wgu = jnp.concatenate([prm['w_gate_e'], prm['w_up_e']], axis=-1).astype(BF16)
    tri_s = jnp.asarray(np.tril(np.ones((MOE_TS, MOE_TS), np.float32)), BF16)
    us, cs, pos, meta = _moe_sort(u, comb, tri_s)
    ys = _moe(meta[:, 0, :MOE_META].reshape(-1), us, cs, wgu, prm['w_down_e'].astype(BF16))

    out = _ple(ys, pos, h1, p3.reshape(T, PLE_DIM), prm['g_ple'].reshape(1, D_MODEL),
               prm['w_ple_gate'].astype(BF16),
               prm['b_ple_gate'].reshape(1, D_MODEL), prm['w_ple_proj'].astype(BF16),
               g_final.reshape(1, D_MODEL), tm)
    return out.reshape(B, S, D_MODEL)


_PARAM_NAMES = ('g_mix', 'w_in', 'b_nsa_gate', 'b_forget', 'pe_cmp_k', 'w_cmp_k1', 'w_cmp_k2',
                'pe_cmp_v', 'w_cmp_v1', 'w_cmp_v2', 'beta_nsa', 'beta_fox', 'w_out', 'g_ffn',
                'w_group', 'b_group', 'w_router', 'b_router', 'w_gate_e', 'w_up_e', 'w_down_e',
                'g_ple', 'w_ple_gate', 'b_ple_gate', 'w_ple_proj')


def kernel(x, p, positions, g_mix, w_in, b_nsa_gate, b_forget, pe_cmp_k, w_cmp_k1, w_cmp_k2, pe_cmp_v,
           w_cmp_v1, w_cmp_v2, beta_nsa, beta_fox, w_out, g_ffn, w_group, b_group, w_router, b_router,
           w_gate_e, w_up_e, w_down_e, g_ple, w_ple_gate, b_ple_gate, w_ple_proj, g_final):
    stacked = (g_mix, w_in, b_nsa_gate, b_forget, pe_cmp_k, w_cmp_k1, w_cmp_k2, pe_cmp_v, w_cmp_v1,
               w_cmp_v2, beta_nsa, beta_fox, w_out, g_ffn, w_group, b_group, w_router, b_router,
               w_gate_e, w_up_e, w_down_e, g_ple, w_ple_gate, b_ple_gate, w_ple_proj)
    depth = w_in.shape[0]
    assert depth == 1, "the final norm is fused into the last layer's embedding kernel"
    prm = {n: a[0] for n, a in zip(_PARAM_NAMES, stacked)}
    return _layer(x, p[0], positions, prm, g_final)
```

```python
import functools
import math

import numpy as np
import jax
import jax.numpy as jnp
from jax import lax
from jax.experimental import pallas as pl
from jax.experimental.pallas import tpu as pltpu

D_MODEL = 1024
HEAD_DIM = 64
NSA_HEADS = 8
FOX_HEADS = 8
NSA_GROUPS = 2
NSA_HPG = 4
NSA_WIDTH = 512
FOX_WIDTH = 512
KV_WIDTH = 128
CMP_BLOCK = 32
CMP_STRIDE = 16
CMP_HIDDEN = 128
SEL_BLOCK = 64
N_SELECT = 16
WINDOW = 512
ROPE_THETA = 500000.0
ROPE_DIM = 16
ROPE_HALF = 8
N_GROUPS = 4
EXPERTS_PER_GROUP = 4
N_EXPERTS = 16
D_EXPERT = 512
PLE_DIM = 256
EPS = 1e-6
NEG_INF = -1e30
FORCE_BONUS = 1e4
LOG2E = math.log2(math.e)
Q_SCALE = 0.125 * LOG2E
NEG_BIG = -(2.0 ** 100)

LANES = 128
SUBLANES = 8
N_GATE = 3 * NSA_HEADS
SMALL_F_OFF = N_GATE
ROUTE_E_OFF = N_GROUPS
ROUTE_G_LANE = 0

VMEM_LIMIT = 56 * 1024 * 1024

F32 = jnp.float32
BF16 = jnp.bfloat16


def _cparams(sem):
    return pltpu.CompilerParams(dimension_semantics=sem, vmem_limit_bytes=VMEM_LIMIT)


def _dot(a, b):
    return jnp.dot(a, b, preferred_element_type=F32)


def _dot_nt(a, b):
    return lax.dot_general(a, b, (((1,), (1,)), ((), ())), preferred_element_type=F32)


def _split3(x):
    hi = x.astype(BF16)
    r1 = x - hi.astype(F32)
    mid = r1.astype(BF16)
    lo = (r1 - mid.astype(F32)).astype(BF16)
    return hi, mid, lo


def _split3_dot_lhs(e, x):
    hi, mid, lo = _split3(x)
    return _dot(e, hi) + _dot(e, mid) + _dot(e, lo)


def _rms(x, g):
    return x * lax.rsqrt(jnp.mean(x * x, axis=-1, keepdims=True) + EPS) * g


def _rope(r, cos, sin_lo, sin_hi):
    return (r * cos + pltpu.roll(r, LANES - ROPE_HALF, 1) * sin_lo
            + pltpu.roll(r, ROPE_HALF, 1) * sin_hi)


def _rope_tables(pos_col, freq, mlo, mhi):
    ang = pos_col.astype(F32) * freq
    cos = jnp.cos(ang)
    sin = jnp.sin(ang)
    return cos, sin * mlo, sin * mhi


def _rope_tables_from_row(pos_row, freq_col, mlo, mhi):
    m = pos_row.shape[1]
    ang_t = freq_col * pos_row.astype(F32)
    pad = jnp.zeros((LANES - ROPE_HALF, m), F32)
    lane = lax.broadcasted_iota(jnp.int32, (m, LANES), 1)
    rot = (lane & (HEAD_DIM - 1)) < ROPE_DIM

    def spread(t):
        x = jnp.concatenate([t, pad], axis=0).T
        x = x + pltpu.roll(x, ROPE_HALF, 1)
        return x + pltpu.roll(x, HEAD_DIM, 1)

    cos = jnp.where(rot, spread(jnp.cos(ang_t)), 1.0)
    sin = spread(jnp.sin(ang_t))
    return cos, sin * mlo, sin * mhi


def _inproj_kernel(x_ref, pos_ref, g_ref, w_ref, wt_ref, bias_ref, freq_ref, mlo_ref, mhi_ref,
                   qa_ref, cmp_ref, knsa_ref, vnsa_ref, fqk_ref, vfox_ref, small_ref):
    x = x_ref[...]
    hb = _rms(x, g_ref[...]).astype(BF16)
    cos, s_lo, s_hi = _rope_tables_from_row(pos_ref[...], freq_ref[...], mlo_ref[...], mhi_ref[...])
    off = 0
    r = _dot(hb, w_ref[:, off:off + NSA_WIDTH])
    for c in range(NSA_WIDTH // LANES):
        rc = _rope(r[:, c * LANES:(c + 1) * LANES], cos, s_lo, s_hi) * Q_SCALE
        qa_ref[:, c * LANES:(c + 1) * LANES] = rc.astype(qa_ref.dtype)
    off += NSA_WIDTH
    cmp_ref[...] = _dot(hb, w_ref[:, off:off + 2 * KV_WIDTH])
    off += 2 * KV_WIDTH
    r = _dot(hb, w_ref[:, off:off + 2 * KV_WIDTH])
    for c in range(2):
        rc = _rope(r[:, c * LANES:(c + 1) * LANES], cos, s_lo, s_hi)
        knsa_ref[:, c * LANES:(c + 1) * LANES] = rc.astype(knsa_ref.dtype)
    off += 2 * KV_WIDTH
    r = _dot(hb, w_ref[:, off:off + FOX_WIDTH])
    fqk_ref[:, 0:FOX_WIDTH] = (r * Q_SCALE).astype(fqk_ref.dtype)
    off += FOX_WIDTH
    r = _dot(hb, w_ref[:, off:off + FOX_WIDTH])
    fqk_ref[:, FOX_WIDTH:2 * FOX_WIDTH] = r.astype(fqk_ref.dtype)
    off += FOX_WIDTH
    z = _dot(hb, w_ref[:, off:off + LANES]) + bias_ref[...]
    lane = lax.broadcasted_iota(jnp.int32, z.shape, 1)
    sig = 1.0 / (1.0 + jnp.exp(-z))
    logsig = jnp.minimum(z, 0.0) - jnp.log(1.0 + jnp.exp(-jnp.abs(z)))
    small_ref[...] = jnp.where(lane < SMALL_F_OFF, sig, logsig)
    vt = _dot_nt(wt_ref[...], hb)
    vnsa_ref[...] = vt[0:2 * KV_WIDTH].astype(vnsa_ref.dtype)
    vfox_ref[...] = vt[2 * KV_WIDTH:2 * KV_WIDTH + FOX_WIDTH].astype(vfox_ref.dtype)


def _inproj(x2, pos2, g_mix, w_all, wt_v, bias_small, freq, mlo, mhi, tm, B, S):
    T = x2.shape[0]
    nt = S // tm
    full = lambda shape: pl.BlockSpec(shape, lambda i: (0,) * len(shape))
    row = lambda w: pl.BlockSpec((tm, w), lambda i: (i, 0))
    tr = lambda w: pl.BlockSpec((None, w, tm), lambda i: (i // nt, 0, i % nt))
    return pl.pallas_call(
        _inproj_kernel,
        grid=(T // tm,),
        in_specs=[row(D_MODEL), pl.BlockSpec((None, 1, tm), lambda i: (i, 0, 0)), full((1, D_MODEL)),
                  full(w_all.shape), full(wt_v.shape),
                  full((1, LANES)), full((ROPE_HALF, 1)), full((1, LANES)), full((1, LANES))],
        out_specs=[row(NSA_WIDTH), row(2 * KV_WIDTH), row(2 * KV_WIDTH), tr(2 * KV_WIDTH),
                   row(2 * FOX_WIDTH), tr(FOX_WIDTH), row(LANES)],
        out_shape=[jax.ShapeDtypeStruct((T, NSA_WIDTH), BF16),
                   jax.ShapeDtypeStruct((T, 2 * KV_WIDTH), F32),
                   jax.ShapeDtypeStruct((T, 2 * KV_WIDTH), BF16),
                   jax.ShapeDtypeStruct((B, 2 * KV_WIDTH, S), BF16),
                   jax.ShapeDtypeStruct((T, 2 * FOX_WIDTH), BF16),
                   jax.ShapeDtypeStruct((B, FOX_WIDTH, S), BF16),
                   jax.ShapeDtypeStruct((T, LANES), F32)],
        compiler_params=_cparams(("parallel",)),
        name="inproj",
    )(x2, pos2, g_mix, w_all, wt_v, bias_small, freq, mlo, mhi)


CUM_CHUNK = 1024
CUM_SUB = 128
FOX_PAIR = 2
FOX_NPAIR = FOX_HEADS // FOX_PAIR
AUX_PIECES = 3


def _cumsum_kernel(x_ref, tri_ref, route_ref, o_ref, carry_ref):
    @pl.when(pl.program_id(1) == 0)
    def _():
        carry_ref[...] = jnp.zeros_like(carry_ref)

    carry = carry_ref[...]
    pieces = []
    for k in range(CUM_CHUNK // CUM_SUB):
        ck = _split3_dot_lhs(tri_ref[...], x_ref[k * CUM_SUB:(k + 1) * CUM_SUB, :]) + carry
        carry = ck[CUM_SUB - 1:CUM_SUB, :]
        pieces.append(ck)
    carry_ref[...] = carry
    c = jnp.concatenate(pieces, axis=0)
    hi, mid, lo = _split3(c * LOG2E)
    aux = _dot(hi, route_ref[0]) + _dot(mid, route_ref[1]) + _dot(lo, route_ref[2])
    o_ref[...] = aux.astype(o_ref.dtype)


def _cumsum(small3, tri, route):
    B, S, _ = small3.shape
    width = FOX_NPAIR * LANES
    return pl.pallas_call(
        _cumsum_kernel,
        grid=(B, S // CUM_CHUNK),
        in_specs=[pl.BlockSpec((None, CUM_CHUNK, LANES), lambda b, i: (b, i, 0)),
                  pl.BlockSpec((CUM_SUB, CUM_SUB), lambda b, i: (0, 0)),
                  pl.BlockSpec((AUX_PIECES, LANES, width), lambda b, i: (0, 0, 0))],
        out_specs=pl.BlockSpec((None, CUM_CHUNK, width), lambda b, i: (b, i, 0)),
        out_shape=jax.ShapeDtypeStruct((B, S, width), BF16),
        scratch_shapes=[pltpu.VMEM((1, LANES), F32)],
        compiler_params=_cparams(("parallel", "arbitrary")),
        name="forget_cumsum",
    )(small3, tri, route)


def _compress_kernel(tokk_ref, tokv_ref, pek_ref, pev_ref, bdk1_ref, bdv1_ref, bdk2_ref, bdv2t_ref,
                     pos_ref, freq_ref, mlo_ref, mhi_ref, kc_ref, vct_ref, *, n_rows):
    half = CMP_BLOCK // 2
    ak = jnp.zeros((n_rows, 2 * CMP_HIDDEN), F32)
    bk = jnp.zeros((n_rows, 2 * CMP_HIDDEN), F32)
    av = jnp.zeros((n_rows, 2 * CMP_HIDDEN), F32)
    bv = jnp.zeros((n_rows, 2 * CMP_HIDDEN), F32)
    for l in range(half):
        xk = tokk_ref[pl.ds(l, n_rows, stride=CMP_STRIDE), :]
        xv = tokv_ref[pl.ds(l, n_rows, stride=CMP_STRIDE), :]
        ak = ak + _dot((xk + pek_ref[l:l + 1, :]).astype(BF16), bdk1_ref[l])
        bk = bk + _dot((xk + pek_ref[half + l:half + l + 1, :]).astype(BF16), bdk1_ref[half + l])
        av = av + _dot((xv + pev_ref[l:l + 1, :]).astype(BF16), bdv1_ref[l])
        bv = bv + _dot((xv + pev_ref[half + l:half + l + 1, :]).astype(BF16), bdv1_ref[half + l])
    hk = ak + pltpu.roll(bk, n_rows - 1, 0)
    hv = av + pltpu.roll(bv, n_rows - 1, 0)
    hk = hk * (1.0 / (1.0 + jnp.exp(-hk)))
    hv = hv * (1.0 / (1.0 + jnp.exp(-hv)))
    kc = _dot(hk.astype(BF16), bdk2_ref[...])
    cos, s_lo, s_hi = _rope_tables(pos_ref[...], freq_ref[...], mlo_ref[...], mhi_ref[...])
    kc_ref[...] = _rope(kc, cos, s_lo, s_hi).astype(kc_ref.dtype)
    vct_ref[...] = _dot_nt(bdv2t_ref[...], hv.astype(BF16)).astype(vct_ref.dtype)


def _compress(cmp_tok3, pek, pev, bdk1, bdv1, bdk2, bdv2t, pos_cmp, freq, mlo, mhi):
    B, S, _ = cmp_tok3.shape
    n_rows = S // CMP_STRIDE
    full = lambda shape: pl.BlockSpec(shape, lambda b: (0,) * len(shape))
    return pl.pallas_call(
        functools.partial(_compress_kernel, n_rows=n_rows),
        grid=(B,),
        in_specs=[pl.BlockSpec((None, S, KV_WIDTH), lambda b: (b, 0, 0)),
                  pl.BlockSpec((None, S, KV_WIDTH), lambda b: (b, 0, 1)),
                  full((CMP_BLOCK, KV_WIDTH)), full((CMP_BLOCK, KV_WIDTH)),
                  full((CMP_BLOCK, KV_WIDTH, 2 * CMP_HIDDEN)), full((CMP_BLOCK, KV_WIDTH, 2 * CMP_HIDDEN)),
                  full((2 * CMP_HIDDEN, KV_WIDTH)), full((KV_WIDTH, 2 * CMP_HIDDEN)),
                  pl.BlockSpec((None, n_rows, 1), lambda b: (b, 0, 0)),
                  full((1, LANES)), full((1, LANES)), full((1, LANES))],
        out_specs=[pl.BlockSpec((None, n_rows, KV_WIDTH), lambda b: (b, 0, 0)),
                   pl.BlockSpec((None, KV_WIDTH, n_rows), lambda b: (b, 0, 0))],
        out_shape=[jax.ShapeDtypeStruct((B, n_rows, KV_WIDTH), BF16),
                   jax.ShapeDtypeStruct((B, KV_WIDTH, n_rows), BF16)],
        compiler_params=_cparams(("parallel",)),
        name="nsa_compress",
    )(cmp_tok3, cmp_tok3, pek, pev, bdk1, bdv1, bdk2, bdv2t, pos_cmp, freq, mlo, mhi)


NSA_QB = 128
NSA_BPQ = NSA_QB // SEL_BLOCK
assert NSA_BPQ in (1, 2) and NSA_QB % LANES == 0
NSA_KC = 512
NSA_GQ = NSA_GROUPS * NSA_QB
NSA_ROWS = NSA_HPG * NSA_GQ
NSA_WSPAN = WINDOW + NSA_QB


def _where_tiles(mask, x, other):
    w = mask.shape[1]
    return jnp.concatenate([jnp.where(mask, x[:, t * w:(t + 1) * w], other)
                            for t in range(x.shape[1] // w)], axis=1)


def _masked_softmax_cols(s, mask):
    l = _where_tiles(mask, s, NEG_INF)
    m = jnp.max(l, axis=0, keepdims=True)
    e = jnp.exp2(l - m)
    return e, jnp.sum(e, axis=0, keepdims=True)


FLASH_NSUB = 2


def _flash_chunks(n_chunks, score_fn, vt_fn, init_fn, s_a, s_b, acc_ref):
    last = jnp.maximum(n_chunks - 1, 0)
    sub = s_a.shape[0] // FLASH_NSUB

    def stage(i_prod, s_prod, i_cons, s_cons, mx, m, l):
        if i_cons is not None:
            m_new = jnp.maximum(m, mx)
            alpha = jnp.exp2(m - m_new)
            l = alpha * l
        if i_prod is not None:
            i_prod = jnp.minimum(i_prod, last)
        mx_prod, pv = None, None
        for t in range(FLASH_NSUB):
            rows = pl.ds(t * sub, sub)
            if i_prod is not None:
                s = score_fn(i_prod, t)
                s_prod[rows, :] = s
                mx_t = jnp.max(s, axis=0, keepdims=True)
                mx_prod = mx_t if mx_prod is None else jnp.maximum(mx_prod, mx_t)
            if i_cons is not None:
                p = jnp.exp2(s_cons[rows, :] - m_new)
                l = l + jnp.sum(p, axis=0, keepdims=True)
                d = _dot(vt_fn(i_cons, t), p.astype(BF16))
                pv = d if pv is None else pv + d
        if i_cons is not None:
            acc_ref[...] = alpha * acc_ref[...] + pv
            m = m_new
        return mx_prod, m, l

    def pair(j, carry):
        mx_a, m, l = carry
        i = 2 * j
        mx_b, m, l = stage(i + 1, s_b, i, s_a, mx_a, m, l)
        mx_a, m, l = stage(i + 2, s_a, i + 1, s_b, mx_b, m, l)
        return mx_a, m, l

    def tail(_, carry):
        mx_a, m, l = carry
        _, m, l = stage(None, None, last, s_a, mx_a, m, l)
        return mx_a, m, l

    mx_a, _, _ = stage(0, s_a, None, None, None, None, None)
    m, l = init_fn()
    carry = (mx_a, m, l)
    carry = lax.fori_loop(0, n_chunks // 2, pair, carry)
    carry = lax.fori_loop(0, n_chunks & 1, tail, carry)
    return carry[2]


def _nsa_kernel(q_ref, gate_ref, kc_ref, vct_ref, ks_ref, kw_ref, vst_ref, vwt_ref, aggt_ref, et_ref,
                eye_ref, o_ref, sa_scr, sb_scr, acc_scr, *, n_cmp, n_blk, n_sel):
    c = pl.program_id(1)
    s0 = c * NSA_QB
    lane = lax.broadcasted_iota(jnp.int32, (NSA_QB, LANES), 1)
    low = lane < HEAD_DIM
    qb = q_ref[...]
    zero = jnp.zeros((NSA_QB, LANES), qb.dtype)
    pieces = []
    for h in range(NSA_HPG):
        qh = qb[:, h * LANES:(h + 1) * LANES]
        pieces.append(jnp.where(low, qh, zero))
        pieces.append(jnp.where(low, zero, qh))
    qp = jnp.concatenate(pieces, axis=0)
    t_l = s0 + (lax.broadcasted_iota(jnp.int32, (1, NSA_GQ), 1) & (NSA_QB - 1))
    cur_l = t_l // SEL_BLOCK
    first_blk = c * NSA_BPQ
    last_blk = first_blk + NSA_BPQ - 1

    n_pad = kc_ref.shape[0]
    n_s = lax.broadcasted_iota(jnp.int32, (n_pad, 1), 0)
    mask_c = ((n_s * CMP_STRIDE + (CMP_BLOCK - 1)) <= t_l) & (n_s < n_cmp)
    start = pl.multiple_of(jnp.maximum(s0 - WINDOW, 0), LANES)
    rel = t_l - (start + lax.broadcasted_iota(jnp.int32, (NSA_WSPAN, 1), 0))
    mask_w = (rel >= 0) & (rel < WINDOW)
    s_c = _dot_nt(kc_ref[...], qp)
    s_w = _dot_nt(kw_ref[pl.ds(start, NSA_WSPAN), :], qp)
    e_c, sum_c = _masked_softmax_cols(s_c, mask_c)
    inv_c = _where_tiles(t_l >= CMP_BLOCK - 1, 1.0 / sum_c, 0.0)
    acc_c = _dot(vct_ref[...], e_c.astype(BF16))

    j_s = lax.broadcasted_iota(jnp.int32, (n_blk, NSA_GQ), 0)
    nv = n_blk // SUBLANES

    def ranked_mask():
        p_c = e_c * inv_c
        psum = p_c[:, 0:NSA_GQ]
        for h in range(1, NSA_HPG):
            psum = psum + p_c[:, h * NSA_GQ:(h + 1) * NSA_GQ]
        imp = _split3_dot_lhs(aggt_ref[...], psum)
        forced = (j_s == 0) | (j_s == cur_l) | (j_s == cur_l - 1)
        score = jnp.where(j_s > cur_l, -1.0, imp + jnp.where(forced, FORCE_BONUS, 0.0))
        sc = [score[SUBLANES * v:SUBLANES * (v + 1), :] for v in range(nv)]
        sub = lax.broadcasted_iota(jnp.int32, (SUBLANES, NSA_GQ), 0)

        def count_group(vi, rk):
            rk = list(rk)
            for ri in range(SUBLANES):
                row = sc[vi][ri:ri + 1, :]
                for v in range(nv):
                    if v < vi:
                        beats = jnp.where(row > sc[v], 1.0, 0.0)
                    elif v > vi:
                        beats = jnp.where(row >= sc[v], 1.0, 0.0)
                    else:
                        beats = jnp.where(sub > ri, jnp.where(row >= sc[v], 1.0, 0.0),
                                          jnp.where(row > sc[v], 1.0, 0.0))
                    rk[v] = rk[v] + beats
            return tuple(rk)

        rk = tuple(jnp.zeros((SUBLANES, NSA_GQ), F32) for _ in range(nv))
        for vi in range(nv):
            rk = lax.cond(vi * SUBLANES <= last_blk, functools.partial(count_group, vi), lambda r: r, rk)
        rank = jnp.concatenate(rk, axis=0)
        return jnp.where((rank < n_sel) & (j_s < first_blk), 0.0, NEG_BIG)

    def all_mask():
        return jnp.where(j_s < first_blk, 0.0, NEG_BIG)

    neg_t = lax.cond(last_blk < n_sel, all_mask, ranked_mask)

    a0 = pl.multiple_of(s0, LANES)
    s_d = _dot_nt(ks_ref[pl.ds(a0, NSA_QB), :], qp)
    neg_t = jnp.concatenate([neg_t, jnp.zeros((LANES - n_blk, NSA_GQ), F32)], axis=0).astype(BF16)
    q_aux = _dot_nt(eye_ref[...], neg_t).astype(BF16)
    qx = jnp.concatenate([qp, jnp.concatenate([q_aux] * NSA_HPG, axis=0)], axis=1)

    sub_keys = NSA_KC // FLASH_NSUB

    def sel_scores(i, t):
        k0 = pl.multiple_of(i * NSA_KC + t * sub_keys, sub_keys)
        kx = jnp.concatenate([ks_ref[pl.ds(k0, sub_keys), :], et_ref[pl.ds(k0, sub_keys), :]], axis=1)
        return _dot_nt(kx, qx)

    def sel_values(i, t):
        return vst_ref[:, pl.ds(pl.multiple_of(i * NSA_KC + t * sub_keys, sub_keys), sub_keys)]

    window = {}

    def after_first_scores():
        kpos = a0 + lax.broadcasted_iota(jnp.int32, (NSA_QB, 1), 0)
        l_d = _where_tiles(kpos <= t_l, s_d, NEG_INF)
        m_d = jnp.max(l_d, axis=0, keepdims=True)
        p_d = jnp.exp2(l_d - m_d)
        acc_scr[...] = _dot(vst_ref[:, pl.ds(a0, NSA_QB)], p_d.astype(BF16))
        e_w, window['l'] = _masked_softmax_cols(s_w, mask_w)
        window['acc'] = _dot(vwt_ref[:, pl.ds(start, NSA_WSPAN)], e_w.astype(BF16))
        return m_d, jnp.sum(p_d, axis=0, keepdims=True)

    n_prev = (s0 + NSA_KC - 1) // NSA_KC
    l_s = _flash_chunks(n_prev, sel_scores, sel_values, after_first_scores, sa_scr, sb_scr, acc_scr)

    gates = gate_ref[...]
    o_t = ((gates[0:1, :] * inv_c) * acc_c + (gates[1:2, :] / l_s) * acc_scr[...]
           + (gates[2:3, :] / window['l']) * window['acc'])
    for h in range(NSA_HPG):
        tt = o_t[:, h * NSA_GQ:(h + 1) * NSA_GQ].T
        o_ref[:, h * LANES:(h + 1) * LANES] = jnp.where(low, tt[0:NSA_QB], tt[NSA_QB:NSA_GQ])


def _nsa_attention(qa3, gexp, kc, vct, knsa3, vnsa_t, aggt, et, eye):
    B, S, _ = qa3.shape
    n_pad = kc.shape[1]
    n_cmp = (S - CMP_BLOCK) // CMP_STRIDE + 1
    n_blk = S // SEL_BLOCK
    n_sel = min(N_SELECT, n_blk)
    const = lambda shape: pl.BlockSpec(shape, lambda b, c: (0,) * len(shape))
    return pl.pallas_call(
        functools.partial(_nsa_kernel, n_cmp=n_cmp, n_blk=n_blk, n_sel=n_sel),
        grid=(B, S // NSA_QB),
        in_specs=[pl.BlockSpec((None, NSA_QB, NSA_WIDTH), lambda b, c: (b, c, 0)),
                  pl.BlockSpec((None, None, 3, NSA_ROWS), lambda b, c: (b, c, 0, 0)),
                  pl.BlockSpec((None, n_pad, KV_WIDTH), lambda b, c: (b, 0, 0)),
                  pl.BlockSpec((None, KV_WIDTH, n_pad), lambda b, c: (b, 0, 0)),
                  pl.BlockSpec((None, S, KV_WIDTH), lambda b, c: (b, 0, 0)),
                  pl.BlockSpec((None, S, KV_WIDTH), lambda b, c: (b, 0, 1)),
                  pl.BlockSpec((None, KV_WIDTH, S), lambda b, c: (b, 0, 0)),
                  pl.BlockSpec((None, KV_WIDTH, S), lambda b, c: (b, 1, 0)),
                  const((n_blk, n_pad)), const((S, LANES)), const((NSA_GQ, NSA_GQ))],
        out_specs=pl.BlockSpec((None, NSA_QB, NSA_WIDTH), lambda b, c: (b, c, 0)),
        out_shape=jax.ShapeDtypeStruct((B, S, NSA_WIDTH), F32),
        scratch_shapes=[pltpu.VMEM((NSA_KC, NSA_ROWS), F32), pltpu.VMEM((NSA_KC, NSA_ROWS), F32),
                        pltpu.VMEM((LANES, NSA_ROWS), F32)],
        compiler_params=_cparams(("parallel", "arbitrary")),
        name="nsa_attention",
    )(qa3, gexp, kc, vct, knsa3, knsa3, vnsa_t, vnsa_t, aggt, et, eye)


FOX_TQ = 512
FOX_KC = FOX_TQ
FOX_ROWS = FOX_PAIR * FOX_TQ


def _fox_kernel(q_ref, k_ref, aux_ref, vt_ref, o_ref, sa_scr, sb_scr, acc_scr):
    qi = pl.program_id(2)
    s0 = qi * FOX_TQ
    lane = lax.broadcasted_iota(jnp.int32, (FOX_TQ, LANES), 1)
    low = lane < HEAD_DIM
    q2 = q_ref[...]
    zero = jnp.zeros_like(q2)
    aux0 = jnp.where(lane < AUX_PIECES, -1.0, 0.0).astype(q2.dtype)
    aux1 = jnp.where((lane >= AUX_PIECES) & (lane < 2 * AUX_PIECES), -1.0, 0.0).astype(q2.dtype)
    qx = jnp.concatenate([jnp.concatenate([jnp.where(low, q2, zero), aux0], axis=1),
                          jnp.concatenate([jnp.where(low, zero, q2), aux1], axis=1)], axis=0)
    t_l = s0 + lax.broadcasted_iota(jnp.int32, (1, FOX_TQ), 1)

    sub_keys = FOX_KC // FLASH_NSUB

    def scores(i, t, n=sub_keys):
        k0 = pl.multiple_of(i * FOX_KC + t * sub_keys, sub_keys)
        kx = jnp.concatenate([k_ref[pl.ds(k0, n), :], aux_ref[pl.ds(k0, n), :]], axis=1)
        return _dot_nt(kx, qx)

    def values(i, t, n=sub_keys):
        return vt_ref[:, pl.ds(pl.multiple_of(i * FOX_KC + t * sub_keys, sub_keys), n)]

    def diagonal():
        kpos = s0 + lax.broadcasted_iota(jnp.int32, (FOX_KC, 1), 0)
        s_d = _where_tiles(kpos <= t_l, scores(qi, 0, FOX_KC), NEG_INF)
        m_d = jnp.max(s_d, axis=0, keepdims=True)
        p_d = jnp.exp2(s_d - m_d)
        acc_scr[...] = _dot(values(qi, 0, FOX_KC), p_d.astype(BF16))
        return m_d, jnp.sum(p_d, axis=0, keepdims=True)

    l_f = _flash_chunks(qi, scores, values, diagonal, sa_scr, sb_scr, acc_scr)
    o_t = acc_scr[...] / l_f
    o_ref[...] = jnp.where(low, o_t[:, 0:FOX_TQ].T, o_t[:, FOX_TQ:FOX_ROWS].T)


def _fox_attention(fqk3, aux3, vfox_t):
    B, S, _ = fqk3.shape
    return pl.pallas_call(
        _fox_kernel,
        grid=(B, FOX_NPAIR, S // FOX_TQ),
        in_specs=[pl.BlockSpec((None, FOX_TQ, LANES), lambda b, h, i: (b, i, h)),
                  pl.BlockSpec((None, S, LANES), lambda b, h, i: (b, 0, FOX_NPAIR + h)),
                  pl.BlockSpec((None, S, LANES), lambda b, h, i: (b, 0, h)),
                  pl.BlockSpec((None, LANES, S), lambda b, h, i: (b, h, 0))],
        out_specs=pl.BlockSpec((None, FOX_TQ, LANES), lambda b, h, i: (b, i, h)),
        out_shape=jax.ShapeDtypeStruct((B, S, FOX_WIDTH), F32),
        scratch_shapes=[pltpu.VMEM((FOX_KC, FOX_ROWS), F32), pltpu.VMEM((FOX_KC, FOX_ROWS), F32),
                        pltpu.VMEM((LANES, FOX_ROWS), F32)],
        compiler_params=_cparams(("parallel", "parallel", "arbitrary")),
        name="fox_attention",
    )(fqk3, fqk3, aux3, vfox_t)


ROUTE_ROWS = 32


def _first_of(vals, v):
    idx = jnp.full(v.shape, len(vals) - 1, jnp.int32)
    for k in range(len(vals) - 2, -1, -1):
        idx = jnp.where(vals[k] == v, k, idx)
    return idx


def _route_t(lt):
    row = [lt[i:i + 1, :] for i in range(N_GROUPS + N_EXPERTS)]
    grp = row[:N_GROUPS]
    gmax = functools.reduce(jnp.maximum, grp)
    g_star = _first_of(grp, gmax)
    p_grp = 1.0 / functools.reduce(jnp.add, [jnp.exp(g - gmax) for g in grp])
    logit = []
    for k in range(EXPERTS_PER_GROUP):
        ek = row[ROUTE_E_OFF + (N_GROUPS - 1) * EXPERTS_PER_GROUP + k]
        for g in range(N_GROUPS - 2, -1, -1):
            ek = jnp.where(g_star == g, row[ROUTE_E_OFF + g * EXPERTS_PER_GROUP + k], ek)
        logit.append(ek)
    emax = functools.reduce(jnp.maximum, logit)
    eexp = [jnp.exp(e - emax) for e in logit]
    esum = functools.reduce(jnp.add, eexp)
    prob = [e / esum for e in eexp]
    v1 = functools.reduce(jnp.maximum, prob)
    i1 = _first_of(prob, v1)
    rest = [jnp.where(i1 == k, -1.0, prob[k]) for k in range(EXPERTS_PER_GROUP)]
    v2 = functools.reduce(jnp.maximum, rest)
    i2 = _first_of(rest, v2)
    den = v1 + v2
    w1 = p_grp * (v1 / den)
    w2 = p_grp * (v2 / den)
    base = ROUTE_E_OFF + g_star * EXPERTS_PER_GROUP
    r_idx = lax.broadcasted_iota(jnp.int32, lt.shape, 0)
    return (jnp.where(r_idx == base + i1, w1, 0.0) + jnp.where(r_idx == base + i2, w2, 0.0)
            + jnp.where(r_idx == ROUTE_G_LANE, g_star.astype(F32), 0.0))


def _outproj_kernel(on_ref, of_ref, x_ref, bn_ref, bf_ref, wn_ref, wf_ref, gffn_ref, wrt_ref, brt_ref,
                    h_ref, u_ref, comb_ref):
    mn = _rms(on_ref[...], bn_ref[...]).astype(BF16)
    mf = _rms(of_ref[...], bf_ref[...]).astype(BF16)
    h = x_ref[...] + (_dot(mn, wn_ref[...]) + _dot(mf, wf_ref[...]))
    h_ref[...] = h
    u = _rms(h, gffn_ref[...]).astype(BF16)
    u_ref[...] = u
    lt = _dot_nt(wrt_ref[...], u) + brt_ref[...]
    comb_t = _route_t(lt[0:ROUTE_ROWS])
    comb_t = jnp.concatenate([comb_t, jnp.zeros((LANES - ROUTE_ROWS, comb_t.shape[1]), F32)], axis=0)
    comb_ref[...] = comb_t.T


def _outproj(o_nsa, o_fox, x2, beta_n, beta_f, w_n, w_f, g_ffn, w_r, b_r, tm):
    T = x2.shape[0]
    full = lambda shape: pl.BlockSpec(shape, lambda i: (0,) * len(shape))
    row = lambda w: pl.BlockSpec((tm, w), lambda i: (i, 0))
    return pl.pallas_call(
        _outproj_kernel,
        grid=(T // tm,),
        in_specs=[row(NSA_WIDTH), row(FOX_WIDTH), row(D_MODEL), full((1, NSA_WIDTH)), full((1, FOX_WIDTH)),
                  full((NSA_WIDTH, D_MODEL)), full((FOX_WIDTH, D_MODEL)), full((1, D_MODEL)),
                  full((LANES, D_MODEL)), full((LANES, 1))],
        out_specs=[row(D_MODEL), row(D_MODEL), row(LANES)],
        out_shape=[jax.ShapeDtypeStruct((T, D_MODEL), F32),
                   jax.ShapeDtypeStruct((T, D_MODEL), BF16),
                   jax.ShapeDtypeStruct((T, LANES), F32)],
        compiler_params=_cparams(("parallel",)),
        name="outproj_router",
    )(o_nsa, o_fox, x2, beta_n, beta_f, w_n, w_f, g_ffn, w_r, b_r)


MOE_TS = 1024
MOE_RB = 128
MOE_TSP = MOE_TS + N_GROUPS * MOE_RB
MOE_META = 8
MOE_TPS = 2


def _moe_sort_kernel(u_ref, comb_ref, tri_ref, us_ref, cs_ref, pos_ref, meta_ref):
    comb = comb_ref[...]
    lane = lax.broadcasted_iota(jnp.int32, comb.shape, 1)
    grp = comb[:, ROUTE_G_LANE:ROUTE_G_LANE + 1].astype(jnp.int32)
    onehot = jnp.where((lane == grp) & (lane < N_GROUPS), 1.0, 0.0)
    rank = _dot(tri_ref[...], onehot.astype(BF16))
    counts = rank[MOE_TS - 1:MOE_TS, :]
    nblk = jnp.floor((counts + (MOE_RB - 1)) * (1.0 / MOE_RB))
    lane1 = lane[0:1, :]
    blk0 = jnp.zeros_like(nblk)
    run = jnp.zeros((1, 1), F32)
    for g in range(1, N_GROUPS):
        run = run + nblk[:, g - 1:g]
        blk0 = blk0 + jnp.where(lane1 == g, run, 0.0)
    val = onehot * (blk0 * MOE_RB + rank - 1.0)
    pos_col = jnp.sum(val, axis=-1, keepdims=True)
    pos_ref[...] = pos_col.astype(jnp.int32)
    ones = jnp.ones((SUBLANES, LANES), BF16)
    hi, mid, lo = _split3(val)
    pos_row = (_dot_nt(ones, hi) + _dot_nt(ones, mid) + _dot_nt(ones, lo))[0:1, :].astype(jnp.int32)
    r_idx = lax.broadcasted_iota(jnp.int32, (MOE_TSP, MOE_TS), 0)
    perm = jnp.where(r_idx == pos_row, 1.0, 0.0).astype(BF16)
    c_hi = comb.astype(BF16)
    c_lo = (comb - c_hi.astype(F32)).astype(BF16)
    moved = _dot(perm, jnp.concatenate([u_ref[...], c_hi, c_lo], axis=1))
    us_ref[...] = moved[:, 0:D_MODEL].astype(us_ref.dtype)
    cs_ref[...] = moved[:, D_MODEL:D_MODEL + LANES] + moved[:, D_MODEL + LANES:D_MODEL + 2 * LANES]
    meta = blk0 + pltpu.roll(nblk, N_GROUPS, 1)
    meta_ref[...] = jnp.where(lane1 < MOE_META, meta, 0.0).astype(jnp.int32)


def _moe_sort(u, comb, tri):
    T = u.shape[0]
    nt = T // MOE_TS
    return pl.pallas_call(
        _moe_sort_kernel,
        grid=(nt,),
        in_specs=[pl.BlockSpec((MOE_TS, D_MODEL), lambda i: (i, 0)),
                  pl.BlockSpec((MOE_TS, LANES), lambda i: (i, 0)),
                  pl.BlockSpec((MOE_TS, MOE_TS), lambda i: (0, 0))],
        out_specs=[pl.BlockSpec((MOE_TSP, D_MODEL), lambda i: (i, 0)),
                   pl.BlockSpec((MOE_TSP, LANES), lambda i: (i, 0)),
                   pl.BlockSpec((MOE_TS, 1), lambda i: (i, 0)),
                   pl.BlockSpec((None, 1, LANES), lambda i: (i, 0, 0))],
        out_shape=[jax.ShapeDtypeStruct((nt * MOE_TSP, D_MODEL), BF16),
                   jax.ShapeDtypeStruct((nt * MOE_TSP, LANES), F32),
                   jax.ShapeDtypeStruct((T, 1), jnp.int32),
                   jax.ShapeDtypeStruct((nt, 1, LANES), jnp.int32)],
        compiler_params=_cparams(("parallel",)),
        name="moe_sort",
    )(u, comb, tri)


def _moe_kernel(meta_ref, us_ref, cs_ref, wgu_ref, wd_ref, o_ref, acc_ref):
    i = pl.program_id(0)
    e = pl.program_id(1)
    g = e // EXPERTS_PER_GROUP

    @pl.when(e == 0)
    def _():
        acc_ref[...] = jnp.zeros_like(acc_ref)

    for tile in range(MOE_TPS):
        base = (i * MOE_TPS + tile) * MOE_META
        first = meta_ref[base + g] + tile * (MOE_TSP // MOE_RB)
        count = meta_ref[base + N_GROUPS + g]

        def blocks(j0, n, first=first):
            rows = [pl.multiple_of((first + j0 + k) * MOE_RB, MOE_RB) for k in range(n)]
            gus = [_dot(us_ref[pl.ds(r0, MOE_RB), :], wgu_ref[...]) for r0 in rows]
            for r0, gu in zip(rows, gus):
                cs = cs_ref[pl.ds(r0, MOE_RB), :]
                lane = lax.broadcasted_iota(jnp.int32, cs.shape, 1)
                ce = jnp.sum(jnp.where(lane == ROUTE_E_OFF + e, cs, 0.0), axis=-1, keepdims=True)
                gt = gu[:, 0:D_EXPERT]
                hid = gt * (1.0 / (1.0 + jnp.exp(-gt))) * gu[:, D_EXPERT:2 * D_EXPERT]
                acc_ref[pl.ds(r0, MOE_RB), :] += _dot((ce * hid).astype(BF16), wd_ref[...])

        def pair(j, carry, blocks=blocks):
            blocks(2 * j, 2)
            return carry

        def single(_, carry, blocks=blocks, count=count):
            blocks(count - 1, 1)
            return carry

        lax.fori_loop(0, count // 2, pair, 0)
        lax.fori_loop(0, count & 1, single, 0)

    @pl.when(e == N_EXPERTS - 1)
    def _():
        o_ref[...] = acc_ref[...].astype(o_ref.dtype)


def _moe(meta, us, cs, wgu, wd):
    rows = MOE_TPS * MOE_TSP
    grid_spec = pltpu.PrefetchScalarGridSpec(
        num_scalar_prefetch=1,
        grid=(us.shape[0] // rows, N_EXPERTS),
        in_specs=[pl.BlockSpec((rows, D_MODEL), lambda i, e, m: (i, 0)),
                  pl.BlockSpec((rows, LANES), lambda i, e, m: (i, 0)),
                  pl.BlockSpec((None, D_MODEL, 2 * D_EXPERT), lambda i, e, m: (e, 0, 0)),
                  pl.BlockSpec((None, D_EXPERT, D_MODEL), lambda i, e, m: (e, 0, 0))],
        out_specs=pl.BlockSpec((rows, D_MODEL), lambda i, e, m: (i, 0)),
        scratch_shapes=[pltpu.VMEM((rows, D_MODEL), F32)])
    return pl.pallas_call(
        _moe_kernel,
        grid_spec=grid_spec,
        out_shape=jax.ShapeDtypeStruct(us.shape, BF16),
        compiler_params=_cparams(("parallel", "arbitrary")),
        name="moe_experts",
    )(meta, us, cs, wgu, wd)


def _ple_kernel(ys_ref, pos_ref, h_ref, p_ref, gple_ref, wg_ref, bg_ref, wp_ref, gfin_ref, o_ref):
    r_idx = lax.broadcasted_iota(jnp.int32, (pos_ref.shape[0], MOE_TSP), 1)
    unperm = jnp.where(r_idx == pos_ref[...], 1.0, 0.0).astype(BF16)
    h = h_ref[...] + _dot(unperm, ys_ref[...])
    v = _rms(h, gple_ref[...]).astype(BF16)
    z = _dot(v, wg_ref[...]) + bg_ref[...]
    gate = 1.0 / (1.0 + jnp.exp(-z))
    proj = _dot(p_ref[...].astype(BF16), wp_ref[...])
    o_ref[...] = _rms(h + gate * proj, gfin_ref[...])


def _ple(ys, pos, h1, p2, g_ple, w_g, b_g, w_p, g_final, tm):
    T = h1.shape[0]
    per = MOE_TS // tm
    full = lambda shape: pl.BlockSpec(shape, lambda i: (0,) * len(shape))
    row = lambda w: pl.BlockSpec((tm, w), lambda i: (i, 0))
    return pl.pallas_call(
        _ple_kernel,
        grid=(T // tm,),
        in_specs=[pl.BlockSpec((MOE_TSP, D_MODEL), lambda i: (i // per, 0)), row(1), row(D_MODEL), row(PLE_DIM),
                  full((1, D_MODEL)), full((D_MODEL, D_MODEL)),
                  full((1, D_MODEL)), full((PLE_DIM, D_MODEL)), full((1, D_MODEL))],
        out_specs=row(D_MODEL),
        out_shape=jax.ShapeDtypeStruct((T, D_MODEL), F32),
        compiler_params=_cparams(("parallel",)),
        name="ple_final",
    )(ys, pos, h1, p2, g_ple, w_g, b_g, w_p, g_final)


def _rope_lane_tables():
    half = ROPE_HALF
    inv_freq = jnp.power(jnp.float32(ROPE_THETA), -jnp.arange(half, dtype=jnp.float32) / half)
    j = np.arange(LANES) % HEAD_DIM
    freq = jnp.where(jnp.asarray(j < ROPE_DIM), inv_freq[jnp.asarray(j % half)], 0.0).reshape(1, LANES)
    mlo = jnp.asarray(np.where(j < half, -1.0, 0.0), F32).reshape(1, LANES)
    mhi = jnp.asarray(np.where((j >= half) & (j < ROPE_DIM), 1.0, 0.0), F32).reshape(1, LANES)
    return inv_freq, freq.astype(F32), mlo, mhi


def _block_diag2(w):
    z = jnp.zeros_like(w)
    return jnp.concatenate([jnp.concatenate([w, z], axis=-1), jnp.concatenate([z, w], axis=-1)], axis=-2)


def _aux_route_table():
    r = np.zeros((AUX_PIECES, LANES, FOX_NPAIR * LANES), np.float32)
    for head in range(FOX_HEADS):
        for k in range(AUX_PIECES):
            r[k, SMALL_F_OFF + head, (head // FOX_PAIR) * LANES + (head % FOX_PAIR) * AUX_PIECES + k] = 1.0
    return jnp.asarray(r, BF16)


def _layer(h3, p3, positions, prm, g_final):
    B, S, _ = h3.shape
    T = B * S
    tm = 512
    tm_proj = 1024
    x2 = h3.reshape(T, D_MODEL)
    inv_freq, freq, mlo, mhi = _rope_lane_tables()

    w_in = prm['w_in']
    offs = np.cumsum([0, NSA_WIDTH, KV_WIDTH, KV_WIDTH, KV_WIDTH, KV_WIDTH, KV_WIDTH, KV_WIDTH,
                      N_GATE, FOX_WIDTH, FOX_WIDTH, FOX_WIDTH, FOX_HEADS])
    seg = lambda k: w_in[:, offs[k]:offs[k + 1]]
    wq = seg(0).reshape(D_MODEL, NSA_GROUPS, NSA_HPG, HEAD_DIM).transpose(0, 2, 1, 3).reshape(D_MODEL, NSA_WIDTH)
    pad = jnp.zeros((D_MODEL, LANES - N_GATE - FOX_HEADS), w_in.dtype)
    w_all = jnp.concatenate([wq, seg(1), seg(2), seg(3), seg(5), seg(8), seg(9), seg(7), seg(11), pad],
                            axis=1).astype(BF16)
    wt_v = jnp.concatenate([seg(4), seg(6), seg(10)], axis=1).T.astype(BF16)
    bias_small = jnp.concatenate([prm['b_nsa_gate'], prm['b_forget'],
                                  jnp.zeros((LANES - N_GATE - FOX_HEADS,), F32)]).reshape(1, LANES)

    qa, cmp_tok, knsa, vnsa_t, fqk, vfox_t, small = _inproj(
        x2, positions.reshape(T // tm_proj, 1, tm_proj), prm['g_mix'].reshape(1, D_MODEL), w_all, wt_v,
        bias_small, inv_freq.reshape(ROPE_HALF, 1), mlo, mhi, tm_proj, B, S)

    tri = jnp.asarray(np.tril(np.ones((CUM_SUB, CUM_SUB), np.float32)), BF16)
    small3 = small.reshape(B, S, LANES)
    aux3 = _cumsum(small3, tri, _aux_route_table())

    n_rows = S // CMP_STRIDE
    tile2 = lambda pe: jnp.concatenate([pe, pe], axis=-1)
    bd1 = lambda w: _block_diag2(w.reshape(CMP_BLOCK, HEAD_DIM, CMP_HIDDEN)).astype(BF16)
    pos_cmp = positions[:, CMP_BLOCK - 1::CMP_STRIDE]
    pos_cmp = jnp.pad(pos_cmp, ((0, 0), (0, n_rows - pos_cmp.shape[1]))).reshape(B, n_rows, 1)
    kc, vct = _compress(cmp_tok.reshape(B, S, 2 * KV_WIDTH), tile2(prm['pe_cmp_k']), tile2(prm['pe_cmp_v']),
                        bd1(prm['w_cmp_k1']), bd1(prm['w_cmp_v1']),
                        _block_diag2(prm['w_cmp_k2']).astype(BF16), _block_diag2(prm['w_cmp_v2']).T.astype(BF16),
                        pos_cmp, freq, mlo, mhi)

    n_cmp = (S - CMP_BLOCK) // CMP_STRIDE + 1
    n_blk = S // SEL_BLOCK
    cs = np.arange(n_rows)[:, None] * CMP_STRIDE
    ss = np.arange(n_blk)[None, :] * SEL_BLOCK
    ov = np.clip(np.minimum(cs + CMP_BLOCK, ss + SEL_BLOCK) - np.maximum(cs, ss), 0, None) / CMP_BLOCK
    ov[n_cmp:] = 0.0
    aggt = jnp.asarray(ov.T, BF16)
    et = jnp.asarray((np.arange(S)[:, None] // SEL_BLOCK) == np.arange(LANES)[None, :], BF16)
    eye = jnp.asarray(np.eye(NSA_GQ, dtype=np.float32), BF16)
    gexp = small3[:, :, :N_GATE].reshape(B, S // NSA_QB, NSA_QB, NSA_GROUPS, NSA_HPG, 3)
    gexp = gexp.transpose(0, 1, 5, 4, 3, 2).reshape(B, S // NSA_QB, 3, NSA_ROWS)
    o_nsa = _nsa_attention(qa.reshape(B, S, NSA_WIDTH), gexp, kc, vct, knsa.reshape(B, S, 2 * KV_WIDTH),
                           vnsa_t, aggt, et, eye)

    o_fox = _fox_attention(fqk.reshape(B, S, 2 * FOX_WIDTH), aux3, vfox_t)

    perm = lambda a: a.reshape(NSA_GROUPS, NSA_HPG, HEAD_DIM, -1).transpose(1, 0, 2, 3).reshape(NSA_WIDTH, -1)
    beta_n = perm(prm['beta_nsa'].reshape(NSA_WIDTH, 1)).reshape(1, NSA_WIDTH)
    w_out = prm['w_out']
    w_n = perm(w_out[:NSA_WIDTH]).astype(BF16)
    w_f = w_out[NSA_WIDTH:].astype(BF16)
    w_r = jnp.concatenate([prm['w_group'], prm['w_router'],
                           jnp.zeros((D_MODEL, LANES - N_GROUPS - N_EXPERTS), F32)], axis=1).T.astype(BF16)
    b_r = jnp.concatenate([prm['b_group'], prm['b_router'],
                           jnp.zeros((LANES - N_GROUPS - N_EXPERTS,), F32)]).reshape(LANES, 1)
    h1, u, comb = _outproj(o_nsa.reshape(T, NSA_WIDTH), o_fox.reshape(T, FOX_WIDTH), x2, beta_n,
                           prm['beta_fox'].reshape(1, FOX_WIDTH), w_n, w_f,
                           prm['g_ffn'].reshape(1, D_MODEL), w_r, b_r, tm_proj)

    wgu = jnp.concatenate([prm['w_gate_e'], prm['w_up_e']], axis=-1).astype(BF16)
    tri_s = jnp.asarray(np.tril(np.ones((MOE_TS, MOE_TS), np.float32)), BF16)
    us, cs, pos, meta = _moe_sort(u, comb, tri_s)
    ys = _moe(meta[:, 0, :MOE_META].reshape(-1), us, cs, wgu, prm['w_down_e'].astype(BF16))

    out = _ple(ys, pos, h1, p3.reshape(T, PLE_DIM), prm['g_ple'].reshape(1, D_MODEL),
               prm['w_ple_gate'].astype(BF16),
               prm['b_ple_gate'].reshape(1, D_MODEL), prm['w_ple_proj'].astype(BF16),
               g_final.reshape(1, D_MODEL), tm)
    return out.reshape(B, S, D_MODEL)


_PARAM_NAMES = ('g_mix', 'w_in', 'b_nsa_gate', 'b_forget', 'pe_cmp_k', 'w_cmp_k1', 'w_cmp_k2',
                'pe_cmp_v', 'w_cmp_v1', 'w_cmp_v2', 'beta_nsa', 'beta_fox', 'w_out', 'g_ffn',
                'w_group', 'b_group', 'w_router', 'b_router', 'w_gate_e', 'w_up_e', 'w_down_e',
                'g_ple', 'w_ple_gate', 'b_ple_gate', 'w_ple_proj')


def kernel(x, p, positions, g_mix, w_in, b_nsa_gate, b_forget, pe_cmp_k, w_cmp_k1, w_cmp_k2, pe_cmp_v,
           w_cmp_v1, w_cmp_v2, beta_nsa, beta_fox, w_out, g_ffn, w_group, b_group, w_router, b_router,
           w_gate_e, w_up_e, w_down_e, g_ple, w_ple_gate, b_ple_gate, w_ple_proj, g_final):
    stacked = (g_mix, w_in, b_nsa_gate, b_forget, pe_cmp_k, w_cmp_k1, w_cmp_k2, pe_cmp_v, w_cmp_v1,
               w_cmp_v2, beta_nsa, beta_fox, w_out, g_ffn, w_group, b_group, w_router, b_router,
               w_gate_e, w_up_e, w_down_e, g_ple, w_ple_gate, b_ple_gate, w_ple_proj)
    depth = w_in.shape[0]
    assert depth == 1, "the final norm is fused into the last layer's embedding kernel"
    prm = {n: a[0] for n, a in zip(_PARAM_NAMES, stacked)}
    return _layer(x, p[0], positions, prm, g_final)
```

```python
import functools
import math

import numpy as np
import jax
import jax.numpy as jnp
from jax import lax
from jax.experimental import pallas as pl
from jax.experimental.pallas import tpu as pltpu

D_MODEL = 1024
HEAD_DIM = 64
NSA_HEADS = 8
FOX_HEADS = 8
NSA_GROUPS = 2
NSA_HPG = 4
NSA_WIDTH = 512
FOX_WIDTH = 512
KV_WIDTH = 128
CMP_BLOCK = 32
CMP_STRIDE = 16
CMP_HIDDEN = 128
SEL_BLOCK = 64
N_SELECT = 16
WINDOW = 512
ROPE_THETA = 500000.0
ROPE_DIM = 16
ROPE_HALF = 8
N_GROUPS = 4
EXPERTS_PER_GROUP = 4
N_EXPERTS = 16
D_EXPERT = 512
PLE_DIM = 256
EPS = 1e-6
NEG_INF = -1e30
FORCE_BONUS = 1e4
LOG2E = math.log2(math.e)
Q_SCALE = 0.125 * LOG2E
NEG_BIG = -(2.0 ** 100)

LANES = 128
SUBLANES = 8
N_GATE = 3 * NSA_HEADS
SMALL_F_OFF = N_GATE
ROUTE_E_OFF = N_GROUPS
ROUTE_G_LANE = 0

VMEM_LIMIT = 56 * 1024 * 1024

F32 = jnp.float32
BF16 = jnp.bfloat16


def _cparams(sem):
    return pltpu.CompilerParams(dimension_semantics=sem, vmem_limit_bytes=VMEM_LIMIT)


def _dot(a, b):
    return jnp.dot(a, b, preferred_element_type=F32)


def _dot_nt(a, b):
    return lax.dot_general(a, b, (((1,), (1,)), ((), ())), preferred_element_type=F32)


def _split3(x):
    hi = x.astype(BF16)
    r1 = x - hi.astype(F32)
    mid = r1.astype(BF16)
    lo = (r1 - mid.astype(F32)).astype(BF16)
    return hi, mid, lo


def _split3_dot_lhs(e, x):
    hi, mid, lo = _split3(x)
    return _dot(e, hi) + _dot(e, mid) + _dot(e, lo)


def _rms(x, g):
    return x * lax.rsqrt(jnp.mean(x * x, axis=-1, keepdims=True) + EPS) * g


def _rope(r, cos, sin_lo, sin_hi):
    return (r * cos + pltpu.roll(r, LANES - ROPE_HALF, 1) * sin_lo
            + pltpu.roll(r, ROPE_HALF, 1) * sin_hi)


def _rope_tables(pos_col, freq, mlo, mhi):
    ang = pos_col.astype(F32) * freq
    cos = jnp.cos(ang)
    sin = jnp.sin(ang)
    return cos, sin * mlo, sin * mhi


def _rope_tables_from_row(pos_row, freq_col, mlo, mhi):
    m = pos_row.shape[1]
    ang_t = freq_col * pos_row.astype(F32)
    pad = jnp.zeros((LANES - ROPE_HALF, m), F32)
    lane = lax.broadcasted_iota(jnp.int32, (m, LANES), 1)
    rot = (lane & (HEAD_DIM - 1)) < ROPE_DIM

    def spread(t):
        x = jnp.concatenate([t, pad], axis=0).T
        x = x + pltpu.roll(x, ROPE_HALF, 1)
        return x + pltpu.roll(x, HEAD_DIM, 1)

    cos = jnp.where(rot, spread(jnp.cos(ang_t)), 1.0)
    sin = spread(jnp.sin(ang_t))
    return cos, sin * mlo, sin * mhi


def _inproj_kernel(x_ref, pos_ref, g_ref, w_ref, wt_ref, bias_ref, freq_ref, mlo_ref, mhi_ref,
                   qa_ref, cmp_ref, knsa_ref, vnsa_ref, fqk_ref, vfox_ref, small_ref):
    x = x_ref[...]
    hb = _rms(x, g_ref[...]).astype(BF16)
    cos, s_lo, s_hi = _rope_tables_from_row(pos_ref[...], freq_ref[...], mlo_ref[...], mhi_ref[...])
    off = 0
    r = _dot(hb, w_ref[:, off:off + NSA_WIDTH])
    for c in range(NSA_WIDTH // LANES):
        rc = _rope(r[:, c * LANES:(c + 1) * LANES], cos, s_lo, s_hi) * Q_SCALE
        qa_ref[:, c * LANES:(c + 1) * LANES] = rc.astype(qa_ref.dtype)
    off += NSA_WIDTH
    cmp_ref[...] = _dot(hb, w_ref[:, off:off + 2 * KV_WIDTH])
    off += 2 * KV_WIDTH
    r = _dot(hb, w_ref[:, off:off + 2 * KV_WIDTH])
    for c in range(2):
        rc = _rope(r[:, c * LANES:(c + 1) * LANES], cos, s_lo, s_hi)
        knsa_ref[:, c * LANES:(c + 1) * LANES] = rc.astype(knsa_ref.dtype)
    off += 2 * KV_WIDTH
    r = _dot(hb, w_ref[:, off:off + FOX_WIDTH])
    fqk_ref[:, 0:FOX_WIDTH] = (r * Q_SCALE).astype(fqk_ref.dtype)
    off += FOX_WIDTH
    r = _dot(hb, w_ref[:, off:off + FOX_WIDTH])
    fqk_ref[:, FOX_WIDTH:2 * FOX_WIDTH] = r.astype(fqk_ref.dtype)
    off += FOX_WIDTH
    z = _dot(hb, w_ref[:, off:off + LANES]) + bias_ref[...]
    lane = lax.broadcasted_iota(jnp.int32, z.shape, 1)
    sig = 1.0 / (1.0 + jnp.exp(-z))
    logsig = jnp.minimum(z, 0.0) - jnp.log(1.0 + jnp.exp(-jnp.abs(z)))
    small_ref[...] = jnp.where(lane < SMALL_F_OFF, sig, logsig)
    vt = _dot_nt(wt_ref[...], hb)
    vnsa_ref[...] = vt[0:2 * KV_WIDTH].astype(vnsa_ref.dtype)
    vfox_ref[...] = vt[2 * KV_WIDTH:2 * KV_WIDTH + FOX_WIDTH].astype(vfox_ref.dtype)


def _inproj(x2, pos2, g_mix, w_all, wt_v, bias_small, freq, mlo, mhi, tm, B, S):
    T = x2.shape[0]
    nt = S // tm
    full = lambda shape: pl.BlockSpec(shape, lambda i: (0,) * len(shape))
    row = lambda w: pl.BlockSpec((tm, w), lambda i: (i, 0))
    tr = lambda w: pl.BlockSpec((None, w, tm), lambda i: (i // nt, 0, i % nt))
    return pl.pallas_call(
        _inproj_kernel,
        grid=(T // tm,),
        in_specs=[row(D_MODEL), pl.BlockSpec((None, 1, tm), lambda i: (i, 0, 0)), full((1, D_MODEL)),
                  full(w_all.shape), full(wt_v.shape),
                  full((1, LANES)), full((ROPE_HALF, 1)), full((1, LANES)), full((1, LANES))],
        out_specs=[row(NSA_WIDTH), row(2 * KV_WIDTH), row(2 * KV_WIDTH), tr(2 * KV_WIDTH),
                   row(2 * FOX_WIDTH), tr(FOX_WIDTH), row(LANES)],
        out_shape=[jax.ShapeDtypeStruct((T, NSA_WIDTH), BF16),
                   jax.ShapeDtypeStruct((T, 2 * KV_WIDTH), F32),
                   jax.ShapeDtypeStruct((T, 2 * KV_WIDTH), BF16),
                   jax.ShapeDtypeStruct((B, 2 * KV_WIDTH, S), BF16),
                   jax.ShapeDtypeStruct((T, 2 * FOX_WIDTH), BF16),
                   jax.ShapeDtypeStruct((B, FOX_WIDTH, S), BF16),
                   jax.ShapeDtypeStruct((T, LANES), F32)],
        compiler_params=_cparams(("parallel",)),
        name="inproj",
    )(x2, pos2, g_mix, w_all, wt_v, bias_small, freq, mlo, mhi)


CUM_CHUNK = 1024
CUM_SUB = 128
FOX_PAIR = 2
FOX_NPAIR = FOX_HEADS // FOX_PAIR
AUX_PIECES = 3


def _cumsum_kernel(x_ref, tri_ref, route_ref, o_ref, carry_ref):
    @pl.when(pl.program_id(1) == 0)
    def _():
        carry_ref[...] = jnp.zeros_like(carry_ref)

    carry = carry_ref[...]
    pieces = []
    for k in range(CUM_CHUNK // CUM_SUB):
        ck = _split3_dot_lhs(tri_ref[...], x_ref[k * CUM_SUB:(k + 1) * CUM_SUB, :]) + carry
        carry = ck[CUM_SUB - 1:CUM_SUB, :]
        pieces.append(ck)
    carry_ref[...] = carry
    c = jnp.concatenate(pieces, axis=0)
    hi, mid, lo = _split3(c * LOG2E)
    aux = _dot(hi, route_ref[0]) + _dot(mid, route_ref[1]) + _dot(lo, route_ref[2])
    o_ref[...] = aux.astype(o_ref.dtype)


def _cumsum(small3, tri, route):
    B, S, _ = small3.shape
    width = FOX_NPAIR * LANES
    return pl.pallas_call(
        _cumsum_kernel,
        grid=(B, S // CUM_CHUNK),
        in_specs=[pl.BlockSpec((None, CUM_CHUNK, LANES), lambda b, i: (b, i, 0)),
                  pl.BlockSpec((CUM_SUB, CUM_SUB), lambda b, i: (0, 0)),
                  pl.BlockSpec((AUX_PIECES, LANES, width), lambda b, i: (0, 0, 0))],
        out_specs=pl.BlockSpec((None, CUM_CHUNK, width), lambda b, i: (b, i, 0)),
        out_shape=jax.ShapeDtypeStruct((B, S, width), BF16),
        scratch_shapes=[pltpu.VMEM((1, LANES), F32)],
        compiler_params=_cparams(("parallel", "arbitrary")),
        name="forget_cumsum",
    )(small3, tri, route)


def _compress_kernel(tokk_ref, tokv_ref, pek_ref, pev_ref, bdk1_ref, bdv1_ref, bdk2_ref, bdv2t_ref,
                     pos_ref, freq_ref, mlo_ref, mhi_ref, kc_ref, vct_ref, *, n_rows):
    half = CMP_BLOCK // 2
    ak = jnp.zeros((n_rows, 2 * CMP_HIDDEN), F32)
    bk = jnp.zeros((n_rows, 2 * CMP_HIDDEN), F32)
    av = jnp.zeros((n_rows, 2 * CMP_HIDDEN), F32)
    bv = jnp.zeros((n_rows, 2 * CMP_HIDDEN), F32)
    for l in range(half):
        xk = tokk_ref[pl.ds(l, n_rows, stride=CMP_STRIDE), :]
        xv = tokv_ref[pl.ds(l, n_rows, stride=CMP_STRIDE), :]
        ak = ak + _dot((xk + pek_ref[l:l + 1, :]).astype(BF16), bdk1_ref[l])
        bk = bk + _dot((xk + pek_ref[half + l:half + l + 1, :]).astype(BF16), bdk1_ref[half + l])
        av = av + _dot((xv + pev_ref[l:l + 1, :]).astype(BF16), bdv1_ref[l])
        bv = bv + _dot((xv + pev_ref[half + l:half + l + 1, :]).astype(BF16), bdv1_ref[half + l])
    hk = ak + pltpu.roll(bk, n_rows - 1, 0)
    hv = av + pltpu.roll(bv, n_rows - 1, 0)
    hk = hk * (1.0 / (1.0 + jnp.exp(-hk)))
    hv = hv * (1.0 / (1.0 + jnp.exp(-hv)))
    kc = _dot(hk.astype(BF16), bdk2_ref[...])
    cos, s_lo, s_hi = _rope_tables(pos_ref[...], freq_ref[...], mlo_ref[...], mhi_ref[...])
    kc_ref[...] = _rope(kc, cos, s_lo, s_hi).astype(kc_ref.dtype)
    vct_ref[...] = _dot_nt(bdv2t_ref[...], hv.astype(BF16)).astype(vct_ref.dtype)


def _compress(cmp_tok3, pek, pev, bdk1, bdv1, bdk2, bdv2t, pos_cmp, freq, mlo, mhi):
    B, S, _ = cmp_tok3.shape
    n_rows = S // CMP_STRIDE
    full = lambda shape: pl.BlockSpec(shape, lambda b: (0,) * len(shape))
    return pl.pallas_call(
        functools.partial(_compress_kernel, n_rows=n_rows),
        grid=(B,),
        in_specs=[pl.BlockSpec((None, S, KV_WIDTH), lambda b: (b, 0, 0)),
                  pl.BlockSpec((None, S, KV_WIDTH), lambda b: (b, 0, 1)),
                  full((CMP_BLOCK, KV_WIDTH)), full((CMP_BLOCK, KV_WIDTH)),
                  full((CMP_BLOCK, KV_WIDTH, 2 * CMP_HIDDEN)), full((CMP_BLOCK, KV_WIDTH, 2 * CMP_HIDDEN)),
                  full((2 * CMP_HIDDEN, KV_WIDTH)), full((KV_WIDTH, 2 * CMP_HIDDEN)),
                  pl.BlockSpec((None, n_rows, 1), lambda b: (b, 0, 0)),
                  full((1, LANES)), full((1, LANES)), full((1, LANES))],
        out_specs=[pl.BlockSpec((None, n_rows, KV_WIDTH), lambda b: (b, 0, 0)),
                   pl.BlockSpec((None, KV_WIDTH, n_rows), lambda b: (b, 0, 0))],
        out_shape=[jax.ShapeDtypeStruct((B, n_rows, KV_WIDTH), BF16),
                   jax.ShapeDtypeStruct((B, KV_WIDTH, n_rows), BF16)],
        compiler_params=_cparams(("parallel",)),
        name="nsa_compress",
    )(cmp_tok3, cmp_tok3, pek, pev, bdk1, bdv1, bdk2, bdv2t, pos_cmp, freq, mlo, mhi)


NSA_QB = 128
NSA_BPQ = NSA_QB // SEL_BLOCK
assert NSA_BPQ in (1, 2) and NSA_QB % LANES == 0
NSA_KC = 512
NSA_GQ = NSA_GROUPS * NSA_QB
NSA_GH = NSA_HPG * NSA_QB
NSA_ROWS = NSA_GROUPS * NSA_GH
NSA_WSPAN = WINDOW + NSA_QB


def _where_tiles(mask, x, other):
    w = mask.shape[1]
    return jnp.concatenate([jnp.where(mask, x[:, t * w:(t + 1) * w], other)
                            for t in range(x.shape[1] // w)], axis=1)


def _masked_softmax_cols(s, mask):
    l = _where_tiles(mask, s, NEG_INF)
    m = jnp.max(l, axis=0, keepdims=True)
    e = jnp.exp2(l - m)
    return e, jnp.sum(e, axis=0, keepdims=True)


FLASH_NSUB = 2


def _flash_chunks(n_chunks, score_fn, vt_fn, init_fn, s_a, s_b, acc_ref, pv_fn=_dot):
    last = jnp.maximum(n_chunks - 1, 0)
    sub = s_a.shape[0] // FLASH_NSUB

    def stage(i_prod, s_prod, i_cons, s_cons, mx, m, l):
        if i_cons is not None:
            m_new = jnp.maximum(m, mx)
            alpha = jnp.exp2(m - m_new)
            l = alpha * l
        if i_prod is not None:
            i_prod = jnp.minimum(i_prod, last)
        mx_prod, pv = None, None
        for t in range(FLASH_NSUB):
            rows = pl.ds(t * sub, sub)
            if i_prod is not None:
                s = score_fn(i_prod, t)
                s_prod[rows, :] = s
                mx_t = jnp.max(s, axis=0, keepdims=True)
                mx_prod = mx_t if mx_prod is None else jnp.maximum(mx_prod, mx_t)
            if i_cons is not None:
                p = jnp.exp2(s_cons[rows, :] - m_new)
                l = l + jnp.sum(p, axis=0, keepdims=True)
                d = pv_fn(vt_fn(i_cons, t), p.astype(BF16))
                pv = d if pv is None else pv + d
        if i_cons is not None:
            acc_ref[...] = alpha * acc_ref[...] + pv
            m = m_new
        return mx_prod, m, l

    def pair(j, carry):
        mx_a, m, l = carry
        i = 2 * j
        mx_b, m, l = stage(i + 1, s_b, i, s_a, mx_a, m, l)
        mx_a, m, l = stage(i + 2, s_a, i + 1, s_b, mx_b, m, l)
        return mx_a, m, l

    def tail(_, carry):
        mx_a, m, l = carry
        _, m, l = stage(None, None, last, s_a, mx_a, m, l)
        return mx_a, m, l

    mx_a, _, _ = stage(0, s_a, None, None, None, None, None)
    m, l = init_fn()
    carry = (mx_a, m, l)
    carry = lax.fori_loop(0, n_chunks // 2, pair, carry)
    carry = lax.fori_loop(0, n_chunks & 1, tail, carry)
    return carry[2]


def _nsa_kernel(q_ref, gate_ref, kc_ref, vct_ref, ks_ref, kw_ref, vst_ref, vwt_ref, aggt_ref, et_ref,
                eye_ref, o_ref, sa_scr, sb_scr, acc_scr, *, n_cmp, n_blk, n_sel):
    c = pl.program_id(1)
    s0 = c * NSA_QB
    lane = lax.broadcasted_iota(jnp.int32, (NSA_QB, LANES), 1)
    low = lane < HEAD_DIM
    qb = q_ref[...]
    zero = jnp.zeros((NSA_QB, LANES), qb.dtype)
    pieces = []
    for g in range(NSA_GROUPS):
        for h in range(NSA_HPG):
            qh = qb[:, h * LANES:(h + 1) * LANES]
            pieces.append(jnp.where(low, qh, zero) if g == 0 else jnp.where(low, zero, qh))
    qp = jnp.concatenate(pieces, axis=0)
    t_l = s0 + lax.broadcasted_iota(jnp.int32, (1, NSA_QB), 1)
    cur_l = (s0 + (lax.broadcasted_iota(jnp.int32, (1, NSA_GQ), 1) & (NSA_QB - 1))) // SEL_BLOCK

    def group_values(vt, p):
        return jnp.concatenate([_dot(vt[g * HEAD_DIM:(g + 1) * HEAD_DIM], p[:, g * NSA_GH:(g + 1) * NSA_GH])
                                for g in range(NSA_GROUPS)], axis=1)
    first_blk = c * NSA_BPQ
    last_blk = first_blk + NSA_BPQ - 1

    n_pad = kc_ref.shape[0]
    n_s = lax.broadcasted_iota(jnp.int32, (n_pad, 1), 0)
    mask_c = ((n_s * CMP_STRIDE + (CMP_BLOCK - 1)) <= t_l) & (n_s < n_cmp)
    start = pl.multiple_of(jnp.maximum(s0 - WINDOW, 0), LANES)
    rel = t_l - (start + lax.broadcasted_iota(jnp.int32, (NSA_WSPAN, 1), 0))
    mask_w = (rel >= 0) & (rel < WINDOW)
    s_c = _dot_nt(kc_ref[...], qp)
    s_w = _dot_nt(kw_ref[pl.ds(start, NSA_WSPAN), :], qp)
    e_c, sum_c = _masked_softmax_cols(s_c, mask_c)
    inv_c = _where_tiles(t_l >= CMP_BLOCK - 1, 1.0 / sum_c, 0.0)
    acc_c = group_values(vct_ref[...], e_c.astype(BF16))

    j_s = lax.broadcasted_iota(jnp.int32, (n_blk, NSA_GQ), 0)
    nv = n_blk // SUBLANES

    def ranked_mask():
        p_c = e_c * inv_c
        per_group = []
        for g in range(NSA_GROUPS):
            tiles = [p_c[:, (g * NSA_HPG + h) * NSA_QB:(g * NSA_HPG + h + 1) * NSA_QB] for h in range(NSA_HPG)]
            per_group.append(functools.reduce(jnp.add, tiles))
        psum = jnp.concatenate(per_group, axis=1)
        imp = _split3_dot_lhs(aggt_ref[...], psum)
        forced = (j_s == 0) | (j_s == cur_l) | (j_s == cur_l - 1)
        score = jnp.where(j_s > cur_l, -1.0, imp + jnp.where(forced, FORCE_BONUS, 0.0))
        sc = [score[SUBLANES * v:SUBLANES * (v + 1), :] for v in range(nv)]
        sub = lax.broadcasted_iota(jnp.int32, (SUBLANES, NSA_GQ), 0)

        def count_group(vi, rk):
            rk = list(rk)
            for ri in range(SUBLANES):
                row = sc[vi][ri:ri + 1, :]
                for v in range(nv):
                    if v < vi:
                        beats = jnp.where(row > sc[v], 1.0, 0.0)
                    elif v > vi:
                        beats = jnp.where(row >= sc[v], 1.0, 0.0)
                    else:
                        beats = jnp.where(sub > ri, jnp.where(row >= sc[v], 1.0, 0.0),
                                          jnp.where(row > sc[v], 1.0, 0.0))
                    rk[v] = rk[v] + beats
            return tuple(rk)

        rk = tuple(jnp.zeros((SUBLANES, NSA_GQ), F32) for _ in range(nv))
        for vi in range(nv):
            rk = lax.cond(vi * SUBLANES <= last_blk, functools.partial(count_group, vi), lambda r: r, rk)
        rank = jnp.concatenate(rk, axis=0)
        return jnp.where((rank < n_sel) & (j_s < first_blk), 0.0, NEG_BIG)

    def all_mask():
        return jnp.where(j_s < first_blk, 0.0, NEG_BIG)

    neg_t = lax.cond(last_blk < n_sel, all_mask, ranked_mask)

    a0 = pl.multiple_of(s0, LANES)
    s_d = _dot_nt(ks_ref[pl.ds(a0, NSA_QB), :], qp)
    neg_t = jnp.concatenate([neg_t, jnp.zeros((LANES - n_blk, NSA_GQ), F32)], axis=0).astype(BF16)
    q_aux = _dot_nt(eye_ref[...], neg_t).astype(BF16)
    aux_rows = [q_aux[g * NSA_QB:(g + 1) * NSA_QB] for g in range(NSA_GROUPS) for _ in range(NSA_HPG)]
    qx = jnp.concatenate([qp, jnp.concatenate(aux_rows, axis=0)], axis=1)

    sub_keys = NSA_KC // FLASH_NSUB

    def sel_scores(i, t):
        k0 = pl.multiple_of(i * NSA_KC + t * sub_keys, sub_keys)
        kx = jnp.concatenate([ks_ref[pl.ds(k0, sub_keys), :], et_ref[pl.ds(k0, sub_keys), :]], axis=1)
        return _dot_nt(kx, qx)

    def sel_values(i, t):
        return vst_ref[:, pl.ds(pl.multiple_of(i * NSA_KC + t * sub_keys, sub_keys), sub_keys)]

    window = {}

    def after_first_scores():
        kpos = a0 + lax.broadcasted_iota(jnp.int32, (NSA_QB, 1), 0)
        l_d = _where_tiles(kpos <= t_l, s_d, NEG_INF)
        m_d = jnp.max(l_d, axis=0, keepdims=True)
        p_d = jnp.exp2(l_d - m_d)
        acc_scr[...] = group_values(vst_ref[:, pl.ds(a0, NSA_QB)], p_d.astype(BF16))
        e_w, window['l'] = _masked_softmax_cols(s_w, mask_w)
        window['acc'] = group_values(vwt_ref[:, pl.ds(start, NSA_WSPAN)], e_w.astype(BF16))
        return m_d, jnp.sum(p_d, axis=0, keepdims=True)

    n_prev = (s0 + NSA_KC - 1) // NSA_KC
    l_s = _flash_chunks(n_prev, sel_scores, sel_values, after_first_scores, sa_scr, sb_scr, acc_scr,
                        pv_fn=group_values)

    gates = gate_ref[...]
    o_t = ((gates[0:1, :] * inv_c) * acc_c + (gates[1:2, :] / l_s) * acc_scr[...]
           + (gates[2:3, :] / window['l']) * window['acc'])
    for h in range(NSA_HPG):
        tiles = [o_t[:, (g * NSA_HPG + h) * NSA_QB:(g * NSA_HPG + h + 1) * NSA_QB].T for g in range(NSA_GROUPS)]
        o_ref[:, h * LANES:(h + 1) * LANES] = jnp.concatenate(tiles, axis=1)


def _nsa_attention(qa3, gexp, kc, vct, knsa3, vnsa_t, aggt, et, eye):
    B, S, _ = qa3.shape
    n_pad = kc.shape[1]
    n_cmp = (S - CMP_BLOCK) // CMP_STRIDE + 1
    n_blk = S // SEL_BLOCK
    n_sel = min(N_SELECT, n_blk)
    const = lambda shape: pl.BlockSpec(shape, lambda b, c: (0,) * len(shape))
    return pl.pallas_call(
        functools.partial(_nsa_kernel, n_cmp=n_cmp, n_blk=n_blk, n_sel=n_sel),
        grid=(B, S // NSA_QB),
        in_specs=[pl.BlockSpec((None, NSA_QB, NSA_WIDTH), lambda b, c: (b, c, 0)),
                  pl.BlockSpec((None, None, 3, NSA_ROWS), lambda b, c: (b, c, 0, 0)),
                  pl.BlockSpec((None, n_pad, KV_WIDTH), lambda b, c: (b, 0, 0)),
                  pl.BlockSpec((None, KV_WIDTH, n_pad), lambda b, c: (b, 0, 0)),
                  pl.BlockSpec((None, S, KV_WIDTH), lambda b, c: (b, 0, 0)),
                  pl.BlockSpec((None, S, KV_WIDTH), lambda b, c: (b, 0, 1)),
                  pl.BlockSpec((None, KV_WIDTH, S), lambda b, c: (b, 0, 0)),
                  pl.BlockSpec((None, KV_WIDTH, S), lambda b, c: (b, 1, 0)),
                  const((n_blk, n_pad)), const((S, LANES)), const((NSA_GQ, NSA_GQ))],
        out_specs=pl.BlockSpec((None, NSA_QB, NSA_WIDTH), lambda b, c: (b, c, 0)),
        out_shape=jax.ShapeDtypeStruct((B, S, NSA_WIDTH), F32),
        scratch_shapes=[pltpu.VMEM((NSA_KC, NSA_ROWS), F32), pltpu.VMEM((NSA_KC, NSA_ROWS), F32),
                        pltpu.VMEM((HEAD_DIM, NSA_ROWS), F32)],
        compiler_params=_cparams(("parallel", "arbitrary")),
        name="nsa_attention",
    )(qa3, gexp, kc, vct, knsa3, knsa3, vnsa_t, vnsa_t, aggt, et, eye)


FOX_TQ = 512
FOX_KC = FOX_TQ
FOX_ROWS = FOX_PAIR * FOX_TQ


def _fox_kernel(q_ref, k_ref, aux_ref, vt_ref, o_ref, sa_scr, sb_scr, acc_scr):
    qi = pl.program_id(2)
    s0 = qi * FOX_TQ
    lane = lax.broadcasted_iota(jnp.int32, (FOX_TQ, LANES), 1)
    low = lane < HEAD_DIM
    q2 = q_ref[...]
    zero = jnp.zeros_like(q2)
    aux0 = jnp.where(lane < AUX_PIECES, -1.0, 0.0).astype(q2.dtype)
    aux1 = jnp.where((lane >= AUX_PIECES) & (lane < 2 * AUX_PIECES), -1.0, 0.0).astype(q2.dtype)
    qx = jnp.concatenate([jnp.concatenate([jnp.where(low, q2, zero), aux0], axis=1),
                          jnp.concatenate([jnp.where(low, zero, q2), aux1], axis=1)], axis=0)
    t_l = s0 + lax.broadcasted_iota(jnp.int32, (1, FOX_TQ), 1)

    sub_keys = FOX_KC // FLASH_NSUB

    def scores(i, t, n=sub_keys):
        k0 = pl.multiple_of(i * FOX_KC + t * sub_keys, sub_keys)
        kx = jnp.concatenate([k_ref[pl.ds(k0, n), :], aux_ref[pl.ds(k0, n), :]], axis=1)
        return _dot_nt(kx, qx)

    def values(i, t, n=sub_keys):
        return vt_ref[:, pl.ds(pl.multiple_of(i * FOX_KC + t * sub_keys, sub_keys), n)]

    def diagonal():
        kpos = s0 + lax.broadcasted_iota(jnp.int32, (FOX_KC, 1), 0)
        s_d = _where_tiles(kpos <= t_l, scores(qi, 0, FOX_KC), NEG_INF)
        m_d = jnp.max(s_d, axis=0, keepdims=True)
        p_d = jnp.exp2(s_d - m_d)
        acc_scr[...] = head_values(values(qi, 0, FOX_KC), p_d.astype(BF16))
        return m_d, jnp.sum(p_d, axis=0, keepdims=True)

    def head_values(vt, p):
        return jnp.concatenate([_dot(vt[h * HEAD_DIM:(h + 1) * HEAD_DIM], p[:, h * FOX_TQ:(h + 1) * FOX_TQ])
                                for h in range(FOX_PAIR)], axis=1)

    l_f = _flash_chunks(qi, scores, values, diagonal, sa_scr, sb_scr, acc_scr, pv_fn=head_values)
    o_t = acc_scr[...] / l_f
    o_ref[...] = jnp.concatenate([o_t[:, h * FOX_TQ:(h + 1) * FOX_TQ].T for h in range(FOX_PAIR)], axis=1)


def _fox_attention(fqk3, aux3, vfox_t):
    B, S, _ = fqk3.shape
    return pl.pallas_call(
        _fox_kernel,
        grid=(B, FOX_NPAIR, S // FOX_TQ),
        in_specs=[pl.BlockSpec((None, FOX_TQ, LANES), lambda b, h, i: (b, i, h)),
                  pl.BlockSpec((None, S, LANES), lambda b, h, i: (b, 0, FOX_NPAIR + h)),
                  pl.BlockSpec((None, S, LANES), lambda b, h, i: (b, 0, h)),
                  pl.BlockSpec((None, LANES, S), lambda b, h, i: (b, h, 0))],
        out_specs=pl.BlockSpec((None, FOX_TQ, LANES), lambda b, h, i: (b, i, h)),
        out_shape=jax.ShapeDtypeStruct((B, S, FOX_WIDTH), F32),
        scratch_shapes=[pltpu.VMEM((FOX_KC, FOX_ROWS), F32), pltpu.VMEM((FOX_KC, FOX_ROWS), F32),
                        pltpu.VMEM((HEAD_DIM, FOX_ROWS), F32)],
        compiler_params=_cparams(("parallel", "parallel", "arbitrary")),
        name="fox_attention",
    )(fqk3, fqk3, aux3, vfox_t)


ROUTE_ROWS = 32


def _first_of(vals, v):
    idx = jnp.full(v.shape, len(vals) - 1, jnp.int32)
    for k in range(len(vals) - 2, -1, -1):
        idx = jnp.where(vals[k] == v, k, idx)
    return idx


def _route_t(lt):
    row = [lt[i:i + 1, :] for i in range(N_GROUPS + N_EXPERTS)]
    grp = row[:N_GROUPS]
    gmax = functools.reduce(jnp.maximum, grp)
    g_star = _first_of(grp, gmax)
    p_grp = 1.0 / functools.reduce(jnp.add, [jnp.exp(g - gmax) for g in grp])
    logit = []
    for k in range(EXPERTS_PER_GROUP):
        ek = row[ROUTE_E_OFF + (N_GROUPS - 1) * EXPERTS_PER_GROUP + k]
        for g in range(N_GROUPS - 2, -1, -1):
            ek = jnp.where(g_star == g, row[ROUTE_E_OFF + g * EXPERTS_PER_GROUP + k], ek)
        logit.append(ek)
    emax = functools.reduce(jnp.maximum, logit)
    eexp = [jnp.exp(e - emax) for e in logit]
    esum = functools.reduce(jnp.add, eexp)
    prob = [e / esum for e in eexp]
    v1 = functools.reduce(jnp.maximum, prob)
    i1 = _first_of(prob, v1)
    rest = [jnp.where(i1 == k, -1.0, prob[k]) for k in range(EXPERTS_PER_GROUP)]
    v2 = functools.reduce(jnp.maximum, rest)
    i2 = _first_of(rest, v2)
    den = v1 + v2
    w1 = p_grp * (v1 / den)
    w2 = p_grp * (v2 / den)
    base = ROUTE_E_OFF + g_star * EXPERTS_PER_GROUP
    r_idx = lax.broadcasted_iota(jnp.int32, lt.shape, 0)
    return (jnp.where(r_idx == base + i1, w1, 0.0) + jnp.where(r_idx == base + i2, w2, 0.0)
            + jnp.where(r_idx == ROUTE_G_LANE, g_star.astype(F32), 0.0))


def _outproj_kernel(on_ref, of_ref, x_ref, bn_ref, bf_ref, wn_ref, wf_ref, gffn_ref, wrt_ref, brt_ref,
                    h_ref, u_ref, comb_ref):
    mn = _rms(on_ref[...], bn_ref[...]).astype(BF16)
    mf = _rms(of_ref[...], bf_ref[...]).astype(BF16)
    h = x_ref[...] + (_dot(mn, wn_ref[...]) + _dot(mf, wf_ref[...]))
    h_ref[...] = h
    u = _rms(h, gffn_ref[...]).astype(BF16)
    u_ref[...] = u
    lt = _dot_nt(wrt_ref[...], u) + brt_ref[...]
    comb_t = _route_t(lt[0:ROUTE_ROWS])
    comb_t = jnp.concatenate([comb_t, jnp.zeros((LANES - ROUTE_ROWS, comb_t.shape[1]), F32)], axis=0)
    comb_ref[...] = comb_t.T


def _outproj(o_nsa, o_fox, x2, beta_n, beta_f, w_n, w_f, g_ffn, w_r, b_r, tm):
    T = x2.shape[0]
    full = lambda shape: pl.BlockSpec(shape, lambda i: (0,) * len(shape))
    row = lambda w: pl.BlockSpec((tm, w), lambda i: (i, 0))
    return pl.pallas_call(
        _outproj_kernel,
        grid=(T // tm,),
        in_specs=[row(NSA_WIDTH), row(FOX_WIDTH), row(D_MODEL), full((1, NSA_WIDTH)), full((1, FOX_WIDTH)),
                  full((NSA_WIDTH, D_MODEL)), full((FOX_WIDTH, D_MODEL)), full((1, D_MODEL)),
                  full((LANES, D_MODEL)), full((LANES, 1))],
        out_specs=[row(D_MODEL), row(D_MODEL), row(LANES)],
        out_shape=[jax.ShapeDtypeStruct((T, D_MODEL), F32),
                   jax.ShapeDtypeStruct((T, D_MODEL), BF16),
                   jax.ShapeDtypeStruct((T, LANES), F32)],
        compiler_params=_cparams(("parallel",)),
        name="outproj_router",
    )(o_nsa, o_fox, x2, beta_n, beta_f, w_n, w_f, g_ffn, w_r, b_r)


MOE_TS = 1024
MOE_RB = 128
MOE_TSP = MOE_TS + N_GROUPS * MOE_RB
MOE_META = 8
MOE_TPS = 2


def _moe_sort_kernel(u_ref, comb_ref, tri_ref, us_ref, cs_ref, pos_ref, meta_ref):
    comb = comb_ref[...]
    lane = lax.broadcasted_iota(jnp.int32, comb.shape, 1)
    grp = comb[:, ROUTE_G_LANE:ROUTE_G_LANE + 1].astype(jnp.int32)
    onehot = jnp.where((lane == grp) & (lane < N_GROUPS), 1.0, 0.0)
    rank = _dot(tri_ref[...], onehot.astype(BF16))
    counts = rank[MOE_TS - 1:MOE_TS, :]
    nblk = jnp.floor((counts + (MOE_RB - 1)) * (1.0 / MOE_RB))
    lane1 = lane[0:1, :]
    blk0 = jnp.zeros_like(nblk)
    run = jnp.zeros((1, 1), F32)
    for g in range(1, N_GROUPS):
        run = run + nblk[:, g - 1:g]
        blk0 = blk0 + jnp.where(lane1 == g, run, 0.0)
    val = onehot * (blk0 * MOE_RB + rank - 1.0)
    pos_col = jnp.sum(val, axis=-1, keepdims=True)
    pos_ref[...] = pos_col.astype(jnp.int32)
    ones = jnp.ones((SUBLANES, LANES), BF16)
    hi, mid, lo = _split3(val)
    pos_row = (_dot_nt(ones, hi) + _dot_nt(ones, mid) + _dot_nt(ones, lo))[0:1, :].astype(jnp.int32)
    r_idx = lax.broadcasted_iota(jnp.int32, (MOE_TSP, MOE_TS), 0)
    perm = jnp.where(r_idx == pos_row, 1.0, 0.0).astype(BF16)
    c_hi = comb.astype(BF16)
    c_lo = (comb - c_hi.astype(F32)).astype(BF16)
    moved = _dot(perm, jnp.concatenate([u_ref[...], c_hi, c_lo], axis=1))
    us_ref[...] = moved[:, 0:D_MODEL].astype(us_ref.dtype)
    cs_ref[...] = moved[:, D_MODEL:D_MODEL + LANES] + moved[:, D_MODEL + LANES:D_MODEL + 2 * LANES]
    meta = blk0 + pltpu.roll(nblk, N_GROUPS, 1)
    meta_ref[...] = jnp.where(lane1 < MOE_META, meta, 0.0).astype(jnp.int32)


def _moe_sort(u, comb, tri):
    T = u.shape[0]
    nt = T // MOE_TS
    return pl.pallas_call(
        _moe_sort_kernel,
        grid=(nt,),
        in_specs=[pl.BlockSpec((MOE_TS, D_MODEL), lambda i: (i, 0)),
                  pl.BlockSpec((MOE_TS, LANES), lambda i: (i, 0)),
                  pl.BlockSpec((MOE_TS, MOE_TS), lambda i: (0, 0))],
        out_specs=[pl.BlockSpec((MOE_TSP, D_MODEL), lambda i: (i, 0)),
                   pl.BlockSpec((MOE_TSP, LANES), lambda i: (i, 0)),
                   pl.BlockSpec((MOE_TS, 1), lambda i: (i, 0)),
                   pl.BlockSpec((None, 1, LANES), lambda i: (i, 0, 0))],
        out_shape=[jax.ShapeDtypeStruct((nt * MOE_TSP, D_MODEL), BF16),
                   jax.ShapeDtypeStruct((nt * MOE_TSP, LANES), F32),
                   jax.ShapeDtypeStruct((T, 1), jnp.int32),
                   jax.ShapeDtypeStruct((nt, 1, LANES), jnp.int32)],
        compiler_params=_cparams(("parallel",)),
        name="moe_sort",
    )(u, comb, tri)


def _moe_kernel(meta_ref, us_ref, cs_ref, wgu_ref, wd_ref, o_ref, acc_ref):
    i = pl.program_id(0)
    e = pl.program_id(1)
    g = e // EXPERTS_PER_GROUP

    @pl.when(e == 0)
    def _():
        acc_ref[...] = jnp.zeros_like(acc_ref)

    for tile in range(MOE_TPS):
        base = (i * MOE_TPS + tile) * MOE_META
        first = meta_ref[base + g] + tile * (MOE_TSP // MOE_RB)
        count = meta_ref[base + N_GROUPS + g]

        def blocks(j0, n, first=first):
            rows = [pl.multiple_of((first + j0 + k) * MOE_RB, MOE_RB) for k in range(n)]
            gus = [_dot(us_ref[pl.ds(r0, MOE_RB), :], wgu_ref[...]) for r0 in rows]
            for r0, gu in zip(rows, gus):
                cs = cs_ref[pl.ds(r0, MOE_RB), :]
                lane = lax.broadcasted_iota(jnp.int32, cs.shape, 1)
                ce = jnp.sum(jnp.where(lane == ROUTE_E_OFF + e, cs, 0.0), axis=-1, keepdims=True)
                gt = gu[:, 0:D_EXPERT]
                hid = gt * (1.0 / (1.0 + jnp.exp(-gt))) * gu[:, D_EXPERT:2 * D_EXPERT]
                acc_ref[pl.ds(r0, MOE_RB), :] += _dot((ce * hid).astype(BF16), wd_ref[...])

        def pair(j, carry, blocks=blocks):
            blocks(2 * j, 2)
            return carry

        def single(_, carry, blocks=blocks, count=count):
            blocks(count - 1, 1)
            return carry

        lax.fori_loop(0, count // 2, pair, 0)
        lax.fori_loop(0, count & 1, single, 0)

    @pl.when(e == N_EXPERTS - 1)
    def _():
        o_ref[...] = acc_ref[...].astype(o_ref.dtype)


def _moe(meta, us, cs, wgu, wd):
    rows = MOE_TPS * MOE_TSP
    grid_spec = pltpu.PrefetchScalarGridSpec(
        num_scalar_prefetch=1,
        grid=(us.shape[0] // rows, N_EXPERTS),
        in_specs=[pl.BlockSpec((rows, D_MODEL), lambda i, e, m: (i, 0)),
                  pl.BlockSpec((rows, LANES), lambda i, e, m: (i, 0)),
                  pl.BlockSpec((None, D_MODEL, 2 * D_EXPERT), lambda i, e, m: (e, 0, 0)),
                  pl.BlockSpec((None, D_EXPERT, D_MODEL), lambda i, e, m: (e, 0, 0))],
        out_specs=pl.BlockSpec((rows, D_MODEL), lambda i, e, m: (i, 0)),
        scratch_shapes=[pltpu.VMEM((rows, D_MODEL), F32)])
    return pl.pallas_call(
        _moe_kernel,
        grid_spec=grid_spec,
        out_shape=jax.ShapeDtypeStruct(us.shape, BF16),
        compiler_params=_cparams(("parallel", "arbitrary")),
        name="moe_experts",
    )(meta, us, cs, wgu, wd)


def _ple_kernel(ys_ref, pos_ref, h_ref, p_ref, gple_ref, wg_ref, bg_ref, wp_ref, gfin_ref, o_ref):
    r_idx = lax.broadcasted_iota(jnp.int32, (pos_ref.shape[0], MOE_TSP), 1)
    unperm = jnp.where(r_idx == pos_ref[...], 1.0, 0.0).astype(BF16)
    h = h_ref[...] + _dot(unperm, ys_ref[...])
    v = _rms(h, gple_ref[...]).astype(BF16)
    z = _dot(v, wg_ref[...]) + bg_ref[...]
    gate = 1.0 / (1.0 + jnp.exp(-z))
    proj = _dot(p_ref[...].astype(BF16), wp_ref[...])
    o_ref[...] = _rms(h + gate * proj, gfin_ref[...])


def _ple(ys, pos, h1, p2, g_ple, w_g, b_g, w_p, g_final, tm):
    T = h1.shape[0]
    per = MOE_TS // tm
    full = lambda shape: pl.BlockSpec(shape, lambda i: (0,) * len(shape))
    row = lambda w: pl.BlockSpec((tm, w), lambda i: (i, 0))
    return pl.pallas_call(
        _ple_kernel,
        grid=(T // tm,),
        in_specs=[pl.BlockSpec((MOE_TSP, D_MODEL), lambda i: (i // per, 0)), row(1), row(D_MODEL), row(PLE_DIM),
                  full((1, D_MODEL)), full((D_MODEL, D_MODEL)),
                  full((1, D_MODEL)), full((PLE_DIM, D_MODEL)), full((1, D_MODEL))],
        out_specs=row(D_MODEL),
        out_shape=jax.ShapeDtypeStruct((T, D_MODEL), F32),
        compiler_params=_cparams(("parallel",)),
        name="ple_final",
    )(ys, pos, h1, p2, g_ple, w_g, b_g, w_p, g_final)


def _rope_lane_tables():
    half = ROPE_HALF
    inv_freq = jnp.power(jnp.float32(ROPE_THETA), -jnp.arange(half, dtype=jnp.float32) / half)
    j = np.arange(LANES) % HEAD_DIM
    freq = jnp.where(jnp.asarray(j < ROPE_DIM), inv_freq[jnp.asarray(j % half)], 0.0).reshape(1, LANES)
    mlo = jnp.asarray(np.where(j < half, -1.0, 0.0), F32).reshape(1, LANES)
    mhi = jnp.asarray(np.where((j >= half) & (j < ROPE_DIM), 1.0, 0.0), F32).reshape(1, LANES)
    return inv_freq, freq.astype(F32), mlo, mhi


def _block_diag2(w):
    z = jnp.zeros_like(w)
    return jnp.concatenate([jnp.concatenate([w, z], axis=-1), jnp.concatenate([z, w], axis=-1)], axis=-2)


def _aux_route_table():
    r = np.zeros((AUX_PIECES, LANES, FOX_NPAIR * LANES), np.float32)
    for head in range(FOX_HEADS):
        for k in range(AUX_PIECES):
            r[k, SMALL_F_OFF + head, (head // FOX_PAIR) * LANES + (head % FOX_PAIR) * AUX_PIECES + k] = 1.0
    return jnp.asarray(r, BF16)


def _layer(h3, p3, positions, prm, g_final):
    B, S, _ = h3.shape
    T = B * S
    tm = 512
    tm_proj = 1024
    x2 = h3.reshape(T, D_MODEL)
    inv_freq, freq, mlo, mhi = _rope_lane_tables()

    w_in = prm['w_in']
    offs = np.cumsum([0, NSA_WIDTH, KV_WIDTH, KV_WIDTH, KV_WIDTH, KV_WIDTH, KV_WIDTH, KV_WIDTH,
                      N_GATE, FOX_WIDTH, FOX_WIDTH, FOX_WIDTH, FOX_HEADS])
    seg = lambda k: w_in[:, offs[k]:offs[k + 1]]
    wq = seg(0).reshape(D_MODEL, NSA_GROUPS, NSA_HPG, HEAD_DIM).transpose(0, 2, 1, 3).reshape(D_MODEL, NSA_WIDTH)
    pad = jnp.zeros((D_MODEL, LANES - N_GATE - FOX_HEADS), w_in.dtype)
    w_all = jnp.concatenate([wq, seg(1), seg(2), seg(3), seg(5), seg(8), seg(9), seg(7), seg(11), pad],
                            axis=1).astype(BF16)
    wt_v = jnp.concatenate([seg(4), seg(6), seg(10)], axis=1).T.astype(BF16)
    bias_small = jnp.concatenate([prm['b_nsa_gate'], prm['b_forget'],
                                  jnp.zeros((LANES - N_GATE - FOX_HEADS,), F32)]).reshape(1, LANES)

    qa, cmp_tok, knsa, vnsa_t, fqk, vfox_t, small = _inproj(
        x2, positions.reshape(T // tm_proj, 1, tm_proj), prm['g_mix'].reshape(1, D_MODEL), w_all, wt_v,
        bias_small, inv_freq.reshape(ROPE_HALF, 1), mlo, mhi, tm_proj, B, S)

    tri = jnp.asarray(np.tril(np.ones((CUM_SUB, CUM_SUB), np.float32)), BF16)
    small3 = small.reshape(B, S, LANES)
    aux3 = _cumsum(small3, tri, _aux_route_table())

    n_rows = S // CMP_STRIDE
    tile2 = lambda pe: jnp.concatenate([pe, pe], axis=-1)
    bd1 = lambda w: _block_diag2(w.reshape(CMP_BLOCK, HEAD_DIM, CMP_HIDDEN)).astype(BF16)
    pos_cmp = positions[:, CMP_BLOCK - 1::CMP_STRIDE]
    pos_cmp = jnp.pad(pos_cmp, ((0, 0), (0, n_rows - pos_cmp.shape[1]))).reshape(B, n_rows, 1)
    kc, vct = _compress(cmp_tok.reshape(B, S, 2 * KV_WIDTH), tile2(prm['pe_cmp_k']), tile2(prm['pe_cmp_v']),
                        bd1(prm['w_cmp_k1']), bd1(prm['w_cmp_v1']),
                        _block_diag2(prm['w_cmp_k2']).astype(BF16), _block_diag2(prm['w_cmp_v2']).T.astype(BF16),
                        pos_cmp, freq, mlo, mhi)

    n_cmp = (S - CMP_BLOCK) // CMP_STRIDE + 1
    n_blk = S // SEL_BLOCK
    cs = np.arange(n_rows)[:, None] * CMP_STRIDE
    ss = np.arange(n_blk)[None, :] * SEL_BLOCK
    ov = np.clip(np.minimum(cs + CMP_BLOCK, ss + SEL_BLOCK) - np.maximum(cs, ss), 0, None) / CMP_BLOCK
    ov[n_cmp:] = 0.0
    aggt = jnp.asarray(ov.T, BF16)
    et = jnp.asarray((np.arange(S)[:, None] // SEL_BLOCK) == np.arange(LANES)[None, :], BF16)
    eye = jnp.asarray(np.eye(NSA_GQ, dtype=np.float32), BF16)
    gexp = small3[:, :, :N_GATE].reshape(B, S // NSA_QB, NSA_QB, NSA_GROUPS, NSA_HPG, 3)
    gexp = gexp.transpose(0, 1, 5, 3, 4, 2).reshape(B, S // NSA_QB, 3, NSA_ROWS)
    o_nsa = _nsa_attention(qa.reshape(B, S, NSA_WIDTH), gexp, kc, vct, knsa.reshape(B, S, 2 * KV_WIDTH),
                           vnsa_t, aggt, et, eye)

    o_fox = _fox_attention(fqk.reshape(B, S, 2 * FOX_WIDTH), aux3, vfox_t)

    perm = lambda a: a.reshape(NSA_GROUPS, NSA_HPG, HEAD_DIM, -1).transpose(1, 0, 2, 3).reshape(NSA_WIDTH, -1)
    beta_n = perm(prm['beta_nsa'].reshape(NSA_WIDTH, 1)).reshape(1, NSA_WIDTH)
    w_out = prm['w_out']
    w_n = perm(w_out[:NSA_WIDTH]).astype(BF16)
    w_f = w_out[NSA_WIDTH:].astype(BF16)
    w_r = jnp.concatenate([prm['w_group'], prm['w_router'],
                           jnp.zeros((D_MODEL, LANES - N_GROUPS - N_EXPERTS), F32)], axis=1).T.astype(BF16)
    b_r = jnp.concatenate([prm['b_group'], prm['b_router'],
                           jnp.zeros((LANES - N_GROUPS - N_EXPERTS,), F32)]).reshape(LANES, 1)
    h1, u, comb = _outproj(o_nsa.reshape(T, NSA_WIDTH), o_fox.reshape(T, FOX_WIDTH), x2, beta_n,
                           prm['beta_fox'].reshape(1, FOX_WIDTH), w_n, w_f,
                           prm['g_ffn'].reshape(1, D_MODEL), w_r, b_r, tm_proj)

    wgu = jnp.concatenate([prm['w_gate_e'], prm['w_up_e']], axis=-1).astype(BF16)
    tri_s = jnp.asarray(np.tril(np.ones((MOE_TS, MOE_TS), np.float32)), BF16)
    us, cs, pos, meta = _moe_sort(u, comb, tri_s)
    ys = _moe(meta[:, 0, :MOE_META].reshape(-1), us, cs, wgu, prm['w_down_e'].astype(BF16))

    out = _ple(ys, pos, h1, p3.reshape(T, PLE_DIM), prm['g_ple'].reshape(1, D_MODEL),
               prm['w_ple_gate'].astype(BF16),
               prm['b_ple_gate'].reshape(1, D_MODEL), prm['w_ple_proj'].astype(BF16),
               g_final.reshape(1, D_MODEL), tm)
    return out.reshape(B, S, D_MODEL)


_PARAM_NAMES = ('g_mix', 'w_in', 'b_nsa_gate', 'b_forget', 'pe_cmp_k', 'w_cmp_k1', 'w_cmp_k2',
                'pe_cmp_v', 'w_cmp_v1', 'w_cmp_v2', 'beta_nsa', 'beta_fox', 'w_out', 'g_ffn',
                'w_group', 'b_group', 'w_router', 'b_router', 'w_gate_e', 'w_up_e', 'w_down_e',
                'g_ple', 'w_ple_gate', 'b_ple_gate', 'w_ple_proj')


def kernel(x, p, positions, g_mix, w_in, b_nsa_gate, b_forget, pe_cmp_k, w_cmp_k1, w_cmp_k2, pe_cmp_v,
           w_cmp_v1, w_cmp_v2, beta_nsa, beta_fox, w_out, g_ffn, w_group, b_group, w_router, b_router,
           w_gate_e, w_up_e, w_down_e, g_ple, w_ple_gate, b_ple_gate, w_ple_proj, g_final):
    stacked = (g_mix, w_in, b_nsa_gate, b_forget, pe_cmp_k, w_cmp_k1, w_cmp_k2, pe_cmp_v, w_cmp_v1,
               w_cmp_v2, beta_nsa, beta_fox, w_out, g_ffn, w_group, b_group, w_router, b_router,
               w_gate_e, w_up_e, w_down_e, g_ple, w_ple_gate, b_ple_gate, w_ple_proj)
    depth = w_in.shape[0]
    assert depth == 1, "the final norm is fused into the last layer's embedding kernel"
    prm = {n: a[0] for n, a in zip(_PARAM_NAMES, stacked)}
    return _layer(x, p[0], positions, prm, g_final)
```

```python
import functools
import math

import numpy as np
import jax
import jax.numpy as jnp
from jax import lax
from jax.experimental import pallas as pl
from jax.experimental.pallas import tpu as pltpu

D_MODEL = 1024
HEAD_DIM = 64
NSA_HEADS = 8
FOX_HEADS = 8
NSA_GROUPS = 2
NSA_HPG = 4
NSA_WIDTH = 512
FOX_WIDTH = 512
KV_WIDTH = 128
CMP_BLOCK = 32
CMP_STRIDE = 16
CMP_HIDDEN = 128
SEL_BLOCK = 64
N_SELECT = 16
WINDOW = 512
ROPE_THETA = 500000.0
ROPE_DIM = 16
ROPE_HALF = 8
N_GROUPS = 4
EXPERTS_PER_GROUP = 4
N_EXPERTS = 16
D_EXPERT = 512
PLE_DIM = 256
EPS = 1e-6
NEG_INF = -1e30
FORCE_BONUS = 1e4
LOG2E = math.log2(math.e)
Q_SCALE = 0.125 * LOG2E
NEG_BIG = -(2.0 ** 100)

LANES = 128
SUBLANES = 8
N_GATE = 3 * NSA_HEADS
SMALL_F_OFF = N_GATE
ROUTE_E_OFF = N_GROUPS
ROUTE_G_LANE = 0

VMEM_LIMIT = 56 * 1024 * 1024

F32 = jnp.float32
BF16 = jnp.bfloat16


def _cparams(sem):
    return pltpu.CompilerParams(dimension_semantics=sem, vmem_limit_bytes=VMEM_LIMIT)


def _dot(a, b):
    return jnp.dot(a, b, preferred_element_type=F32)


def _dot_nt(a, b):
    return lax.dot_general(a, b, (((1,), (1,)), ((), ())), preferred_element_type=F32)


def _split3(x):
    hi = x.astype(BF16)
    r1 = x - hi.astype(F32)
    mid = r1.astype(BF16)
    lo = (r1 - mid.astype(F32)).astype(BF16)
    return hi, mid, lo


def _split3_dot_lhs(e, x):
    hi, mid, lo = _split3(x)
    return _dot(e, hi) + _dot(e, mid) + _dot(e, lo)


def _rms(x, g):
    return x * lax.rsqrt(jnp.mean(x * x, axis=-1, keepdims=True) + EPS) * g


def _rope(r, cos, sin_lo, sin_hi):
    return (r * cos + pltpu.roll(r, LANES - ROPE_HALF, 1) * sin_lo
            + pltpu.roll(r, ROPE_HALF, 1) * sin_hi)


def _rope_tables(pos_col, freq, mlo, mhi):
    ang = pos_col.astype(F32) * freq
    cos = jnp.cos(ang)
    sin = jnp.sin(ang)
    return cos, sin * mlo, sin * mhi


def _rope_tables_from_row(pos_row, freq_col, mlo, mhi):
    m = pos_row.shape[1]
    ang_t = freq_col * pos_row.astype(F32)
    pad = jnp.zeros((LANES - ROPE_HALF, m), F32)
    lane = lax.broadcasted_iota(jnp.int32, (m, LANES), 1)
    rot = (lane & (HEAD_DIM - 1)) < ROPE_DIM

    def spread(t):
        x = jnp.concatenate([t, pad], axis=0).T
        x = x + pltpu.roll(x, ROPE_HALF, 1)
        return x + pltpu.roll(x, HEAD_DIM, 1)

    cos = jnp.where(rot, spread(jnp.cos(ang_t)), 1.0)
    sin = spread(jnp.sin(ang_t))
    return cos, sin * mlo, sin * mhi


def _inproj_kernel(x_ref, pos_ref, g_ref, w_ref, wt_ref, bias_ref, freq_ref, mlo_ref, mhi_ref,
                   qa_ref, cmp_ref, knsa_ref, vnsa_ref, fqk_ref, vfox_ref, small_ref):
    x = x_ref[...]
    hb = _rms(x, g_ref[...]).astype(BF16)
    cos, s_lo, s_hi = _rope_tables_from_row(pos_ref[...], freq_ref[...], mlo_ref[...], mhi_ref[...])
    off = 0
    r = _dot(hb, w_ref[:, off:off + NSA_WIDTH])
    for c in range(NSA_WIDTH // LANES):
        rc = _rope(r[:, c * LANES:(c + 1) * LANES], cos, s_lo, s_hi) * Q_SCALE
        qa_ref[:, c * LANES:(c + 1) * LANES] = rc.astype(qa_ref.dtype)
    off += NSA_WIDTH
    cmp_ref[...] = _dot(hb, w_ref[:, off:off + 2 * KV_WIDTH])
    off += 2 * KV_WIDTH
    r = _dot(hb, w_ref[:, off:off + 2 * KV_WIDTH])
    for c in range(2):
        rc = _rope(r[:, c * LANES:(c + 1) * LANES], cos, s_lo, s_hi)
        knsa_ref[:, c * LANES:(c + 1) * LANES] = rc.astype(knsa_ref.dtype)
    off += 2 * KV_WIDTH
    r = _dot(hb, w_ref[:, off:off + FOX_WIDTH])
    fqk_ref[:, 0:FOX_WIDTH] = (r * Q_SCALE).astype(fqk_ref.dtype)
    off += FOX_WIDTH
    r = _dot(hb, w_ref[:, off:off + FOX_WIDTH])
    fqk_ref[:, FOX_WIDTH:2 * FOX_WIDTH] = r.astype(fqk_ref.dtype)
    off += FOX_WIDTH
    z = _dot(hb, w_ref[:, off:off + LANES]) + bias_ref[...]
    lane = lax.broadcasted_iota(jnp.int32, z.shape, 1)
    sig = 1.0 / (1.0 + jnp.exp(-z))
    logsig = jnp.minimum(z, 0.0) - jnp.log(1.0 + jnp.exp(-jnp.abs(z)))
    small_ref[...] = jnp.where(lane < SMALL_F_OFF, sig, logsig)
    vt = _dot_nt(wt_ref[...], hb)
    vnsa_ref[...] = vt[0:2 * KV_WIDTH].astype(vnsa_ref.dtype)
    vfox_ref[...] = vt[2 * KV_WIDTH:2 * KV_WIDTH + FOX_WIDTH].astype(vfox_ref.dtype)


def _inproj(x2, pos2, g_mix, w_all, wt_v, bias_small, freq, mlo, mhi, tm, B, S):
    T = x2.shape[0]
    nt = S // tm
    full = lambda shape: pl.BlockSpec(shape, lambda i: (0,) * len(shape))
    row = lambda w: pl.BlockSpec((tm, w), lambda i: (i, 0))
    tr = lambda w: pl.BlockSpec((None, w, tm), lambda i: (i // nt, 0, i % nt))
    return pl.pallas_call(
        _inproj_kernel,
        grid=(T // tm,),
        in_specs=[row(D_MODEL), pl.BlockSpec((None, 1, tm), lambda i: (i, 0, 0)), full((1, D_MODEL)),
                  full(w_all.shape), full(wt_v.shape),
                  full((1, LANES)), full((ROPE_HALF, 1)), full((1, LANES)), full((1, LANES))],
        out_specs=[row(NSA_WIDTH), row(2 * KV_WIDTH), row(2 * KV_WIDTH), tr(2 * KV_WIDTH),
                   row(2 * FOX_WIDTH), tr(FOX_WIDTH), row(LANES)],
        out_shape=[jax.ShapeDtypeStruct((T, NSA_WIDTH), BF16),
                   jax.ShapeDtypeStruct((T, 2 * KV_WIDTH), F32),
                   jax.ShapeDtypeStruct((T, 2 * KV_WIDTH), BF16),
                   jax.ShapeDtypeStruct((B, 2 * KV_WIDTH, S), BF16),
                   jax.ShapeDtypeStruct((T, 2 * FOX_WIDTH), BF16),
                   jax.ShapeDtypeStruct((B, FOX_WIDTH, S), BF16),
                   jax.ShapeDtypeStruct((T, LANES), F32)],
        compiler_params=_cparams(("parallel",)),
        name="inproj",
    )(x2, pos2, g_mix, w_all, wt_v, bias_small, freq, mlo, mhi)


CUM_CHUNK = 1024
CUM_SUB = 128
FOX_PAIR = 2
FOX_NPAIR = FOX_HEADS // FOX_PAIR
AUX_PIECES = 3


def _cumsum_kernel(x_ref, tri_ref, route_ref, o_ref, carry_ref):
    @pl.when(pl.program_id(1) == 0)
    def _():
        carry_ref[...] = jnp.zeros_like(carry_ref)

    carry = carry_ref[...]
    pieces = []
    for k in range(CUM_CHUNK // CUM_SUB):
        ck = _split3_dot_lhs(tri_ref[...], x_ref[k * CUM_SUB:(k + 1) * CUM_SUB, :]) + carry
        carry = ck[CUM_SUB - 1:CUM_SUB, :]
        pieces.append(ck)
    carry_ref[...] = carry
    c = jnp.concatenate(pieces, axis=0)
    hi, mid, lo = _split3(c * LOG2E)
    aux = _dot(hi, route_ref[0]) + _dot(mid, route_ref[1]) + _dot(lo, route_ref[2])
    o_ref[...] = aux.astype(o_ref.dtype)


def _cumsum(small3, tri, route):
    B, S, _ = small3.shape
    width = FOX_NPAIR * LANES
    return pl.pallas_call(
        _cumsum_kernel,
        grid=(B, S // CUM_CHUNK),
        in_specs=[pl.BlockSpec((None, CUM_CHUNK, LANES), lambda b, i: (b, i, 0)),
                  pl.BlockSpec((CUM_SUB, CUM_SUB), lambda b, i: (0, 0)),
                  pl.BlockSpec((AUX_PIECES, LANES, width), lambda b, i: (0, 0, 0))],
        out_specs=pl.BlockSpec((None, CUM_CHUNK, width), lambda b, i: (b, i, 0)),
        out_shape=jax.ShapeDtypeStruct((B, S, width), BF16),
        scratch_shapes=[pltpu.VMEM((1, LANES), F32)],
        compiler_params=_cparams(("parallel", "arbitrary")),
        name="forget_cumsum",
    )(small3, tri, route)


def _compress_kernel(tokk_ref, tokv_ref, pek_ref, pev_ref, bdk1_ref, bdv1_ref, bdk2_ref, bdv2t_ref,
                     pos_ref, freq_ref, mlo_ref, mhi_ref, kc_ref, vct_ref, *, n_rows):
    half = CMP_BLOCK // 2
    ak = jnp.zeros((n_rows, 2 * CMP_HIDDEN), F32)
    bk = jnp.zeros((n_rows, 2 * CMP_HIDDEN), F32)
    av = jnp.zeros((n_rows, 2 * CMP_HIDDEN), F32)
    bv = jnp.zeros((n_rows, 2 * CMP_HIDDEN), F32)
    for l in range(half):
        xk = tokk_ref[pl.ds(l, n_rows, stride=CMP_STRIDE), :]
        xv = tokv_ref[pl.ds(l, n_rows, stride=CMP_STRIDE), :]
        ak = ak + _dot((xk + pek_ref[l:l + 1, :]).astype(BF16), bdk1_ref[l])
        bk = bk + _dot((xk + pek_ref[half + l:half + l + 1, :]).astype(BF16), bdk1_ref[half + l])
        av = av + _dot((xv + pev_ref[l:l + 1, :]).astype(BF16), bdv1_ref[l])
        bv = bv + _dot((xv + pev_ref[half + l:half + l + 1, :]).astype(BF16), bdv1_ref[half + l])
    hk = ak + pltpu.roll(bk, n_rows - 1, 0)
    hv = av + pltpu.roll(bv, n_rows - 1, 0)
    hk = hk * (1.0 / (1.0 + jnp.exp(-hk)))
    hv = hv * (1.0 / (1.0 + jnp.exp(-hv)))
    kc = _dot(hk.astype(BF16), bdk2_ref[...])
    cos, s_lo, s_hi = _rope_tables(pos_ref[...], freq_ref[...], mlo_ref[...], mhi_ref[...])
    kc_ref[...] = _rope(kc, cos, s_lo, s_hi).astype(kc_ref.dtype)
    vct_ref[...] = _dot_nt(bdv2t_ref[...], hv.astype(BF16)).astype(vct_ref.dtype)


def _compress(cmp_tok3, pek, pev, bdk1, bdv1, bdk2, bdv2t, pos_cmp, freq, mlo, mhi):
    B, S, _ = cmp_tok3.shape
    n_rows = S // CMP_STRIDE
    full = lambda shape: pl.BlockSpec(shape, lambda b: (0,) * len(shape))
    return pl.pallas_call(
        functools.partial(_compress_kernel, n_rows=n_rows),
        grid=(B,),
        in_specs=[pl.BlockSpec((None, S, KV_WIDTH), lambda b: (b, 0, 0)),
                  pl.BlockSpec((None, S, KV_WIDTH), lambda b: (b, 0, 1)),
                  full((CMP_BLOCK, KV_WIDTH)), full((CMP_BLOCK, KV_WIDTH)),
                  full((CMP_BLOCK, KV_WIDTH, 2 * CMP_HIDDEN)), full((CMP_BLOCK, KV_WIDTH, 2 * CMP_HIDDEN)),
                  full((2 * CMP_HIDDEN, KV_WIDTH)), full((KV_WIDTH, 2 * CMP_HIDDEN)),
                  pl.BlockSpec((None, n_rows, 1), lambda b: (b, 0, 0)),
                  full((1, LANES)), full((1, LANES)), full((1, LANES))],
        out_specs=[pl.BlockSpec((None, n_rows, KV_WIDTH), lambda b: (b, 0, 0)),
                   pl.BlockSpec((None, KV_WIDTH, n_rows), lambda b: (b, 0, 0))],
        out_shape=[jax.ShapeDtypeStruct((B, n_rows, KV_WIDTH), BF16),
                   jax.ShapeDtypeStruct((B, KV_WIDTH, n_rows), BF16)],
        compiler_params=_cparams(("parallel",)),
        name="nsa_compress",
    )(cmp_tok3, cmp_tok3, pek, pev, bdk1, bdv1, bdk2, bdv2t, pos_cmp, freq, mlo, mhi)


NSA_QB = 128
NSA_BPQ = NSA_QB // SEL_BLOCK
assert NSA_BPQ in (1, 2) and NSA_QB % LANES == 0
NSA_KC = 512
NSA_GQ = NSA_GROUPS * NSA_QB
NSA_GH = NSA_HPG * NSA_QB
NSA_ROWS = NSA_GROUPS * NSA_GH
NSA_WSPAN = WINDOW + NSA_QB


def _where_tiles(mask, x, other):
    w = mask.shape[1]
    return jnp.concatenate([jnp.where(mask, x[:, t * w:(t + 1) * w], other)
                            for t in range(x.shape[1] // w)], axis=1)


def _masked_softmax_cols(s, mask):
    l = _where_tiles(mask, s, NEG_INF)
    m = jnp.max(l, axis=0, keepdims=True)
    e = jnp.exp2(l - m)
    return e, jnp.sum(e, axis=0, keepdims=True)


FLASH_NSUB = 2


def _flash_chunks(n_chunks, score_fn, vt_fn, init_fn, s_a, s_b, acc_ref, pv_fn=_dot):
    last = jnp.maximum(n_chunks - 1, 0)
    sub = s_a.shape[0] // FLASH_NSUB

    def stage(i_prod, s_prod, i_cons, s_cons, mx, m, l):
        if i_cons is not None:
            m_new = jnp.maximum(m, mx)
            alpha = jnp.exp2(m - m_new)
            l = alpha * l
        if i_prod is not None:
            i_prod = jnp.minimum(i_prod, last)
        mx_prod, pv = None, None
        for t in range(FLASH_NSUB):
            rows = pl.ds(t * sub, sub)
            if i_prod is not None:
                s = score_fn(i_prod, t)
                s_prod[rows, :] = s
                mx_t = jnp.max(s, axis=0, keepdims=True)
                mx_prod = mx_t if mx_prod is None else jnp.maximum(mx_prod, mx_t)
            if i_cons is not None:
                p = jnp.exp2(s_cons[rows, :] - m_new)
                l = l + jnp.sum(p, axis=0, keepdims=True)
                d = pv_fn(vt_fn(i_cons, t), p.astype(BF16))
                pv = d if pv is None else pv + d
        if i_cons is not None:
            acc_ref[...] = alpha * acc_ref[...] + pv
            m = m_new
        return mx_prod, m, l

    def pair(j, carry):
        mx_a, m, l = carry
        i = 2 * j
        mx_b, m, l = stage(i + 1, s_b, i, s_a, mx_a, m, l)
        mx_a, m, l = stage(i + 2, s_a, i + 1, s_b, mx_b, m, l)
        return mx_a, m, l

    def tail(_, carry):
        mx_a, m, l = carry
        _, m, l = stage(None, None, last, s_a, mx_a, m, l)
        return mx_a, m, l

    mx_a, _, _ = stage(0, s_a, None, None, None, None, None)
    m, l = init_fn()
    carry = (mx_a, m, l)
    carry = lax.fori_loop(0, n_chunks // 2, pair, carry)
    carry = lax.fori_loop(0, n_chunks & 1, tail, carry)
    return carry[2]


def _nsa_kernel(q_ref, gate_ref, kc_ref, vct_ref, ks_ref, kw_ref, vst_ref, vwt_ref, aggt_ref, et_ref,
                eye_ref, o_ref, sa_scr, sb_scr, acc_scr, *, n_cmp, n_blk, n_sel):
    c = pl.program_id(1)
    s0 = c * NSA_QB
    lane = lax.broadcasted_iota(jnp.int32, (NSA_QB, LANES), 1)
    low = lane < HEAD_DIM
    qb = q_ref[...]
    zero = jnp.zeros((NSA_QB, LANES), qb.dtype)
    pieces = []
    for g in range(NSA_GROUPS):
        for h in range(NSA_HPG):
            qh = qb[:, h * LANES:(h + 1) * LANES]
            pieces.append(jnp.where(low, qh, zero) if g == 0 else jnp.where(low, zero, qh))
    qp = jnp.concatenate(pieces, axis=0)
    t_l = s0 + lax.broadcasted_iota(jnp.int32, (1, NSA_QB), 1)
    cur_l = (s0 + (lax.broadcasted_iota(jnp.int32, (1, NSA_GQ), 1) & (NSA_QB - 1))) // SEL_BLOCK

    def group_values(vt, p):
        return jnp.concatenate([_dot(vt[g * HEAD_DIM:(g + 1) * HEAD_DIM], p[:, g * NSA_GH:(g + 1) * NSA_GH])
                                for g in range(NSA_GROUPS)], axis=1)
    first_blk = c * NSA_BPQ
    last_blk = first_blk + NSA_BPQ - 1

    n_pad = kc_ref.shape[0]
    n_s = lax.broadcasted_iota(jnp.int32, (n_pad, 1), 0)
    mask_c = ((n_s * CMP_STRIDE + (CMP_BLOCK - 1)) <= t_l) & (n_s < n_cmp)
    start = pl.multiple_of(jnp.maximum(s0 - WINDOW, 0), LANES)
    rel = t_l - (start + lax.broadcasted_iota(jnp.int32, (NSA_WSPAN, 1), 0))
    mask_w = (rel >= 0) & (rel < WINDOW)
    s_c = _dot_nt(kc_ref[...], qp)
    s_w = _dot_nt(kw_ref[pl.ds(start, NSA_WSPAN), :], qp)
    e_c, sum_c = _masked_softmax_cols(s_c, mask_c)
    inv_c = _where_tiles(t_l >= CMP_BLOCK - 1, 1.0 / sum_c, 0.0)
    acc_c = group_values(vct_ref[...], e_c.astype(BF16))

    j_s = lax.broadcasted_iota(jnp.int32, (n_blk, NSA_GQ), 0)
    nv = n_blk // SUBLANES

    def ranked_mask():
        p_c = e_c * inv_c
        per_group = []
        for g in range(NSA_GROUPS):
            tiles = [p_c[:, (g * NSA_HPG + h) * NSA_QB:(g * NSA_HPG + h + 1) * NSA_QB] for h in range(NSA_HPG)]
            per_group.append(functools.reduce(jnp.add, tiles))
        psum = jnp.concatenate(per_group, axis=1)
        imp = _split3_dot_lhs(aggt_ref[...], psum)
        forced = (j_s == 0) | (j_s == cur_l) | (j_s == cur_l - 1)
        score = jnp.where(j_s > cur_l, -1.0, imp + jnp.where(forced, FORCE_BONUS, 0.0))
        sc = [score[SUBLANES * v:SUBLANES * (v + 1), :] for v in range(nv)]
        sub = lax.broadcasted_iota(jnp.int32, (SUBLANES, NSA_GQ), 0)

        def count_group(vi, rk):
            rk = list(rk)
            for ri in range(SUBLANES):
                row = sc[vi][ri:ri + 1, :]
                for v in range(nv):
                    if v < vi:
                        beats = jnp.where(row > sc[v], 1.0, 0.0)
                    elif v > vi:
                        beats = jnp.where(row >= sc[v], 1.0, 0.0)
                    else:
                        beats = jnp.where(sub > ri, jnp.where(row >= sc[v], 1.0, 0.0),
                                          jnp.where(row > sc[v], 1.0, 0.0))
                    rk[v] = rk[v] + beats
            return tuple(rk)

        rk = tuple(jnp.zeros((SUBLANES, NSA_GQ), F32) for _ in range(nv))
        for vi in range(nv):
            rk = lax.cond(vi * SUBLANES <= last_blk, functools.partial(count_group, vi), lambda r: r, rk)
        rank = jnp.concatenate(rk, axis=0)
        return jnp.where((rank < n_sel) & (j_s < first_blk), 0.0, NEG_BIG)

    def all_mask():
        return jnp.where(j_s < first_blk, 0.0, NEG_BIG)

    neg_t = lax.cond(last_blk < n_sel, all_mask, ranked_mask)

    a0 = pl.multiple_of(s0, LANES)
    s_d = _dot_nt(ks_ref[pl.ds(a0, NSA_QB), :], qp)
    neg_t = jnp.concatenate([neg_t, jnp.zeros((LANES - n_blk, NSA_GQ), F32)], axis=0).astype(BF16)
    q_aux = _dot_nt(eye_ref[...], neg_t).astype(BF16)
    aux_rows = [q_aux[g * NSA_QB:(g + 1) * NSA_QB] for g in range(NSA_GROUPS) for _ in range(NSA_HPG)]
    qx = jnp.concatenate([qp, jnp.concatenate(aux_rows, axis=0)], axis=1)

    sub_keys = NSA_KC // FLASH_NSUB

    def sel_scores(i, t):
        k0 = pl.multiple_of(i * NSA_KC + t * sub_keys, sub_keys)
        kx = jnp.concatenate([ks_ref[pl.ds(k0, sub_keys), :], et_ref[pl.ds(k0, sub_keys), :]], axis=1)
        return _dot_nt(kx, qx)

    def sel_values(i, t):
        return vst_ref[:, pl.ds(pl.multiple_of(i * NSA_KC + t * sub_keys, sub_keys), sub_keys)]

    window = {}

    def after_first_scores():
        kpos = a0 + lax.broadcasted_iota(jnp.int32, (NSA_QB, 1), 0)
        l_d = _where_tiles(kpos <= t_l, s_d, NEG_INF)
        m_d = jnp.max(l_d, axis=0, keepdims=True)
        p_d = jnp.exp2(l_d - m_d)
        acc_scr[...] = group_values(vst_ref[:, pl.ds(a0, NSA_QB)], p_d.astype(BF16))
        e_w, window['l'] = _masked_softmax_cols(s_w, mask_w)
        window['acc'] = group_values(vwt_ref[:, pl.ds(start, NSA_WSPAN)], e_w.astype(BF16))
        return m_d, jnp.sum(p_d, axis=0, keepdims=True)

    n_prev = (s0 + NSA_KC - 1) // NSA_KC
    l_s = _flash_chunks(n_prev, sel_scores, sel_values, after_first_scores, sa_scr, sb_scr, acc_scr,
                        pv_fn=group_values)

    gates = gate_ref[...]
    o_t = ((gates[0:1, :] * inv_c) * acc_c + (gates[1:2, :] / l_s) * acc_scr[...]
           + (gates[2:3, :] / window['l']) * window['acc'])
    for h in range(NSA_HPG):
        tiles = [o_t[:, (g * NSA_HPG + h) * NSA_QB:(g * NSA_HPG + h + 1) * NSA_QB].T for g in range(NSA_GROUPS)]
        o_ref[:, h * LANES:(h + 1) * LANES] = jnp.concatenate(tiles, axis=1)


def _nsa_attention(qa3, gexp, kc, vct, knsa3, vnsa_t, aggt, et, eye):
    B, S, _ = qa3.shape
    n_pad = kc.shape[1]
    n_cmp = (S - CMP_BLOCK) // CMP_STRIDE + 1
    n_blk = S // SEL_BLOCK
    n_sel = min(N_SELECT, n_blk)
    const = lambda shape: pl.BlockSpec(shape, lambda b, c: (0,) * len(shape))
    return pl.pallas_call(
        functools.partial(_nsa_kernel, n_cmp=n_cmp, n_blk=n_blk, n_sel=n_sel),
        grid=(B, S // NSA_QB),
        in_specs=[pl.BlockSpec((None, NSA_QB, NSA_WIDTH), lambda b, c: (b, c, 0)),
                  pl.BlockSpec((None, None, 3, NSA_ROWS), lambda b, c: (b, c, 0, 0)),
                  pl.BlockSpec((None, n_pad, KV_WIDTH), lambda b, c: (b, 0, 0)),
                  pl.BlockSpec((None, KV_WIDTH, n_pad), lambda b, c: (b, 0, 0)),
                  pl.BlockSpec((None, S, KV_WIDTH), lambda b, c: (b, 0, 0)),
                  pl.BlockSpec((None, S, KV_WIDTH), lambda b, c: (b, 0, 1)),
                  pl.BlockSpec((None, KV_WIDTH, S), lambda b, c: (b, 0, 0)),
                  pl.BlockSpec((None, KV_WIDTH, S), lambda b, c: (b, 1, 0)),
                  const((n_blk, n_pad)), const((S, LANES)), const((NSA_GQ, NSA_GQ))],
        out_specs=pl.BlockSpec((None, NSA_QB, NSA_WIDTH), lambda b, c: (b, c, 0)),
        out_shape=jax.ShapeDtypeStruct((B, S, NSA_WIDTH), F32),
        scratch_shapes=[pltpu.VMEM((NSA_KC, NSA_ROWS), F32), pltpu.VMEM((NSA_KC, NSA_ROWS), F32),
                        pltpu.VMEM((HEAD_DIM, NSA_ROWS), F32)],
        compiler_params=_cparams(("parallel", "arbitrary")),
        name="nsa_attention",
    )(qa3, gexp, kc, vct, knsa3, knsa3, vnsa_t, vnsa_t, aggt, et, eye)


FOX_TQ = 512
FOX_KC = FOX_TQ
FOX_ROWS = FOX_PAIR * FOX_TQ


def _fox_kernel(q_ref, k_ref, aux_ref, vt_ref, o_ref, sa_scr, sb_scr, acc_scr):
    qi = pl.program_id(2)
    s0 = qi * FOX_TQ
    lane = lax.broadcasted_iota(jnp.int32, (FOX_TQ, LANES), 1)
    low = lane < HEAD_DIM
    q2 = q_ref[...]
    zero = jnp.zeros_like(q2)
    aux0 = jnp.where(lane < AUX_PIECES, -1.0, 0.0).astype(q2.dtype)
    aux1 = jnp.where((lane >= AUX_PIECES) & (lane < 2 * AUX_PIECES), -1.0, 0.0).astype(q2.dtype)
    qx = jnp.concatenate([jnp.concatenate([jnp.where(low, q2, zero), aux0], axis=1),
                          jnp.concatenate([jnp.where(low, zero, q2), aux1], axis=1)], axis=0)
    t_l = s0 + lax.broadcasted_iota(jnp.int32, (1, FOX_TQ), 1)

    sub_keys = FOX_KC // FLASH_NSUB

    def scores(i, t, n=sub_keys):
        k0 = pl.multiple_of(i * FOX_KC + t * sub_keys, sub_keys)
        kx = jnp.concatenate([k_ref[pl.ds(k0, n), :], aux_ref[pl.ds(k0, n), :]], axis=1)
        return _dot_nt(kx, qx)

    def values(i, t, n=sub_keys):
        return vt_ref[:, pl.ds(pl.multiple_of(i * FOX_KC + t * sub_keys, sub_keys), n)]

    def diagonal():
        kpos = s0 + lax.broadcasted_iota(jnp.int32, (FOX_KC, 1), 0)
        s_d = _where_tiles(kpos <= t_l, scores(qi, 0, FOX_KC), NEG_INF)
        m_d = jnp.max(s_d, axis=0, keepdims=True)
        p_d = jnp.exp2(s_d - m_d)
        acc_scr[...] = head_values(values(qi, 0, FOX_KC), p_d.astype(BF16))
        return m_d, jnp.sum(p_d, axis=0, keepdims=True)

    def head_values(vt, p):
        return jnp.concatenate([_dot(vt[h * HEAD_DIM:(h + 1) * HEAD_DIM], p[:, h * FOX_TQ:(h + 1) * FOX_TQ])
                                for h in range(FOX_PAIR)], axis=1)

    l_f = _flash_chunks(qi, scores, values, diagonal, sa_scr, sb_scr, acc_scr, pv_fn=head_values)
    o_t = acc_scr[...] / l_f
    o_ref[...] = jnp.concatenate([o_t[:, h * FOX_TQ:(h + 1) * FOX_TQ].T for h in range(FOX_PAIR)], axis=1)


def _fox_attention(fqk3, aux3, vfox_t):
    B, S, _ = fqk3.shape
    return pl.pallas_call(
        _fox_kernel,
        grid=(B, FOX_NPAIR, S // FOX_TQ),
        in_specs=[pl.BlockSpec((None, FOX_TQ, LANES), lambda b, h, i: (b, i, h)),
                  pl.BlockSpec((None, S, LANES), lambda b, h, i: (b, 0, FOX_NPAIR + h)),
                  pl.BlockSpec((None, S, LANES), lambda b, h, i: (b, 0, h)),
                  pl.BlockSpec((None, LANES, S), lambda b, h, i: (b, h, 0))],
        out_specs=pl.BlockSpec((None, FOX_TQ, LANES), lambda b, h, i: (b, i, h)),
        out_shape=jax.ShapeDtypeStruct((B, S, FOX_WIDTH), F32),
        scratch_shapes=[pltpu.VMEM((FOX_KC, FOX_ROWS), F32), pltpu.VMEM((FOX_KC, FOX_ROWS), F32),
                        pltpu.VMEM((HEAD_DIM, FOX_ROWS), F32)],
        compiler_params=_cparams(("parallel", "parallel", "arbitrary")),
        name="fox_attention",
    )(fqk3, fqk3, aux3, vfox_t)


ROUTE_ROWS = 32


def _first_of(vals, v):
    idx = jnp.full(v.shape, len(vals) - 1, jnp.int32)
    for k in range(len(vals) - 2, -1, -1):
        idx = jnp.where(vals[k] == v, k, idx)
    return idx


def _route_t(lt):
    row = [lt[i:i + 1, :] for i in range(N_GROUPS + N_EXPERTS)]
    grp = row[:N_GROUPS]
    gmax = functools.reduce(jnp.maximum, grp)
    g_star = _first_of(grp, gmax)
    p_grp = 1.0 / functools.reduce(jnp.add, [jnp.exp(g - gmax) for g in grp])
    logit = []
    for k in range(EXPERTS_PER_GROUP):
        ek = row[ROUTE_E_OFF + (N_GROUPS - 1) * EXPERTS_PER_GROUP + k]
        for g in range(N_GROUPS - 2, -1, -1):
            ek = jnp.where(g_star == g, row[ROUTE_E_OFF + g * EXPERTS_PER_GROUP + k], ek)
        logit.append(ek)
    emax = functools.reduce(jnp.maximum, logit)
    eexp = [jnp.exp(e - emax) for e in logit]
    esum = functools.reduce(jnp.add, eexp)
    prob = [e / esum for e in eexp]
    v1 = functools.reduce(jnp.maximum, prob)
    i1 = _first_of(prob, v1)
    rest = [jnp.where(i1 == k, -1.0, prob[k]) for k in range(EXPERTS_PER_GROUP)]
    v2 = functools.reduce(jnp.maximum, rest)
    i2 = _first_of(rest, v2)
    den = v1 + v2
    w1 = p_grp * (v1 / den)
    w2 = p_grp * (v2 / den)
    base = ROUTE_E_OFF + g_star * EXPERTS_PER_GROUP
    r_idx = lax.broadcasted_iota(jnp.int32, lt.shape, 0)
    return (jnp.where(r_idx == base + i1, w1, 0.0) + jnp.where(r_idx == base + i2, w2, 0.0)
            + jnp.where(r_idx == ROUTE_G_LANE, g_star.astype(F32), 0.0))


def _outproj_kernel(on_ref, of_ref, x_ref, bn_ref, bf_ref, wn_ref, wf_ref, gffn_ref, wrt_ref, brt_ref,
                    h_ref, u_ref, comb_ref):
    mn = _rms(on_ref[...], bn_ref[...]).astype(BF16)
    mf = _rms(of_ref[...], bf_ref[...]).astype(BF16)
    h = x_ref[...] + (_dot(mn, wn_ref[...]) + _dot(mf, wf_ref[...]))
    h_ref[...] = h
    u = _rms(h, gffn_ref[...]).astype(BF16)
    u_ref[...] = u
    lt = _dot_nt(wrt_ref[...], u) + brt_ref[...]
    comb_t = _route_t(lt[0:ROUTE_ROWS])
    comb_t = jnp.concatenate([comb_t, jnp.zeros((LANES - ROUTE_ROWS, comb_t.shape[1]), F32)], axis=0)
    comb_ref[...] = comb_t.T


def _outproj(o_nsa, o_fox, x2, beta_n, beta_f, w_n, w_f, g_ffn, w_r, b_r, tm):
    T = x2.shape[0]
    full = lambda shape: pl.BlockSpec(shape, lambda i: (0,) * len(shape))
    row = lambda w: pl.BlockSpec((tm, w), lambda i: (i, 0))
    return pl.pallas_call(
        _outproj_kernel,
        grid=(T // tm,),
        in_specs=[row(NSA_WIDTH), row(FOX_WIDTH), row(D_MODEL), full((1, NSA_WIDTH)), full((1, FOX_WIDTH)),
                  full((NSA_WIDTH, D_MODEL)), full((FOX_WIDTH, D_MODEL)), full((1, D_MODEL)),
                  full((LANES, D_MODEL)), full((LANES, 1))],
        out_specs=[row(D_MODEL), row(D_MODEL), row(LANES)],
        out_shape=[jax.ShapeDtypeStruct((T, D_MODEL), F32),
                   jax.ShapeDtypeStruct((T, D_MODEL), BF16),
                   jax.ShapeDtypeStruct((T, LANES), F32)],
        compiler_params=_cparams(("parallel",)),
        name="outproj_router",
    )(o_nsa, o_fox, x2, beta_n, beta_f, w_n, w_f, g_ffn, w_r, b_r)


MOE_TS = 1024
MOE_RB = 64
MOE_TSP = MOE_TS + N_GROUPS * MOE_RB
MOE_META = 8
MOE_TPS = 2


def _moe_sort_kernel(u_ref, comb_ref, tri_ref, us_ref, cs_ref, pos_ref, meta_ref):
    comb = comb_ref[...]
    lane = lax.broadcasted_iota(jnp.int32, comb.shape, 1)
    grp = comb[:, ROUTE_G_LANE:ROUTE_G_LANE + 1].astype(jnp.int32)
    onehot = jnp.where((lane == grp) & (lane < N_GROUPS), 1.0, 0.0)
    rank = _dot(tri_ref[...], onehot.astype(BF16))
    counts = rank[MOE_TS - 1:MOE_TS, :]
    nblk = jnp.floor((counts + (MOE_RB - 1)) * (1.0 / MOE_RB))
    lane1 = lane[0:1, :]
    blk0 = jnp.zeros_like(nblk)
    run = jnp.zeros((1, 1), F32)
    for g in range(1, N_GROUPS):
        run = run + nblk[:, g - 1:g]
        blk0 = blk0 + jnp.where(lane1 == g, run, 0.0)
    val = onehot * (blk0 * MOE_RB + rank - 1.0)
    pos_col = jnp.sum(val, axis=-1, keepdims=True)
    pos_ref[...] = pos_col.astype(jnp.int32)
    ones = jnp.ones((SUBLANES, LANES), BF16)
    hi, mid, lo = _split3(val)
    pos_row = (_dot_nt(ones, hi) + _dot_nt(ones, mid) + _dot_nt(ones, lo))[0:1, :].astype(jnp.int32)
    r_idx = lax.broadcasted_iota(jnp.int32, (MOE_TSP, MOE_TS), 0)
    perm = jnp.where(r_idx == pos_row, 1.0, 0.0).astype(BF16)
    c_hi = comb.astype(BF16)
    c_lo = (comb - c_hi.astype(F32)).astype(BF16)
    moved = _dot(perm, jnp.concatenate([u_ref[...], c_hi, c_lo], axis=1))
    us_ref[...] = moved[:, 0:D_MODEL].astype(us_ref.dtype)
    cs_ref[...] = moved[:, D_MODEL:D_MODEL + LANES] + moved[:, D_MODEL + LANES:D_MODEL + 2 * LANES]
    meta = blk0 + pltpu.roll(nblk, N_GROUPS, 1)
    meta_ref[...] = jnp.where(lane1 < MOE_META, meta, 0.0).astype(jnp.int32)


def _moe_sort(u, comb, tri):
    T = u.shape[0]
    nt = T // MOE_TS
    return pl.pallas_call(
        _moe_sort_kernel,
        grid=(nt,),
        in_specs=[pl.BlockSpec((MOE_TS, D_MODEL), lambda i: (i, 0)),
                  pl.BlockSpec((MOE_TS, LANES), lambda i: (i, 0)),
                  pl.BlockSpec((MOE_TS, MOE_TS), lambda i: (0, 0))],
        out_specs=[pl.BlockSpec((MOE_TSP, D_MODEL), lambda i: (i, 0)),
                   pl.BlockSpec((MOE_TSP, LANES), lambda i: (i, 0)),
                   pl.BlockSpec((MOE_TS, 1), lambda i: (i, 0)),
                   pl.BlockSpec((None, 1, LANES), lambda i: (i, 0, 0))],
        out_shape=[jax.ShapeDtypeStruct((nt * MOE_TSP, D_MODEL), BF16),
                   jax.ShapeDtypeStruct((nt * MOE_TSP, LANES), F32),
                   jax.ShapeDtypeStruct((T, 1), jnp.int32),
                   jax.ShapeDtypeStruct((nt, 1, LANES), jnp.int32)],
        compiler_params=_cparams(("parallel",)),
        name="moe_sort",
    )(u, comb, tri)


def _moe_kernel(meta_ref, us_ref, cs_ref, wgu_ref, wd_ref, o_ref, acc_ref):
    i = pl.program_id(0)
    e = pl.program_id(1)
    g = e // EXPERTS_PER_GROUP

    @pl.when(e == 0)
    def _():
        acc_ref[...] = jnp.zeros_like(acc_ref)

    for tile in range(MOE_TPS):
        base = (i * MOE_TPS + tile) * MOE_META
        first = meta_ref[base + g] + tile * (MOE_TSP // MOE_RB)
        count = meta_ref[base + N_GROUPS + g]

        def blocks(j0, n, first=first):
            halves = max(1, n // 2)
            size = n * MOE_RB // halves
            rows = [pl.multiple_of((first + j0) * MOE_RB + k * size, MOE_RB) for k in range(halves)]
            gus = [_dot(us_ref[pl.ds(r0, size), :], wgu_ref[...]) for r0 in rows]
            for r0, gu in zip(rows, gus):
                cs = cs_ref[pl.ds(r0, size), :]
                lane = lax.broadcasted_iota(jnp.int32, cs.shape, 1)
                ce = jnp.sum(jnp.where(lane == ROUTE_E_OFF + e, cs, 0.0), axis=-1, keepdims=True)
                gt = gu[:, 0:D_EXPERT]
                hid = gt * (1.0 / (1.0 + jnp.exp(-gt))) * gu[:, D_EXPERT:2 * D_EXPERT]
                acc_ref[pl.ds(r0, size), :] += _dot((ce * hid).astype(BF16), wd_ref[...])

        done = 0
        for width in (4, 2, 1):
            def run(j, carry, blocks=blocks, done=done, width=width):
                blocks(done + width * j, width)
                return carry

            trips = (count - done) // width
            lax.fori_loop(0, trips, run, 0)
            done = done + trips * width

    @pl.when(e == N_EXPERTS - 1)
    def _():
        o_ref[...] = acc_ref[...].astype(o_ref.dtype)


def _moe(meta, us, cs, wgu, wd):
    rows = MOE_TPS * MOE_TSP
    grid_spec = pltpu.PrefetchScalarGridSpec(
        num_scalar_prefetch=1,
        grid=(us.shape[0] // rows, N_EXPERTS),
        in_specs=[pl.BlockSpec((rows, D_MODEL), lambda i, e, m: (i, 0)),
                  pl.BlockSpec((rows, LANES), lambda i, e, m: (i, 0)),
                  pl.BlockSpec((None, D_MODEL, 2 * D_EXPERT), lambda i, e, m: (e, 0, 0)),
                  pl.BlockSpec((None, D_EXPERT, D_MODEL), lambda i, e, m: (e, 0, 0))],
        out_specs=pl.BlockSpec((rows, D_MODEL), lambda i, e, m: (i, 0)),
        scratch_shapes=[pltpu.VMEM((rows, D_MODEL), F32)])
    return pl.pallas_call(
        _moe_kernel,
        grid_spec=grid_spec,
        out_shape=jax.ShapeDtypeStruct(us.shape, BF16),
        compiler_params=_cparams(("parallel", "arbitrary")),
        name="moe_experts",
    )(meta, us, cs, wgu, wd)


def _ple_kernel(ys_ref, pos_ref, h_ref, p_ref, gple_ref, wg_ref, bg_ref, wp_ref, gfin_ref, o_ref):
    r_idx = lax.broadcasted_iota(jnp.int32, (pos_ref.shape[0], MOE_TSP), 1)
    unperm = jnp.where(r_idx == pos_ref[...], 1.0, 0.0).astype(BF16)
    h = h_ref[...] + _dot(unperm, ys_ref[...])
    v = _rms(h, gple_ref[...]).astype(BF16)
    z = _dot(v, wg_ref[...]) + bg_ref[...]
    gate = 1.0 / (1.0 + jnp.exp(-z))
    proj = _dot(p_ref[...].astype(BF16), wp_ref[...])
    o_ref[...] = _rms(h + gate * proj, gfin_ref[...])


def _ple(ys, pos, h1, p2, g_ple, w_g, b_g, w_p, g_final, tm):
    T = h1.shape[0]
    per = MOE_TS // tm
    full = lambda shape: pl.BlockSpec(shape, lambda i: (0,) * len(shape))
    row = lambda w: pl.BlockSpec((tm, w), lambda i: (i, 0))
    return pl.pallas_call(
        _ple_kernel,
        grid=(T // tm,),
        in_specs=[pl.BlockSpec((MOE_TSP, D_MODEL), lambda i: (i // per, 0)), row(1), row(D_MODEL), row(PLE_DIM),
                  full((1, D_MODEL)), full((D_MODEL, D_MODEL)),
                  full((1, D_MODEL)), full((PLE_DIM, D_MODEL)), full((1, D_MODEL))],
        out_specs=row(D_MODEL),
        out_shape=jax.ShapeDtypeStruct((T, D_MODEL), F32),
        compiler_params=_cparams(("parallel",)),
        name="ple_final",
    )(ys, pos, h1, p2, g_ple, w_g, b_g, w_p, g_final)


def _rope_lane_tables():
    half = ROPE_HALF
    inv_freq = jnp.power(jnp.float32(ROPE_THETA), -jnp.arange(half, dtype=jnp.float32) / half)
    j = np.arange(LANES) % HEAD_DIM
    freq = jnp.where(jnp.asarray(j < ROPE_DIM), inv_freq[jnp.asarray(j % half)], 0.0).reshape(1, LANES)
    mlo = jnp.asarray(np.where(j < half, -1.0, 0.0), F32).reshape(1, LANES)
    mhi = jnp.asarray(np.where((j >= half) & (j < ROPE_DIM), 1.0, 0.0), F32).reshape(1, LANES)
    return inv_freq, freq.astype(F32), mlo, mhi


def _block_diag2(w):
    z = jnp.zeros_like(w)
    return jnp.concatenate([jnp.concatenate([w, z], axis=-1), jnp.concatenate([z, w], axis=-1)], axis=-2)


def _aux_route_table():
    r = np.zeros((AUX_PIECES, LANES, FOX_NPAIR * LANES), np.float32)
    for head in range(FOX_HEADS):
        for k in range(AUX_PIECES):
            r[k, SMALL_F_OFF + head, (head // FOX_PAIR) * LANES + (head % FOX_PAIR) * AUX_PIECES + k] = 1.0
    return jnp.asarray(r, BF16)


def _layer(h3, p3, positions, prm, g_final):
    B, S, _ = h3.shape
    T = B * S
    tm = 512
    tm_proj = 1024
    x2 = h3.reshape(T, D_MODEL)
    inv_freq, freq, mlo, mhi = _rope_lane_tables()

    w_in = prm['w_in']
    offs = np.cumsum([0, NSA_WIDTH, KV_WIDTH, KV_WIDTH, KV_WIDTH, KV_WIDTH, KV_WIDTH, KV_WIDTH,
                      N_GATE, FOX_WIDTH, FOX_WIDTH, FOX_WIDTH, FOX_HEADS])
    seg = lambda k: w_in[:, offs[k]:offs[k + 1]]
    wq = seg(0).reshape(D_MODEL, NSA_GROUPS, NSA_HPG, HEAD_DIM).transpose(0, 2, 1, 3).reshape(D_MODEL, NSA_WIDTH)
    pad = jnp.zeros((D_MODEL, LANES - N_GATE - FOX_HEADS), w_in.dtype)
    w_all = jnp.concatenate([wq, seg(1), seg(2), seg(3), seg(5), seg(8), seg(9), seg(7), seg(11), pad],
                            axis=1).astype(BF16)
    wt_v = jnp.concatenate([seg(4), seg(6), seg(10)], axis=1).T.astype(BF16)
    bias_small = jnp.concatenate([prm['b_nsa_gate'], prm['b_forget'],
                                  jnp.zeros((LANES - N_GATE - FOX_HEADS,), F32)]).reshape(1, LANES)

    qa, cmp_tok, knsa, vnsa_t, fqk, vfox_t, small = _inproj(
        x2, positions.reshape(T // tm_proj, 1, tm_proj), prm['g_mix'].reshape(1, D_MODEL), w_all, wt_v,
        bias_small, inv_freq.reshape(ROPE_HALF, 1), mlo, mhi, tm_proj, B, S)

    tri = jnp.asarray(np.tril(np.ones((CUM_SUB, CUM_SUB), np.float32)), BF16)
    small3 = small.reshape(B, S, LANES)
    aux3 = _cumsum(small3, tri, _aux_route_table())

    n_rows = S // CMP_STRIDE
    tile2 = lambda pe: jnp.concatenate([pe, pe], axis=-1)
    bd1 = lambda w: _block_diag2(w.reshape(CMP_BLOCK, HEAD_DIM, CMP_HIDDEN)).astype(BF16)
    pos_cmp = positions[:, CMP_BLOCK - 1::CMP_STRIDE]
    pos_cmp = jnp.pad(pos_cmp, ((0, 0), (0, n_rows - pos_cmp.shape[1]))).reshape(B, n_rows, 1)
    kc, vct = _compress(cmp_tok.reshape(B, S, 2 * KV_WIDTH), tile2(prm['pe_cmp_k']), tile2(prm['pe_cmp_v']),
                        bd1(prm['w_cmp_k1']), bd1(prm['w_cmp_v1']),
                        _block_diag2(prm['w_cmp_k2']).astype(BF16), _block_diag2(prm['w_cmp_v2']).T.astype(BF16),
                        pos_cmp, freq, mlo, mhi)

    n_cmp = (S - CMP_BLOCK) // CMP_STRIDE + 1
    n_blk = S // SEL_BLOCK
    cs = np.arange(n_rows)[:, None] * CMP_STRIDE
    ss = np.arange(n_blk)[None, :] * SEL_BLOCK
    ov = np.clip(np.minimum(cs + CMP_BLOCK, ss + SEL_BLOCK) - np.maximum(cs, ss), 0, None) / CMP_BLOCK
    ov[n_cmp:] = 0.0
    aggt = jnp.asarray(ov.T, BF16)
    et = jnp.asarray((np.arange(S)[:, None] // SEL_BLOCK) == np.arange(LANES)[None, :], BF16)
    eye = jnp.asarray(np.eye(NSA_GQ, dtype=np.float32), BF16)
    gexp = small3[:, :, :N_GATE].reshape(B, S // NSA_QB, NSA_QB, NSA_GROUPS, NSA_HPG, 3)
    gexp = gexp.transpose(0, 1, 5, 3, 4, 2).reshape(B, S // NSA_QB, 3, NSA_ROWS)
    o_nsa = _nsa_attention(qa.reshape(B, S, NSA_WIDTH), gexp, kc, vct, knsa.reshape(B, S, 2 * KV_WIDTH),
                           vnsa_t, aggt, et, eye)

    o_fox = _fox_attention(fqk.reshape(B, S, 2 * FOX_WIDTH), aux3, vfox_t)

    perm = lambda a: a.reshape(NSA_GROUPS, NSA_HPG, HEAD_DIM, -1).transpose(1, 0, 2, 3).reshape(NSA_WIDTH, -1)
    beta_n = perm(prm['beta_nsa'].reshape(NSA_WIDTH, 1)).reshape(1, NSA_WIDTH)
    w_out = prm['w_out']
    w_n = perm(w_out[:NSA_WIDTH]).astype(BF16)
    w_f = w_out[NSA_WIDTH:].astype(BF16)
    w_r = jnp.concatenate([prm['w_group'], prm['w_router'],
                           jnp.zeros((D_MODEL, LANES - N_GROUPS - N_EXPERTS), F32)], axis=1).T.astype(BF16)
    b_r = jnp.concatenate([prm['b_group'], prm['b_router'],
                           jnp.zeros((LANES - N_GROUPS - N_EXPERTS,), F32)]).reshape(LANES, 1)
    h1, u, comb = _outproj(o_nsa.reshape(T, NSA_WIDTH), o_fox.reshape(T, FOX_WIDTH), x2, beta_n,
                           prm['beta_fox'].reshape(1, FOX_WIDTH), w_n, w_f,
                           prm['g_ffn'].reshape(1, D_MODEL), w_r, b_r, tm_proj)

    wgu = jnp.concatenate([prm['w_gate_e'], prm['w_up_e']], axis=-1).astype(BF16)
    tri_s = jnp.asarray(np.tril(np.ones((MOE_TS, MOE_TS), np.float32)), BF16)
    us, cs, pos, meta = _moe_sort(u, comb, tri_s)
    ys = _moe(meta[:, 0, :MOE_META].reshape(-1), us, cs, wgu, prm['w_down_e'].astype(BF16))

    out = _ple(ys, pos, h1, p3.reshape(T, PLE_DIM), prm['g_ple'].reshape(1, D_MODEL),
               prm['w_ple_gate'].astype(BF16),
               prm['b_ple_gate'].reshape(1, D_MODEL), prm['w_ple_proj'].astype(BF16),
               g_final.reshape(1, D_MODEL), tm)
    return out.reshape(B, S, D_MODEL)


_PARAM_NAMES = ('g_mix', 'w_in', 'b_nsa_gate', 'b_forget', 'pe_cmp_k', 'w_cmp_k1', 'w_cmp_k2',
                'pe_cmp_v', 'w_cmp_v1', 'w_cmp_v2', 'beta_nsa', 'beta_fox', 'w_out', 'g_ffn',
                'w_group', 'b_group', 'w_router', 'b_router', 'w_gate_e', 'w_up_e', 'w_down_e',
                'g_ple', 'w_ple_gate', 'b_ple_gate', 'w_ple_proj')


def kernel(x, p, positions, g_mix, w_in, b_nsa_gate, b_forget, pe_cmp_k, w_cmp_k1, w_cmp_k2, pe_cmp_v,
           w_cmp_v1, w_cmp_v2, beta_nsa, beta_fox, w_out, g_ffn, w_group, b_group, w_router, b_router,
           w_gate_e, w_up_e, w_down_e, g_ple, w_ple_gate, b_ple_gate, w_ple_proj, g_final):
    stacked = (g_mix, w_in, b_nsa_gate, b_forget, pe_cmp_k, w_cmp_k1, w_cmp_k2, pe_cmp_v, w_cmp_v1,
               w_cmp_v2, beta_nsa, beta_fox, w_out, g_ffn, w_group, b_group, w_router, b_router,
               w_gate_e, w_up_e, w_down_e, g_ple, w_ple_gate, b_ple_gate, w_ple_proj)
    depth = w_in.shape[0]
    assert depth == 1, "the final norm is fused into the last layer's embedding kernel"
    prm = {n: a[0] for n, a in zip(_PARAM_NAMES, stacked)}
    return _layer(x, p[0], positions, prm, g_final)
```

```python
import functools
import math

import numpy as np
import jax
import jax.numpy as jnp
from jax import lax
from jax.experimental import pallas as pl
from jax.experimental.pallas import tpu as pltpu

D_MODEL = 1024
HEAD_DIM = 64
NSA_HEADS = 8
FOX_HEADS = 8
NSA_GROUPS = 2
NSA_HPG = 4
NSA_WIDTH = 512
FOX_WIDTH = 512
KV_WIDTH = 128
CMP_BLOCK = 32
CMP_STRIDE = 16
CMP_HIDDEN = 128
SEL_BLOCK = 64
N_SELECT = 16
WINDOW = 512
ROPE_THETA = 500000.0
ROPE_DIM = 16
ROPE_HALF = 8
N_GROUPS = 4
EXPERTS_PER_GROUP = 4
N_EXPERTS = 16
D_EXPERT = 512
PLE_DIM = 256
EPS = 1e-6
NEG_INF = -1e30
FORCE_BONUS = 1e4
LOG2E = math.log2(math.e)
Q_SCALE = 0.125 * LOG2E
NEG_BIG = -(2.0 ** 100)

LANES = 128
SUBLANES = 8
N_GATE = 3 * NSA_HEADS
SMALL_F_OFF = N_GATE
ROUTE_E_OFF = N_GROUPS
ROUTE_G_LANE = 0

VMEM_LIMIT = 56 * 1024 * 1024

F32 = jnp.float32
BF16 = jnp.bfloat16


def _cparams(sem):
    return pltpu.CompilerParams(dimension_semantics=sem, vmem_limit_bytes=VMEM_LIMIT)


def _dot(a, b):
    return jnp.dot(a, b, preferred_element_type=F32)


def _dot_nt(a, b):
    return lax.dot_general(a, b, (((1,), (1,)), ((), ())), preferred_element_type=F32)


def _split3(x):
    hi = x.astype(BF16)
    r1 = x - hi.astype(F32)
    mid = r1.astype(BF16)
    lo = (r1 - mid.astype(F32)).astype(BF16)
    return hi, mid, lo


def _split3_dot_lhs(e, x):
    hi, mid, lo = _split3(x)
    return _dot(e, hi) + _dot(e, mid) + _dot(e, lo)


def _rms(x, g):
    return x * lax.rsqrt(jnp.mean(x * x, axis=-1, keepdims=True) + EPS) * g


def _rope(r, cos, sin_lo, sin_hi):
    return (r * cos + pltpu.roll(r, LANES - ROPE_HALF, 1) * sin_lo
            + pltpu.roll(r, ROPE_HALF, 1) * sin_hi)


def _rope_tables(pos_col, freq, mlo, mhi):
    ang = pos_col.astype(F32) * freq
    cos = jnp.cos(ang)
    sin = jnp.sin(ang)
    return cos, sin * mlo, sin * mhi


def _rope_tables_from_row(pos_row, freq_col, mlo, mhi):
    m = pos_row.shape[1]
    ang_t = freq_col * pos_row.astype(F32)
    pad = jnp.zeros((LANES - ROPE_HALF, m), F32)
    lane = lax.broadcasted_iota(jnp.int32, (m, LANES), 1)
    rot = (lane & (HEAD_DIM - 1)) < ROPE_DIM

    def spread(t):
        x = jnp.concatenate([t, pad], axis=0).T
        x = x + pltpu.roll(x, ROPE_HALF, 1)
        return x + pltpu.roll(x, HEAD_DIM, 1)

    cos = jnp.where(rot, spread(jnp.cos(ang_t)), 1.0)
    sin = spread(jnp.sin(ang_t))
    return cos, sin * mlo, sin * mhi


def _inproj_kernel(x_ref, pos_ref, g_ref, w_ref, wt_ref, bias_ref, freq_ref, mlo_ref, mhi_ref,
                   qa_ref, cmp_ref, knsa_ref, vnsa_ref, fqk_ref, vfox_ref, small_ref):
    x = x_ref[...]
    hb = _rms(x, g_ref[...]).astype(BF16)
    cos, s_lo, s_hi = _rope_tables_from_row(pos_ref[...], freq_ref[...], mlo_ref[...], mhi_ref[...])
    off = 0
    r = _dot(hb, w_ref[:, off:off + NSA_WIDTH])
    for c in range(NSA_WIDTH // LANES):
        rc = _rope(r[:, c * LANES:(c + 1) * LANES], cos, s_lo, s_hi) * Q_SCALE
        qa_ref[:, c * LANES:(c + 1) * LANES] = rc.astype(qa_ref.dtype)
    off += NSA_WIDTH
    cmp_ref[...] = _dot(hb, w_ref[:, off:off + 2 * KV_WIDTH])
    off += 2 * KV_WIDTH
    r = _dot(hb, w_ref[:, off:off + 2 * KV_WIDTH])
    for c in range(2):
        rc = _rope(r[:, c * LANES:(c + 1) * LANES], cos, s_lo, s_hi)
        knsa_ref[:, c * LANES:(c + 1) * LANES] = rc.astype(knsa_ref.dtype)
    off += 2 * KV_WIDTH
    r = _dot(hb, w_ref[:, off:off + FOX_WIDTH])
    fqk_ref[:, 0:FOX_WIDTH] = (r * Q_SCALE).astype(fqk_ref.dtype)
    off += FOX_WIDTH
    r = _dot(hb, w_ref[:, off:off + FOX_WIDTH])
    fqk_ref[:, FOX_WIDTH:2 * FOX_WIDTH] = r.astype(fqk_ref.dtype)
    off += FOX_WIDTH
    z = _dot(hb, w_ref[:, off:off + LANES]) + bias_ref[...]
    lane = lax.broadcasted_iota(jnp.int32, z.shape, 1)
    sig = 1.0 / (1.0 + jnp.exp(-z))
    logsig = jnp.minimum(z, 0.0) - jnp.log(1.0 + jnp.exp(-jnp.abs(z)))
    small_ref[...] = jnp.where(lane < SMALL_F_OFF, sig, logsig)
    vt = _dot_nt(wt_ref[...], hb)
    vnsa_ref[...] = vt[0:2 * KV_WIDTH].astype(vnsa_ref.dtype)
    vfox_ref[...] = vt[2 * KV_WIDTH:2 * KV_WIDTH + FOX_WIDTH].astype(vfox_ref.dtype)


def _inproj(x2, pos2, g_mix, w_all, wt_v, bias_small, freq, mlo, mhi, tm, B, S):
    T = x2.shape[0]
    nt = S // tm
    full = lambda shape: pl.BlockSpec(shape, lambda i: (0,) * len(shape))
    row = lambda w: pl.BlockSpec((tm, w), lambda i: (i, 0))
    tr = lambda w: pl.BlockSpec((None, w, tm), lambda i: (i // nt, 0, i % nt))
    return pl.pallas_call(
        _inproj_kernel,
        grid=(T // tm,),
        in_specs=[row(D_MODEL), pl.BlockSpec((None, 1, tm), lambda i: (i, 0, 0)), full((1, D_MODEL)),
                  full(w_all.shape), full(wt_v.shape),
                  full((1, LANES)), full((ROPE_HALF, 1)), full((1, LANES)), full((1, LANES))],
        out_specs=[row(NSA_WIDTH), row(2 * KV_WIDTH), row(2 * KV_WIDTH), tr(2 * KV_WIDTH),
                   row(2 * FOX_WIDTH), tr(FOX_WIDTH), row(LANES)],
        out_shape=[jax.ShapeDtypeStruct((T, NSA_WIDTH), BF16),
                   jax.ShapeDtypeStruct((T, 2 * KV_WIDTH), F32),
                   jax.ShapeDtypeStruct((T, 2 * KV_WIDTH), BF16),
                   jax.ShapeDtypeStruct((B, 2 * KV_WIDTH, S), BF16),
                   jax.ShapeDtypeStruct((T, 2 * FOX_WIDTH), BF16),
                   jax.ShapeDtypeStruct((B, FOX_WIDTH, S), BF16),
                   jax.ShapeDtypeStruct((T, LANES), F32)],
        compiler_params=_cparams(("parallel",)),
        name="inproj",
    )(x2, pos2, g_mix, w_all, wt_v, bias_small, freq, mlo, mhi)


CUM_CHUNK = 1024
CUM_SUB = 128
FOX_PAIR = 2
FOX_NPAIR = FOX_HEADS // FOX_PAIR
AUX_PIECES = 3


def _cumsum_kernel(x_ref, tri_ref, route_ref, o_ref, carry_ref):
    @pl.when(pl.program_id(1) == 0)
    def _():
        carry_ref[...] = jnp.zeros_like(carry_ref)

    carry = carry_ref[...]
    pieces = []
    for k in range(CUM_CHUNK // CUM_SUB):
        ck = _split3_dot_lhs(tri_ref[...], x_ref[k * CUM_SUB:(k + 1) * CUM_SUB, :]) + carry
        carry = ck[CUM_SUB - 1:CUM_SUB, :]
        pieces.append(ck)
    carry_ref[...] = carry
    c = jnp.concatenate(pieces, axis=0)
    hi, mid, lo = _split3(c * LOG2E)
    aux = _dot(hi, route_ref[0]) + _dot(mid, route_ref[1]) + _dot(lo, route_ref[2])
    o_ref[...] = aux.astype(o_ref.dtype)


def _cumsum(small3, tri, route):
    B, S, _ = small3.shape
    width = FOX_NPAIR * LANES
    return pl.pallas_call(
        _cumsum_kernel,
        grid=(B, S // CUM_CHUNK),
        in_specs=[pl.BlockSpec((None, CUM_CHUNK, LANES), lambda b, i: (b, i, 0)),
                  pl.BlockSpec((CUM_SUB, CUM_SUB), lambda b, i: (0, 0)),
                  pl.BlockSpec((AUX_PIECES, LANES, width), lambda b, i: (0, 0, 0))],
        out_specs=pl.BlockSpec((None, CUM_CHUNK, width), lambda b, i: (b, i, 0)),
        out_shape=jax.ShapeDtypeStruct((B, S, width), BF16),
        scratch_shapes=[pltpu.VMEM((1, LANES), F32)],
        compiler_params=_cparams(("parallel", "arbitrary")),
        name="forget_cumsum",
    )(small3, tri, route)


def _compress_kernel(tokk_ref, tokv_ref, pek_ref, pev_ref, bdk1_ref, bdv1_ref, bdk2_ref, bdv2t_ref,
                     pos_ref, freq_ref, mlo_ref, mhi_ref, kc_ref, vct_ref, *, n_rows):
    half = CMP_BLOCK // 2
    ak = jnp.zeros((n_rows, 2 * CMP_HIDDEN), F32)
    bk = jnp.zeros((n_rows, 2 * CMP_HIDDEN), F32)
    av = jnp.zeros((n_rows, 2 * CMP_HIDDEN), F32)
    bv = jnp.zeros((n_rows, 2 * CMP_HIDDEN), F32)
    for l in range(half):
        xk = tokk_ref[pl.ds(l, n_rows, stride=CMP_STRIDE), :]
        xv = tokv_ref[pl.ds(l, n_rows, stride=CMP_STRIDE), :]
        ak = ak + _dot((xk + pek_ref[l:l + 1, :]).astype(BF16), bdk1_ref[l])
        bk = bk + _dot((xk + pek_ref[half + l:half + l + 1, :]).astype(BF16), bdk1_ref[half + l])
        av = av + _dot((xv + pev_ref[l:l + 1, :]).astype(BF16), bdv1_ref[l])
        bv = bv + _dot((xv + pev_ref[half + l:half + l + 1, :]).astype(BF16), bdv1_ref[half + l])
    hk = ak + pltpu.roll(bk, n_rows - 1, 0)
    hv = av + pltpu.roll(bv, n_rows - 1, 0)
    hk = hk * (1.0 / (1.0 + jnp.exp(-hk)))
    hv = hv * (1.0 / (1.0 + jnp.exp(-hv)))
    kc = _dot(hk.astype(BF16), bdk2_ref[...])
    cos, s_lo, s_hi = _rope_tables(pos_ref[...], freq_ref[...], mlo_ref[...], mhi_ref[...])
    kc_ref[...] = _rope(kc, cos, s_lo, s_hi).astype(kc_ref.dtype)
    vct_ref[...] = _dot_nt(bdv2t_ref[...], hv.astype(BF16)).astype(vct_ref.dtype)


def _compress(cmp_tok3, pek, pev, bdk1, bdv1, bdk2, bdv2t, pos_cmp, freq, mlo, mhi):
    B, S, _ = cmp_tok3.shape
    n_rows = S // CMP_STRIDE
    full = lambda shape: pl.BlockSpec(shape, lambda b: (0,) * len(shape))
    return pl.pallas_call(
        functools.partial(_compress_kernel, n_rows=n_rows),
        grid=(B,),
        in_specs=[pl.BlockSpec((None, S, KV_WIDTH), lambda b: (b, 0, 0)),
                  pl.BlockSpec((None, S, KV_WIDTH), lambda b: (b, 0, 1)),
                  full((CMP_BLOCK, KV_WIDTH)), full((CMP_BLOCK, KV_WIDTH)),
                  full((CMP_BLOCK, KV_WIDTH, 2 * CMP_HIDDEN)), full((CMP_BLOCK, KV_WIDTH, 2 * CMP_HIDDEN)),
                  full((2 * CMP_HIDDEN, KV_WIDTH)), full((KV_WIDTH, 2 * CMP_HIDDEN)),
                  pl.BlockSpec((None, n_rows, 1), lambda b: (b, 0, 0)),
                  full((1, LANES)), full((1, LANES)), full((1, LANES))],
        out_specs=[pl.BlockSpec((None, n_rows, KV_WIDTH), lambda b: (b, 0, 0)),
                   pl.BlockSpec((None, KV_WIDTH, n_rows), lambda b: (b, 0, 0))],
        out_shape=[jax.ShapeDtypeStruct((B, n_rows, KV_WIDTH), BF16),
                   jax.ShapeDtypeStruct((B, KV_WIDTH, n_rows), BF16)],
        compiler_params=_cparams(("parallel",)),
        name="nsa_compress",
    )(cmp_tok3, cmp_tok3, pek, pev, bdk1, bdv1, bdk2, bdv2t, pos_cmp, freq, mlo, mhi)


NSA_QB = 128
NSA_BPQ = NSA_QB // SEL_BLOCK
assert NSA_BPQ in (1, 2) and NSA_QB % LANES == 0
NSA_KC = 512
NSA_GQ = NSA_GROUPS * NSA_QB
NSA_GH = NSA_HPG * NSA_QB
NSA_ROWS = NSA_GROUPS * NSA_GH
NSA_WSPAN = WINDOW + NSA_QB


def _where_tiles(mask, x, other):
    w = mask.shape[1]
    return jnp.concatenate([jnp.where(mask, x[:, t * w:(t + 1) * w], other)
                            for t in range(x.shape[1] // w)], axis=1)


def _masked_softmax_cols(s, mask):
    l = _where_tiles(mask, s, NEG_INF)
    m = jnp.max(l, axis=0, keepdims=True)
    e = jnp.exp2(l - m)
    return e, jnp.sum(e, axis=0, keepdims=True)


FLASH_NSUB = 2


def _flash_chunks(n_chunks, score_fn, vt_fn, init_fn, s_a, s_b, acc_ref, pv_fn=_dot):
    last = jnp.maximum(n_chunks - 1, 0)
    sub = s_a.shape[0] // FLASH_NSUB

    def stage(i_prod, s_prod, i_cons, s_cons, mx, m, l):
        if i_cons is not None:
            m_new = jnp.maximum(m, mx)
            alpha = jnp.exp2(m - m_new)
            l = alpha * l
        if i_prod is not None:
            i_prod = jnp.minimum(i_prod, last)
        mx_prod, pv = None, None
        for t in range(FLASH_NSUB):
            rows = pl.ds(t * sub, sub)
            if i_prod is not None:
                s = score_fn(i_prod, t)
                s_prod[rows, :] = s
                mx_t = jnp.max(s, axis=0, keepdims=True)
                mx_prod = mx_t if mx_prod is None else jnp.maximum(mx_prod, mx_t)
            if i_cons is not None:
                p = jnp.exp2(s_cons[rows, :] - m_new)
                l = l + jnp.sum(p, axis=0, keepdims=True)
                d = pv_fn(vt_fn(i_cons, t), p.astype(BF16))
                pv = d if pv is None else pv + d
        if i_cons is not None:
            acc_ref[...] = alpha * acc_ref[...] + pv
            m = m_new
        return mx_prod, m, l

    def pair(j, carry):
        mx_a, m, l = carry
        i = 2 * j
        mx_b, m, l = stage(i + 1, s_b, i, s_a, mx_a, m, l)
        mx_a, m, l = stage(i + 2, s_a, i + 1, s_b, mx_b, m, l)
        return mx_a, m, l

    n_pairs = jnp.maximum(n_chunks - 1, 0) // 2
    base = 2 * n_pairs
    left = n_chunks - base

    def tail_two(_, carry):
        mx_a, m, l = carry
        mx_b, m, l = stage(base + 1, s_b, base, s_a, mx_a, m, l)
        _, m, l = stage(None, None, base + 1, s_b, mx_b, m, l)
        return mx_a, m, l

    def tail_one(_, carry):
        mx_a, m, l = carry
        _, m, l = stage(None, None, base, s_a, mx_a, m, l)
        return mx_a, m, l

    mx_a, _, _ = stage(0, s_a, None, None, None, None, None)
    m, l = init_fn()
    carry = (mx_a, m, l)
    carry = lax.fori_loop(0, n_pairs, pair, carry)
    carry = lax.fori_loop(0, (left == 2).astype(jnp.int32), tail_two, carry)
    carry = lax.fori_loop(0, (left == 1).astype(jnp.int32), tail_one, carry)
    return carry[2]


def _nsa_kernel(q_ref, gate_ref, kc_ref, vct_ref, ks_ref, kw_ref, vst_ref, vwt_ref, aggt_ref, et_ref,
                eye_ref, o_ref, sa_scr, sb_scr, acc_scr, *, n_cmp, n_blk, n_sel):
    c = pl.program_id(1)
    s0 = c * NSA_QB
    lane = lax.broadcasted_iota(jnp.int32, (NSA_QB, LANES), 1)
    low = lane < HEAD_DIM
    qb = q_ref[...]
    zero = jnp.zeros((NSA_QB, LANES), qb.dtype)
    pieces = []
    for g in range(NSA_GROUPS):
        for h in range(NSA_HPG):
            qh = qb[:, h * LANES:(h + 1) * LANES]
            pieces.append(jnp.where(low, qh, zero) if g == 0 else jnp.where(low, zero, qh))
    qp = jnp.concatenate(pieces, axis=0)
    t_l = s0 + lax.broadcasted_iota(jnp.int32, (1, NSA_QB), 1)
    cur_l = (s0 + (lax.broadcasted_iota(jnp.int32, (1, NSA_GQ), 1) & (NSA_QB - 1))) // SEL_BLOCK

    def group_values(vt, p):
        return jnp.concatenate([_dot(vt[g * HEAD_DIM:(g + 1) * HEAD_DIM], p[:, g * NSA_GH:(g + 1) * NSA_GH])
                                for g in range(NSA_GROUPS)], axis=1)
    first_blk = c * NSA_BPQ
    last_blk = first_blk + NSA_BPQ - 1

    n_pad = kc_ref.shape[0]
    n_s = lax.broadcasted_iota(jnp.int32, (n_pad, 1), 0)
    mask_c = ((n_s * CMP_STRIDE + (CMP_BLOCK - 1)) <= t_l) & (n_s < n_cmp)
    start = pl.multiple_of(jnp.maximum(s0 - WINDOW, 0), LANES)
    rel = t_l - (start + lax.broadcasted_iota(jnp.int32, (NSA_WSPAN, 1), 0))
    mask_w = (rel >= 0) & (rel < WINDOW)
    s_c = _dot_nt(kc_ref[...], qp)
    s_w = _dot_nt(kw_ref[pl.ds(start, NSA_WSPAN), :], qp)
    e_c, sum_c = _masked_softmax_cols(s_c, mask_c)
    inv_c = _where_tiles(t_l >= CMP_BLOCK - 1, 1.0 / sum_c, 0.0)
    acc_c = group_values(vct_ref[...], e_c.astype(BF16))

    j_s = lax.broadcasted_iota(jnp.int32, (n_blk, NSA_GQ), 0)
    nv = n_blk // SUBLANES

    def ranked_mask():
        p_c = e_c * inv_c
        per_group = []
        for g in range(NSA_GROUPS):
            tiles = [p_c[:, (g * NSA_HPG + h) * NSA_QB:(g * NSA_HPG + h + 1) * NSA_QB] for h in range(NSA_HPG)]
            per_group.append(functools.reduce(jnp.add, tiles))
        psum = jnp.concatenate(per_group, axis=1)
        imp = _split3_dot_lhs(aggt_ref[...], psum)
        forced = (j_s == 0) | (j_s == cur_l) | (j_s == cur_l - 1)
        score = jnp.where(j_s > cur_l, -1.0, imp + jnp.where(forced, FORCE_BONUS, 0.0))
        sc = [score[SUBLANES * v:SUBLANES * (v + 1), :] for v in range(nv)]
        sub = lax.broadcasted_iota(jnp.int32, (SUBLANES, NSA_GQ), 0)

        def count_group(vi, rk):
            rk = list(rk)
            for ri in range(SUBLANES):
                row = sc[vi][ri:ri + 1, :]
                for v in range(nv):
                    if v < vi:
                        beats = jnp.where(row > sc[v], 1.0, 0.0)
                    elif v > vi:
                        beats = jnp.where(row >= sc[v], 1.0, 0.0)
                    else:
                        beats = jnp.where(sub > ri, jnp.where(row >= sc[v], 1.0, 0.0),
                                          jnp.where(row > sc[v], 1.0, 0.0))
                    rk[v] = rk[v] + beats
            return tuple(rk)

        rk = tuple(jnp.zeros((SUBLANES, NSA_GQ), F32) for _ in range(nv))
        for vi in range(nv):
            rk = lax.cond(vi * SUBLANES <= last_blk, functools.partial(count_group, vi), lambda r: r, rk)
        rank = jnp.concatenate(rk, axis=0)
        return jnp.where((rank < n_sel) & (j_s < first_blk), 0.0, NEG_BIG)

    def all_mask():
        return jnp.where(j_s < first_blk, 0.0, NEG_BIG)

    neg_t = lax.cond(last_blk < n_sel, all_mask, ranked_mask)

    a0 = pl.multiple_of(s0, LANES)
    s_d = _dot_nt(ks_ref[pl.ds(a0, NSA_QB), :], qp)
    neg_t = jnp.concatenate([neg_t, jnp.zeros((LANES - n_blk, NSA_GQ), F32)], axis=0).astype(BF16)
    q_aux = _dot_nt(eye_ref[...], neg_t).astype(BF16)
    aux_rows = [q_aux[g * NSA_QB:(g + 1) * NSA_QB] for g in range(NSA_GROUPS) for _ in range(NSA_HPG)]
    qx = jnp.concatenate([qp, jnp.concatenate(aux_rows, axis=0)], axis=1)

    sub_keys = NSA_KC // FLASH_NSUB

    def sel_scores(i, t):
        k0 = pl.multiple_of(i * NSA_KC + t * sub_keys, sub_keys)
        kx = jnp.concatenate([ks_ref[pl.ds(k0, sub_keys), :], et_ref[pl.ds(k0, sub_keys), :]], axis=1)
        return _dot_nt(kx, qx)

    def sel_values(i, t):
        return vst_ref[:, pl.ds(pl.multiple_of(i * NSA_KC + t * sub_keys, sub_keys), sub_keys)]

    window = {}

    def after_first_scores():
        kpos = a0 + lax.broadcasted_iota(jnp.int32, (NSA_QB, 1), 0)
        l_d = _where_tiles(kpos <= t_l, s_d, NEG_INF)
        m_d = jnp.max(l_d, axis=0, keepdims=True)
        p_d = jnp.exp2(l_d - m_d)
        acc_scr[...] = group_values(vst_ref[:, pl.ds(a0, NSA_QB)], p_d.astype(BF16))
        e_w, window['l'] = _masked_softmax_cols(s_w, mask_w)
        window['acc'] = group_values(vwt_ref[:, pl.ds(start, NSA_WSPAN)], e_w.astype(BF16))
        return m_d, jnp.sum(p_d, axis=0, keepdims=True)

    n_prev = (s0 + NSA_KC - 1) // NSA_KC
    l_s = _flash_chunks(n_prev, sel_scores, sel_values, after_first_scores, sa_scr, sb_scr, acc_scr,
                        pv_fn=group_values)

    gates = gate_ref[...]
    o_t = ((gates[0:1, :] * inv_c) * acc_c + (gates[1:2, :] / l_s) * acc_scr[...]
           + (gates[2:3, :] / window['l']) * window['acc'])
    for h in range(NSA_HPG):
        tiles = [o_t[:, (g * NSA_HPG + h) * NSA_QB:(g * NSA_HPG + h + 1) * NSA_QB].T for g in range(NSA_GROUPS)]
        o_ref[:, h * LANES:(h + 1) * LANES] = jnp.concatenate(tiles, axis=1)


def _nsa_attention(qa3, gexp, kc, vct, knsa3, vnsa_t, aggt, et, eye):
    B, S, _ = qa3.shape
    n_pad = kc.shape[1]
    n_cmp = (S - CMP_BLOCK) // CMP_STRIDE + 1
    n_blk = S // SEL_BLOCK
    n_sel = min(N_SELECT, n_blk)
    const = lambda shape: pl.BlockSpec(shape, lambda b, c: (0,) * len(shape))
    return pl.pallas_call(
        functools.partial(_nsa_kernel, n_cmp=n_cmp, n_blk=n_blk, n_sel=n_sel),
        grid=(B, S // NSA_QB),
        in_specs=[pl.BlockSpec((None, NSA_QB, NSA_WIDTH), lambda b, c: (b, c, 0)),
                  pl.BlockSpec((None, None, 3, NSA_ROWS), lambda b, c: (b, c, 0, 0)),
                  pl.BlockSpec((None, n_pad, KV_WIDTH), lambda b, c: (b, 0, 0)),
                  pl.BlockSpec((None, KV_WIDTH, n_pad), lambda b, c: (b, 0, 0)),
                  pl.BlockSpec((None, S, KV_WIDTH), lambda b, c: (b, 0, 0)),
                  pl.BlockSpec((None, S, KV_WIDTH), lambda b, c: (b, 0, 1)),
                  pl.BlockSpec((None, KV_WIDTH, S), lambda b, c: (b, 0, 0)),
                  pl.BlockSpec((None, KV_WIDTH, S), lambda b, c: (b, 1, 0)),
                  const((n_blk, n_pad)), const((S, LANES)), const((NSA_GQ, NSA_GQ))],
        out_specs=pl.BlockSpec((None, NSA_QB, NSA_WIDTH), lambda b, c: (b, c, 0)),
        out_shape=jax.ShapeDtypeStruct((B, S, NSA_WIDTH), F32),
        scratch_shapes=[pltpu.VMEM((NSA_KC, NSA_ROWS), F32), pltpu.VMEM((NSA_KC, NSA_ROWS), F32),
                        pltpu.VMEM((HEAD_DIM, NSA_ROWS), F32)],
        compiler_params=_cparams(("parallel", "arbitrary")),
        name="nsa_attention",
    )(qa3, gexp, kc, vct, knsa3, knsa3, vnsa_t, vnsa_t, aggt, et, eye)


FOX_TQ = 512
FOX_KC = FOX_TQ
FOX_ROWS = FOX_PAIR * FOX_TQ


def _fox_kernel(q_ref, k_ref, aux_ref, vt_ref, o_ref, sa_scr, sb_scr, acc_scr):
    qi = pl.program_id(2)
    s0 = qi * FOX_TQ
    lane = lax.broadcasted_iota(jnp.int32, (FOX_TQ, LANES), 1)
    low = lane < HEAD_DIM
    q2 = q_ref[...]
    zero = jnp.zeros_like(q2)
    aux0 = jnp.where(lane < AUX_PIECES, -1.0, 0.0).astype(q2.dtype)
    aux1 = jnp.where((lane >= AUX_PIECES) & (lane < 2 * AUX_PIECES), -1.0, 0.0).astype(q2.dtype)
    qx = jnp.concatenate([jnp.concatenate([jnp.where(low, q2, zero), aux0], axis=1),
                          jnp.concatenate([jnp.where(low, zero, q2), aux1], axis=1)], axis=0)
    t_l = s0 + lax.broadcasted_iota(jnp.int32, (1, FOX_TQ), 1)

    sub_keys = FOX_KC // FLASH_NSUB

    def scores(i, t, n=sub_keys):
        k0 = pl.multiple_of(i * FOX_KC + t * sub_keys, sub_keys)
        kx = jnp.concatenate([k_ref[pl.ds(k0, n), :], aux_ref[pl.ds(k0, n), :]], axis=1)
        return _dot_nt(kx, qx)

    def values(i, t, n=sub_keys):
        return vt_ref[:, pl.ds(pl.multiple_of(i * FOX_KC + t * sub_keys, sub_keys), n)]

    def diagonal():
        kpos = s0 + lax.broadcasted_iota(jnp.int32, (FOX_KC, 1), 0)
        s_d = _where_tiles(kpos <= t_l, scores(qi, 0, FOX_KC), NEG_INF)
        m_d = jnp.max(s_d, axis=0, keepdims=True)
        p_d = jnp.exp2(s_d - m_d)
        acc_scr[...] = head_values(values(qi, 0, FOX_KC), p_d.astype(BF16))
        return m_d, jnp.sum(p_d, axis=0, keepdims=True)

    def head_values(vt, p):
        return jnp.concatenate([_dot(vt[h * HEAD_DIM:(h + 1) * HEAD_DIM], p[:, h * FOX_TQ:(h + 1) * FOX_TQ])
                                for h in range(FOX_PAIR)], axis=1)

    l_f = _flash_chunks(qi, scores, values, diagonal, sa_scr, sb_scr, acc_scr, pv_fn=head_values)
    o_t = acc_scr[...] / l_f
    o_ref[...] = jnp.concatenate([o_t[:, h * FOX_TQ:(h + 1) * FOX_TQ].T for h in range(FOX_PAIR)], axis=1)


def _fox_attention(fqk3, aux3, vfox_t):
    B, S, _ = fqk3.shape
    return pl.pallas_call(
        _fox_kernel,
        grid=(B, FOX_NPAIR, S // FOX_TQ),
        in_specs=[pl.BlockSpec((None, FOX_TQ, LANES), lambda b, h, i: (b, i, h)),
                  pl.BlockSpec((None, S, LANES), lambda b, h, i: (b, 0, FOX_NPAIR + h)),
                  pl.BlockSpec((None, S, LANES), lambda b, h, i: (b, 0, h)),
                  pl.BlockSpec((None, LANES, S), lambda b, h, i: (b, h, 0))],
        out_specs=pl.BlockSpec((None, FOX_TQ, LANES), lambda b, h, i: (b, i, h)),
        out_shape=jax.ShapeDtypeStruct((B, S, FOX_WIDTH), F32),
        scratch_shapes=[pltpu.VMEM((FOX_KC, FOX_ROWS), F32), pltpu.VMEM((FOX_KC, FOX_ROWS), F32),
                        pltpu.VMEM((HEAD_DIM, FOX_ROWS), F32)],
        compiler_params=_cparams(("parallel", "parallel", "arbitrary")),
        name="fox_attention",
    )(fqk3, fqk3, aux3, vfox_t)


ROUTE_ROWS = 32


def _first_of(vals, v):
    idx = jnp.full(v.shape, len(vals) - 1, jnp.int32)
    for k in range(len(vals) - 2, -1, -1):
        idx = jnp.where(vals[k] == v, k, idx)
    return idx


def _route_t(lt):
    row = [lt[i:i + 1, :] for i in range(N_GROUPS + N_EXPERTS)]
    grp = row[:N_GROUPS]
    gmax = functools.reduce(jnp.maximum, grp)
    g_star = _first_of(grp, gmax)
    p_grp = 1.0 / functools.reduce(jnp.add, [jnp.exp(g - gmax) for g in grp])
    logit = []
    for k in range(EXPERTS_PER_GROUP):
        ek = row[ROUTE_E_OFF + (N_GROUPS - 1) * EXPERTS_PER_GROUP + k]
        for g in range(N_GROUPS - 2, -1, -1):
            ek = jnp.where(g_star == g, row[ROUTE_E_OFF + g * EXPERTS_PER_GROUP + k], ek)
        logit.append(ek)
    emax = functools.reduce(jnp.maximum, logit)
    eexp = [jnp.exp(e - emax) for e in logit]
    esum = functools.reduce(jnp.add, eexp)
    prob = [e / esum for e in eexp]
    v1 = functools.reduce(jnp.maximum, prob)
    i1 = _first_of(prob, v1)
    rest = [jnp.where(i1 == k, -1.0, prob[k]) for k in range(EXPERTS_PER_GROUP)]
    v2 = functools.reduce(jnp.maximum, rest)
    i2 = _first_of(rest, v2)
    den = v1 + v2
    w1 = p_grp * (v1 / den)
    w2 = p_grp * (v2 / den)
    base = ROUTE_E_OFF + g_star * EXPERTS_PER_GROUP
    r_idx = lax.broadcasted_iota(jnp.int32, lt.shape, 0)
    return (jnp.where(r_idx == base + i1, w1, 0.0) + jnp.where(r_idx == base + i2, w2, 0.0)
            + jnp.where(r_idx == ROUTE_G_LANE, g_star.astype(F32), 0.0))


def _outproj_kernel(on_ref, of_ref, x_ref, bn_ref, bf_ref, wn_ref, wf_ref, gffn_ref, wrt_ref, brt_ref,
                    h_ref, u_ref, comb_ref):
    mn = _rms(on_ref[...], bn_ref[...]).astype(BF16)
    mf = _rms(of_ref[...], bf_ref[...]).astype(BF16)
    h = x_ref[...] + (_dot(mn, wn_ref[...]) + _dot(mf, wf_ref[...]))
    h_ref[...] = h
    u = _rms(h, gffn_ref[...]).astype(BF16)
    u_ref[...] = u
    lt = _dot_nt(wrt_ref[...], u) + brt_ref[...]
    comb_t = _route_t(lt[0:ROUTE_ROWS])
    comb_t = jnp.concatenate([comb_t, jnp.zeros((LANES - ROUTE_ROWS, comb_t.shape[1]), F32)], axis=0)
    comb_ref[...] = comb_t.T


def _outproj(o_nsa, o_fox, x2, beta_n, beta_f, w_n, w_f, g_ffn, w_r, b_r, tm):
    T = x2.shape[0]
    full = lambda shape: pl.BlockSpec(shape, lambda i: (0,) * len(shape))
    row = lambda w: pl.BlockSpec((tm, w), lambda i: (i, 0))
    return pl.pallas_call(
        _outproj_kernel,
        grid=(T // tm,),
        in_specs=[row(NSA_WIDTH), row(FOX_WIDTH), row(D_MODEL), full((1, NSA_WIDTH)), full((1, FOX_WIDTH)),
                  full((NSA_WIDTH, D_MODEL)), full((FOX_WIDTH, D_MODEL)), full((1, D_MODEL)),
                  full((LANES, D_MODEL)), full((LANES, 1))],
        out_specs=[row(D_MODEL), row(D_MODEL), row(LANES)],
        out_shape=[jax.ShapeDtypeStruct((T, D_MODEL), F32),
                   jax.ShapeDtypeStruct((T, D_MODEL), BF16),
                   jax.ShapeDtypeStruct((T, LANES), F32)],
        compiler_params=_cparams(("parallel",)),
        name="outproj_router",
    )(o_nsa, o_fox, x2, beta_n, beta_f, w_n, w_f, g_ffn, w_r, b_r)


MOE_TS = 1024
MOE_RB = 128
MOE_TSP = MOE_TS + N_GROUPS * MOE_RB
MOE_META = 8
MOE_TPS = 2


def _moe_sort_kernel(u_ref, comb_ref, tri_ref, us_ref, cs_ref, pos_ref, meta_ref):
    comb = comb_ref[...]
    lane = lax.broadcasted_iota(jnp.int32, comb.shape, 1)
    grp = comb[:, ROUTE_G_LANE:ROUTE_G_LANE + 1].astype(jnp.int32)
    onehot = jnp.where((lane == grp) & (lane < N_GROUPS), 1.0, 0.0)
    rank = _dot(tri_ref[...], onehot.astype(BF16))
    counts = rank[MOE_TS - 1:MOE_TS, :]
    nblk = jnp.floor((counts + (MOE_RB - 1)) * (1.0 / MOE_RB))
    lane1 = lane[0:1, :]
    blk0 = jnp.zeros_like(nblk)
    run = jnp.zeros((1, 1), F32)
    for g in range(1, N_GROUPS):
        run = run + nblk[:, g - 1:g]
        blk0 = blk0 + jnp.where(lane1 == g, run, 0.0)
    val = onehot * (blk0 * MOE_RB + rank - 1.0)
    pos_col = jnp.sum(val, axis=-1, keepdims=True)
    pos_ref[...] = pos_col.astype(jnp.int32)
    ones = jnp.ones((SUBLANES, LANES), BF16)
    hi, mid, lo = _split3(val)
    pos_row = (_dot_nt(ones, hi) + _dot_nt(ones, mid) + _dot_nt(ones, lo))[0:1, :].astype(jnp.int32)
    r_idx = lax.broadcasted_iota(jnp.int32, (MOE_TSP, MOE_TS), 0)
    perm = jnp.where(r_idx == pos_row, 1.0, 0.0).astype(BF16)
    c_hi = comb.astype(BF16)
    c_lo = (comb - c_hi.astype(F32)).astype(BF16)
    moved = _dot(perm, jnp.concatenate([u_ref[...], c_hi, c_lo], axis=1))
    us_ref[...] = moved[:, 0:D_MODEL].astype(us_ref.dtype)
    cs_ref[...] = moved[:, D_MODEL:D_MODEL + LANES] + moved[:, D_MODEL + LANES:D_MODEL + 2 * LANES]
    meta = blk0 + pltpu.roll(nblk, N_GROUPS, 1)
    meta_ref[...] = jnp.where(lane1 < MOE_META, meta, 0.0).astype(jnp.int32)


def _moe_sort(u, comb, tri):
    T = u.shape[0]
    nt = T // MOE_TS
    return pl.pallas_call(
        _moe_sort_kernel,
        grid=(nt,),
        in_specs=[pl.BlockSpec((MOE_TS, D_MODEL), lambda i: (i, 0)),
                  pl.BlockSpec((MOE_TS, LANES), lambda i: (i, 0)),
                  pl.BlockSpec((MOE_TS, MOE_TS), lambda i: (0, 0))],
        out_specs=[pl.BlockSpec((MOE_TSP, D_MODEL), lambda i: (i, 0)),
                   pl.BlockSpec((MOE_TSP, LANES), lambda i: (i, 0)),
                   pl.BlockSpec((MOE_TS, 1), lambda i: (i, 0)),
                   pl.BlockSpec((None, 1, LANES), lambda i: (i, 0, 0))],
        out_shape=[jax.ShapeDtypeStruct((nt * MOE_TSP, D_MODEL), BF16),
                   jax.ShapeDtypeStruct((nt * MOE_TSP, LANES), F32),
                   jax.ShapeDtypeStruct((T, 1), jnp.int32),
                   jax.ShapeDtypeStruct((nt, 1, LANES), jnp.int32)],
        compiler_params=_cparams(("parallel",)),
        name="moe_sort",
    )(u, comb, tri)


def _moe_kernel(meta_ref, us_ref, cs_ref, wgu_ref, wd_ref, o_ref, acc_ref):
    i = pl.program_id(0)
    e = pl.program_id(1)
    g = e // EXPERTS_PER_GROUP

    @pl.when(e == 0)
    def _():
        acc_ref[...] = jnp.zeros_like(acc_ref)

    for tile in range(MOE_TPS):
        base = (i * MOE_TPS + tile) * MOE_META
        first = meta_ref[base + g] + tile * (MOE_TSP // MOE_RB)
        count = meta_ref[base + N_GROUPS + g]

        def blocks(j0, n, first=first):
            rows = [pl.multiple_of((first + j0 + k) * MOE_RB, MOE_RB) for k in range(n)]
            gus = [_dot(us_ref[pl.ds(r0, MOE_RB), :], wgu_ref[...]) for r0 in rows]
            for r0, gu in zip(rows, gus):
                cs = cs_ref[pl.ds(r0, MOE_RB), :]
                lane = lax.broadcasted_iota(jnp.int32, cs.shape, 1)
                ce = jnp.sum(jnp.where(lane == ROUTE_E_OFF + e, cs, 0.0), axis=-1, keepdims=True)
                gt = gu[:, 0:D_EXPERT]
                hid = gt * (1.0 / (1.0 + jnp.exp(-gt))) * gu[:, D_EXPERT:2 * D_EXPERT]
                acc_ref[pl.ds(r0, MOE_RB), :] += _dot((ce * hid).astype(BF16), wd_ref[...])

        def pair(j, carry, blocks=blocks):
            blocks(2 * j, 2)
            return carry

        def single(_, carry, blocks=blocks, count=count):
            blocks(count - 1, 1)
            return carry

        lax.fori_loop(0, count // 2, pair, 0)
        lax.fori_loop(0, count & 1, single, 0)

    @pl.when(e == N_EXPERTS - 1)
    def _():
        o_ref[...] = acc_ref[...].astype(o_ref.dtype)


def _moe(meta, us, cs, wgu, wd):
    rows = MOE_TPS * MOE_TSP
    grid_spec = pltpu.PrefetchScalarGridSpec(
        num_scalar_prefetch=1,
        grid=(us.shape[0] // rows, N_EXPERTS),
        in_specs=[pl.BlockSpec((rows, D_MODEL), lambda i, e, m: (i, 0)),
                  pl.BlockSpec((rows, LANES), lambda i, e, m: (i, 0)),
                  pl.BlockSpec((None, D_MODEL, 2 * D_EXPERT), lambda i, e, m: (e, 0, 0)),
                  pl.BlockSpec((None, D_EXPERT, D_MODEL), lambda i, e, m: (e, 0, 0))],
        out_specs=pl.BlockSpec((rows, D_MODEL), lambda i, e, m: (i, 0)),
        scratch_shapes=[pltpu.VMEM((rows, D_MODEL), F32)])
    return pl.pallas_call(
        _moe_kernel,
        grid_spec=grid_spec,
        out_shape=jax.ShapeDtypeStruct(us.shape, BF16),
        compiler_params=_cparams(("parallel", "arbitrary")),
        name="moe_experts",
    )(meta, us, cs, wgu, wd)


def _ple_kernel(ys_ref, pos_ref, h_ref, p_ref, gple_ref, wg_ref, bg_ref, wp_ref, gfin_ref, o_ref):
    r_idx = lax.broadcasted_iota(jnp.int32, (pos_ref.shape[0], MOE_TSP), 1)
    unperm = jnp.where(r_idx == pos_ref[...], 1.0, 0.0).astype(BF16)
    h = h_ref[...] + _dot(unperm, ys_ref[...])
    v = _rms(h, gple_ref[...]).astype(BF16)
    z = _dot(v, wg_ref[...]) + bg_ref[...]
    gate = 1.0 / (1.0 + jnp.exp(-z))
    proj = _dot(p_ref[...].astype(BF16), wp_ref[...])
    o_ref[...] = _rms(h + gate * proj, gfin_ref[...])


def _ple(ys, pos, h1, p2, g_ple, w_g, b_g, w_p, g_final, tm):
    T = h1.shape[0]
    per = MOE_TS // tm
    full = lambda shape: pl.BlockSpec(shape, lambda i: (0,) * len(shape))
    row = lambda w: pl.BlockSpec((tm, w), lambda i: (i, 0))
    return pl.pallas_call(
        _ple_kernel,
        grid=(T // tm,),
        in_specs=[pl.BlockSpec((MOE_TSP, D_MODEL), lambda i: (i // per, 0)), row(1), row(D_MODEL), row(PLE_DIM),
                  full((1, D_MODEL)), full((D_MODEL, D_MODEL)),
                  full((1, D_MODEL)), full((PLE_DIM, D_MODEL)), full((1, D_MODEL))],
        out_specs=row(D_MODEL),
        out_shape=jax.ShapeDtypeStruct((T, D_MODEL), F32),
        compiler_params=_cparams(("parallel",)),
        name="ple_final",
    )(ys, pos, h1, p2, g_ple, w_g, b_g, w_p, g_final)


def _rope_lane_tables():
    half = ROPE_HALF
    inv_freq = jnp.power(jnp.float32(ROPE_THETA), -jnp.arange(half, dtype=jnp.float32) / half)
    j = np.arange(LANES) % HEAD_DIM
    freq = jnp.where(jnp.asarray(j < ROPE_DIM), inv_freq[jnp.asarray(j % half)], 0.0).reshape(1, LANES)
    mlo = jnp.asarray(np.where(j < half, -1.0, 0.0), F32).reshape(1, LANES)
    mhi = jnp.asarray(np.where((j >= half) & (j < ROPE_DIM), 1.0, 0.0), F32).reshape(1, LANES)
    return inv_freq, freq.astype(F32), mlo, mhi


def _block_diag2(w):
    z = jnp.zeros_like(w)
    return jnp.concatenate([jnp.concatenate([w, z], axis=-1), jnp.concatenate([z, w], axis=-1)], axis=-2)


def _aux_route_table():
    r = np.zeros((AUX_PIECES, LANES, FOX_NPAIR * LANES), np.float32)
    for head in range(FOX_HEADS):
        for k in range(AUX_PIECES):
            r[k, SMALL_F_OFF + head, (head // FOX_PAIR) * LANES + (head % FOX_PAIR) * AUX_PIECES + k] = 1.0
    return jnp.asarray(r, BF16)


def _layer(h3, p3, positions, prm, g_final):
    B, S, _ = h3.shape
    T = B * S
    tm = 512
    tm_proj = 1024
    x2 = h3.reshape(T, D_MODEL)
    inv_freq, freq, mlo, mhi = _rope_lane_tables()

    w_in = prm['w_in']
    offs = np.cumsum([0, NSA_WIDTH, KV_WIDTH, KV_WIDTH, KV_WIDTH, KV_WIDTH, KV_WIDTH, KV_WIDTH,
                      N_GATE, FOX_WIDTH, FOX_WIDTH, FOX_WIDTH, FOX_HEADS])
    seg = lambda k: w_in[:, offs[k]:offs[k + 1]]
    wq = seg(0).reshape(D_MODEL, NSA_GROUPS, NSA_HPG, HEAD_DIM).transpose(0, 2, 1, 3).reshape(D_MODEL, NSA_WIDTH)
    pad = jnp.zeros((D_MODEL, LANES - N_GATE - FOX_HEADS), w_in.dtype)
    w_all = jnp.concatenate([wq, seg(1), seg(2), seg(3), seg(5), seg(8), seg(9), seg(7), seg(11), pad],
                            axis=1).astype(BF16)
    wt_v = jnp.concatenate([seg(4), seg(6), seg(10)], axis=1).T.astype(BF16)
    bias_small = jnp.concatenate([prm['b_nsa_gate'], prm['b_forget'],
                                  jnp.zeros((LANES - N_GATE - FOX_HEADS,), F32)]).reshape(1, LANES)

    qa, cmp_tok, knsa, vnsa_t, fqk, vfox_t, small = _inproj(
        x2, positions.reshape(T // tm_proj, 1, tm_proj), prm['g_mix'].reshape(1, D_MODEL), w_all, wt_v,
        bias_small, inv_freq.reshape(ROPE_HALF, 1), mlo, mhi, tm_proj, B, S)

    tri = jnp.asarray(np.tril(np.ones((CUM_SUB, CUM_SUB), np.float32)), BF16)
    small3 = small.reshape(B, S, LANES)
    aux3 = _cumsum(small3, tri, _aux_route_table())

    n_rows = S // CMP_STRIDE
    tile2 = lambda pe: jnp.concatenate([pe, pe], axis=-1)
    bd1 = lambda w: _block_diag2(w.reshape(CMP_BLOCK, HEAD_DIM, CMP_HIDDEN)).astype(BF16)
    pos_cmp = positions[:, CMP_BLOCK - 1::CMP_STRIDE]
    pos_cmp = jnp.pad(pos_cmp, ((0, 0), (0, n_rows - pos_cmp.shape[1]))).reshape(B, n_rows, 1)
    kc, vct = _compress(cmp_tok.reshape(B, S, 2 * KV_WIDTH), tile2(prm['pe_cmp_k']), tile2(prm['pe_cmp_v']),
                        bd1(prm['w_cmp_k1']), bd1(prm['w_cmp_v1']),
                        _block_diag2(prm['w_cmp_k2']).astype(BF16), _block_diag2(prm['w_cmp_v2']).T.astype(BF16),
                        pos_cmp, freq, mlo, mhi)

    n_cmp = (S - CMP_BLOCK) // CMP_STRIDE + 1
    n_blk = S // SEL_BLOCK
    cs = np.arange(n_rows)[:, None] * CMP_STRIDE
    ss = np.arange(n_blk)[None, :] * SEL_BLOCK
    ov = np.clip(np.minimum(cs + CMP_BLOCK, ss + SEL_BLOCK) - np.maximum(cs, ss), 0, None) / CMP_BLOCK
    ov[n_cmp:] = 0.0
    aggt = jnp.asarray(ov.T, BF16)
    et = jnp.asarray((np.arange(S)[:, None] // SEL_BLOCK) == np.arange(LANES)[None, :], BF16)
    eye = jnp.asarray(np.eye(NSA_GQ, dtype=np.float32), BF16)
    gexp = small3[:, :, :N_GATE].reshape(B, S // NSA_QB, NSA_QB, NSA_GROUPS, NSA_HPG, 3)
    gexp = gexp.transpose(0, 1, 5, 3, 4, 2).reshape(B, S // NSA_QB, 3, NSA_ROWS)
    o_nsa = _nsa_attention(qa.reshape(B, S, NSA_WIDTH), gexp, kc, vct, knsa.reshape(B, S, 2 * KV_WIDTH),
                           vnsa_t, aggt, et, eye)

    o_fox = _fox_attention(fqk.reshape(B, S, 2 * FOX_WIDTH), aux3, vfox_t)

    perm = lambda a: a.reshape(NSA_GROUPS, NSA_HPG, HEAD_DIM, -1).transpose(1, 0, 2, 3).reshape(NSA_WIDTH, -1)
    beta_n = perm(prm['beta_nsa'].reshape(NSA_WIDTH, 1)).reshape(1, NSA_WIDTH)
    w_out = prm['w_out']
    w_n = perm(w_out[:NSA_WIDTH]).astype(BF16)
    w_f = w_out[NSA_WIDTH:].astype(BF16)
    w_r = jnp.concatenate([prm['w_group'], prm['w_router'],
                           jnp.zeros((D_MODEL, LANES - N_GROUPS - N_EXPERTS), F32)], axis=1).T.astype(BF16)
    b_r = jnp.concatenate([prm['b_group'], prm['b_router'],
                           jnp.zeros((LANES - N_GROUPS - N_EXPERTS,), F32)]).reshape(LANES, 1)
    h1, u, comb = _outproj(o_nsa.reshape(T, NSA_WIDTH), o_fox.reshape(T, FOX_WIDTH), x2, beta_n,
                           prm['beta_fox'].reshape(1, FOX_WIDTH), w_n, w_f,
                           prm['g_ffn'].reshape(1, D_MODEL), w_r, b_r, tm_proj)

    wgu = jnp.concatenate([prm['w_gate_e'], prm['w_up_e']], axis=-1).astype(BF16)
    tri_s = jnp.asarray(np.tril(np.ones((MOE_TS, MOE_TS), np.float32)), BF16)
    us, cs, pos, meta = _moe_sort(u, comb, tri_s)
    ys = _moe(meta[:, 0, :MOE_META].reshape(-1), us, cs, wgu, prm['w_down_e'].astype(BF16))

    out = _ple(ys, pos, h1, p3.reshape(T, PLE_DIM), prm['g_ple'].reshape(1, D_MODEL),
               prm['w_ple_gate'].astype(BF16),
               prm['b_ple_gate'].reshape(1, D_MODEL), prm['w_ple_proj'].astype(BF16),
               g_final.reshape(1, D_MODEL), tm)
    return out.reshape(B, S, D_MODEL)


_PARAM_NAMES = ('g_mix', 'w_in', 'b_nsa_gate', 'b_forget', 'pe_cmp_k', 'w_cmp_k1', 'w_cmp_k2',
                'pe_cmp_v', 'w_cmp_v1', 'w_cmp_v2', 'beta_nsa', 'beta_fox', 'w_out', 'g_ffn',
                'w_group', 'b_group', 'w_router', 'b_router', 'w_gate_e', 'w_up_e', 'w_down_e',
                'g_ple', 'w_ple_gate', 'b_ple_gate', 'w_ple_proj')


def kernel(x, p, positions, g_mix, w_in, b_nsa_gate, b_forget, pe_cmp_k, w_cmp_k1, w_cmp_k2, pe_cmp_v,
           w_cmp_v1, w_cmp_v2, beta_nsa, beta_fox, w_out, g_ffn, w_group, b_group, w_router, b_router,
           w_gate_e, w_up_e, w_down_e, g_ple, w_ple_gate, b_ple_gate, w_ple_proj, g_final):
    stacked = (g_mix, w_in, b_nsa_gate, b_forget, pe_cmp_k, w_cmp_k1, w_cmp_k2, pe_cmp_v, w_cmp_v1,
               w_cmp_v2, beta_nsa, beta_fox, w_out, g_ffn, w_group, b_group, w_router, b_router,
               w_gate_e, w_up_e, w_down_e, g_ple, w_ple_gate, b_ple_gate, w_ple_proj)
    depth = w_in.shape[0]
    assert depth == 1, "the final norm is fused into the last layer's embedding kernel"
    prm = {n: a[0] for n, a in zip(_PARAM_NAMES, stacked)}
    return _layer(x, p[0], positions, prm, g_final)
```

```python
import functools
import math

import numpy as np
import jax
import jax.numpy as jnp
from jax import lax
from jax.experimental import pallas as pl
from jax.experimental.pallas import tpu as pltpu

D_MODEL = 1024
HEAD_DIM = 64
NSA_HEADS = 8
FOX_HEADS = 8
NSA_GROUPS = 2
NSA_HPG = 4
NSA_WIDTH = 512
FOX_WIDTH = 512
KV_WIDTH = 128
CMP_BLOCK = 32
CMP_STRIDE = 16
CMP_HIDDEN = 128
SEL_BLOCK = 64
N_SELECT = 16
WINDOW = 512
ROPE_THETA = 500000.0
ROPE_DIM = 16
ROPE_HALF = 8
N_GROUPS = 4
EXPERTS_PER_GROUP = 4
N_EXPERTS = 16
D_EXPERT = 512
PLE_DIM = 256
EPS = 1e-6
NEG_INF = -1e30
FORCE_BONUS = 1e4
LOG2E = math.log2(math.e)
Q_SCALE = 0.125 * LOG2E
NEG_BIG = -(2.0 ** 100)

LANES = 128
SUBLANES = 8
N_GATE = 3 * NSA_HEADS
SMALL_F_OFF = N_GATE
ROUTE_E_OFF = N_GROUPS
ROUTE_G_LANE = 0

VMEM_LIMIT = 56 * 1024 * 1024

F32 = jnp.float32
BF16 = jnp.bfloat16


def _cparams(sem):
    return pltpu.CompilerParams(dimension_semantics=sem, vmem_limit_bytes=VMEM_LIMIT)


def _dot(a, b):
    return jnp.dot(a, b, preferred_element_type=F32)


def _dot_nt(a, b):
    return lax.dot_general(a, b, (((1,), (1,)), ((), ())), preferred_element_type=F32)


def _split3(x):
    hi = x.astype(BF16)
    r1 = x - hi.astype(F32)
    mid = r1.astype(BF16)
    lo = (r1 - mid.astype(F32)).astype(BF16)
    return hi, mid, lo


def _split3_dot_lhs(e, x):
    hi, mid, lo = _split3(x)
    return _dot(e, hi) + _dot(e, mid) + _dot(e, lo)


def _rms(x, g):
    return x * lax.rsqrt(jnp.mean(x * x, axis=-1, keepdims=True) + EPS) * g


def _rope(r, cos, sin_lo, sin_hi):
    return (r * cos + pltpu.roll(r, LANES - ROPE_HALF, 1) * sin_lo
            + pltpu.roll(r, ROPE_HALF, 1) * sin_hi)


def _rope_tables(pos_col, freq, mlo, mhi):
    ang = pos_col.astype(F32) * freq
    cos = jnp.cos(ang)
    sin = jnp.sin(ang)
    return cos, sin * mlo, sin * mhi


def _rope_tables_from_row(pos_row, freq_col, mlo, mhi):
    m = pos_row.shape[1]
    ang_t = freq_col * pos_row.astype(F32)
    pad = jnp.zeros((LANES - ROPE_HALF, m), F32)
    lane = lax.broadcasted_iota(jnp.int32, (m, LANES), 1)
    rot = (lane & (HEAD_DIM - 1)) < ROPE_DIM

    def spread(t):
        x = jnp.concatenate([t, pad], axis=0).T
        x = x + pltpu.roll(x, ROPE_HALF, 1)
        return x + pltpu.roll(x, HEAD_DIM, 1)

    cos = jnp.where(rot, spread(jnp.cos(ang_t)), 1.0)
    sin = spread(jnp.sin(ang_t))
    return cos, sin * mlo, sin * mhi


def _inproj_kernel(x_ref, pos_ref, g_ref, w_ref, wt_ref, bias_ref, freq_ref, mlo_ref, mhi_ref,
                   qa_ref, cmp_ref, knsa_ref, vnsa_ref, fqk_ref, vfox_ref, small_ref):
    x = x_ref[...]
    hb = _rms(x, g_ref[...]).astype(BF16)
    cos, s_lo, s_hi = _rope_tables_from_row(pos_ref[...], freq_ref[...], mlo_ref[...], mhi_ref[...])
    off = 0
    r = _dot(hb, w_ref[:, off:off + NSA_WIDTH])
    for c in range(NSA_WIDTH // LANES):
        rc = _rope(r[:, c * LANES:(c + 1) * LANES], cos, s_lo, s_hi) * Q_SCALE
        qa_ref[:, c * LANES:(c + 1) * LANES] = rc.astype(qa_ref.dtype)
    off += NSA_WIDTH
    cmp_ref[...] = _dot(hb, w_ref[:, off:off + 2 * KV_WIDTH])
    off += 2 * KV_WIDTH
    r = _dot(hb, w_ref[:, off:off + 2 * KV_WIDTH])
    for c in range(2):
        rc = _rope(r[:, c * LANES:(c + 1) * LANES], cos, s_lo, s_hi)
        knsa_ref[:, c * LANES:(c + 1) * LANES] = rc.astype(knsa_ref.dtype)
    off += 2 * KV_WIDTH
    r = _dot(hb, w_ref[:, off:off + FOX_WIDTH])
    fqk_ref[:, 0:FOX_WIDTH] = (r * Q_SCALE).astype(fqk_ref.dtype)
    off += FOX_WIDTH
    r = _dot(hb, w_ref[:, off:off + FOX_WIDTH])
    fqk_ref[:, FOX_WIDTH:2 * FOX_WIDTH] = r.astype(fqk_ref.dtype)
    off += FOX_WIDTH
    z = _dot(hb, w_ref[:, off:off + LANES]) + bias_ref[...]
    lane = lax.broadcasted_iota(jnp.int32, z.shape, 1)
    sig = 1.0 / (1.0 + jnp.exp(-z))
    logsig = jnp.minimum(z, 0.0) - jnp.log(1.0 + jnp.exp(-jnp.abs(z)))
    small_ref[...] = jnp.where(lane < SMALL_F_OFF, sig, logsig)
    vt = _dot_nt(wt_ref[...], hb)
    vnsa_ref[...] = vt[0:2 * KV_WIDTH].astype(vnsa_ref.dtype)
    vfox_ref[...] = vt[2 * KV_WIDTH:2 * KV_WIDTH + FOX_WIDTH].astype(vfox_ref.dtype)


def _inproj(x2, pos2, g_mix, w_all, wt_v, bias_small, freq, mlo, mhi, tm, B, S):
    T = x2.shape[0]
    nt = S // tm
    full = lambda shape: pl.BlockSpec(shape, lambda i: (0,) * len(shape))
    row = lambda w: pl.BlockSpec((tm, w), lambda i: (i, 0))
    tr = lambda w: pl.BlockSpec((None, w, tm), lambda i: (i // nt, 0, i % nt))
    return pl.pallas_call(
        _inproj_kernel,
        grid=(T // tm,),
        in_specs=[row(D_MODEL), pl.BlockSpec((None, 1, tm), lambda i: (i, 0, 0)), full((1, D_MODEL)),
                  full(w_all.shape), full(wt_v.shape),
                  full((1, LANES)), full((ROPE_HALF, 1)), full((1, LANES)), full((1, LANES))],
        out_specs=[row(NSA_WIDTH), row(2 * KV_WIDTH), row(2 * KV_WIDTH), tr(2 * KV_WIDTH),
                   row(2 * FOX_WIDTH), tr(FOX_WIDTH), row(LANES)],
        out_shape=[jax.ShapeDtypeStruct((T, NSA_WIDTH), BF16),
                   jax.ShapeDtypeStruct((T, 2 * KV_WIDTH), F32),
                   jax.ShapeDtypeStruct((T, 2 * KV_WIDTH), BF16),
                   jax.ShapeDtypeStruct((B, 2 * KV_WIDTH, S), BF16),
                   jax.ShapeDtypeStruct((T, 2 * FOX_WIDTH), BF16),
                   jax.ShapeDtypeStruct((B, FOX_WIDTH, S), BF16),
                   jax.ShapeDtypeStruct((T, LANES), F32)],
        compiler_params=_cparams(("parallel",)),
        name="inproj",
    )(x2, pos2, g_mix, w_all, wt_v, bias_small, freq, mlo, mhi)


CUM_CHUNK = 1024
CUM_SUB = 128
FOX_PAIR = 2
FOX_NPAIR = FOX_HEADS // FOX_PAIR
AUX_PIECES = 3


def _cumsum_kernel(x_ref, tri_ref, route_ref, o_ref, carry_ref):
    @pl.when(pl.program_id(1) == 0)
    def _():
        carry_ref[...] = jnp.zeros_like(carry_ref)

    carry = carry_ref[...]
    pieces = []
    for k in range(CUM_CHUNK // CUM_SUB):
        ck = _split3_dot_lhs(tri_ref[...], x_ref[k * CUM_SUB:(k + 1) * CUM_SUB, :]) + carry
        carry = ck[CUM_SUB - 1:CUM_SUB, :]
        pieces.append(ck)
    carry_ref[...] = carry
    c = jnp.concatenate(pieces, axis=0)
    hi, mid, lo = _split3(c * LOG2E)
    aux = _dot(hi, route_ref[0]) + _dot(mid, route_ref[1]) + _dot(lo, route_ref[2])
    o_ref[...] = aux.astype(o_ref.dtype)


def _cumsum(small3, tri, route):
    B, S, _ = small3.shape
    width = FOX_NPAIR * LANES
    return pl.pallas_call(
        _cumsum_kernel,
        grid=(B, S // CUM_CHUNK),
        in_specs=[pl.BlockSpec((None, CUM_CHUNK, LANES), lambda b, i: (b, i, 0)),
                  pl.BlockSpec((CUM_SUB, CUM_SUB), lambda b, i: (0, 0)),
                  pl.BlockSpec((AUX_PIECES, LANES, width), lambda b, i: (0, 0, 0))],
        out_specs=pl.BlockSpec((None, CUM_CHUNK, width), lambda b, i: (b, i, 0)),
        out_shape=jax.ShapeDtypeStruct((B, S, width), BF16),
        scratch_shapes=[pltpu.VMEM((1, LANES), F32)],
        compiler_params=_cparams(("parallel", "arbitrary")),
        name="forget_cumsum",
    )(small3, tri, route)


def _compress_kernel(tokk_ref, tokv_ref, pek_ref, pev_ref, bdk1_ref, bdv1_ref, bdk2_ref, bdv2t_ref,
                     pos_ref, freq_ref, mlo_ref, mhi_ref, kc_ref, vct_ref, *, n_rows):
    half = CMP_BLOCK // 2
    ak = jnp.zeros((n_rows, 2 * CMP_HIDDEN), F32)
    bk = jnp.zeros((n_rows, 2 * CMP_HIDDEN), F32)
    av = jnp.zeros((n_rows, 2 * CMP_HIDDEN), F32)
    bv = jnp.zeros((n_rows, 2 * CMP_HIDDEN), F32)
    for l in range(half):
        xk = tokk_ref[pl.ds(l, n_rows, stride=CMP_STRIDE), :]
        xv = tokv_ref[pl.ds(l, n_rows, stride=CMP_STRIDE), :]
        ak = ak + _dot((xk + pek_ref[l:l + 1, :]).astype(BF16), bdk1_ref[l])
        bk = bk + _dot((xk + pek_ref[half + l:half + l + 1, :]).astype(BF16), bdk1_ref[half + l])
        av = av + _dot((xv + pev_ref[l:l + 1, :]).astype(BF16), bdv1_ref[l])
        bv = bv + _dot((xv + pev_ref[half + l:half + l + 1, :]).astype(BF16), bdv1_ref[half + l])
    hk = ak + pltpu.roll(bk, n_rows - 1, 0)
    hv = av + pltpu.roll(bv, n_rows - 1, 0)
    hk = hk * (1.0 / (1.0 + jnp.exp(-hk)))
    hv = hv * (1.0 / (1.0 + jnp.exp(-hv)))
    kc = _dot(hk.astype(BF16), bdk2_ref[...])
    cos, s_lo, s_hi = _rope_tables(pos_ref[...], freq_ref[...], mlo_ref[...], mhi_ref[...])
    kc_ref[...] = _rope(kc, cos, s_lo, s_hi).astype(kc_ref.dtype)
    vct_ref[...] = _dot_nt(bdv2t_ref[...], hv.astype(BF16)).astype(vct_ref.dtype)


def _compress(cmp_tok3, pek, pev, bdk1, bdv1, bdk2, bdv2t, pos_cmp, freq, mlo, mhi):
    B, S, _ = cmp_tok3.shape
    n_rows = S // CMP_STRIDE
    full = lambda shape: pl.BlockSpec(shape, lambda b: (0,) * len(shape))
    return pl.pallas_call(
        functools.partial(_compress_kernel, n_rows=n_rows),
        grid=(B,),
        in_specs=[pl.BlockSpec((None, S, KV_WIDTH), lambda b: (b, 0, 0)),
                  pl.BlockSpec((None, S, KV_WIDTH), lambda b: (b, 0, 1)),
                  full((CMP_BLOCK, KV_WIDTH)), full((CMP_BLOCK, KV_WIDTH)),
                  full((CMP_BLOCK, KV_WIDTH, 2 * CMP_HIDDEN)), full((CMP_BLOCK, KV_WIDTH, 2 * CMP_HIDDEN)),
                  full((2 * CMP_HIDDEN, KV_WIDTH)), full((KV_WIDTH, 2 * CMP_HIDDEN)),
                  pl.BlockSpec((None, n_rows, 1), lambda b: (b, 0, 0)),
                  full((1, LANES)), full((1, LANES)), full((1, LANES))],
        out_specs=[pl.BlockSpec((None, n_rows, KV_WIDTH), lambda b: (b, 0, 0)),
                   pl.BlockSpec((None, KV_WIDTH, n_rows), lambda b: (b, 0, 0))],
        out_shape=[jax.ShapeDtypeStruct((B, n_rows, KV_WIDTH), BF16),
                   jax.ShapeDtypeStruct((B, KV_WIDTH, n_rows), BF16)],
        compiler_params=_cparams(("parallel",)),
        name="nsa_compress",
    )(cmp_tok3, cmp_tok3, pek, pev, bdk1, bdv1, bdk2, bdv2t, pos_cmp, freq, mlo, mhi)


NSA_QB = 128
NSA_BPQ = NSA_QB // SEL_BLOCK
assert NSA_BPQ in (1, 2) and NSA_QB % LANES == 0
NSA_KC = 512
NSA_GQ = NSA_GROUPS * NSA_QB
NSA_GH = NSA_HPG * NSA_QB
NSA_ROWS = NSA_GROUPS * NSA_GH
NSA_WSPAN = WINDOW + NSA_QB


def _where_tiles(mask, x, other):
    w = mask.shape[1]
    return jnp.concatenate([jnp.where(mask, x[:, t * w:(t + 1) * w], other)
                            for t in range(x.shape[1] // w)], axis=1)


def _masked_softmax_cols(s, mask):
    l = _where_tiles(mask, s, NEG_INF)
    m = jnp.max(l, axis=0, keepdims=True)
    e = jnp.exp2(l - m)
    return e, jnp.sum(e, axis=0, keepdims=True)


FLASH_NSUB = 2


def _flash_chunks(n_chunks, score_fn, vt_fn, init_fn, s_a, s_b, acc_ref, pv_fn=_dot):
    sub = s_a.shape[0] // FLASH_NSUB

    def stage(i_prod, s_prod, i_cons, s_cons, mx, m, l):
        if i_cons is not None:
            m_new = jnp.maximum(m, mx)
            alpha = jnp.exp2(m - m_new)
            l = alpha * l
        mx_prod, pv = None, None
        for t in range(FLASH_NSUB):
            rows = pl.ds(t * sub, sub)
            if i_prod is not None:
                s = score_fn(i_prod, t)
                s_prod[rows, :] = s
                mx_t = jnp.max(s, axis=0, keepdims=True)
                mx_prod = mx_t if mx_prod is None else jnp.maximum(mx_prod, mx_t)
            if i_cons is not None:
                p = jnp.exp2(s_cons[rows, :] - m_new)
                l = l + jnp.sum(p, axis=0, keepdims=True)
                d = pv_fn(vt_fn(i_cons, t), p.astype(BF16))
                pv = d if pv is None else pv + d
        if i_cons is not None:
            acc_ref[...] = alpha * acc_ref[...] + pv
            m = m_new
        return mx_prod, m, l

    def pair(j, carry):
        mx_a, m, l = carry
        i = 2 * j
        mx_b, m, l = stage(i + 1, s_b, i, s_a, mx_a, m, l)
        mx_a, m, l = stage(i + 2, s_a, i + 1, s_b, mx_b, m, l)
        return mx_a, m, l

    n_pairs = jnp.maximum(n_chunks - 1, 0) // 2
    base = 2 * n_pairs
    left = n_chunks - base

    def tail_two(_, carry):
        mx_a, m, l = carry
        mx_b, m, l = stage(base + 1, s_b, base, s_a, mx_a, m, l)
        _, m, l = stage(None, None, base + 1, s_b, mx_b, m, l)
        return mx_a, m, l

    def tail_one(_, carry):
        mx_a, m, l = carry
        _, m, l = stage(None, None, base, s_a, mx_a, m, l)
        return mx_a, m, l

    mx_a, _, _ = stage(0, s_a, None, None, None, None, None)
    m, l = init_fn()
    carry = (mx_a, m, l)
    carry = lax.fori_loop(0, n_pairs, pair, carry)
    carry = lax.fori_loop(0, (left == 2).astype(jnp.int32), tail_two, carry)
    carry = lax.fori_loop(0, (left == 1).astype(jnp.int32), tail_one, carry)
    return carry[2]


def _nsa_kernel(q_ref, gate_ref, kc_ref, vct_ref, ks_ref, kw_ref, vst_ref, vwt_ref, aggt_ref, et_ref,
                eye_ref, o_ref, sa_scr, sb_scr, acc_scr, *, n_cmp, n_blk, n_sel):
    c = pl.program_id(1)
    s0 = c * NSA_QB
    lane = lax.broadcasted_iota(jnp.int32, (NSA_QB, LANES), 1)
    low = lane < HEAD_DIM
    qb = q_ref[...]
    zero = jnp.zeros((NSA_QB, LANES), qb.dtype)
    pieces = []
    for g in range(NSA_GROUPS):
        for h in range(NSA_HPG):
            qh = qb[:, h * LANES:(h + 1) * LANES]
            pieces.append(jnp.where(low, qh, zero) if g == 0 else jnp.where(low, zero, qh))
    qp = jnp.concatenate(pieces, axis=0)
    t_l = s0 + lax.broadcasted_iota(jnp.int32, (1, NSA_QB), 1)
    cur_l = (s0 + (lax.broadcasted_iota(jnp.int32, (1, NSA_GQ), 1) & (NSA_QB - 1))) // SEL_BLOCK

    def group_values(vt, p):
        return jnp.concatenate([_dot(vt[g * HEAD_DIM:(g + 1) * HEAD_DIM], p[:, g * NSA_GH:(g + 1) * NSA_GH])
                                for g in range(NSA_GROUPS)], axis=1)
    first_blk = c * NSA_BPQ
    last_blk = first_blk + NSA_BPQ - 1

    n_pad = kc_ref.shape[0]
    n_s = lax.broadcasted_iota(jnp.int32, (n_pad, 1), 0)
    mask_c = ((n_s * CMP_STRIDE + (CMP_BLOCK - 1)) <= t_l) & (n_s < n_cmp)
    start = pl.multiple_of(jnp.maximum(s0 - WINDOW, 0), LANES)
    rel = t_l - (start + lax.broadcasted_iota(jnp.int32, (NSA_WSPAN, 1), 0))
    mask_w = (rel >= 0) & (rel < WINDOW)
    s_c = _dot_nt(kc_ref[...], qp)
    s_w = _dot_nt(kw_ref[pl.ds(start, NSA_WSPAN), :], qp)
    e_c, sum_c = _masked_softmax_cols(s_c, mask_c)
    inv_c = _where_tiles(t_l >= CMP_BLOCK - 1, 1.0 / sum_c, 0.0)
    acc_c = group_values(vct_ref[...], e_c.astype(BF16))

    j_s = lax.broadcasted_iota(jnp.int32, (n_blk, NSA_GQ), 0)
    nv = n_blk // SUBLANES

    def ranked_mask():
        p_c = e_c * inv_c
        per_group = []
        for g in range(NSA_GROUPS):
            tiles = [p_c[:, (g * NSA_HPG + h) * NSA_QB:(g * NSA_HPG + h + 1) * NSA_QB] for h in range(NSA_HPG)]
            per_group.append(functools.reduce(jnp.add, tiles))
        psum = jnp.concatenate(per_group, axis=1)
        imp = _split3_dot_lhs(aggt_ref[...], psum)
        forced = (j_s == 0) | (j_s == cur_l) | (j_s == cur_l - 1)
        score = jnp.where(j_s > cur_l, -1.0, imp + jnp.where(forced, FORCE_BONUS, 0.0))
        sc = [score[SUBLANES * v:SUBLANES * (v + 1), :] for v in range(nv)]
        sub = lax.broadcasted_iota(jnp.int32, (SUBLANES, NSA_GQ), 0)

        def count_group(vi, rk):
            rk = list(rk)
            for ri in range(SUBLANES):
                row = sc[vi][ri:ri + 1, :]
                for v in range(nv):
                    if v < vi:
                        beats = jnp.where(row > sc[v], 1.0, 0.0)
                    elif v > vi:
                        beats = jnp.where(row >= sc[v], 1.0, 0.0)
                    else:
                        beats = jnp.where(sub > ri, jnp.where(row >= sc[v], 1.0, 0.0),
                                          jnp.where(row > sc[v], 1.0, 0.0))
                    rk[v] = rk[v] + beats
            return tuple(rk)

        rk = tuple(jnp.zeros((SUBLANES, NSA_GQ), F32) for _ in range(nv))
        for vi in range(nv):
            rk = lax.cond(vi * SUBLANES <= last_blk, functools.partial(count_group, vi), lambda r: r, rk)
        rank = jnp.concatenate(rk, axis=0)
        return jnp.where((rank < n_sel) & (j_s < first_blk), 0.0, NEG_BIG)

    def all_mask():
        return jnp.where(j_s < first_blk, 0.0, NEG_BIG)

    neg_t = lax.cond(last_blk < n_sel, all_mask, ranked_mask)

    a0 = pl.multiple_of(s0, LANES)
    s_d = _dot_nt(ks_ref[pl.ds(a0, NSA_QB), :], qp)
    neg_t = jnp.concatenate([neg_t, jnp.zeros((LANES - n_blk, NSA_GQ), F32)], axis=0).astype(BF16)
    q_aux = _dot_nt(eye_ref[...], neg_t).astype(BF16)
    aux_rows = [q_aux[g * NSA_QB:(g + 1) * NSA_QB] for g in range(NSA_GROUPS) for _ in range(NSA_HPG)]
    qx = jnp.concatenate([qp, jnp.concatenate(aux_rows, axis=0)], axis=1)

    sub_keys = NSA_KC // FLASH_NSUB

    def sel_scores(i, t):
        k0 = pl.multiple_of(i * NSA_KC + t * sub_keys, sub_keys)
        kx = jnp.concatenate([ks_ref[pl.ds(k0, sub_keys), :], et_ref[pl.ds(k0, sub_keys), :]], axis=1)
        return _dot_nt(kx, qx)

    def sel_values(i, t):
        return vst_ref[:, pl.ds(pl.multiple_of(i * NSA_KC + t * sub_keys, sub_keys), sub_keys)]

    window = {}

    def after_first_scores():
        kpos = a0 + lax.broadcasted_iota(jnp.int32, (NSA_QB, 1), 0)
        l_d = _where_tiles(kpos <= t_l, s_d, NEG_INF)
        m_d = jnp.max(l_d, axis=0, keepdims=True)
        p_d = jnp.exp2(l_d - m_d)
        acc_scr[...] = group_values(vst_ref[:, pl.ds(a0, NSA_QB)], p_d.astype(BF16))
        e_w, window['l'] = _masked_softmax_cols(s_w, mask_w)
        window['acc'] = group_values(vwt_ref[:, pl.ds(start, NSA_WSPAN)], e_w.astype(BF16))
        return m_d, jnp.sum(p_d, axis=0, keepdims=True)

    n_prev = (s0 + NSA_KC - 1) // NSA_KC
    l_s = _flash_chunks(n_prev, sel_scores, sel_values, after_first_scores, sa_scr, sb_scr, acc_scr,
                        pv_fn=group_values)

    gates = gate_ref[...]
    o_t = ((gates[0:1, :] * inv_c) * acc_c + (gates[1:2, :] / l_s) * acc_scr[...]
           + (gates[2:3, :] / window['l']) * window['acc'])
    for h in range(NSA_HPG):
        tiles = [o_t[:, (g * NSA_HPG + h) * NSA_QB:(g * NSA_HPG + h + 1) * NSA_QB].T for g in range(NSA_GROUPS)]
        o_ref[:, h * LANES:(h + 1) * LANES] = jnp.concatenate(tiles, axis=1)


def _nsa_attention(qa3, gexp, kc, vct, knsa3, vnsa_t, aggt, et, eye):
    B, S, _ = qa3.shape
    n_pad = kc.shape[1]
    n_cmp = (S - CMP_BLOCK) // CMP_STRIDE + 1
    n_blk = S // SEL_BLOCK
    n_sel = min(N_SELECT, n_blk)
    const = lambda shape: pl.BlockSpec(shape, lambda b, c: (0,) * len(shape))
    return pl.pallas_call(
        functools.partial(_nsa_kernel, n_cmp=n_cmp, n_blk=n_blk, n_sel=n_sel),
        grid=(B, S // NSA_QB),
        in_specs=[pl.BlockSpec((None, NSA_QB, NSA_WIDTH), lambda b, c: (b, c, 0)),
                  pl.BlockSpec((None, None, 3, NSA_ROWS), lambda b, c: (b, c, 0, 0)),
                  pl.BlockSpec((None, n_pad, KV_WIDTH), lambda b, c: (b, 0, 0)),
                  pl.BlockSpec((None, KV_WIDTH, n_pad), lambda b, c: (b, 0, 0)),
                  pl.BlockSpec((None, S, KV_WIDTH), lambda b, c: (b, 0, 0)),
                  pl.BlockSpec((None, S, KV_WIDTH), lambda b, c: (b, 0, 1)),
                  pl.BlockSpec((None, KV_WIDTH, S), lambda b, c: (b, 0, 0)),
                  pl.BlockSpec((None, KV_WIDTH, S), lambda b, c: (b, 1, 0)),
                  const((n_blk, n_pad)), const((S, LANES)), const((NSA_GQ, NSA_GQ))],
        out_specs=pl.BlockSpec((None, NSA_QB, NSA_WIDTH), lambda b, c: (b, c, 0)),
        out_shape=jax.ShapeDtypeStruct((B, S, NSA_WIDTH), F32),
        scratch_shapes=[pltpu.VMEM((NSA_KC, NSA_ROWS), F32), pltpu.VMEM((NSA_KC, NSA_ROWS), F32),
                        pltpu.VMEM((HEAD_DIM, NSA_ROWS), F32)],
        compiler_params=_cparams(("parallel", "arbitrary")),
        name="nsa_attention",
    )(qa3, gexp, kc, vct, knsa3, knsa3, vnsa_t, vnsa_t, aggt, et, eye)


FOX_TQ = 512
FOX_KC = FOX_TQ
FOX_ROWS = FOX_PAIR * FOX_TQ


def _fox_kernel(q_ref, k_ref, aux_ref, vt_ref, o_ref, sa_scr, sb_scr, acc_scr):
    qi = pl.program_id(2)
    s0 = qi * FOX_TQ
    lane = lax.broadcasted_iota(jnp.int32, (FOX_TQ, LANES), 1)
    low = lane < HEAD_DIM
    q2 = q_ref[...]
    zero = jnp.zeros_like(q2)
    aux0 = jnp.where(lane < AUX_PIECES, -1.0, 0.0).astype(q2.dtype)
    aux1 = jnp.where((lane >= AUX_PIECES) & (lane < 2 * AUX_PIECES), -1.0, 0.0).astype(q2.dtype)
    qx = jnp.concatenate([jnp.concatenate([jnp.where(low, q2, zero), aux0], axis=1),
                          jnp.concatenate([jnp.where(low, zero, q2), aux1], axis=1)], axis=0)
    t_l = s0 + lax.broadcasted_iota(jnp.int32, (1, FOX_TQ), 1)

    sub_keys = FOX_KC // FLASH_NSUB

    def scores(i, t, n=sub_keys):
        k0 = pl.multiple_of(i * FOX_KC + t * sub_keys, sub_keys)
        kx = jnp.concatenate([k_ref[pl.ds(k0, n), :], aux_ref[pl.ds(k0, n), :]], axis=1)
        return _dot_nt(kx, qx)

    def values(i, t, n=sub_keys):
        return vt_ref[:, pl.ds(pl.multiple_of(i * FOX_KC + t * sub_keys, sub_keys), n)]

    def diagonal():
        kpos = s0 + lax.broadcasted_iota(jnp.int32, (FOX_KC, 1), 0)
        s_d = _where_tiles(kpos <= t_l, scores(qi, 0, FOX_KC), NEG_INF)
        m_d = jnp.max(s_d, axis=0, keepdims=True)
        p_d = jnp.exp2(s_d - m_d)
        acc_scr[...] = head_values(values(qi, 0, FOX_KC), p_d.astype(BF16))
        return m_d, jnp.sum(p_d, axis=0, keepdims=True)

    def head_values(vt, p):
        return jnp.concatenate([_dot(vt[h * HEAD_DIM:(h + 1) * HEAD_DIM], p[:, h * FOX_TQ:(h + 1) * FOX_TQ])
                                for h in range(FOX_PAIR)], axis=1)

    l_f = _flash_chunks(qi, scores, values, diagonal, sa_scr, sb_scr, acc_scr, pv_fn=head_values)
    o_t = acc_scr[...] / l_f
    o_ref[...] = jnp.concatenate([o_t[:, h * FOX_TQ:(h + 1) * FOX_TQ].T for h in range(FOX_PAIR)], axis=1)


def _fox_attention(fqk3, aux3, vfox_t):
    B, S, _ = fqk3.shape
    return pl.pallas_call(
        _fox_kernel,
        grid=(B, FOX_NPAIR, S // FOX_TQ),
        in_specs=[pl.BlockSpec((None, FOX_TQ, LANES), lambda b, h, i: (b, i, h)),
                  pl.BlockSpec((None, S, LANES), lambda b, h, i: (b, 0, FOX_NPAIR + h)),
                  pl.BlockSpec((None, S, LANES), lambda b, h, i: (b, 0, h)),
                  pl.BlockSpec((None, LANES, S), lambda b, h, i: (b, h, 0))],
        out_specs=pl.BlockSpec((None, FOX_TQ, LANES), lambda b, h, i: (b, i, h)),
        out_shape=jax.ShapeDtypeStruct((B, S, FOX_WIDTH), F32),
        scratch_shapes=[pltpu.VMEM((FOX_KC, FOX_ROWS), F32), pltpu.VMEM((FOX_KC, FOX_ROWS), F32),
                        pltpu.VMEM((HEAD_DIM, FOX_ROWS), F32)],
        compiler_params=_cparams(("parallel", "parallel", "arbitrary")),
        name="fox_attention",
    )(fqk3, fqk3, aux3, vfox_t)


ROUTE_ROWS = 32


def _first_of(vals, v):
    idx = jnp.full(v.shape, len(vals) - 1, jnp.int32)
    for k in range(len(vals) - 2, -1, -1):
        idx = jnp.where(vals[k] == v, k, idx)
    return idx


def _route_t(lt):
    row = [lt[i:i + 1, :] for i in range(N_GROUPS + N_EXPERTS)]
    grp = row[:N_GROUPS]
    gmax = functools.reduce(jnp.maximum, grp)
    g_star = _first_of(grp, gmax)
    p_grp = 1.0 / functools.reduce(jnp.add, [jnp.exp(g - gmax) for g in grp])
    logit = []
    for k in range(EXPERTS_PER_GROUP):
        ek = row[ROUTE_E_OFF + (N_GROUPS - 1) * EXPERTS_PER_GROUP + k]
        for g in range(N_GROUPS - 2, -1, -1):
            ek = jnp.where(g_star == g, row[ROUTE_E_OFF + g * EXPERTS_PER_GROUP + k], ek)
        logit.append(ek)
    emax = functools.reduce(jnp.maximum, logit)
    eexp = [jnp.exp(e - emax) for e in logit]
    esum = functools.reduce(jnp.add, eexp)
    prob = [e / esum for e in eexp]
    v1 = functools.reduce(jnp.maximum, prob)
    i1 = _first_of(prob, v1)
    rest = [jnp.where(i1 == k, -1.0, prob[k]) for k in range(EXPERTS_PER_GROUP)]
    v2 = functools.reduce(jnp.maximum, rest)
    i2 = _first_of(rest, v2)
    den = v1 + v2
    w1 = p_grp * (v1 / den)
    w2 = p_grp * (v2 / den)
    base = ROUTE_E_OFF + g_star * EXPERTS_PER_GROUP
    r_idx = lax.broadcasted_iota(jnp.int32, lt.shape, 0)
    return (jnp.where(r_idx == base + i1, w1, 0.0) + jnp.where(r_idx == base + i2, w2, 0.0)
            + jnp.where(r_idx == ROUTE_G_LANE, g_star.astype(F32), 0.0))


def _outproj_kernel(on_ref, of_ref, x_ref, bn_ref, bf_ref, wn_ref, wf_ref, gffn_ref, wrt_ref, brt_ref,
                    h_ref, u_ref, comb_ref):
    mn = _rms(on_ref[...], bn_ref[...]).astype(BF16)
    mf = _rms(of_ref[...], bf_ref[...]).astype(BF16)
    h = x_ref[...] + (_dot(mn, wn_ref[...]) + _dot(mf, wf_ref[...]))
    h_ref[...] = h
    u = _rms(h, gffn_ref[...]).astype(BF16)
    u_ref[...] = u
    lt = _dot_nt(wrt_ref[...], u) + brt_ref[...]
    comb_t = _route_t(lt[0:ROUTE_ROWS])
    comb_t = jnp.concatenate([comb_t, jnp.zeros((LANES - ROUTE_ROWS, comb_t.shape[1]), F32)], axis=0)
    comb_ref[...] = comb_t.T


def _outproj(o_nsa, o_fox, x2, beta_n, beta_f, w_n, w_f, g_ffn, w_r, b_r, tm):
    T = x2.shape[0]
    full = lambda shape: pl.BlockSpec(shape, lambda i: (0,) * len(shape))
    row = lambda w: pl.BlockSpec((tm, w), lambda i: (i, 0))
    return pl.pallas_call(
        _outproj_kernel,
        grid=(T // tm,),
        in_specs=[row(NSA_WIDTH), row(FOX_WIDTH), row(D_MODEL), full((1, NSA_WIDTH)), full((1, FOX_WIDTH)),
                  full((NSA_WIDTH, D_MODEL)), full((FOX_WIDTH, D_MODEL)), full((1, D_MODEL)),
                  full((LANES, D_MODEL)), full((LANES, 1))],
        out_specs=[row(D_MODEL), row(D_MODEL), row(LANES)],
        out_shape=[jax.ShapeDtypeStruct((T, D_MODEL), F32),
                   jax.ShapeDtypeStruct((T, D_MODEL), BF16),
                   jax.ShapeDtypeStruct((T, LANES), F32)],
        compiler_params=_cparams(("parallel",)),
        name="outproj_router",
    )(o_nsa, o_fox, x2, beta_n, beta_f, w_n, w_f, g_ffn, w_r, b_r)


MOE_TS = 1024
MOE_RB = 128
MOE_TSP = MOE_TS + N_GROUPS * MOE_RB
MOE_META = 8
MOE_TPS = 2


def _moe_sort_kernel(u_ref, comb_ref, tri_ref, us_ref, cs_ref, pos_ref, meta_ref):
    comb = comb_ref[...]
    lane = lax.broadcasted_iota(jnp.int32, comb.shape, 1)
    grp = comb[:, ROUTE_G_LANE:ROUTE_G_LANE + 1].astype(jnp.int32)
    onehot = jnp.where((lane == grp) & (lane < N_GROUPS), 1.0, 0.0)
    rank = _dot(tri_ref[...], onehot.astype(BF16))
    counts = rank[MOE_TS - 1:MOE_TS, :]
    nblk = jnp.floor((counts + (MOE_RB - 1)) * (1.0 / MOE_RB))
    lane1 = lane[0:1, :]
    blk0 = jnp.zeros_like(nblk)
    run = jnp.zeros((1, 1), F32)
    for g in range(1, N_GROUPS):
        run = run + nblk[:, g - 1:g]
        blk0 = blk0 + jnp.where(lane1 == g, run, 0.0)
    val = onehot * (blk0 * MOE_RB + rank - 1.0)
    pos_col = jnp.sum(val, axis=-1, keepdims=True)
    pos_ref[...] = pos_col.astype(jnp.int32)
    ones = jnp.ones((SUBLANES, LANES), BF16)
    hi, mid, lo = _split3(val)
    pos_row = (_dot_nt(ones, hi) + _dot_nt(ones, mid) + _dot_nt(ones, lo))[0:1, :].astype(jnp.int32)
    r_idx = lax.broadcasted_iota(jnp.int32, (MOE_TSP, MOE_TS), 0)
    perm = jnp.where(r_idx == pos_row, 1.0, 0.0).astype(BF16)
    c_hi = comb.astype(BF16)
    c_lo = (comb - c_hi.astype(F32)).astype(BF16)
    moved = _dot(perm, jnp.concatenate([u_ref[...], c_hi, c_lo], axis=1))
    us_ref[...] = moved[:, 0:D_MODEL].astype(us_ref.dtype)
    cs_ref[...] = moved[:, D_MODEL:D_MODEL + LANES] + moved[:, D_MODEL + LANES:D_MODEL + 2 * LANES]
    meta = blk0 + pltpu.roll(nblk, N_GROUPS, 1)
    meta_ref[...] = jnp.where(lane1 < MOE_META, meta, 0.0).astype(jnp.int32)


def _moe_sort(u, comb, tri):
    T = u.shape[0]
    nt = T // MOE_TS
    return pl.pallas_call(
        _moe_sort_kernel,
        grid=(nt,),
        in_specs=[pl.BlockSpec((MOE_TS, D_MODEL), lambda i: (i, 0)),
                  pl.BlockSpec((MOE_TS, LANES), lambda i: (i, 0)),
                  pl.BlockSpec((MOE_TS, MOE_TS), lambda i: (0, 0))],
        out_specs=[pl.BlockSpec((MOE_TSP, D_MODEL), lambda i: (i, 0)),
                   pl.BlockSpec((MOE_TSP, LANES), lambda i: (i, 0)),
                   pl.BlockSpec((MOE_TS, 1), lambda i: (i, 0)),
                   pl.BlockSpec((None, 1, LANES), lambda i: (i, 0, 0))],
        out_shape=[jax.ShapeDtypeStruct((nt * MOE_TSP, D_MODEL), BF16),
                   jax.ShapeDtypeStruct((nt * MOE_TSP, LANES), F32),
                   jax.ShapeDtypeStruct((T, 1), jnp.int32),
                   jax.ShapeDtypeStruct((nt, 1, LANES), jnp.int32)],
        compiler_params=_cparams(("parallel",)),
        name="moe_sort",
    )(u, comb, tri)


def _moe_kernel(meta_ref, us_ref, cs_ref, wgu_ref, wd_ref, o_ref, acc_ref):
    i = pl.program_id(0)
    e = pl.program_id(1)
    g = e // EXPERTS_PER_GROUP

    @pl.when(e == 0)
    def _():
        acc_ref[...] = jnp.zeros_like(acc_ref)

    for tile in range(MOE_TPS):
        base = (i * MOE_TPS + tile) * MOE_META
        first = meta_ref[base + g] + tile * (MOE_TSP // MOE_RB)
        count = meta_ref[base + N_GROUPS + g]

        def blocks(j0, n, first=first):
            rows = [pl.multiple_of((first + j0 + k) * MOE_RB, MOE_RB) for k in range(n)]
            gus = [_dot(us_ref[pl.ds(r0, MOE_RB), :], wgu_ref[...]) for r0 in rows]
            for r0, gu in zip(rows, gus):
                cs = cs_ref[pl.ds(r0, MOE_RB), :]
                lane = lax.broadcasted_iota(jnp.int32, cs.shape, 1)
                ce = jnp.sum(jnp.where(lane == ROUTE_E_OFF + e, cs, 0.0), axis=-1, keepdims=True)
                gt = gu[:, 0:D_EXPERT]
                hid = gt * (1.0 / (1.0 + jnp.exp(-gt))) * gu[:, D_EXPERT:2 * D_EXPERT]
                acc_ref[pl.ds(r0, MOE_RB), :] += _dot((ce * hid).astype(BF16), wd_ref[...])

        def pair(j, carry, blocks=blocks):
            blocks(2 * j, 2)
            return carry

        def single(_, carry, blocks=blocks, count=count):
            blocks(count - 1, 1)
            return carry

        lax.fori_loop(0, count // 2, pair, 0)
        lax.fori_loop(0, count & 1, single, 0)

    @pl.when(e == N_EXPERTS - 1)
    def _():
        o_ref[...] = acc_ref[...].astype(o_ref.dtype)


def _moe(meta, us, cs, wgu, wd):
    rows = MOE_TPS * MOE_TSP
    grid_spec = pltpu.PrefetchScalarGridSpec(
        num_scalar_prefetch=1,
        grid=(us.shape[0] // rows, N_EXPERTS),
        in_specs=[pl.BlockSpec((rows, D_MODEL), lambda i, e, m: (i, 0)),
                  pl.BlockSpec((rows, LANES), lambda i, e, m: (i, 0)),
                  pl.BlockSpec((None, D_MODEL, 2 * D_EXPERT), lambda i, e, m: (e, 0, 0)),
                  pl.BlockSpec((None, D_EXPERT, D_MODEL), lambda i, e, m: (e, 0, 0))],
        out_specs=pl.BlockSpec((rows, D_MODEL), lambda i, e, m: (i, 0)),
        scratch_shapes=[pltpu.VMEM((rows, D_MODEL), F32)])
    return pl.pallas_call(
        _moe_kernel,
        grid_spec=grid_spec,
        out_shape=jax.ShapeDtypeStruct(us.shape, BF16),
        compiler_params=_cparams(("parallel", "arbitrary")),
        name="moe_experts",
    )(meta, us, cs, wgu, wd)


def _ple_kernel(ys_ref, pos_ref, h_ref, p_ref, gple_ref, wg_ref, bg_ref, wp_ref, gfin_ref, o_ref):
    r_idx = lax.broadcasted_iota(jnp.int32, (pos_ref.shape[0], MOE_TSP), 1)
    unperm = jnp.where(r_idx == pos_ref[...], 1.0, 0.0).astype(BF16)
    h = h_ref[...] + _dot(unperm, ys_ref[...])
    v = _rms(h, gple_ref[...]).astype(BF16)
    z = _dot(v, wg_ref[...]) + bg_ref[...]
    gate = 1.0 / (1.0 + jnp.exp(-z))
    proj = _dot(p_ref[...].astype(BF16), wp_ref[...])
    o_ref[...] = _rms(h + gate * proj, gfin_ref[...])


def _ple(ys, pos, h1, p2, g_ple, w_g, b_g, w_p, g_final, tm):
    T = h1.shape[0]
    per = MOE_TS // tm
    full = lambda shape: pl.BlockSpec(shape, lambda i: (0,) * len(shape))
    row = lambda w: pl.BlockSpec((tm, w), lambda i: (i, 0))
    return pl.pallas_call(
        _ple_kernel,
        grid=(T // tm,),
        in_specs=[pl.BlockSpec((MOE_TSP, D_MODEL), lambda i: (i // per, 0)), row(1), row(D_MODEL), row(PLE_DIM),
                  full((1, D_MODEL)), full((D_MODEL, D_MODEL)),
                  full((1, D_MODEL)), full((PLE_DIM, D_MODEL)), full((1, D_MODEL))],
        out_specs=row(D_MODEL),
        out_shape=jax.ShapeDtypeStruct((T, D_MODEL), F32),
        compiler_params=_cparams(("parallel",)),
        name="ple_final",
    )(ys, pos, h1, p2, g_ple, w_g, b_g, w_p, g_final)


def _rope_lane_tables():
    half = ROPE_HALF
    inv_freq = jnp.power(jnp.float32(ROPE_THETA), -jnp.arange(half, dtype=jnp.float32) / half)
    j = np.arange(LANES) % HEAD_DIM
    freq = jnp.where(jnp.asarray(j < ROPE_DIM), inv_freq[jnp.asarray(j % half)], 0.0).reshape(1, LANES)
    mlo = jnp.asarray(np.where(j < half, -1.0, 0.0), F32).reshape(1, LANES)
    mhi = jnp.asarray(np.where((j >= half) & (j < ROPE_DIM), 1.0, 0.0), F32).reshape(1, LANES)
    return inv_freq, freq.astype(F32), mlo, mhi


def _block_diag2(w):
    z = jnp.zeros_like(w)
    return jnp.concatenate([jnp.concatenate([w, z], axis=-1), jnp.concatenate([z, w], axis=-1)], axis=-2)


def _aux_route_table():
    r = np.zeros((AUX_PIECES, LANES, FOX_NPAIR * LANES), np.float32)
    for head in range(FOX_HEADS):
        for k in range(AUX_PIECES):
            r[k, SMALL_F_OFF + head, (head // FOX_PAIR) * LANES + (head % FOX_PAIR) * AUX_PIECES + k] = 1.0
    return jnp.asarray(r, BF16)


def _layer(h3, p3, positions, prm, g_final):
    B, S, _ = h3.shape
    T = B * S
    tm = 512
    tm_proj = 1024
    x2 = h3.reshape(T, D_MODEL)
    inv_freq, freq, mlo, mhi = _rope_lane_tables()

    w_in = prm['w_in']
    offs = np.cumsum([0, NSA_WIDTH, KV_WIDTH, KV_WIDTH, KV_WIDTH, KV_WIDTH, KV_WIDTH, KV_WIDTH,
                      N_GATE, FOX_WIDTH, FOX_WIDTH, FOX_WIDTH, FOX_HEADS])
    seg = lambda k: w_in[:, offs[k]:offs[k + 1]]
    wq = seg(0).reshape(D_MODEL, NSA_GROUPS, NSA_HPG, HEAD_DIM).transpose(0, 2, 1, 3).reshape(D_MODEL, NSA_WIDTH)
    pad = jnp.zeros((D_MODEL, LANES - N_GATE - FOX_HEADS), w_in.dtype)
    w_all = jnp.concatenate([wq, seg(1), seg(2), seg(3), seg(5), seg(8), seg(9), seg(7), seg(11), pad],
                            axis=1).astype(BF16)
    wt_v = jnp.concatenate([seg(4), seg(6), seg(10)], axis=1).T.astype(BF16)
    bias_small = jnp.concatenate([prm['b_nsa_gate'], prm['b_forget'],
                                  jnp.zeros((LANES - N_GATE - FOX_HEADS,), F32)]).reshape(1, LANES)

    qa, cmp_tok, knsa, vnsa_t, fqk, vfox_t, small = _inproj(
        x2, positions.reshape(T // tm_proj, 1, tm_proj), prm['g_mix'].reshape(1, D_MODEL), w_all, wt_v,
        bias_small, inv_freq.reshape(ROPE_HALF, 1), mlo, mhi, tm_proj, B, S)

    tri = jnp.asarray(np.tril(np.ones((CUM_SUB, CUM_SUB), np.float32)), BF16)
    small3 = small.reshape(B, S, LANES)
    aux3 = _cumsum(small3, tri, _aux_route_table())

    n_rows = S // CMP_STRIDE
    tile2 = lambda pe: jnp.concatenate([pe, pe], axis=-1)
    bd1 = lambda w: _block_diag2(w.reshape(CMP_BLOCK, HEAD_DIM, CMP_HIDDEN)).astype(BF16)
    pos_cmp = positions[:, CMP_BLOCK - 1::CMP_STRIDE]
    pos_cmp = jnp.pad(pos_cmp, ((0, 0), (0, n_rows - pos_cmp.shape[1]))).reshape(B, n_rows, 1)
    kc, vct = _compress(cmp_tok.reshape(B, S, 2 * KV_WIDTH), tile2(prm['pe_cmp_k']), tile2(prm['pe_cmp_v']),
                        bd1(prm['w_cmp_k1']), bd1(prm['w_cmp_v1']),
                        _block_diag2(prm['w_cmp_k2']).astype(BF16), _block_diag2(prm['w_cmp_v2']).T.astype(BF16),
                        pos_cmp, freq, mlo, mhi)

    n_cmp = (S - CMP_BLOCK) // CMP_STRIDE + 1
    n_blk = S // SEL_BLOCK
    cs = np.arange(n_rows)[:, None] * CMP_STRIDE
    ss = np.arange(n_blk)[None, :] * SEL_BLOCK
    ov = np.clip(np.minimum(cs + CMP_BLOCK, ss + SEL_BLOCK) - np.maximum(cs, ss), 0, None) / CMP_BLOCK
    ov[n_cmp:] = 0.0
    aggt = jnp.asarray(ov.T, BF16)
    et = jnp.asarray((np.arange(S)[:, None] // SEL_BLOCK) == np.arange(LANES)[None, :], BF16)
    eye = jnp.asarray(np.eye(NSA_GQ, dtype=np.float32), BF16)
    gexp = small3[:, :, :N_GATE].reshape(B, S // NSA_QB, NSA_QB, NSA_GROUPS, NSA_HPG, 3)
    gexp = gexp.transpose(0, 1, 5, 3, 4, 2).reshape(B, S // NSA_QB, 3, NSA_ROWS)
    o_nsa = _nsa_attention(qa.reshape(B, S, NSA_WIDTH), gexp, kc, vct, knsa.reshape(B, S, 2 * KV_WIDTH),
                           vnsa_t, aggt, et, eye)

    o_fox = _fox_attention(fqk.reshape(B, S, 2 * FOX_WIDTH), aux3, vfox_t)

    perm = lambda a: a.reshape(NSA_GROUPS, NSA_HPG, HEAD_DIM, -1).transpose(1, 0, 2, 3).reshape(NSA_WIDTH, -1)
    beta_n = perm(prm['beta_nsa'].reshape(NSA_WIDTH, 1)).reshape(1, NSA_WIDTH)
    w_out = prm['w_out']
    w_n = perm(w_out[:NSA_WIDTH]).astype(BF16)
    w_f = w_out[NSA_WIDTH:].astype(BF16)
    w_r = jnp.concatenate([prm['w_group'], prm['w_router'],
                           jnp.zeros((D_MODEL, LANES - N_GROUPS - N_EXPERTS), F32)], axis=1).T.astype(BF16)
    b_r = jnp.concatenate([prm['b_group'], prm['b_router'],
                           jnp.zeros((LANES - N_GROUPS - N_EXPERTS,), F32)]).reshape(LANES, 1)
    h1, u, comb = _outproj(o_nsa.reshape(T, NSA_WIDTH), o_fox.reshape(T, FOX_WIDTH), x2, beta_n,
                           prm['beta_fox'].reshape(1, FOX_WIDTH), w_n, w_f,
                           prm['g_ffn'].reshape(1, D_MODEL), w_r, b_r, tm_proj)

    wgu = jnp.concatenate([prm['w_gate_e'], prm['w_up_e']], axis=-1).astype(BF16)
    tri_s = jnp.asarray(np.tril(np.ones((MOE_TS, MOE_TS), np.float32)), BF16)
    us, cs, pos, meta = _moe_sort(u, comb, tri_s)
    ys = _moe(meta[:, 0, :MOE_META].reshape(-1), us, cs, wgu, prm['w_down_e'].astype(BF16))

    out = _ple(ys, pos, h1, p3.reshape(T, PLE_DIM), prm['g_ple'].reshape(1, D_MODEL),
               prm['w_ple_gate'].astype(BF16),
               prm['b_ple_gate'].reshape(1, D_MODEL), prm['w_ple_proj'].astype(BF16),
               g_final.reshape(1, D_MODEL), tm)
    return out.reshape(B, S, D_MODEL)


_PARAM_NAMES = ('g_mix', 'w_in', 'b_nsa_gate', 'b_forget', 'pe_cmp_k', 'w_cmp_k1', 'w_cmp_k2',
                'pe_cmp_v', 'w_cmp_v1', 'w_cmp_v2', 'beta_nsa', 'beta_fox', 'w_out', 'g_ffn',
                'w_group', 'b_group', 'w_router', 'b_router', 'w_gate_e', 'w_up_e', 'w_down_e',
                'g_ple', 'w_ple_gate', 'b_ple_gate', 'w_ple_proj')


def kernel(x, p, positions, g_mix, w_in, b_nsa_gate, b_forget, pe_cmp_k, w_cmp_k1, w_cmp_k2, pe_cmp_v,
           w_cmp_v1, w_cmp_v2, beta_nsa, beta_fox, w_out, g_ffn, w_group, b_group, w_router, b_router,
           w_gate_e, w_up_e, w_down_e, g_ple, w_ple_gate, b_ple_gate, w_ple_proj, g_final):
    stacked = (g_mix, w_in, b_nsa_gate, b_forget, pe_cmp_k, w_cmp_k1, w_cmp_k2, pe_cmp_v, w_cmp_v1,
               w_cmp_v2, beta_nsa, beta_fox, w_out, g_ffn, w_group, b_group, w_router, b_router,
               w_gate_e, w_up_e, w_down_e, g_ple, w_ple_gate, b_ple_gate, w_ple_proj)
    depth = w_in.shape[0]
    assert depth == 1, "the final norm is fused into the last layer's embedding kernel"
    prm = {n: a[0] for n, a in zip(_PARAM_NAMES, stacked)}
    return _layer(x, p[0], positions, prm, g_final)
```
